```python
import math
import jax, jax.numpy as jnp
from jax import lax
import numpy as np

D_MODEL = 1024
BATCH = 16
SEQ = 2048
DEPTH = 1

LRU_WIDTH = 512
LRU_HEADS = 8
LRU_BLOCK = LRU_WIDTH // LRU_HEADS
CONV_WIDTH = 4
LRU_C = 8.0
NSA_Q_HEADS = 8
NSA_KV_HEADS = 2
NSA_GROUP = NSA_Q_HEADS // NSA_KV_HEADS
HEAD_DIM = 64
NSA_WIDTH = NSA_Q_HEADS * HEAD_DIM
KV_WIDTH = NSA_KV_HEADS * HEAD_DIM
CMP_BLOCK = 32
CMP_STRIDE = 16
CMP_HIDDEN = 256
SLC_BLOCK = 64
SLC_TOP_N = 16
N_LOCAL_BLOCKS = 2
WINDOW = 512
WIN_QBLOCK = 128
SLC_QCHUNK = 32
ROPE_THETA = 10000.0
MIX_WIDTH = LRU_WIDTH + NSA_WIDTH
IN_SPLIT_SIZES = (LRU_WIDTH, LRU_WIDTH, NSA_WIDTH, KV_WIDTH, KV_WIDTH, KV_WIDTH, KV_WIDTH,
                  KV_WIDTH, KV_WIDTH, 3 * NSA_Q_HEADS)
IN_COLS = sum(IN_SPLIT_SIZES)
D_FF = 2816
ALPHA = (2.0 * DEPTH) ** 0.25
BETA = (8.0 * DEPTH) ** -0.25
LN_EPS = 1e-5
RMS_EPS = 1e-6
NEG = -1e30

kernel_name = "hymba_rglru_nsa_macaron_deepnorm"


def layer_norm(x, g, b):
    xf = x.astype(jnp.float32)
    mu = jnp.mean(xf, -1, keepdims=True)
    var = jnp.mean(jnp.square(xf - mu), -1, keepdims=True)
    return ((xf - mu) * lax.rsqrt(var + LN_EPS) * g + b).astype(x.dtype)


def rms_norm(x, g):
    xf = x.astype(jnp.float32)
    y = xf * lax.rsqrt(jnp.mean(xf * xf, -1, keepdims=True) + RMS_EPS)
    return (y * g).astype(x.dtype)


def swiglu_ffn(x, w_in, w_out):
    gate, up = jnp.split(x @ w_in, 2, axis=-1)
    return (jax.nn.silu(gate) * up) @ w_out


def rope(x):
    S = x.shape[1]
    half = HEAD_DIM // 2
    inv = ROPE_THETA ** (-jnp.arange(half, dtype=jnp.float32) / half)
    ang = jnp.arange(S, dtype=jnp.float32)[:, None] * inv[None, :]
    cos = jnp.cos(ang)[None, :, None, :]
    sin = jnp.sin(ang)[None, :, None, :]
    x1 = x[..., :half].astype(jnp.float32)
    x2 = x[..., half:].astype(jnp.float32)
    out = jnp.concatenate([x1 * cos - x2 * sin, x2 * cos + x1 * sin], axis=-1)
    return out.astype(x.dtype)


def causal_conv(x, w, b):
    S = x.shape[1]
    xp = jnp.pad(x, ((0, 0), (CONV_WIDTH - 1, 0), (0, 0)))
    y = b
    for j in range(CONV_WIDTH):
        y = y + xp[:, j:j + S] * w[j]
    return y


def rg_lru(x, w_a, b_a, w_x, b_x, lam):
    B, S, C = x.shape
    xh = x.reshape(B, S, LRU_HEADS, LRU_BLOCK)
    r = jax.nn.sigmoid(jnp.einsum('bshi,hij->bshj', xh, w_a) + b_a).reshape(B, S, C)
    i = jax.nn.sigmoid(jnp.einsum('bshi,hij->bshj', xh, w_x) + b_x).reshape(B, S, C)
    log_a = -LRU_C * r.astype(jnp.float32) * jax.nn.softplus(-lam.astype(jnp.float32))
    a = jnp.exp(log_a)
    mult = jnp.sqrt(-jnp.expm1(2.0 * log_a))
    mult = jnp.where(jnp.arange(S)[None, :, None] == 0, 1.0, mult)
    u = mult * (i * x).astype(jnp.float32)

    def combine(c1, c2):
        a1, b1 = c1
        a2, b2 = c2
        return a1 * a2, a2 * b1 + b2

    _, h = lax.associative_scan(combine, (a, u), axis=1)
    return h.astype(x.dtype)


def compress_blocks(k, pe, w1, w2):
    B, S, G, D = k.shape
    n_seg = S // CMP_STRIDE
    r = CMP_BLOCK // CMP_STRIDE
    n_cmp = n_seg - r + 1
    seg = k.reshape(B, n_seg, CMP_STRIDE, G, D)
    blocks = jnp.concatenate([seg[:, j:j + n_cmp] for j in range(r)], axis=2)
    blocks = blocks + pe[:, None, :]
    flat = blocks.transpose(0, 3, 1, 2, 4).reshape(B, G, n_cmp, CMP_BLOCK * D)
    return jax.nn.silu(flat @ w1) @ w2


def cmp_to_slc_overlap(n_cmp, n_blk):
    cs = np.arange(n_cmp) * CMP_STRIDE
    ce = cs + CMP_BLOCK - 1
    ss = np.arange(n_blk) * SLC_BLOCK
    se = ss + SLC_BLOCK - 1
    return ((cs[:, None] <= se[None, :]) & (ce[:, None] >= ss[None, :])).astype(np.float32)


def nsa_attention(q, k_cmp, v_cmp, k_slc, v_slc, k_win, v_win, gate_logits,
                  cmp_pe_k, cmp_w1_k, cmp_w2_k, cmp_pe_v, cmp_w1_v, cmp_w2_v):
    B, S = q.shape[0], q.shape[1]
    G, R, D = NSA_KV_HEADS, NSA_GROUP, HEAD_DIM
    scale = HEAD_DIM ** -0.5
    t = jnp.arange(S)
    qg = q.reshape(B, S, G, R, D).transpose(0, 2, 3, 1, 4)

    kc = compress_blocks(k_cmp, cmp_pe_k, cmp_w1_k, cmp_w2_k)
    vc = compress_blocks(v_cmp, cmp_pe_v, cmp_w1_v, cmp_w2_v)
    n_cmp = kc.shape[2]
    cmp_end = jnp.arange(n_cmp) * CMP_STRIDE + CMP_BLOCK - 1
    cmp_mask = cmp_end[None, :] <= t[:, None]
    s_cmp = jnp.einsum('bgrsd,bgnd->bgrsn', qg, kc).astype(jnp.float32) * scale
    p_cmp = jax.nn.softmax(jnp.where(cmp_mask, s_cmp, NEG), axis=-1)
    p_cmp = p_cmp * cmp_mask.any(-1)[:, None]
    o_cmp = jnp.einsum('bgrsn,bgnd->bgrsd', p_cmp.astype(vc.dtype), vc)

    n_blk = S // SLC_BLOCK
    overlap = jnp.asarray(cmp_to_slc_overlap(n_cmp, n_blk))
    imp = jnp.einsum('bgrsn,nj->bgsj', p_cmp, overlap)
    j = jnp.arange(n_blk)
    blk_valid = (j * SLC_BLOCK)[None, :] <= t[:, None]
    back = (t // SLC_BLOCK)[:, None] - j[None, :]
    forced = (j[None, :] == 0) | ((back >= 0) & (back < N_LOCAL_BLOCKS))
    score = jnp.where(blk_valid, jnp.where(forced, jnp.inf, imp), -jnp.inf)
    n_sel = min(SLC_TOP_N, n_blk)
    top_val, top_idx = lax.top_k(score, n_sel)
    sel_valid = top_val > -jnp.inf

    kb = k_slc.transpose(0, 2, 1, 3).reshape(B, G, n_blk, SLC_BLOCK, D)
    vb = v_slc.transpose(0, 2, 1, 3).reshape(B, G, n_blk, SLC_BLOCK, D)
    n_chunk = S // SLC_QCHUNK
    n_keys = n_sel * SLC_BLOCK
    q_ch = jnp.moveaxis(qg.reshape(B, G, R, n_chunk, SLC_QCHUNK, D), 3, 0)
    i_ch = jnp.moveaxis(top_idx.reshape(B, G, n_chunk, SLC_QCHUNK, n_sel), 2, 0)
    m_ch = jnp.moveaxis(sel_valid.reshape(B, G, n_chunk, SLC_QCHUNK, n_sel), 2, 0)
    starts = jnp.arange(n_chunk) * SLC_QCHUNK
    b_ix = jnp.arange(B)[:, None, None, None]
    g_ix = jnp.arange(G)[None, :, None, None]
    offs = jnp.arange(SLC_BLOCK)

    def sel_chunk(args):
        qc, ic, mc, c0 = args
        kg = kb[b_ix, g_ix, ic].reshape(B, G, SLC_QCHUNK, n_keys, D)
        vg = vb[b_ix, g_ix, ic].reshape(B, G, SLC_QCHUNK, n_keys, D)
        kpos = (ic[..., None] * SLC_BLOCK + offs).reshape(B, G, SLC_QCHUNK, n_keys)
        kval = jnp.broadcast_to(mc[..., None], mc.shape + (SLC_BLOCK,)).reshape(B, G, SLC_QCHUNK, n_keys)
        tq = c0 + jnp.arange(SLC_QCHUNK)
        mask = kval & (kpos <= tq[None, None, :, None])
        s = jnp.einsum('bgrtd,bgtkd->bgrtk', qc, kg).astype(jnp.float32) * scale
        p = jax.nn.softmax(jnp.where(mask[:, :, None], s, NEG), axis=-1)
        return jnp.einsum('bgrtk,bgtkd->bgrtd', p.astype(vg.dtype), vg)

    o_slc = lax.map(sel_chunk, (q_ch, i_ch, m_ch, starts))
    o_slc = jnp.moveaxis(o_slc, 0, 3).reshape(B, G, R, S, D)

    span = WINDOW + WIN_QBLOCK
    n_wb = S // WIN_QBLOCK
    pad = ((0, 0), (0, 0), (WINDOW, 0), (0, 0))
    kw = jnp.pad(k_win.transpose(0, 2, 1, 3), pad)
    vw = jnp.pad(v_win.transpose(0, 2, 1, 3), pad)
    q_wb = jnp.moveaxis(qg.reshape(B, G, R, n_wb, WIN_QBLOCK, D), 3, 0)

    def win_block(args):
        qb, i = args
        s0 = i * WIN_QBLOCK
        kk = lax.dynamic_slice_in_dim(kw, s0, span, axis=2)
        vv = lax.dynamic_slice_in_dim(vw, s0, span, axis=2)
        tq = s0 + jnp.arange(WIN_QBLOCK)
        kp = s0 - WINDOW + jnp.arange(span)
        diff = tq[:, None] - kp[None, :]
        mask = (diff >= 0) & (diff < WINDOW) & (kp >= 0)[None, :]
        s = jnp.einsum('bgrtd,bgkd->bgrtk', qb, kk).astype(jnp.float32) * scale
        p = jax.nn.softmax(jnp.where(mask, s, NEG), axis=-1)
        return jnp.einsum('bgrtk,bgkd->bgrtd', p.astype(vv.dtype), vv)

    o_win = lax.map(win_block, (q_wb, jnp.arange(n_wb)))
    o_win = jnp.moveaxis(o_win, 0, 3).reshape(B, G, R, S, D)

    g = jax.nn.sigmoid(gate_logits.reshape(B, S, G, R, 3).transpose(0, 2, 3, 1, 4))
    o = g[..., 0:1] * o_cmp + g[..., 1:2] * o_slc + g[..., 2:3] * o_win
    return o.transpose(0, 3, 1, 2, 4).reshape(B, S, NSA_WIDTH)


def hybrid_mixer(x, w_in, conv_w, conv_b, lru_w_a, lru_b_a, lru_w_x, lru_b_x, lru_lam,
                 cmp_pe_k, cmp_w1_k, cmp_w2_k, cmp_pe_v, cmp_w1_v, cmp_w2_v,
                 gn_lru, gn_nsa, w_out):
    B, S, _ = x.shape
    proj = x @ w_in
    cuts = np.cumsum(IN_SPLIT_SIZES)[:-1].tolist()
    (lru_x, lru_gate, q, k_cmp, v_cmp, k_slc, v_slc, k_win, v_win,
     gate_logits) = jnp.split(proj, cuts, axis=-1)

    h = rg_lru(causal_conv(lru_x, conv_w, conv_b), lru_w_a, lru_b_a, lru_w_x, lru_b_x, lru_lam)
    y_lru = h * jax.nn.gelu(lru_gate)

    def heads(a, n):
        return a.reshape(B, S, n, HEAD_DIM)
    y_nsa = nsa_attention(rope(heads(q, NSA_Q_HEADS)),
                          rope(heads(k_cmp, NSA_KV_HEADS)), heads(v_cmp, NSA_KV_HEADS),
                          rope(heads(k_slc, NSA_KV_HEADS)), heads(v_slc, NSA_KV_HEADS),
                          rope(heads(k_win, NSA_KV_HEADS)), heads(v_win, NSA_KV_HEADS),
                          gate_logits, cmp_pe_k, cmp_w1_k, cmp_w2_k, cmp_pe_v, cmp_w1_v, cmp_w2_v)

    y = jnp.concatenate([rms_norm(y_lru, gn_lru), rms_norm(y_nsa, gn_nsa)], axis=-1)
    return y @ w_out


def setup_inputs(seed: int = 0) -> dict:
    key = jax.random.key(seed)
    ks = jax.random.split(key, 32)
    f32 = jnp.float32
    L = DEPTH

    def nrm(k, shape, scale):
        return jax.random.normal(k, shape, f32) * scale

    u = jax.random.uniform(ks[12], (L, LRU_WIDTH), f32, 0.9, 0.999)
    a_base = u ** (1.0 / LRU_C)
    lru_lam = jnp.log(a_base) - jnp.log1p(-a_base)
    return {
        "x": nrm(ks[0], (BATCH, SEQ, D_MODEL), 1.0),
        "ffn1_w_in": nrm(ks[1], (L, D_MODEL, 2 * D_FF), D_MODEL ** -0.5),
        "ffn1_w_out": nrm(ks[2], (L, D_FF, D_MODEL), BETA * D_FF ** -0.5),
        "ln1_g": 1.0 + nrm(ks[3], (L, D_MODEL), 0.02),
        "ln1_b": nrm(ks[4], (L, D_MODEL), 0.02),
        "mix_w_in": nrm(ks[5], (L, D_MODEL, IN_COLS), D_MODEL ** -0.5),
        "conv_w": nrm(ks[6], (L, CONV_WIDTH, LRU_WIDTH), CONV_WIDTH ** -0.5),
        "conv_b": nrm(ks[7], (L, LRU_WIDTH), 0.02),
        "lru_w_a": nrm(ks[8], (L, LRU_HEADS, LRU_BLOCK, LRU_BLOCK), LRU_BLOCK ** -0.5),
        "lru_b_a": nrm(ks[9], (L, LRU_HEADS, LRU_BLOCK), 0.02),
        "lru_w_x": nrm(ks[10], (L, LRU_HEADS, LRU_BLOCK, LRU_BLOCK), LRU_BLOCK ** -0.5),
        "lru_b_x": nrm(ks[11], (L, LRU_HEADS, LRU_BLOCK), 0.02),
        "lru_lam": lru_lam,
        "cmp_pe_k": nrm(ks[13], (L, CMP_BLOCK, HEAD_DIM), 0.02),
        "cmp_w1_k": nrm(ks[14], (L, CMP_BLOCK * HEAD_DIM, CMP_HIDDEN), (CMP_BLOCK * HEAD_DIM) ** -0.5),
        "cmp_w2_k": nrm(ks[15], (L, CMP_HIDDEN, HEAD_DIM), CMP_HIDDEN ** -0.5),
        "cmp_pe_v": nrm(ks[16], (L, CMP_BLOCK, HEAD_DIM), 0.02),
        "cmp_w1_v": nrm(ks[17], (L, CMP_BLOCK * HEAD_DIM, CMP_HIDDEN), (CMP_BLOCK * HEAD_DIM) ** -0.5),
        "cmp_w2_v": nrm(ks[18], (L, CMP_HIDDEN, HEAD_DIM), CMP_HIDDEN ** -0.5),
        "gn_lru": 1.0 + nrm(ks[19], (L, LRU_WIDTH), 0.02),
        "gn_nsa": 1.0 + nrm(ks[20], (L, NSA_WIDTH), 0.02),
        "mix_w_out": nrm(ks[21], (L, MIX_WIDTH, D_MODEL), BETA * MIX_WIDTH ** -0.5),
        "ln2_g": 1.0 + nrm(ks[22], (L, D_MODEL), 0.02),
        "ln2_b": nrm(ks[23], (L, D_MODEL), 0.02),
        "ffn2_w_in": nrm(ks[24], (L, D_MODEL, 2 * D_FF), D_MODEL ** -0.5),
        "ffn2_w_out": nrm(ks[25], (L, D_FF, D_MODEL), BETA * D_FF ** -0.5),
        "ln3_g": 1.0 + nrm(ks[26], (L, D_MODEL), 0.02),
        "ln3_b": nrm(ks[27], (L, D_MODEL), 0.02),
    }


def reference(x, ffn1_w_in, ffn1_w_out, ln1_g, ln1_b, mix_w_in, conv_w, conv_b,
              lru_w_a, lru_b_a, lru_w_x, lru_b_x, lru_lam,
              cmp_pe_k, cmp_w1_k, cmp_w2_k, cmp_pe_v, cmp_w1_v, cmp_w2_v,
              gn_lru, gn_nsa, mix_w_out, ln2_g, ln2_b,
              ffn2_w_in, ffn2_w_out, ln3_g, ln3_b):
    h = x
    for l in range(DEPTH):
        h = layer_norm(ALPHA * h + 0.5 * swiglu_ffn(h, ffn1_w_in[l], ffn1_w_out[l]), ln1_g[l], ln1_b[l])
        mix = hybrid_mixer(h, mix_w_in[l], conv_w[l], conv_b[l], lru_w_a[l], lru_b_a[l],
                           lru_w_x[l], lru_b_x[l], lru_lam[l],
                           cmp_pe_k[l], cmp_w1_k[l], cmp_w2_k[l], cmp_pe_v[l], cmp_w1_v[l], cmp_w2_v[l],
                           gn_lru[l], gn_nsa[l], mix_w_out[l])
        h = layer_norm(ALPHA * h + mix, ln2_g[l], ln2_b[l])
        h = layer_norm(ALPHA * h + 0.5 * swiglu_ffn(h, ffn2_w_in[l], ffn2_w_out[l]), ln3_g[l], ln3_b[l])
    return h
```

```python
import functools
import math

import numpy as np
import jax
import jax.numpy as jnp
from jax import lax
from jax.experimental import pallas as pl
from jax.experimental.pallas import tpu as pltpu

F32 = jnp.float32
BF16 = jnp.bfloat16

D_MODEL = 1024
LRU_WIDTH = 512
LRU_HEADS = 8
LRU_BLOCK = 64
CONV_WIDTH = 4
LRU_C = 8.0
NSA_Q_HEADS = 8
NSA_KV_HEADS = 2
NSA_GROUP = 4
HEAD_DIM = 64
NSA_WIDTH = 512
KV_WIDTH = 128
CMP_BLOCK = 32
CMP_STRIDE = 16
CMP_HIDDEN = 256
SLC_BLOCK = 64
SLC_SHIFT = 6
SLC_TOP_N = 16
N_LOCAL_BLOCKS = 2
WINDOW = 512
ROPE_THETA = 10000.0
D_FF = 2816
DEPTH = 1
ALPHA = (2.0 * DEPTH) ** 0.25
LN_EPS = 1e-5
RMS_EPS = 1e-6
NEG = -1e30

LANES = 128
VMEM_LIMIT = 48 * 1024 * 1024


def _dot(a, b):
    return jnp.dot(a, b, preferred_element_type=F32)


def _dot_nt(a, b):
    return lax.dot_general(a, b, (((1,), (1,)), ((), ())), preferred_element_type=F32)


def _layer_norm(y, g, b):
    mu = jnp.mean(y, axis=-1, keepdims=True)
    d = y - mu
    var = jnp.mean(d * d, axis=-1, keepdims=True)
    return d * lax.rsqrt(var + LN_EPS) * g + b


def _rms_norm(y, g):
    return y * lax.rsqrt(jnp.mean(y * y, axis=-1, keepdims=True) + RMS_EPS) * g


def _silu(x):
    return x * jax.nn.sigmoid(x)


def _ffn_ln_kernel(x_ref, wg_ref, wu_ref, wo_ref, g_ref, b_ref, o_ref, acc_ref):
    j = pl.program_id(1)

    @pl.when(j == 0)
    def _():
        acc_ref[...] = jnp.zeros_like(acc_ref)

    xb = x_ref[...].astype(BF16)
    gate = _dot(xb, wg_ref[...])
    up = _dot(xb, wu_ref[...])
    act = (_silu(gate) * up).astype(BF16)
    acc_ref[...] += _dot(act, wo_ref[...])

    @pl.when(j == pl.num_programs(1) - 1)
    def _():
        y = ALPHA * x_ref[...] + 0.5 * acc_ref[...]
        o_ref[...] = _layer_norm(y, g_ref[...], b_ref[...])


def _ffn_ln(x, w_in, w_out, g, b, *, tm=512, tf=1408):
    T, D = x.shape
    wg = w_in[:, :D_FF].astype(BF16)
    wu = w_in[:, D_FF:].astype(BF16)
    wo = w_out.astype(BF16)
    return pl.pallas_call(
        _ffn_ln_kernel,
        grid=(T // tm, D_FF // tf),
        in_specs=[
            pl.BlockSpec((tm, D), lambda i, j: (i, 0)),
            pl.BlockSpec((D, tf), lambda i, j: (0, j)),
            pl.BlockSpec((D, tf), lambda i, j: (0, j)),
            pl.BlockSpec((tf, D), lambda i, j: (j, 0)),
            pl.BlockSpec((1, D), lambda i, j: (0, 0)),
            pl.BlockSpec((1, D), lambda i, j: (0, 0)),
        ],
        out_specs=pl.BlockSpec((tm, D), lambda i, j: (i, 0)),
        out_shape=jax.ShapeDtypeStruct((T, D), F32),
        scratch_shapes=[pltpu.VMEM((tm, D), F32)],
        compiler_params=pltpu.CompilerParams(
            dimension_semantics=("parallel", "arbitrary"), vmem_limit_bytes=VMEM_LIMIT),
        name="ffn_ln",
    )(x, wg, wu, wo, g.reshape(1, D), b.reshape(1, D))


PROJ_COLS = 20 * LANES
Q_COL0 = 2 * LRU_WIDTH
KV_COL0 = Q_COL0 + NSA_WIDTH
GATE_COL0 = KV_COL0 + 6 * KV_WIDTH


def _rope_chunk(xc, cos, sin_signed):
    lane = lax.broadcasted_iota(jnp.int32, xc.shape, 1)
    first = (lane & (HEAD_DIM - 1)) < (HEAD_DIM // 2)
    partner = jnp.where(first, pltpu.roll(xc, LANES - HEAD_DIM // 2, axis=1),
                        pltpu.roll(xc, HEAD_DIM // 2, axis=1))
    return xc * cos + partner * sin_signed


def _proj_kernel(h_ref, w_ref, cos_ref, sin_ref, lru_ref, q_ref, kv_ref, gate_ref):
    hb = h_ref[...].astype(BF16)
    p = _dot(hb, w_ref[...])
    lru_ref[...] = p[:, :Q_COL0]
    cos = cos_ref[...]
    sin = sin_ref[...]
    scale = HEAD_DIM ** -0.5
    for c in range(NSA_WIDTH // LANES):
        xc = p[:, Q_COL0 + c * LANES:Q_COL0 + (c + 1) * LANES]
        xc = (_rope_chunk(xc, cos, sin) * scale).astype(BF16)
        q_ref[2 * c] = xc[:, :HEAD_DIM]
        q_ref[2 * c + 1] = xc[:, HEAD_DIM:]
    for c in range(6):
        xc = p[:, KV_COL0 + c * LANES:KV_COL0 + (c + 1) * LANES]
        if c % 2 == 0:
            xc = _rope_chunk(xc, cos, sin)
        xc = xc.astype(BF16)
        kv_ref[2 * c] = xc[:, :HEAD_DIM]
        kv_ref[2 * c + 1] = xc[:, HEAD_DIM:]
    for g in range(NSA_KV_HEADS):
        gate_ref[g] = jax.nn.sigmoid(p[:, GATE_COL0 + g * LANES:GATE_COL0 + (g + 1) * LANES])


def _proj(h, w_in, S, *, tm=512):
    T, D = h.shape
    n_gate = 3 * NSA_GROUP
    w = jnp.zeros((D, PROJ_COLS), F32)
    w = w.at[:, :GATE_COL0].set(w_in[:, :GATE_COL0])
    for g in range(NSA_KV_HEADS):
        w = w.at[:, GATE_COL0 + g * LANES:GATE_COL0 + g * LANES + n_gate].set(
            w_in[:, GATE_COL0 + g * n_gate:GATE_COL0 + (g + 1) * n_gate])
    w = w.astype(BF16)

    half = HEAD_DIM // 2
    inv = ROPE_THETA ** (-jnp.arange(half, dtype=F32) / half)
    ang = jnp.arange(S, dtype=F32)[:, None] * inv[None, :]
    cos = jnp.cos(ang)
    sin = jnp.sin(ang)
    cos_t = jnp.concatenate([cos, cos, cos, cos], axis=1)
    sin_t = jnp.concatenate([-sin, sin, -sin, sin], axis=1)

    nS = S // tm
    return pl.pallas_call(
        _proj_kernel,
        grid=(T // tm,),
        in_specs=[
            pl.BlockSpec((tm, D), lambda i: (i, 0)),
            pl.BlockSpec((D, PROJ_COLS), lambda i: (0, 0)),
            pl.BlockSpec((tm, LANES), lambda i: (i % nS, 0)),
            pl.BlockSpec((tm, LANES), lambda i: (i % nS, 0)),
        ],
        out_specs=[
            pl.BlockSpec((tm, Q_COL0), lambda i: (i, 0)),
            pl.BlockSpec((NSA_Q_HEADS, tm, HEAD_DIM), lambda i: (0, i, 0)),
            pl.BlockSpec((12, tm, HEAD_DIM), lambda i: (0, i, 0)),
            pl.BlockSpec((NSA_KV_HEADS, tm, LANES), lambda i: (0, i, 0)),
        ],
        out_shape=[
            jax.ShapeDtypeStruct((T, Q_COL0), F32),
            jax.ShapeDtypeStruct((NSA_Q_HEADS, T, HEAD_DIM), BF16),
            jax.ShapeDtypeStruct((12, T, HEAD_DIM), BF16),
            jax.ShapeDtypeStruct((NSA_KV_HEADS, T, LANES), F32),
        ],
        compiler_params=pltpu.CompilerParams(
            dimension_semantics=("parallel",), vmem_limit_bytes=VMEM_LIMIT),
        name="proj",
    )(h, w, cos_t, sin_t)


SCAN_PAD = 1024


def _lru_kernel(x_ref, gate_ref, cw_ref, cb_ref, wa_ref, ba_ref, wx_ref, bx_ref, lam_ref,
                y_ref, a_scr, b_scr):
    S = x_ref.shape[0]
    x = x_ref[...]
    row = lax.broadcasted_iota(jnp.int32, x.shape, 0)

    a_scr[0:SCAN_PAD, :] = jnp.zeros((SCAN_PAD, LANES), F32)
    a_scr[SCAN_PAD:, :] = x
    xc = cb_ref[...] + x * cw_ref[CONV_WIDTH - 1:CONV_WIDTH, :]
    for d in range(1, CONV_WIDTH):
        xs = a_scr[SCAN_PAD - d:SCAN_PAD - d + S, :]
        xc = xc + xs * cw_ref[CONV_WIDTH - 1 - d:CONV_WIDTH - d, :]

    xb = xc.astype(BF16)
    r = jax.nn.sigmoid(_dot(xb, wa_ref[0]) + ba_ref[...])
    ig = jax.nn.sigmoid(_dot(xb, wx_ref[0]) + bx_ref[...])
    neg_lam = -lam_ref[...]
    softplus = jnp.maximum(neg_lam, 0.0) + jnp.log1p(jnp.exp(-jnp.abs(neg_lam)))
    log_a = (-LRU_C) * r * softplus
    a = jnp.exp(log_a)
    mult = jnp.sqrt(jnp.tanh(-log_a) * (a * a + 1.0))
    mult = jnp.where(row == 0, 1.0, mult)
    b = mult * (ig * xc)

    a_scr[0:SCAN_PAD, :] = jnp.ones((SCAN_PAD, LANES), F32)
    b_scr[0:SCAN_PAD, :] = jnp.zeros((SCAN_PAD, LANES), F32)
    d = 1
    while d < S:
        a_scr[SCAN_PAD:, :] = a
        b_scr[SCAN_PAD:, :] = b
        a_sh = a_scr[SCAN_PAD - d:SCAN_PAD - d + S, :]
        b_sh = b_scr[SCAN_PAD - d:SCAN_PAD - d + S, :]
        b = a * b_sh + b
        a = a * a_sh
        d *= 2

    y_ref[...] = b * jax.nn.gelu(gate_ref[...])


def _lru(lru, conv_w, conv_b, w_a, b_a, w_x, b_x, lam, B, S):
    T = lru.shape[0]
    n_ch = LRU_WIDTH // LANES

    def blockdiag(w):
        w = w.reshape(n_ch, 2, LRU_BLOCK, LRU_BLOCK)
        z = jnp.zeros((n_ch, LRU_BLOCK, LRU_BLOCK), w.dtype)
        top = jnp.concatenate([w[:, 0], z], axis=2)
        bot = jnp.concatenate([z, w[:, 1]], axis=2)
        return jnp.concatenate([top, bot], axis=1).astype(BF16)

    vec = lambda v: v.reshape(1, LRU_WIDTH)
    chan = lambda rows: pl.BlockSpec((rows, LANES), lambda b, c: (0, c))
    return pl.pallas_call(
        _lru_kernel,
        grid=(B, n_ch),
        in_specs=[
            pl.BlockSpec((S, LANES), lambda b, c: (b, c)),
            pl.BlockSpec((S, LANES), lambda b, c: (b, n_ch + c)),
            chan(CONV_WIDTH), chan(1),
            pl.BlockSpec((1, LANES, LANES), lambda b, c: (c, 0, 0)), chan(1),
            pl.BlockSpec((1, LANES, LANES), lambda b, c: (c, 0, 0)), chan(1),
            chan(1),
        ],
        out_specs=pl.BlockSpec((S, LANES), lambda b, c: (b, c)),
        out_shape=jax.ShapeDtypeStruct((T, LRU_WIDTH), F32),
        scratch_shapes=[pltpu.VMEM((SCAN_PAD + S, LANES), F32),
                        pltpu.VMEM((SCAN_PAD + S, LANES), F32)],
        compiler_params=pltpu.CompilerParams(
            dimension_semantics=("parallel", "parallel"), vmem_limit_bytes=VMEM_LIMIT),
        name="lru",
    )(lru, lru, conv_w, vec(conv_b), blockdiag(w_a), vec(b_a.reshape(-1)),
      blockdiag(w_x), vec(b_x.reshape(-1)), vec(lam))


def _cmp_kernel(seg_ref, pek_ref, w1k_ref, w2k_ref, pev_ref, w1v_ref, w2v_ref, out_ref):
    n_seg = seg_ref.shape[2]
    half = CMP_STRIDE * HEAD_DIM
    for idx in range(2 * NSA_KV_HEADS):
        pe_ref, w1_ref, w2_ref = ((pek_ref, w1k_ref, w2k_ref) if idx < NSA_KV_HEADS
                                  else (pev_ref, w1v_ref, w2v_ref))
        seg = seg_ref[idx, 0]
        first = _dot(seg, w1_ref[0:half, :])
        second = _dot(seg, w1_ref[half:2 * half, :])
        bias = _dot(pe_ref[...], w1_ref[...])[0:1, :]
        hid = first + pltpu.roll(second, n_seg - 1, axis=0) + bias
        out_ref[0, idx] = _dot(_silu(hid).astype(BF16), w2_ref[...]).astype(BF16)


def _compress(kv, pe_k, w1_k, w2_k, pe_v, w1_v, w2_v, B, S):
    n_seg = S // CMP_STRIDE
    seg = kv[0:4].reshape(4, B, n_seg, CMP_STRIDE * HEAD_DIM)
    pe8 = lambda pe: jnp.broadcast_to(pe.reshape(1, -1), (8, CMP_BLOCK * HEAD_DIM)).astype(BF16)
    full = lambda shape: pl.BlockSpec(shape, lambda b: (0,) * len(shape))
    flat = CMP_BLOCK * HEAD_DIM
    return pl.pallas_call(
        _cmp_kernel,
        grid=(B,),
        in_specs=[
            pl.BlockSpec((4, 1, n_seg, CMP_STRIDE * HEAD_DIM), lambda b: (0, b, 0, 0)),
            full((8, flat)), full((flat, CMP_HIDDEN)), full((CMP_HIDDEN, HEAD_DIM)),
            full((8, flat)), full((flat, CMP_HIDDEN)), full((CMP_HIDDEN, HEAD_DIM)),
        ],
        out_specs=pl.BlockSpec((1, 4, n_seg, HEAD_DIM), lambda b: (b, 0, 0, 0)),
        out_shape=jax.ShapeDtypeStruct((B, 4, n_seg, HEAD_DIM), BF16),
        compiler_params=pltpu.CompilerParams(
            dimension_semantics=("parallel",), vmem_limit_bytes=VMEM_LIMIT),
        name="compress",
    )(seg, pe8(pe_k), w1_k.astype(BF16), w2_k.astype(BF16),
      pe8(pe_v), w1_v.astype(BF16), w2_v.astype(BF16))


def _flash_tile(q, k, v, allowed, carry):
    m, l, acc = carry
    R, tq = m.shape[0], m.shape[1]
    tk = k.shape[0]
    s = _dot_nt(q, k).reshape(R, tq, tk)
    s = jnp.where(allowed[None], s, NEG)
    m_new = jnp.maximum(m, jnp.max(s, axis=-1, keepdims=True))
    alpha = jnp.exp(m - m_new)
    p = jnp.exp(s - m_new)
    l = alpha * l + jnp.sum(p, axis=-1, keepdims=True)
    pv = _dot(p.reshape(R * tq, tk).astype(BF16), v)
    acc = alpha.reshape(R * tq, 1) * acc + pv
    return m_new, l, acc


def _attn_kernel(q_ref, kc_ref, vc_ref, ks_ref, vs_ref, kw_ref, vw_ref, gate_ref, ovl_ref,
                 o_ref, *, tq):
    R = NSA_GROUP
    tk = tq
    i = pl.program_id(2)
    t0 = i * tq
    q = q_ref[...].reshape(R * tq, HEAD_DIM)
    n_cmp = kc_ref.shape[0]

    tpos_c = t0 + lax.broadcasted_iota(jnp.int32, (tq, n_cmp), 0)
    cend = lax.broadcasted_iota(jnp.int32, (tq, n_cmp), 1) * CMP_STRIDE + (CMP_BLOCK - 1)
    cmask = cend <= tpos_c
    s = _dot_nt(q, kc_ref[...]).reshape(R, tq, n_cmp)
    s = jnp.where(cmask[None], s, NEG)
    e = jnp.exp(s - jnp.max(s, axis=-1, keepdims=True))
    p_cmp = e / jnp.sum(e, axis=-1, keepdims=True)
    any_cmp = tpos_c >= (CMP_BLOCK - 1)
    p_cmp = jnp.where(any_cmp[None], p_cmp, 0.0)
    o_cmp = _dot(p_cmp.reshape(R * tq, n_cmp).astype(BF16), vc_ref[...])

    psum = p_cmp[0] + p_cmp[1] + p_cmp[2] + p_cmp[3]
    p_hi = psum.astype(BF16)
    p_lo = (psum - p_hi.astype(F32)).astype(BF16)
    ovl = ovl_ref[...]
    imp = _dot_nt(ovl, p_hi) + _dot_nt(ovl, p_lo)
    n_blk = imp.shape[0]
    jblk = lax.broadcasted_iota(jnp.int32, (n_blk, tq), 0)
    tpos_b = t0 + lax.broadcasted_iota(jnp.int32, (n_blk, tq), 1)
    blk_valid = jblk * SLC_BLOCK <= tpos_b
    back = (tpos_b >> SLC_SHIFT) - jblk
    forced = (jblk == 0) | ((back >= 0) & (back < N_LOCAL_BLOCKS))
    score = jnp.where(blk_valid, jnp.where(forced, jnp.inf, imp), -jnp.inf)
    rank = jnp.zeros((n_blk, tq), jnp.int32)
    for kb in range(n_blk):
        sk = score[kb:kb + 1, :]
        ahead = (sk > score) | ((sk == score) & (jblk > kb))
        rank = rank + ahead.astype(jnp.int32)
    sel_t = (blk_valid & (rank < SLC_TOP_N)).astype(F32)
    sel_t = jnp.concatenate([sel_t, jnp.zeros((LANES - n_blk, tq), F32)], axis=0)
    sel = sel_t.T.astype(BF16)

    zeros = (jnp.full((R, tq, 1), NEG, F32), jnp.zeros((R, tq, 1), F32),
             jnp.zeros((R * tq, HEAD_DIM), F32))
    tpos = t0 + lax.broadcasted_iota(jnp.int32, (tq, tk), 0)
    kofs = lax.broadcasted_iota(jnp.int32, (tq, tk), 1)
    e_blk = lax.broadcasted_iota(jnp.int32, (LANES, tk), 0)
    e_key = lax.broadcasted_iota(jnp.int32, (LANES, tk), 1)

    def slc_body(kt, carry):
        k0 = pl.multiple_of(kt * tk, tk)
        expand = (e_blk == ((k0 + e_key) >> SLC_SHIFT)).astype(BF16)
        picked = _dot(sel, expand)
        allowed = (picked > 0.5) & ((k0 + kofs) <= tpos)
        return _flash_tile(q, ks_ref[pl.ds(k0, tk), :], vs_ref[pl.ds(k0, tk), :], allowed, carry)

    m_s, l_s, acc_s = lax.fori_loop(0, i + 1, slc_body, zeros)
    o_slc = acc_s / l_s.reshape(R * tq, 1)

    def win_body(kt, carry):
        k0 = pl.multiple_of(kt * tk, tk)
        diff = tpos - (k0 + kofs)
        allowed = (diff >= 0) & (diff < WINDOW)
        return _flash_tile(q, kw_ref[pl.ds(k0, tk), :], vw_ref[pl.ds(k0, tk), :], allowed, carry)

    m_w, l_w, acc_w = lax.fori_loop(jnp.maximum(i - WINDOW // tk, 0), i + 1, win_body, zeros)
    o_win = acc_w / l_w.reshape(R * tq, 1)

    gt = gate_ref[...]
    outs = []
    for r in range(R):
        rows = slice(r * tq, (r + 1) * tq)
        outs.append(gt[:, 3 * r:3 * r + 1] * o_cmp[rows]
                    + gt[:, 3 * r + 1:3 * r + 2] * o_slc[rows]
                    + gt[:, 3 * r + 2:3 * r + 3] * o_win[rows])
    o_ref[...] = jnp.concatenate(outs, axis=1)


def _overlap_t(n_cmp_pad, n_blk):
    cs = np.arange(n_cmp_pad) * CMP_STRIDE
    ce = cs + CMP_BLOCK - 1
    ss = np.arange(n_blk) * SLC_BLOCK
    se = ss + SLC_BLOCK - 1
    return ((cs[None, :] <= se[:, None]) & (ce[None, :] >= ss[:, None])).astype(np.float32)


def _attention(q, kv, kcvc, gates, B, S, *, tq=256):
    T = q.shape[1]
    nq = S // tq
    n_cmp = S // CMP_STRIDE
    n_blk = S // SLC_BLOCK
    G = NSA_KV_HEADS
    ovl = jnp.asarray(_overlap_t(n_cmp, n_blk), BF16)
    kv_spec = lambda c: pl.BlockSpec((None, S, HEAD_DIM), lambda b, g, i: (2 * c + g, b, 0))
    return pl.pallas_call(
        functools.partial(_attn_kernel, tq=tq),
        grid=(B, G, nq),
        in_specs=[
            pl.BlockSpec((NSA_GROUP, tq, HEAD_DIM), lambda b, g, i: (g, b * nq + i, 0)),
            pl.BlockSpec((None, None, n_cmp, HEAD_DIM), lambda b, g, i: (b, g, 0, 0)),
            pl.BlockSpec((None, None, n_cmp, HEAD_DIM), lambda b, g, i: (b, G + g, 0, 0)),
            kv_spec(2), kv_spec(3), kv_spec(4), kv_spec(5),
            pl.BlockSpec((None, tq, LANES), lambda b, g, i: (g, b * nq + i, 0)),
            pl.BlockSpec((n_blk, n_cmp), lambda b, g, i: (0, 0)),
        ],
        out_specs=pl.BlockSpec((tq, NSA_GROUP * HEAD_DIM), lambda b, g, i: (b * nq + i, g)),
        out_shape=jax.ShapeDtypeStruct((T, NSA_WIDTH), F32),
        compiler_params=pltpu.CompilerParams(
            dimension_semantics=("parallel", "parallel", "arbitrary"),
            vmem_limit_bytes=VMEM_LIMIT),
        name="nsa_attn",
    )(q, kcvc, kcvc, kv, kv, kv, kv, gates, ovl)


def _out_kernel(ylru_ref, ynsa_ref, h_ref, gl_ref, gn_ref, w_ref, g_ref, b_ref, o_ref):
    yl = _rms_norm(ylru_ref[...], gl_ref[...]).astype(BF16)
    yn = _rms_norm(ynsa_ref[...], gn_ref[...]).astype(BF16)
    mix = _dot(yl, w_ref[0:LRU_WIDTH, :]) + _dot(yn, w_ref[LRU_WIDTH:, :])
    o_ref[...] = _layer_norm(ALPHA * h_ref[...] + mix, g_ref[...], b_ref[...])


def _out_proj(y_lru, y_nsa, h, gn_lru, gn_nsa, w_out, g, b, *, tm=512):
    T, D = h.shape
    row = lambda n: pl.BlockSpec((1, n), lambda i: (0, 0))
    return pl.pallas_call(
        _out_kernel,
        grid=(T // tm,),
        in_specs=[
            pl.BlockSpec((tm, LRU_WIDTH), lambda i: (i, 0)),
            pl.BlockSpec((tm, NSA_WIDTH), lambda i: (i, 0)),
            pl.BlockSpec((tm, D), lambda i: (i, 0)),
            row(LRU_WIDTH), row(NSA_WIDTH),
            pl.BlockSpec((LRU_WIDTH + NSA_WIDTH, D), lambda i: (0, 0)),
            row(D), row(D),
        ],
        out_specs=pl.BlockSpec((tm, D), lambda i: (i, 0)),
        out_shape=jax.ShapeDtypeStruct((T, D), F32),
        compiler_params=pltpu.CompilerParams(
            dimension_semantics=("parallel",), vmem_limit_bytes=VMEM_LIMIT),
        name="out_proj",
    )(y_lru, y_nsa, h, gn_lru.reshape(1, -1), gn_nsa.reshape(1, -1), w_out.astype(BF16),
      g.reshape(1, D), b.reshape(1, D))


def kernel(x, ffn1_w_in, ffn1_w_out, ln1_g, ln1_b, mix_w_in, conv_w, conv_b, lru_w_a, lru_b_a,
           lru_w_x, lru_b_x, lru_lam, cmp_pe_k, cmp_w1_k, cmp_w2_k, cmp_pe_v, cmp_w1_v,
           cmp_w2_v, gn_lru, gn_nsa, mix_w_out, ln2_g, ln2_b, ffn2_w_in, ffn2_w_out, ln3_g,
           ln3_b):
    B, S, D = x.shape
    h = x.reshape(B * S, D)
    for l in range(DEPTH):
        h = _ffn_ln(h, ffn1_w_in[l], ffn1_w_out[l], ln1_g[l], ln1_b[l])
        lru, q, kv, gates = _proj(h, mix_w_in[l], S)
        y_lru = _lru(lru, conv_w[l], conv_b[l], lru_w_a[l], lru_b_a[l], lru_w_x[l], lru_b_x[l],
                     lru_lam[l], B, S)
        kcvc = _compress(kv, cmp_pe_k[l], cmp_w1_k[l], cmp_w2_k[l],
                         cmp_pe_v[l], cmp_w1_v[l], cmp_w2_v[l], B, S)
        y_nsa = _attention(q, kv, kcvc, gates, B, S)
        h = _out_proj(y_lru, y_nsa, h, gn_lru[l], gn_nsa[l], mix_w_out[l], ln2_g[l], ln2_b[l])
        h = _ffn_ln(h, ffn2_w_in[l], ffn2_w_out[l], ln3_g[l], ln3_b[l])
    return h.reshape(B, S, D)
```

```python
import functools
import math

import numpy as np
import jax
import jax.numpy as jnp
from jax import lax
from jax.experimental import pallas as pl
from jax.experimental.pallas import tpu as pltpu

F32 = jnp.float32
BF16 = jnp.bfloat16

D_MODEL = 1024
LRU_WIDTH = 512
LRU_HEADS = 8
LRU_BLOCK = 64
CONV_WIDTH = 4
LRU_C = 8.0
NSA_Q_HEADS = 8
NSA_KV_HEADS = 2
NSA_GROUP = 4
HEAD_DIM = 64
NSA_WIDTH = 512
KV_WIDTH = 128
CMP_BLOCK = 32
CMP_STRIDE = 16
CMP_HIDDEN = 256
SLC_BLOCK = 64
SLC_SHIFT = 6
SLC_TOP_N = 16
N_LOCAL_BLOCKS = 2
WINDOW = 512
ROPE_THETA = 10000.0
D_FF = 2816
DEPTH = 1
ALPHA = (2.0 * DEPTH) ** 0.25
LN_EPS = 1e-5
RMS_EPS = 1e-6
NEG = -1e30

LANES = 128
VMEM_LIMIT = 48 * 1024 * 1024


def _dot(a, b):
    return jnp.dot(a, b, preferred_element_type=F32)


def _dot_nt(a, b):
    return lax.dot_general(a, b, (((1,), (1,)), ((), ())), preferred_element_type=F32)


def _layer_norm(y, g, b):
    mu = jnp.mean(y, axis=-1, keepdims=True)
    d = y - mu
    var = jnp.mean(d * d, axis=-1, keepdims=True)
    return d * lax.rsqrt(var + LN_EPS) * g + b


def _rms_norm(y, g):
    return y * lax.rsqrt(jnp.mean(y * y, axis=-1, keepdims=True) + RMS_EPS) * g


def _silu(x):
    return x * jax.nn.sigmoid(x)


def _ffn_ln_kernel(x_ref, wg_ref, wu_ref, wo_ref, g_ref, b_ref, o_ref, acc_ref):
    j = pl.program_id(1)

    @pl.when(j == 0)
    def _():
        acc_ref[...] = jnp.zeros_like(acc_ref)

    xb = x_ref[...].astype(BF16)
    gate = _dot(xb, wg_ref[...])
    up = _dot(xb, wu_ref[...])
    act = (_silu(gate) * up).astype(BF16)
    acc_ref[...] += _dot(act, wo_ref[...])

    @pl.when(j == pl.num_programs(1) - 1)
    def _():
        y = ALPHA * x_ref[...] + 0.5 * acc_ref[...]
        o_ref[...] = _layer_norm(y, g_ref[...], b_ref[...])


def _ffn_ln(x, w_in, w_out, g, b, *, tm=512, tf=1408):
    T, D = x.shape
    wg = w_in[:, :D_FF].astype(BF16)
    wu = w_in[:, D_FF:].astype(BF16)
    wo = w_out.astype(BF16)
    return pl.pallas_call(
        _ffn_ln_kernel,
        grid=(T // tm, D_FF // tf),
        in_specs=[
            pl.BlockSpec((tm, D), lambda i, j: (i, 0)),
            pl.BlockSpec((D, tf), lambda i, j: (0, j)),
            pl.BlockSpec((D, tf), lambda i, j: (0, j)),
            pl.BlockSpec((tf, D), lambda i, j: (j, 0)),
            pl.BlockSpec((1, D), lambda i, j: (0, 0)),
            pl.BlockSpec((1, D), lambda i, j: (0, 0)),
        ],
        out_specs=pl.BlockSpec((tm, D), lambda i, j: (i, 0)),
        out_shape=jax.ShapeDtypeStruct((T, D), F32),
        scratch_shapes=[pltpu.VMEM((tm, D), F32)],
        compiler_params=pltpu.CompilerParams(
            dimension_semantics=("parallel", "arbitrary"), vmem_limit_bytes=VMEM_LIMIT),
        name="ffn_ln",
    )(x, wg, wu, wo, g.reshape(1, D), b.reshape(1, D))


PROJ_COLS = 20 * LANES
Q_COL0 = 2 * LRU_WIDTH
KV_COL0 = Q_COL0 + NSA_WIDTH
GATE_COL0 = KV_COL0 + 6 * KV_WIDTH


def _rope_chunk(xc, cos, sin_signed):
    lane = lax.broadcasted_iota(jnp.int32, xc.shape, 1)
    first = (lane & (HEAD_DIM - 1)) < (HEAD_DIM // 2)
    partner = jnp.where(first, pltpu.roll(xc, LANES - HEAD_DIM // 2, axis=1),
                        pltpu.roll(xc, HEAD_DIM // 2, axis=1))
    return xc * cos + partner * sin_signed


def _proj_kernel(h_ref, w_ref, cos_ref, sin_ref, lru_ref, q_ref, cmp_ref, kv_ref, gate_ref, *,
                 seq_tiles):
    tm = h_ref.shape[0]
    hb = h_ref[...].astype(BF16)
    p = _dot(hb, w_ref[...])
    lru_ref[...] = p[:, :Q_COL0]
    cos = cos_ref[...]
    sin = sin_ref[...]
    scale = HEAD_DIM ** -0.5
    lane = lax.broadcasted_iota(jnp.int32, (tm, LANES), 1)
    low = lane < HEAD_DIM
    pos = (pl.program_id(0) % seq_tiles) * tm + lax.broadcasted_iota(jnp.int32, (tm, LANES), 0)
    ext_blk = jnp.where(lane - HEAD_DIM == (pos >> SLC_SHIFT), NEG, 0.0)
    ext_one = jnp.where(lane == HEAD_DIM, 1.0, 0.0)
    ext_zero = jnp.zeros((tm, LANES), F32)

    def heads(xc, ext):
        return (jnp.where(low, xc, ext).astype(BF16),
                jnp.where(low, pltpu.roll(xc, HEAD_DIM, axis=1), ext).astype(BF16))

    for c in range(NSA_WIDTH // LANES):
        xc = p[:, Q_COL0 + c * LANES:Q_COL0 + (c + 1) * LANES]
        q_ref[2 * c], q_ref[2 * c + 1] = heads(_rope_chunk(xc, cos, sin) * scale, ext_zero)
    for c in range(2):
        xc = p[:, KV_COL0 + c * LANES:KV_COL0 + (c + 1) * LANES]
        if c == 0:
            xc = _rope_chunk(xc, cos, sin)
        xc = xc.astype(BF16)
        cmp_ref[2 * c] = xc[:, :HEAD_DIM]
        cmp_ref[2 * c + 1] = xc[:, HEAD_DIM:]
    for c, ext in enumerate((ext_blk, ext_one, ext_zero, ext_one)):
        xc = p[:, KV_COL0 + (c + 2) * LANES:KV_COL0 + (c + 3) * LANES]
        if c % 2 == 0:
            xc = _rope_chunk(xc, cos, sin)
        kv_ref[2 * c], kv_ref[2 * c + 1] = heads(xc, ext)
    for g in range(NSA_KV_HEADS):
        gate_ref[g] = jax.nn.sigmoid(p[:, GATE_COL0 + g * LANES:GATE_COL0 + (g + 1) * LANES])


def _proj(h, w_in, S, *, tm=512):
    T, D = h.shape
    n_gate = 3 * NSA_GROUP
    w = jnp.zeros((D, PROJ_COLS), F32)
    w = w.at[:, :GATE_COL0].set(w_in[:, :GATE_COL0])
    for g in range(NSA_KV_HEADS):
        w = w.at[:, GATE_COL0 + g * LANES:GATE_COL0 + g * LANES + n_gate].set(
            w_in[:, GATE_COL0 + g * n_gate:GATE_COL0 + (g + 1) * n_gate])
    w = w.astype(BF16)

    half = HEAD_DIM // 2
    inv = ROPE_THETA ** (-jnp.arange(half, dtype=F32) / half)
    ang = jnp.arange(S, dtype=F32)[:, None] * inv[None, :]
    cos = jnp.cos(ang)
    sin = jnp.sin(ang)
    cos_t = jnp.concatenate([cos, cos, cos, cos], axis=1)
    sin_t = jnp.concatenate([-sin, sin, -sin, sin], axis=1)

    nS = S // tm
    n_cmp_in = 2 * NSA_KV_HEADS
    n_kv = 4 * NSA_KV_HEADS
    return pl.pallas_call(
        functools.partial(_proj_kernel, seq_tiles=nS),
        grid=(T // tm,),
        in_specs=[
            pl.BlockSpec((tm, D), lambda i: (i, 0)),
            pl.BlockSpec((D, PROJ_COLS), lambda i: (0, 0)),
            pl.BlockSpec((tm, LANES), lambda i: (i % nS, 0)),
            pl.BlockSpec((tm, LANES), lambda i: (i % nS, 0)),
        ],
        out_specs=[
            pl.BlockSpec((tm, Q_COL0), lambda i: (i, 0)),
            pl.BlockSpec((NSA_Q_HEADS, tm, LANES), lambda i: (0, i, 0)),
            pl.BlockSpec((n_cmp_in, tm, HEAD_DIM), lambda i: (0, i, 0)),
            pl.BlockSpec((n_kv, tm, LANES), lambda i: (0, i, 0)),
            pl.BlockSpec((NSA_KV_HEADS, tm, LANES), lambda i: (0, i, 0)),
        ],
        out_shape=[
            jax.ShapeDtypeStruct((T, Q_COL0), F32),
            jax.ShapeDtypeStruct((NSA_Q_HEADS, T, LANES), BF16),
            jax.ShapeDtypeStruct((n_cmp_in, T, HEAD_DIM), BF16),
            jax.ShapeDtypeStruct((n_kv, T, LANES), BF16),
            jax.ShapeDtypeStruct((NSA_KV_HEADS, T, LANES), F32),
        ],
        compiler_params=pltpu.CompilerParams(
            dimension_semantics=("parallel",), vmem_limit_bytes=VMEM_LIMIT),
        name="proj",
    )(h, w, cos_t, sin_t)


SCAN_PAD = 1024


def _lru_kernel(x_ref, gate_ref, cw_ref, cb_ref, wa_ref, ba_ref, wx_ref, bx_ref, lam_ref,
                y_ref, a_scr, b_scr):
    S = x_ref.shape[0]
    x = x_ref[...]
    row = lax.broadcasted_iota(jnp.int32, x.shape, 0)

    a_scr[0:SCAN_PAD, :] = jnp.zeros((SCAN_PAD, LANES), F32)
    a_scr[SCAN_PAD:, :] = x
    xc = cb_ref[...] + x * cw_ref[CONV_WIDTH - 1:CONV_WIDTH, :]
    for d in range(1, CONV_WIDTH):
        xs = a_scr[SCAN_PAD - d:SCAN_PAD - d + S, :]
        xc = xc + xs * cw_ref[CONV_WIDTH - 1 - d:CONV_WIDTH - d, :]

    xb = xc.astype(BF16)
    r = jax.nn.sigmoid(_dot(xb, wa_ref[0]) + ba_ref[...])
    ig = jax.nn.sigmoid(_dot(xb, wx_ref[0]) + bx_ref[...])
    neg_lam = -lam_ref[...]
    softplus = jnp.maximum(neg_lam, 0.0) + jnp.log1p(jnp.exp(-jnp.abs(neg_lam)))
    log_a = (-LRU_C) * r * softplus
    a = jnp.exp(log_a)
    mult = jnp.sqrt(jnp.tanh(-log_a) * (a * a + 1.0))
    mult = jnp.where(row == 0, 1.0, mult)
    b = mult * (ig * xc)

    a_scr[0:SCAN_PAD, :] = jnp.ones((SCAN_PAD, LANES), F32)
    b_scr[0:SCAN_PAD, :] = jnp.zeros((SCAN_PAD, LANES), F32)
    d = 1
    while d < S:
        a_scr[SCAN_PAD:, :] = a
        b_scr[SCAN_PAD:, :] = b
        a_sh = a_scr[SCAN_PAD - d:SCAN_PAD - d + S, :]
        b_sh = b_scr[SCAN_PAD - d:SCAN_PAD - d + S, :]
        b = a * b_sh + b
        a = a * a_sh
        d *= 2

    y_ref[...] = b * jax.nn.gelu(gate_ref[...])


def _lru(lru, conv_w, conv_b, w_a, b_a, w_x, b_x, lam, B, S):
    T = lru.shape[0]
    n_ch = LRU_WIDTH // LANES

    def blockdiag(w):
        w = w.reshape(n_ch, 2, LRU_BLOCK, LRU_BLOCK)
        z = jnp.zeros((n_ch, LRU_BLOCK, LRU_BLOCK), w.dtype)
        top = jnp.concatenate([w[:, 0], z], axis=2)
        bot = jnp.concatenate([z, w[:, 1]], axis=2)
        return jnp.concatenate([top, bot], axis=1).astype(BF16)

    vec = lambda v: v.reshape(1, LRU_WIDTH)
    chan = lambda rows: pl.BlockSpec((rows, LANES), lambda b, c: (0, c))
    return pl.pallas_call(
        _lru_kernel,
        grid=(B, n_ch),
        in_specs=[
            pl.BlockSpec((S, LANES), lambda b, c: (b, c)),
            pl.BlockSpec((S, LANES), lambda b, c: (b, n_ch + c)),
            chan(CONV_WIDTH), chan(1),
            pl.BlockSpec((1, LANES, LANES), lambda b, c: (c, 0, 0)), chan(1),
            pl.BlockSpec((1, LANES, LANES), lambda b, c: (c, 0, 0)), chan(1),
            chan(1),
        ],
        out_specs=pl.BlockSpec((S, LANES), lambda b, c: (b, c)),
        out_shape=jax.ShapeDtypeStruct((T, LRU_WIDTH), F32),
        scratch_shapes=[pltpu.VMEM((SCAN_PAD + S, LANES), F32),
                        pltpu.VMEM((SCAN_PAD + S, LANES), F32)],
        compiler_params=pltpu.CompilerParams(
            dimension_semantics=("parallel", "parallel"), vmem_limit_bytes=VMEM_LIMIT),
        name="lru",
    )(lru, lru, conv_w, vec(conv_b), blockdiag(w_a), vec(b_a.reshape(-1)),
      blockdiag(w_x), vec(b_x.reshape(-1)), vec(lam))


def _cmp_kernel(seg_ref, pek_ref, w1k_ref, w2k_ref, pev_ref, w1v_ref, w2v_ref, out_ref):
    n_seg = seg_ref.shape[2]
    half = CMP_STRIDE * HEAD_DIM
    for idx in range(2 * NSA_KV_HEADS):
        pe_ref, w1_ref, w2_ref = ((pek_ref, w1k_ref, w2k_ref) if idx < NSA_KV_HEADS
                                  else (pev_ref, w1v_ref, w2v_ref))
        seg = seg_ref[idx, 0]
        first = _dot(seg, w1_ref[0:half, :])
        second = _dot(seg, w1_ref[half:2 * half, :])
        bias = _dot(pe_ref[...], w1_ref[...])[0:1, :]
        hid = first + pltpu.roll(second, n_seg - 1, axis=0) + bias
        tok = _dot(_silu(hid).astype(BF16), w2_ref[...])
        out_ref[0, idx] = jnp.concatenate([tok, jnp.zeros_like(tok)], axis=1).astype(BF16)


def _compress(cmp_in, pe_k, w1_k, w2_k, pe_v, w1_v, w2_v, B, S):
    n_seg = S // CMP_STRIDE
    seg = cmp_in.reshape(2 * NSA_KV_HEADS, B, n_seg, CMP_STRIDE * HEAD_DIM)
    pe8 = lambda pe: jnp.broadcast_to(pe.reshape(1, -1), (8, CMP_BLOCK * HEAD_DIM)).astype(BF16)
    full = lambda shape: pl.BlockSpec(shape, lambda b: (0,) * len(shape))
    flat = CMP_BLOCK * HEAD_DIM
    return pl.pallas_call(
        _cmp_kernel,
        grid=(B,),
        in_specs=[
            pl.BlockSpec((4, 1, n_seg, CMP_STRIDE * HEAD_DIM), lambda b: (0, b, 0, 0)),
            full((8, flat)), full((flat, CMP_HIDDEN)), full((CMP_HIDDEN, HEAD_DIM)),
            full((8, flat)), full((flat, CMP_HIDDEN)), full((CMP_HIDDEN, HEAD_DIM)),
        ],
        out_specs=pl.BlockSpec((1, 4, n_seg, LANES), lambda b: (b, 0, 0, 0)),
        out_shape=jax.ShapeDtypeStruct((B, 4, n_seg, LANES), BF16),
        compiler_params=pltpu.CompilerParams(
            dimension_semantics=("parallel",), vmem_limit_bytes=VMEM_LIMIT),
        name="compress",
    )(seg, pe8(pe_k), w1_k.astype(BF16), w2_k.astype(BF16),
      pe8(pe_v), w1_v.astype(BF16), w2_v.astype(BF16))


def _add_bias(s, bias):
    tq, tk = bias.shape
    return (s.reshape(-1, tq, tk) + bias[None]).reshape(-1, tk)


def _softmax_pv(s, v, carry):
    m, acc = carry
    m_new = jnp.maximum(m, jnp.max(s, axis=-1, keepdims=True))
    alpha = jnp.exp(m - m_new)
    p = jnp.exp(s - m_new)
    acc = alpha * acc + _dot(p.astype(BF16), v)
    return m_new, acc


def _attn_kernel(q_ref, kc_ref, vc_ref, ks_ref, vs_ref, kw_ref, vw_ref, gate_ref, ovl_ref,
                 o_ref, *, tq):
    R = NSA_GROUP
    tk = tq
    i = pl.program_id(2)
    t0 = i * tq
    q = q_ref[...].reshape(R * tq, LANES)
    n_cmp = kc_ref.shape[0]

    tpos_c = t0 + lax.broadcasted_iota(jnp.int32, (tq, n_cmp), 0)
    cend = lax.broadcasted_iota(jnp.int32, (tq, n_cmp), 1) * CMP_STRIDE + (CMP_BLOCK - 1)
    cmask = cend <= tpos_c
    s = _dot_nt(q, kc_ref[...]).reshape(R, tq, n_cmp)
    s = jnp.where(cmask[None], s, NEG)
    e = jnp.exp(s - jnp.max(s, axis=-1, keepdims=True))
    p_cmp = e / jnp.sum(e, axis=-1, keepdims=True)
    any_cmp = tpos_c >= (CMP_BLOCK - 1)
    p_cmp = jnp.where(any_cmp[None], p_cmp, 0.0)
    o_cmp = _dot(p_cmp.reshape(R * tq, n_cmp).astype(BF16), vc_ref[...])

    psum = p_cmp[0] + p_cmp[1] + p_cmp[2] + p_cmp[3]
    p_hi = psum.astype(BF16)
    p_lo = (psum - p_hi.astype(F32)).astype(BF16)
    ovl = ovl_ref[...]
    imp = _dot_nt(ovl, p_hi) + _dot_nt(ovl, p_lo)
    n_blk = imp.shape[0]
    jblk = lax.broadcasted_iota(jnp.int32, (n_blk, tq), 0)
    tpos_b = t0 + lax.broadcasted_iota(jnp.int32, (n_blk, tq), 1)
    blk_valid = jblk * SLC_BLOCK <= tpos_b
    back = (tpos_b >> SLC_SHIFT) - jblk
    forced = (jblk == 0) | ((back >= 0) & (back < N_LOCAL_BLOCKS))
    score = jnp.where(blk_valid, jnp.where(forced, jnp.inf, imp), -jnp.inf)
    rank = jnp.zeros((n_blk, tq), jnp.int32)
    for kb in range(n_blk):
        sk = score[kb:kb + 1, :]
        ahead = (sk > score) | ((sk == score) & (jblk > kb))
        rank = rank + ahead.astype(jnp.int32)
    unsel_t = 1.0 - (blk_valid & (rank < SLC_TOP_N)).astype(F32)
    unsel_t = jnp.concatenate([jnp.zeros((HEAD_DIM, tq), F32), unsel_t,
                               jnp.zeros((LANES - HEAD_DIM - n_blk, tq), F32)], axis=0)
    unsel = unsel_t.T.astype(BF16)
    q_sel = (q.reshape(R, tq, LANES) + unsel[None]).reshape(R * tq, LANES)

    span = WINDOW + tq
    kw0 = pl.multiple_of(jnp.maximum(t0 - WINDOW, 0), tk)
    diff = (t0 + lax.broadcasted_iota(jnp.int32, (tq, span), 0)
            - (kw0 + lax.broadcasted_iota(jnp.int32, (tq, span), 1)))
    win_bias = jnp.where((diff >= 0) & (diff < WINDOW), 0.0, NEG)
    s_w = _add_bias(_dot_nt(q, kw_ref[pl.ds(kw0, span), :]), win_bias)
    p_w = jnp.exp(s_w - jnp.max(s_w, axis=-1, keepdims=True))
    acc_w = _dot(p_w.astype(BF16), vw_ref[pl.ds(kw0, span), :])

    def slc_scores(kt):
        return _dot_nt(q_sel, ks_ref[pl.ds(pl.multiple_of(kt * tk, tk), tk), :])

    def slc_body(kt, carry):
        s_cur, m, acc = carry
        s_next = slc_scores(kt + 1)
        m, acc = _softmax_pv(s_cur, vs_ref[pl.ds(pl.multiple_of(kt * tk, tk), tk), :], (m, acc))
        return s_next, m, acc

    init = (slc_scores(0), jnp.full((R * tq, 1), NEG, F32), jnp.zeros((R * tq, LANES), F32))
    s_diag, m_s, acc_s = lax.fori_loop(0, i, slc_body, init)
    causal_bias = jnp.where(lax.broadcasted_iota(jnp.int32, (tq, tk), 1)
                            <= lax.broadcasted_iota(jnp.int32, (tq, tk), 0), 0.0, NEG)
    _, acc_s = _softmax_pv(_add_bias(s_diag, causal_bias),
                           vs_ref[pl.ds(pl.multiple_of(t0, tk), tk), :], (m_s, acc_s))

    o_slc = acc_s / acc_s[:, HEAD_DIM:HEAD_DIM + 1]
    o_win = acc_w / acc_w[:, HEAD_DIM:HEAD_DIM + 1]
    gt = gate_ref[...]
    outs = []
    for r in range(R):
        rows = slice(r * tq, (r + 1) * tq)
        o_r = (gt[:, 3 * r:3 * r + 1] * o_cmp[rows]
               + gt[:, 3 * r + 1:3 * r + 2] * o_slc[rows]
               + gt[:, 3 * r + 2:3 * r + 3] * o_win[rows])
        outs.append(o_r[:, :HEAD_DIM])
    o_ref[...] = jnp.concatenate(outs, axis=1)


def _overlap_t(n_cmp_pad, n_blk):
    cs = np.arange(n_cmp_pad) * CMP_STRIDE
    ce = cs + CMP_BLOCK - 1
    ss = np.arange(n_blk) * SLC_BLOCK
    se = ss + SLC_BLOCK - 1
    return ((cs[None, :] <= se[:, None]) & (ce[None, :] >= ss[:, None])).astype(np.float32)


def _attention(q, kv, kcvc, gates, B, S, *, tq=256):
    T = q.shape[1]
    nq = S // tq
    n_cmp = S // CMP_STRIDE
    n_blk = S // SLC_BLOCK
    G = NSA_KV_HEADS
    ovl = jnp.asarray(_overlap_t(n_cmp, n_blk), BF16)
    assert WINDOW % tq == 0 and WINDOW + tq <= S
    kv_spec = lambda c: pl.BlockSpec((None, S, LANES), lambda b, g, i: (G * c + g, b, 0))
    return pl.pallas_call(
        functools.partial(_attn_kernel, tq=tq),
        grid=(B, G, nq),
        in_specs=[
            pl.BlockSpec((NSA_GROUP, tq, LANES), lambda b, g, i: (g, b * nq + i, 0)),
            pl.BlockSpec((None, None, n_cmp, LANES), lambda b, g, i: (b, g, 0, 0)),
            pl.BlockSpec((None, None, n_cmp, LANES), lambda b, g, i: (b, G + g, 0, 0)),
            kv_spec(0), kv_spec(1), kv_spec(2), kv_spec(3),
            pl.BlockSpec((None, tq, LANES), lambda b, g, i: (g, b * nq + i, 0)),
            pl.BlockSpec((n_blk, n_cmp), lambda b, g, i: (0, 0)),
        ],
        out_specs=pl.BlockSpec((tq, NSA_GROUP * HEAD_DIM), lambda b, g, i: (b * nq + i, g)),
        out_shape=jax.ShapeDtypeStruct((T, NSA_WIDTH), F32),
        compiler_params=pltpu.CompilerParams(
            dimension_semantics=("parallel", "parallel", "arbitrary"),
            vmem_limit_bytes=VMEM_LIMIT),
        name="nsa_attn",
    )(q, kcvc, kcvc, kv, kv, kv, kv, gates, ovl)


def _out_kernel(ylru_ref, ynsa_ref, h_ref, gl_ref, gn_ref, w_ref, g_ref, b_ref, o_ref):
    yl = _rms_norm(ylru_ref[...], gl_ref[...]).astype(BF16)
    yn = _rms_norm(ynsa_ref[...], gn_ref[...]).astype(BF16)
    mix = _dot(yl, w_ref[0:LRU_WIDTH, :]) + _dot(yn, w_ref[LRU_WIDTH:, :])
    o_ref[...] = _layer_norm(ALPHA * h_ref[...] + mix, g_ref[...], b_ref[...])


def _out_proj(y_lru, y_nsa, h, gn_lru, gn_nsa, w_out, g, b, *, tm=512):
    T, D = h.shape
    row = lambda n: pl.BlockSpec((1, n), lambda i: (0, 0))
    return pl.pallas_call(
        _out_kernel,
        grid=(T // tm,),
        in_specs=[
            pl.BlockSpec((tm, LRU_WIDTH), lambda i: (i, 0)),
            pl.BlockSpec((tm, NSA_WIDTH), lambda i: (i, 0)),
            pl.BlockSpec((tm, D), lambda i: (i, 0)),
            row(LRU_WIDTH), row(NSA_WIDTH),
            pl.BlockSpec((LRU_WIDTH + NSA_WIDTH, D), lambda i: (0, 0)),
            row(D), row(D),
        ],
        out_specs=pl.BlockSpec((tm, D), lambda i: (i, 0)),
        out_shape=jax.ShapeDtypeStruct((T, D), F32),
        compiler_params=pltpu.CompilerParams(
            dimension_semantics=("parallel",), vmem_limit_bytes=VMEM_LIMIT),
        name="out_proj",
    )(y_lru, y_nsa, h, gn_lru.reshape(1, -1), gn_nsa.reshape(1, -1), w_out.astype(BF16),
      g.reshape(1, D), b.reshape(1, D))


def kernel(x, ffn1_w_in, ffn1_w_out, ln1_g, ln1_b, mix_w_in, conv_w, conv_b, lru_w_a, lru_b_a,
           lru_w_x, lru_b_x, lru_lam, cmp_pe_k, cmp_w1_k, cmp_w2_k, cmp_pe_v, cmp_w1_v,
           cmp_w2_v, gn_lru, gn_nsa, mix_w_out, ln2_g, ln2_b, ffn2_w_in, ffn2_w_out, ln3_g,
           ln3_b):
    B, S, D = x.shape
    h = x.reshape(B * S, D)
    for l in range(DEPTH):
        h = _ffn_ln(h, ffn1_w_in[l], ffn1_w_out[l], ln1_g[l], ln1_b[l])
        lru, q, cmp_in, kv, gates = _proj(h, mix_w_in[l], S)
        y_lru = _lru(lru, conv_w[l], conv_b[l], lru_w_a[l], lru_b_a[l], lru_w_x[l], lru_b_x[l],
                     lru_lam[l], B, S)
        kcvc = _compress(cmp_in, cmp_pe_k[l], cmp_w1_k[l], cmp_w2_k[l],
                         cmp_pe_v[l], cmp_w1_v[l], cmp_w2_v[l], B, S)
        y_nsa = _attention(q, kv, kcvc, gates, B, S)
        h = _out_proj(y_lru, y_nsa, h, gn_lru[l], gn_nsa[l], mix_w_out[l], ln2_g[l], ln2_b[l])
        h = _ffn_ln(h, ffn2_w_in[l], ffn2_w_out[l], ln3_g[l], ln3_b[l])
    return h.reshape(B, S, D)
```

```python
import functools
import math

import numpy as np
import jax
import jax.numpy as jnp
from jax import lax
from jax.experimental import pallas as pl
from jax.experimental.pallas import tpu as pltpu

F32 = jnp.float32
BF16 = jnp.bfloat16

D_MODEL = 1024
LRU_WIDTH = 512
LRU_HEADS = 8
LRU_BLOCK = 64
CONV_WIDTH = 4
LRU_C = 8.0
NSA_Q_HEADS = 8
NSA_KV_HEADS = 2
NSA_GROUP = 4
HEAD_DIM = 64
NSA_WIDTH = 512
KV_WIDTH = 128
CMP_BLOCK = 32
CMP_STRIDE = 16
CMP_HIDDEN = 256
SLC_BLOCK = 64
SLC_SHIFT = 6
SLC_TOP_N = 16
N_LOCAL_BLOCKS = 2
WINDOW = 512
ROPE_THETA = 10000.0
D_FF = 2816
DEPTH = 1
ALPHA = (2.0 * DEPTH) ** 0.25
LN_EPS = 1e-5
RMS_EPS = 1e-6
NEG = -1e30

LANES = 128
VMEM_LIMIT = 48 * 1024 * 1024


def _dot(a, b):
    return jnp.dot(a, b, preferred_element_type=F32)


def _dot_nt(a, b):
    return lax.dot_general(a, b, (((1,), (1,)), ((), ())), preferred_element_type=F32)


def _layer_norm(y, g, b):
    mu = jnp.mean(y, axis=-1, keepdims=True)
    d = y - mu
    var = jnp.mean(d * d, axis=-1, keepdims=True)
    return d * lax.rsqrt(var + LN_EPS) * g + b


def _rms_norm(y, g):
    return y * lax.rsqrt(jnp.mean(y * y, axis=-1, keepdims=True) + RMS_EPS) * g


def _silu(x):
    return x * jax.nn.sigmoid(x)


def _ffn_ln_kernel(x_ref, wg_ref, wu_ref, wo_ref, g_ref, b_ref, o_ref):
    x = x_ref[...]
    xb = x.astype(BF16)
    gate = _dot(xb, wg_ref[...])
    up = _dot(xb, wu_ref[...])
    act = (_silu(gate) * up).astype(BF16)
    y = ALPHA * x + 0.5 * _dot(act, wo_ref[...])
    o_ref[...] = _layer_norm(y, g_ref[...], b_ref[...])


def _resident(shape):
    return pl.BlockSpec(shape, lambda i: (0,) * len(shape), pipeline_mode=pl.Buffered(1))


def _ffn_ln(x, w_in, w_out, g, b, *, tm=512):
    T, D = x.shape
    wg = w_in[:, :D_FF].astype(BF16)
    wu = w_in[:, D_FF:].astype(BF16)
    wo = w_out.astype(BF16)
    return pl.pallas_call(
        _ffn_ln_kernel,
        grid=(T // tm,),
        in_specs=[
            pl.BlockSpec((tm, D), lambda i: (i, 0)),
            _resident((D, D_FF)), _resident((D, D_FF)), _resident((D_FF, D)),
            _resident((1, D)), _resident((1, D)),
        ],
        out_specs=pl.BlockSpec((tm, D), lambda i: (i, 0)),
        out_shape=jax.ShapeDtypeStruct((T, D), F32),
        compiler_params=pltpu.CompilerParams(
            dimension_semantics=("parallel",), vmem_limit_bytes=VMEM_LIMIT),
        name="ffn_ln",
    )(x, wg, wu, wo, g.reshape(1, D), b.reshape(1, D))


PROJ_COLS = 20 * LANES
Q_COL0 = 2 * LRU_WIDTH
KV_COL0 = Q_COL0 + NSA_WIDTH
GATE_COL0 = KV_COL0 + 6 * KV_WIDTH


def _rope_chunk(xc, cos, sin_signed):
    lane = lax.broadcasted_iota(jnp.int32, xc.shape, 1)
    first = (lane & (HEAD_DIM - 1)) < (HEAD_DIM // 2)
    partner = jnp.where(first, pltpu.roll(xc, LANES - HEAD_DIM // 2, axis=1),
                        pltpu.roll(xc, HEAD_DIM // 2, axis=1))
    return xc * cos + partner * sin_signed


def _proj_kernel(h_ref, w_ref, cos_ref, sin_ref, lru_ref, q_ref, cmp_ref, kv_ref, gate_ref, *,
                 seq_tiles):
    tm = h_ref.shape[0]
    hb = h_ref[...].astype(BF16)
    p = _dot(hb, w_ref[...])
    lru_ref[...] = p[:, :Q_COL0]
    cos = cos_ref[...]
    sin = sin_ref[...]
    scale = HEAD_DIM ** -0.5
    lane = lax.broadcasted_iota(jnp.int32, (tm, LANES), 1)
    low = lane < HEAD_DIM
    pos = (pl.program_id(0) % seq_tiles) * tm + lax.broadcasted_iota(jnp.int32, (tm, LANES), 0)
    ext_blk = jnp.where(lane - HEAD_DIM == (pos >> SLC_SHIFT), NEG, 0.0)
    ext_one = jnp.where(lane == HEAD_DIM, 1.0, 0.0)
    ext_zero = jnp.zeros((tm, LANES), F32)

    def heads(xc, ext):
        return (jnp.where(low, xc, ext).astype(BF16),
                jnp.where(low, pltpu.roll(xc, HEAD_DIM, axis=1), ext).astype(BF16))

    for c in range(NSA_WIDTH // LANES):
        xc = p[:, Q_COL0 + c * LANES:Q_COL0 + (c + 1) * LANES]
        q_ref[2 * c], q_ref[2 * c + 1] = heads(_rope_chunk(xc, cos, sin) * scale, ext_zero)
    for c in range(2):
        xc = p[:, KV_COL0 + c * LANES:KV_COL0 + (c + 1) * LANES]
        if c == 0:
            xc = _rope_chunk(xc, cos, sin)
        xc = xc.astype(BF16)
        cmp_ref[2 * c] = xc[:, :HEAD_DIM]
        cmp_ref[2 * c + 1] = xc[:, HEAD_DIM:]
    for c, ext in enumerate((ext_blk, ext_one, ext_zero, ext_one)):
        xc = p[:, KV_COL0 + (c + 2) * LANES:KV_COL0 + (c + 3) * LANES]
        if c % 2 == 0:
            xc = _rope_chunk(xc, cos, sin)
        kv_ref[2 * c], kv_ref[2 * c + 1] = heads(xc, ext)
    for g in range(NSA_KV_HEADS):
        gate_ref[g] = jax.nn.sigmoid(p[:, GATE_COL0 + g * LANES:GATE_COL0 + (g + 1) * LANES])


def _proj(h, w_in, S, *, tm=512):
    T, D = h.shape
    n_gate = 3 * NSA_GROUP
    w = jnp.zeros((D, PROJ_COLS), F32)
    w = w.at[:, :GATE_COL0].set(w_in[:, :GATE_COL0])
    for g in range(NSA_KV_HEADS):
        w = w.at[:, GATE_COL0 + g * LANES:GATE_COL0 + g * LANES + n_gate].set(
            w_in[:, GATE_COL0 + g * n_gate:GATE_COL0 + (g + 1) * n_gate])
    w = w.astype(BF16)

    half = HEAD_DIM // 2
    inv = ROPE_THETA ** (-jnp.arange(half, dtype=F32) / half)
    ang = jnp.arange(S, dtype=F32)[:, None] * inv[None, :]
    cos = jnp.cos(ang)
    sin = jnp.sin(ang)
    cos_t = jnp.concatenate([cos, cos, cos, cos], axis=1)
    sin_t = jnp.concatenate([-sin, sin, -sin, sin], axis=1)

    nS = S // tm
    n_cmp_in = 2 * NSA_KV_HEADS
    n_kv = 4 * NSA_KV_HEADS
    return pl.pallas_call(
        functools.partial(_proj_kernel, seq_tiles=nS),
        grid=(T // tm,),
        in_specs=[
            pl.BlockSpec((tm, D), lambda i: (i, 0)),
            pl.BlockSpec((D, PROJ_COLS), lambda i: (0, 0)),
            pl.BlockSpec((tm, LANES), lambda i: (i % nS, 0)),
            pl.BlockSpec((tm, LANES), lambda i: (i % nS, 0)),
        ],
        out_specs=[
            pl.BlockSpec((tm, Q_COL0), lambda i: (i, 0)),
            pl.BlockSpec((NSA_Q_HEADS, tm, LANES), lambda i: (0, i, 0)),
            pl.BlockSpec((n_cmp_in, tm, HEAD_DIM), lambda i: (0, i, 0)),
            pl.BlockSpec((n_kv, tm, LANES), lambda i: (0, i, 0)),
            pl.BlockSpec((NSA_KV_HEADS, tm, LANES), lambda i: (0, i, 0)),
        ],
        out_shape=[
            jax.ShapeDtypeStruct((T, Q_COL0), F32),
            jax.ShapeDtypeStruct((NSA_Q_HEADS, T, LANES), BF16),
            jax.ShapeDtypeStruct((n_cmp_in, T, HEAD_DIM), BF16),
            jax.ShapeDtypeStruct((n_kv, T, LANES), BF16),
            jax.ShapeDtypeStruct((NSA_KV_HEADS, T, LANES), F32),
        ],
        compiler_params=pltpu.CompilerParams(
            dimension_semantics=("parallel",), vmem_limit_bytes=VMEM_LIMIT),
        name="proj",
    )(h, w, cos_t, sin_t)


SCAN_PAD = 1024


def _lru_kernel(x_ref, gate_ref, cw_ref, cb_ref, wa_ref, ba_ref, wx_ref, bx_ref, lam_ref,
                y_ref, a_scr, b_scr):
    S = x_ref.shape[0]
    x = x_ref[...]
    row = lax.broadcasted_iota(jnp.int32, x.shape, 0)

    a_scr[0:SCAN_PAD, :] = jnp.zeros((SCAN_PAD, LANES), F32)
    a_scr[SCAN_PAD:, :] = x
    xc = cb_ref[...] + x * cw_ref[CONV_WIDTH - 1:CONV_WIDTH, :]
    for d in range(1, CONV_WIDTH):
        xs = a_scr[SCAN_PAD - d:SCAN_PAD - d + S, :]
        xc = xc + xs * cw_ref[CONV_WIDTH - 1 - d:CONV_WIDTH - d, :]

    xb = xc.astype(BF16)
    r = jax.nn.sigmoid(_dot(xb, wa_ref[0]) + ba_ref[...])
    ig = jax.nn.sigmoid(_dot(xb, wx_ref[0]) + bx_ref[...])
    neg_lam = -lam_ref[...]
    softplus = jnp.maximum(neg_lam, 0.0) + jnp.log1p(jnp.exp(-jnp.abs(neg_lam)))
    log_a = (-LRU_C) * r * softplus
    a = jnp.exp(log_a)
    mult = jnp.sqrt(jnp.tanh(-log_a) * (a * a + 1.0))
    mult = jnp.where(row == 0, 1.0, mult)
    b = mult * (ig * xc)

    a_scr[0:SCAN_PAD, :] = jnp.ones((SCAN_PAD, LANES), F32)
    b_scr[0:SCAN_PAD, :] = jnp.zeros((SCAN_PAD, LANES), F32)
    d = 1
    while d < S:
        a_scr[SCAN_PAD:, :] = a
        b_scr[SCAN_PAD:, :] = b
        a_sh = a_scr[SCAN_PAD - d:SCAN_PAD - d + S, :]
        b_sh = b_scr[SCAN_PAD - d:SCAN_PAD - d + S, :]
        b = a * b_sh + b
        a = a * a_sh
        d *= 2

    y_ref[...] = b * jax.nn.gelu(gate_ref[...])


def _lru(lru, conv_w, conv_b, w_a, b_a, w_x, b_x, lam, B, S):
    T = lru.shape[0]
    n_ch = LRU_WIDTH // LANES

    def blockdiag(w):
        w = w.reshape(n_ch, 2, LRU_BLOCK, LRU_BLOCK)
        z = jnp.zeros((n_ch, LRU_BLOCK, LRU_BLOCK), w.dtype)
        top = jnp.concatenate([w[:, 0], z], axis=2)
        bot = jnp.concatenate([z, w[:, 1]], axis=2)
        return jnp.concatenate([top, bot], axis=1).astype(BF16)

    vec = lambda v: v.reshape(1, LRU_WIDTH)
    chan = lambda rows: pl.BlockSpec((rows, LANES), lambda b, c: (0, c))
    return pl.pallas_call(
        _lru_kernel,
        grid=(B, n_ch),
        in_specs=[
            pl.BlockSpec((S, LANES), lambda b, c: (b, c)),
            pl.BlockSpec((S, LANES), lambda b, c: (b, n_ch + c)),
            chan(CONV_WIDTH), chan(1),
            pl.BlockSpec((1, LANES, LANES), lambda b, c: (c, 0, 0)), chan(1),
            pl.BlockSpec((1, LANES, LANES), lambda b, c: (c, 0, 0)), chan(1),
            chan(1),
        ],
        out_specs=pl.BlockSpec((S, LANES), lambda b, c: (b, c)),
        out_shape=jax.ShapeDtypeStruct((T, LRU_WIDTH), F32),
        scratch_shapes=[pltpu.VMEM((SCAN_PAD + S, LANES), F32),
                        pltpu.VMEM((SCAN_PAD + S, LANES), F32)],
        compiler_params=pltpu.CompilerParams(
            dimension_semantics=("parallel", "parallel"), vmem_limit_bytes=VMEM_LIMIT),
        name="lru",
    )(lru, lru, conv_w, vec(conv_b), blockdiag(w_a), vec(b_a.reshape(-1)),
      blockdiag(w_x), vec(b_x.reshape(-1)), vec(lam))


def _cmp_kernel(seg_ref, pek_ref, w1k_ref, w2k_ref, pev_ref, w1v_ref, w2v_ref, out_ref):
    n_seg = seg_ref.shape[2]
    half = CMP_STRIDE * HEAD_DIM
    for idx in range(2 * NSA_KV_HEADS):
        pe_ref, w1_ref, w2_ref = ((pek_ref, w1k_ref, w2k_ref) if idx < NSA_KV_HEADS
                                  else (pev_ref, w1v_ref, w2v_ref))
        seg = seg_ref[idx, 0]
        first = _dot(seg, w1_ref[0:half, :])
        second = _dot(seg, w1_ref[half:2 * half, :])
        bias = _dot(pe_ref[...], w1_ref[...])[0:1, :]
        hid = first + pltpu.roll(second, n_seg - 1, axis=0) + bias
        tok = _dot(_silu(hid).astype(BF16), w2_ref[...])
        out_ref[0, idx] = jnp.concatenate([tok, jnp.zeros_like(tok)], axis=1).astype(BF16)


def _compress(cmp_in, pe_k, w1_k, w2_k, pe_v, w1_v, w2_v, B, S):
    n_seg = S // CMP_STRIDE
    seg = cmp_in.reshape(2 * NSA_KV_HEADS, B, n_seg, CMP_STRIDE * HEAD_DIM)
    pe8 = lambda pe: jnp.broadcast_to(pe.reshape(1, -1), (8, CMP_BLOCK * HEAD_DIM)).astype(BF16)
    full = lambda shape: pl.BlockSpec(shape, lambda b: (0,) * len(shape))
    flat = CMP_BLOCK * HEAD_DIM
    return pl.pallas_call(
        _cmp_kernel,
        grid=(B,),
        in_specs=[
            pl.BlockSpec((4, 1, n_seg, CMP_STRIDE * HEAD_DIM), lambda b: (0, b, 0, 0)),
            full((8, flat)), full((flat, CMP_HIDDEN)), full((CMP_HIDDEN, HEAD_DIM)),
            full((8, flat)), full((flat, CMP_HIDDEN)), full((CMP_HIDDEN, HEAD_DIM)),
        ],
        out_specs=pl.BlockSpec((1, 4, n_seg, LANES), lambda b: (b, 0, 0, 0)),
        out_shape=jax.ShapeDtypeStruct((B, 4, n_seg, LANES), BF16),
        compiler_params=pltpu.CompilerParams(
            dimension_semantics=("parallel",), vmem_limit_bytes=VMEM_LIMIT),
        name="compress",
    )(seg, pe8(pe_k), w1_k.astype(BF16), w2_k.astype(BF16),
      pe8(pe_v), w1_v.astype(BF16), w2_v.astype(BF16))


def _add_bias(s, bias):
    tq, tk = bias.shape
    return (s.reshape(-1, tq, tk) + bias[None]).reshape(-1, tk)


def _attn_kernel(q_ref, kc_ref, vc_ref, ks_ref, vs_ref, kw_ref, vw_ref, gate_ref, ovl_ref,
                 o_ref, *, tq, n_tiles):
    for n in range(n_tiles):
        pl.when(pl.program_id(2) == n)(functools.partial(
            _attn_step, n, q_ref, kc_ref, vc_ref, ks_ref, vs_ref, kw_ref, vw_ref, gate_ref,
            ovl_ref, o_ref, tq))


def _attn_step(n, q_ref, kc_ref, vc_ref, ks_ref, vs_ref, kw_ref, vw_ref, gate_ref, ovl_ref,
               o_ref, tq):
    R = NSA_GROUP
    t0 = n * tq
    q = q_ref[...].reshape(R * tq, LANES)
    n_cmp = kc_ref.shape[0]

    tpos_c = t0 + lax.broadcasted_iota(jnp.int32, (tq, n_cmp), 0)
    cend = lax.broadcasted_iota(jnp.int32, (tq, n_cmp), 1) * CMP_STRIDE + (CMP_BLOCK - 1)
    cmask = cend <= tpos_c
    s = _dot_nt(q, kc_ref[...]).reshape(R, tq, n_cmp)
    s = jnp.where(cmask[None], s, NEG)
    e = jnp.exp(s - jnp.max(s, axis=-1, keepdims=True))
    p_cmp = e / jnp.sum(e, axis=-1, keepdims=True)
    any_cmp = tpos_c >= (CMP_BLOCK - 1)
    p_cmp = jnp.where(any_cmp[None], p_cmp, 0.0)
    o_cmp = _dot(p_cmp.reshape(R * tq, n_cmp).astype(BF16), vc_ref[...])

    psum = p_cmp[0] + p_cmp[1] + p_cmp[2] + p_cmp[3]
    p_hi = psum.astype(BF16)
    p_lo = (psum - p_hi.astype(F32)).astype(BF16)
    ovl = ovl_ref[...]
    imp = _dot_nt(ovl, p_hi) + _dot_nt(ovl, p_lo)
    n_blk = imp.shape[0]
    jblk = lax.broadcasted_iota(jnp.int32, (n_blk, tq), 0)
    tpos_b = t0 + lax.broadcasted_iota(jnp.int32, (n_blk, tq), 1)
    blk_valid = jblk * SLC_BLOCK <= tpos_b
    back = (tpos_b >> SLC_SHIFT) - jblk
    forced = (jblk == 0) | ((back >= 0) & (back < N_LOCAL_BLOCKS))
    score = jnp.where(blk_valid, jnp.where(forced, jnp.inf, imp), -jnp.inf)
    rank = jnp.zeros((n_blk, tq), jnp.int32)
    for kb in range(n_blk):
        sk = score[kb:kb + 1, :]
        ahead = (sk > score) | ((sk == score) & (jblk > kb))
        rank = rank + ahead.astype(jnp.int32)
    unsel_t = 1.0 - (blk_valid & (rank < SLC_TOP_N)).astype(F32)
    unsel_t = jnp.concatenate([jnp.zeros((HEAD_DIM, tq), F32), unsel_t,
                               jnp.zeros((LANES - HEAD_DIM - n_blk, tq), F32)], axis=0)
    unsel = unsel_t.T.astype(BF16)
    q_sel = (q.reshape(R, tq, LANES) + unsel[None]).reshape(R * tq, LANES)

    kw0 = max(t0 - WINDOW, 0)
    span = t0 + tq - kw0
    diff = (t0 + lax.broadcasted_iota(jnp.int32, (tq, span), 0)
            - (kw0 + lax.broadcasted_iota(jnp.int32, (tq, span), 1)))
    win_bias = jnp.where((diff >= 0) & (diff < WINDOW), 0.0, NEG)
    s_w = _add_bias(_dot_nt(q, kw_ref[kw0:kw0 + span, :]), win_bias)
    p_w = jnp.exp(s_w - jnp.max(s_w, axis=-1, keepdims=True))
    acc_w = _dot(p_w.astype(BF16), vw_ref[kw0:kw0 + span, :])

    causal_bias = jnp.where(lax.broadcasted_iota(jnp.int32, (tq, tq), 1)
                            <= lax.broadcasted_iota(jnp.int32, (tq, tq), 0), 0.0, NEG)
    s_d = _add_bias(_dot_nt(q_sel, ks_ref[t0:t0 + tq, :]), causal_bias)
    m_s = jnp.max(s_d, axis=-1, keepdims=True)
    if n > 0:
        s_o = _dot_nt(q_sel, ks_ref[0:t0, :])
        m_s = jnp.maximum(m_s, jnp.max(s_o, axis=-1, keepdims=True))
    acc_s = _dot(jnp.exp(s_d - m_s).astype(BF16), vs_ref[t0:t0 + tq, :])
    if n > 0:
        acc_s = acc_s + _dot(jnp.exp(s_o - m_s).astype(BF16), vs_ref[0:t0, :])

    o_slc = acc_s / acc_s[:, HEAD_DIM:HEAD_DIM + 1]
    o_win = acc_w / acc_w[:, HEAD_DIM:HEAD_DIM + 1]
    gt = gate_ref[...]
    outs = []
    for r in range(R):
        rows = slice(r * tq, (r + 1) * tq)
        o_r = (gt[:, 3 * r:3 * r + 1] * o_cmp[rows]
               + gt[:, 3 * r + 1:3 * r + 2] * o_slc[rows]
               + gt[:, 3 * r + 2:3 * r + 3] * o_win[rows])
        outs.append(o_r[:, :HEAD_DIM])
    o_ref[...] = jnp.concatenate(outs, axis=1)


def _overlap_t(n_cmp_pad, n_blk):
    cs = np.arange(n_cmp_pad) * CMP_STRIDE
    ce = cs + CMP_BLOCK - 1
    ss = np.arange(n_blk) * SLC_BLOCK
    se = ss + SLC_BLOCK - 1
    return ((cs[None, :] <= se[:, None]) & (ce[None, :] >= ss[:, None])).astype(np.float32)


def _attention(q, kv, kcvc, gates, B, S, *, tq=256):
    T = q.shape[1]
    nq = S // tq
    n_cmp = S // CMP_STRIDE
    n_blk = S // SLC_BLOCK
    G = NSA_KV_HEADS
    ovl = jnp.asarray(_overlap_t(n_cmp, n_blk), BF16)
    assert WINDOW % tq == 0 and WINDOW + tq <= S
    kv_spec = lambda c: pl.BlockSpec((None, S, LANES), lambda b, g, i: (G * c + g, b, 0))
    return pl.pallas_call(
        functools.partial(_attn_kernel, tq=tq, n_tiles=nq),
        grid=(B, G, nq),
        in_specs=[
            pl.BlockSpec((NSA_GROUP, tq, LANES), lambda b, g, i: (g, b * nq + i, 0)),
            pl.BlockSpec((None, None, n_cmp, LANES), lambda b, g, i: (b, g, 0, 0)),
            pl.BlockSpec((None, None, n_cmp, LANES), lambda b, g, i: (b, G + g, 0, 0)),
            kv_spec(0), kv_spec(1), kv_spec(2), kv_spec(3),
            pl.BlockSpec((None, tq, LANES), lambda b, g, i: (g, b * nq + i, 0)),
            pl.BlockSpec((n_blk, n_cmp), lambda b, g, i: (0, 0)),
        ],
        out_specs=pl.BlockSpec((tq, NSA_GROUP * HEAD_DIM), lambda b, g, i: (b * nq + i, g)),
        out_shape=jax.ShapeDtypeStruct((T, NSA_WIDTH), F32),
        compiler_params=pltpu.CompilerParams(
            dimension_semantics=("parallel", "parallel", "arbitrary"),
            vmem_limit_bytes=VMEM_LIMIT),
        name="nsa_attn",
    )(q, kcvc, kcvc, kv, kv, kv, kv, gates, ovl)


def _out_kernel(ylru_ref, ynsa_ref, h_ref, gl_ref, gn_ref, w_ref, g_ref, b_ref, o_ref):
    yl = _rms_norm(ylru_ref[...], gl_ref[...]).astype(BF16)
    yn = _rms_norm(ynsa_ref[...], gn_ref[...]).astype(BF16)
    mix = _dot(yl, w_ref[0:LRU_WIDTH, :]) + _dot(yn, w_ref[LRU_WIDTH:, :])
    o_ref[...] = _layer_norm(ALPHA * h_ref[...] + mix, g_ref[...], b_ref[...])


def _out_proj(y_lru, y_nsa, h, gn_lru, gn_nsa, w_out, g, b, *, tm=512):
    T, D = h.shape
    row = lambda n: pl.BlockSpec((1, n), lambda i: (0, 0))
    return pl.pallas_call(
        _out_kernel,
        grid=(T // tm,),
        in_specs=[
            pl.BlockSpec((tm, LRU_WIDTH), lambda i: (i, 0)),
            pl.BlockSpec((tm, NSA_WIDTH), lambda i: (i, 0)),
            pl.BlockSpec((tm, D), lambda i: (i, 0)),
            row(LRU_WIDTH), row(NSA_WIDTH),
            pl.BlockSpec((LRU_WIDTH + NSA_WIDTH, D), lambda i: (0, 0)),
            row(D), row(D),
        ],
        out_specs=pl.BlockSpec((tm, D), lambda i: (i, 0)),
        out_shape=jax.ShapeDtypeStruct((T, D), F32),
        compiler_params=pltpu.CompilerParams(
            dimension_semantics=("parallel",), vmem_limit_bytes=VMEM_LIMIT),
        name="out_proj",
    )(y_lru, y_nsa, h, gn_lru.reshape(1, -1), gn_nsa.reshape(1, -1), w_out.astype(BF16),
      g.reshape(1, D), b.reshape(1, D))


def kernel(x, ffn1_w_in, ffn1_w_out, ln1_g, ln1_b, mix_w_in, conv_w, conv_b, lru_w_a, lru_b_a,
           lru_w_x, lru_b_x, lru_lam, cmp_pe_k, cmp_w1_k, cmp_w2_k, cmp_pe_v, cmp_w1_v,
           cmp_w2_v, gn_lru, gn_nsa, mix_w_out, ln2_g, ln2_b, ffn2_w_in, ffn2_w_out, ln3_g,
           ln3_b):
    B, S, D = x.shape
    h = x.reshape(B * S, D)
    for l in range(DEPTH):
        h = _ffn_ln(h, ffn1_w_in[l], ffn1_w_out[l], ln1_g[l], ln1_b[l])
        lru, q, cmp_in, kv, gates = _proj(h, mix_w_in[l], S)
        y_lru = _lru(lru, conv_w[l], conv_b[l], lru_w_a[l], lru_b_a[l], lru_w_x[l], lru_b_x[l],
                     lru_lam[l], B, S)
        kcvc = _compress(cmp_in, cmp_pe_k[l], cmp_w1_k[l], cmp_w2_k[l],
                         cmp_pe_v[l], cmp_w1_v[l], cmp_w2_v[l], B, S)
        y_nsa = _attention(q, kv, kcvc, gates, B, S)
        h = _out_proj(y_lru, y_nsa, h, gn_lru[l], gn_nsa[l], mix_w_out[l], ln2_g[l], ln2_b[l])
        h = _ffn_ln(h, ffn2_w_in[l], ffn2_w_out[l], ln3_g[l], ln3_b[l])
    return h.reshape(B, S, D)
```

```python
import functools

import numpy as np
import jax
import jax.numpy as jnp
from jax import lax
from jax.experimental import pallas as pl
from jax.experimental.pallas import tpu as pltpu

F32 = jnp.float32
BF16 = jnp.bfloat16

D_MODEL = 1024
LRU_WIDTH = 512
LRU_HEADS = 8
LRU_BLOCK = 64
CONV_WIDTH = 4
LRU_C = 8.0
NSA_Q_HEADS = 8
NSA_KV_HEADS = 2
NSA_GROUP = 4
HEAD_DIM = 64
NSA_WIDTH = 512
KV_WIDTH = 128
CMP_BLOCK = 32
CMP_STRIDE = 16
CMP_HIDDEN = 256
SLC_BLOCK = 64
SLC_SHIFT = 6
SLC_TOP_N = 16
N_LOCAL_BLOCKS = 2
WINDOW = 512
ROPE_THETA = 10000.0
D_FF = 2816
DEPTH = 1
ALPHA = (2.0 * DEPTH) ** 0.25
LN_EPS = 1e-5
RMS_EPS = 1e-6
NEG = -1e30
LOG2E = 1.4426950408889634

LANES = 128
BF16_ROWS = 16
VMEM_LIMIT = 48 * 1024 * 1024
N_GATE = 3 * NSA_GROUP
GATE_ROWS = 16
V_ROWS = HEAD_DIM + BF16_ROWS


def _dot(a, b):
    return jnp.dot(a, b, preferred_element_type=F32)


def _dot_nt(a, b):
    return lax.dot_general(a, b, (((1,), (1,)), ((), ())), preferred_element_type=F32)


def _layer_norm(y, g, b):
    mu = jnp.mean(y, axis=-1, keepdims=True)
    d = y - mu
    var = jnp.mean(d * d, axis=-1, keepdims=True)
    return d * lax.rsqrt(var + LN_EPS) * g + b


def _rms_norm(y, g):
    return y * lax.rsqrt(jnp.mean(y * y, axis=-1, keepdims=True) + RMS_EPS) * g


def _silu(x):
    return x * jax.nn.sigmoid(x)


def _resident(shape):
    return pl.BlockSpec(shape, lambda *_: (0,) * len(shape), pipeline_mode=pl.Buffered(1))


def _ffn_ln_kernel(x_ref, wg_ref, wu_ref, wo_ref, g_ref, b_ref, o_ref):
    x = x_ref[...]
    xb = x.astype(BF16)
    gate = _dot(xb, wg_ref[...])
    up = _dot(xb, wu_ref[...])
    act = (_silu(gate) * up).astype(BF16)
    y = ALPHA * x + 0.5 * _dot(act, wo_ref[...])
    o_ref[...] = _layer_norm(y, g_ref[...], b_ref[...])


def _ffn_ln(x, w_in, w_out, g, b, *, tm=512):
    T, D = x.shape
    wg = w_in[:, :D_FF].astype(BF16)
    wu = w_in[:, D_FF:].astype(BF16)
    wo = w_out.astype(BF16)
    return pl.pallas_call(
        _ffn_ln_kernel,
        grid=(T // tm,),
        in_specs=[
            pl.BlockSpec((tm, D), lambda i: (i, 0)),
            _resident((D, D_FF)), _resident((D, D_FF)), _resident((D_FF, D)),
            _resident((1, D)), _resident((1, D)),
        ],
        out_specs=pl.BlockSpec((tm, D), lambda i: (i, 0)),
        out_shape=jax.ShapeDtypeStruct((T, D), F32),
        compiler_params=pltpu.CompilerParams(
            dimension_semantics=("parallel",), vmem_limit_bytes=VMEM_LIMIT),
        name="ffn_ln",
    )(x, wg, wu, wo, g.reshape(1, D), b.reshape(1, D))


Q_COL0 = 2 * LRU_WIDTH
KCMP_COL0 = Q_COL0 + NSA_WIDTH
VCMP_COL0 = KCMP_COL0 + KV_WIDTH
KSLC_COL0 = VCMP_COL0 + KV_WIDTH
VSLC_COL0 = KSLC_COL0 + KV_WIDTH
KWIN_COL0 = VSLC_COL0 + KV_WIDTH
VWIN_COL0 = KWIN_COL0 + KV_WIDTH
GATE_COL0 = VWIN_COL0 + KV_WIDTH
VT_ROW0 = NSA_WIDTH
GT_ROW0 = VT_ROW0 + 2 * KV_WIDTH
WT_ROWS = GT_ROW0 + NSA_KV_HEADS * GATE_ROWS


def _rope_chunk(xc, cos, sin_signed):
    lane = lax.broadcasted_iota(jnp.int32, xc.shape, 1)
    first = (lane & (HEAD_DIM - 1)) < (HEAD_DIM // 2)
    partner = jnp.where(first, pltpu.roll(xc, LANES - HEAD_DIM // 2, axis=1),
                        pltpu.roll(xc, HEAD_DIM // 2, axis=1))
    return xc * cos + partner * sin_signed


def _proj_kernel(h_ref, wn_ref, wt_ref, cos_ref, sin_ref, cost_ref, sint_ref,
                 lru_ref, qt_ref, cmp_ref, k_ref, vt_ref, gt_ref, *, seq_tiles):
    tm = h_ref.shape[0]
    half = HEAD_DIM // 2
    hb = h_ref[...].astype(BF16)
    p = _dot(hb, wn_ref[...])
    pt = _dot_nt(wt_ref[...], hb)
    lru_ref[...] = p[:, :Q_COL0]

    cos_t = cost_ref[...]
    sin_t = sint_ref[...]
    scale = HEAD_DIM ** -0.5 * LOG2E
    for hd in range(NSA_Q_HEADS):
        x1 = pt[hd * HEAD_DIM:hd * HEAD_DIM + half]
        x2 = pt[hd * HEAD_DIM + half:(hd + 1) * HEAD_DIM]
        qt_ref[hd] = (jnp.concatenate([x1 * cos_t - x2 * sin_t, x2 * cos_t + x1 * sin_t], axis=0)
                      * scale).astype(BF16)
    ones = jnp.ones((V_ROWS - HEAD_DIM, tm), F32)
    for c in range(2 * NSA_KV_HEADS):
        v = pt[VT_ROW0 + c * HEAD_DIM:VT_ROW0 + (c + 1) * HEAD_DIM]
        vt_ref[c] = jnp.concatenate([v, ones], axis=0).astype(BF16)
    for g in range(NSA_KV_HEADS):
        gt_ref[g] = jax.nn.sigmoid(pt[GT_ROW0 + g * GATE_ROWS:GT_ROW0 + (g + 1) * GATE_ROWS])

    cos = cos_ref[...]
    sin = sin_ref[...]
    kcmp = _rope_chunk(p[:, Q_COL0:Q_COL0 + LANES], cos, sin).astype(BF16)
    vcmp = p[:, Q_COL0 + LANES:Q_COL0 + 2 * LANES].astype(BF16)
    for g in range(NSA_KV_HEADS):
        cmp_ref[g] = kcmp[:, g * HEAD_DIM:(g + 1) * HEAD_DIM]
        cmp_ref[NSA_KV_HEADS + g] = vcmp[:, g * HEAD_DIM:(g + 1) * HEAD_DIM]
    lane = lax.broadcasted_iota(jnp.int32, (tm, LANES), 1)
    low = lane < HEAD_DIM
    pos = (pl.program_id(0) % seq_tiles) * tm + lax.broadcasted_iota(jnp.int32, (tm, LANES), 0)
    ext_blk = jnp.where(lane - HEAD_DIM == (pos >> SLC_SHIFT), NEG, 0.0)
    ext_zero = jnp.zeros((tm, LANES), F32)
    for c, ext in enumerate((ext_blk, ext_zero)):
        xc = _rope_chunk(p[:, Q_COL0 + (2 + c) * LANES:Q_COL0 + (3 + c) * LANES], cos, sin)
        k_ref[2 * c] = jnp.where(low, xc, ext).astype(BF16)
        k_ref[2 * c + 1] = jnp.where(low, pltpu.roll(xc, HEAD_DIM, axis=1), ext).astype(BF16)


def _proj(h, w_in, S, *, tm=512):
    T, D = h.shape
    G = NSA_KV_HEADS
    col = lambda c0, n: w_in[:, c0:c0 + n]
    w_nat = jnp.concatenate(
        [col(0, Q_COL0), col(KCMP_COL0, KV_WIDTH), col(VCMP_COL0, KV_WIDTH),
         col(KSLC_COL0, KV_WIDTH), col(KWIN_COL0, KV_WIDTH)], axis=1).astype(BF16)
    gate_rows = [jnp.pad(col(GATE_COL0 + g * N_GATE, N_GATE).T, ((0, GATE_ROWS - N_GATE), (0, 0)))
                 for g in range(G)]
    w_t = jnp.concatenate(
        [col(Q_COL0, NSA_WIDTH).T, col(VSLC_COL0, KV_WIDTH).T, col(VWIN_COL0, KV_WIDTH).T]
        + gate_rows, axis=0).astype(BF16)
    n_nat = w_nat.shape[1]

    half = HEAD_DIM // 2
    inv = ROPE_THETA ** (-jnp.arange(half, dtype=F32) / half)
    ang = jnp.arange(S, dtype=F32)[:, None] * inv[None, :]
    cos = jnp.cos(ang)
    sin = jnp.sin(ang)
    cos_n = jnp.concatenate([cos, cos, cos, cos], axis=1)
    sin_n = jnp.concatenate([-sin, sin, -sin, sin], axis=1)

    nS = S // tm
    return pl.pallas_call(
        functools.partial(_proj_kernel, seq_tiles=nS),
        grid=(T // tm,),
        in_specs=[
            pl.BlockSpec((tm, D), lambda i: (i, 0)),
            _resident((D, n_nat)), _resident((WT_ROWS, D)),
            pl.BlockSpec((tm, LANES), lambda i: (i % nS, 0)),
            pl.BlockSpec((tm, LANES), lambda i: (i % nS, 0)),
            pl.BlockSpec((half, tm), lambda i: (0, i % nS)),
            pl.BlockSpec((half, tm), lambda i: (0, i % nS)),
        ],
        out_specs=[
            pl.BlockSpec((tm, Q_COL0), lambda i: (i, 0)),
            pl.BlockSpec((NSA_Q_HEADS, HEAD_DIM, tm), lambda i: (0, 0, i)),
            pl.BlockSpec((2 * G, tm, HEAD_DIM), lambda i: (0, i, 0)),
            pl.BlockSpec((2 * G, tm, LANES), lambda i: (0, i, 0)),
            pl.BlockSpec((2 * G, V_ROWS, tm), lambda i: (0, 0, i)),
            pl.BlockSpec((G, GATE_ROWS, tm), lambda i: (0, 0, i)),
        ],
        out_shape=[
            jax.ShapeDtypeStruct((T, Q_COL0), F32),
            jax.ShapeDtypeStruct((NSA_Q_HEADS, HEAD_DIM, T), BF16),
            jax.ShapeDtypeStruct((2 * G, T, HEAD_DIM), BF16),
            jax.ShapeDtypeStruct((2 * G, T, LANES), BF16),
            jax.ShapeDtypeStruct((2 * G, V_ROWS, T), BF16),
            jax.ShapeDtypeStruct((G, GATE_ROWS, T), F32),
        ],
        compiler_params=pltpu.CompilerParams(
            dimension_semantics=("parallel",), vmem_limit_bytes=VMEM_LIMIT),
        name="proj",
    )(h, w_nat, w_t, cos_n, sin_n, cos.T, sin.T)


SCAN_PAD = 1024


def _lru_kernel(x_ref, gate_ref, cw_ref, cb_ref, wa_ref, ba_ref, wx_ref, bx_ref, lam_ref,
                y_ref, a_scr, b_scr):
    S = x_ref.shape[0]
    x = x_ref[...]
    row = lax.broadcasted_iota(jnp.int32, x.shape, 0)

    a_scr[0:SCAN_PAD, :] = jnp.zeros((SCAN_PAD, LANES), F32)
    a_scr[SCAN_PAD:, :] = x
    xc = cb_ref[...] + x * cw_ref[CONV_WIDTH - 1:CONV_WIDTH, :]
    for d in range(1, CONV_WIDTH):
        xs = a_scr[SCAN_PAD - d:SCAN_PAD - d + S, :]
        xc = xc + xs * cw_ref[CONV_WIDTH - 1 - d:CONV_WIDTH - d, :]

    xb = xc.astype(BF16)
    r = jax.nn.sigmoid(_dot(xb, wa_ref[0]) + ba_ref[...])
    ig = jax.nn.sigmoid(_dot(xb, wx_ref[0]) + bx_ref[...])
    neg_lam = -lam_ref[...]
    softplus = jnp.maximum(neg_lam, 0.0) + jnp.log1p(jnp.exp(-jnp.abs(neg_lam)))
    log_a = (-LRU_C) * r * softplus
    a = jnp.exp(log_a)
    mult = jnp.sqrt(jnp.tanh(-log_a) * (a * a + 1.0))
    mult = jnp.where(row == 0, 1.0, mult)
    b = mult * (ig * xc)

    a_scr[0:SCAN_PAD, :] = jnp.ones((SCAN_PAD, LANES), F32)
    b_scr[0:SCAN_PAD, :] = jnp.zeros((SCAN_PAD, LANES), F32)
    d = 1
    while d < S:
        a_scr[SCAN_PAD:, :] = a
        b_scr[SCAN_PAD:, :] = b
        a_sh = a_scr[SCAN_PAD - d:SCAN_PAD - d + S, :]
        b_sh = b_scr[SCAN_PAD - d:SCAN_PAD - d + S, :]
        b = a * b_sh + b
        a = a * a_sh
        d *= 2

    y_ref[...] = b * jax.nn.gelu(gate_ref[...])


def _lru(lru, conv_w, conv_b, w_a, b_a, w_x, b_x, lam, B, S):
    T = lru.shape[0]
    n_ch = LRU_WIDTH // LANES

    def blockdiag(w):
        w = w.reshape(n_ch, 2, LRU_BLOCK, LRU_BLOCK)
        z = jnp.zeros((n_ch, LRU_BLOCK, LRU_BLOCK), w.dtype)
        top = jnp.concatenate([w[:, 0], z], axis=2)
        bot = jnp.concatenate([z, w[:, 1]], axis=2)
        return jnp.concatenate([top, bot], axis=1).astype(BF16)

    vec = lambda v: v.reshape(1, LRU_WIDTH)
    chan = lambda rows: pl.BlockSpec((rows, LANES), lambda b, c: (0, c))
    return pl.pallas_call(
        _lru_kernel,
        grid=(B, n_ch),
        in_specs=[
            pl.BlockSpec((S, LANES), lambda b, c: (b, c)),
            pl.BlockSpec((S, LANES), lambda b, c: (b, n_ch + c)),
            chan(CONV_WIDTH), chan(1),
            pl.BlockSpec((1, LANES, LANES), lambda b, c: (c, 0, 0)), chan(1),
            pl.BlockSpec((1, LANES, LANES), lambda b, c: (c, 0, 0)), chan(1),
            chan(1),
        ],
        out_specs=pl.BlockSpec((S, LANES), lambda b, c: (b, c)),
        out_shape=jax.ShapeDtypeStruct((T, LRU_WIDTH), F32),
        scratch_shapes=[pltpu.VMEM((SCAN_PAD + S, LANES), F32),
                        pltpu.VMEM((SCAN_PAD + S, LANES), F32)],
        compiler_params=pltpu.CompilerParams(
            dimension_semantics=("parallel", "parallel"), vmem_limit_bytes=VMEM_LIMIT),
        name="lru",
    )(lru, lru, conv_w, vec(conv_b), blockdiag(w_a), vec(b_a.reshape(-1)),
      blockdiag(w_x), vec(b_x.reshape(-1)), vec(lam))


def _cmp_kernel(seg_ref, pek_ref, w1k_ref, w2k_ref, pev_ref, w1v_ref, w2vt_ref, kc_ref, vct_ref):
    n_seg = seg_ref.shape[2]
    half = CMP_STRIDE * HEAD_DIM

    def hidden(idx, pe_ref, w1_ref):
        seg = seg_ref[idx, 0]
        first = _dot(seg, w1_ref[0:half, :])
        second = _dot(seg, w1_ref[half:2 * half, :])
        bias = _dot(pe_ref[...], w1_ref[...])[0:1, :]
        return _silu(first + pltpu.roll(second, n_seg - 1, axis=0) + bias).astype(BF16)

    for g in range(NSA_KV_HEADS):
        tok = _dot(hidden(g, pek_ref, w1k_ref), w2k_ref[...])
        kc_ref[0, g] = jnp.concatenate([tok, jnp.zeros_like(tok)], axis=1).astype(BF16)
        vct_ref[0, g] = _dot_nt(w2vt_ref[...], hidden(NSA_KV_HEADS + g, pev_ref, w1v_ref)
                                ).astype(BF16)


def _compress(cmp_in, pe_k, w1_k, w2_k, pe_v, w1_v, w2_v, B, S):
    G = NSA_KV_HEADS
    n_seg = S // CMP_STRIDE
    seg = cmp_in.reshape(2 * G, B, n_seg, CMP_STRIDE * HEAD_DIM)
    flat = CMP_BLOCK * HEAD_DIM
    pe8 = lambda pe: jnp.broadcast_to(pe.reshape(1, -1), (8, flat)).astype(BF16)
    return pl.pallas_call(
        _cmp_kernel,
        grid=(B,),
        in_specs=[
            pl.BlockSpec((2 * G, 1, n_seg, CMP_STRIDE * HEAD_DIM), lambda b: (0, b, 0, 0)),
            _resident((8, flat)), _resident((flat, CMP_HIDDEN)), _resident((CMP_HIDDEN, HEAD_DIM)),
            _resident((8, flat)), _resident((flat, CMP_HIDDEN)), _resident((HEAD_DIM, CMP_HIDDEN)),
        ],
        out_specs=[
            pl.BlockSpec((1, G, n_seg, LANES), lambda b: (b, 0, 0, 0)),
            pl.BlockSpec((1, G, HEAD_DIM, n_seg), lambda b: (b, 0, 0, 0)),
        ],
        out_shape=[
            jax.ShapeDtypeStruct((B, G, n_seg, LANES), BF16),
            jax.ShapeDtypeStruct((B, G, HEAD_DIM, n_seg), BF16),
        ],
        compiler_params=pltpu.CompilerParams(
            dimension_semantics=("parallel",), vmem_limit_bytes=VMEM_LIMIT),
        name="compress",
    )(seg, pe8(pe_k), w1_k.astype(BF16), w2_k.astype(BF16),
      pe8(pe_v), w1_v.astype(BF16), w2_v.T.astype(BF16))


def _attn_kernel(qt_ref, kc_ref, vct_ref, ks_ref, vst_ref, kw_ref, vwt_ref, gt_ref, ovl_ref,
                 bias_ref, o_ref, *, tq, n_tiles):
    for n in range(n_tiles):
        pl.when(pl.program_id(2) == n)(functools.partial(
            _attn_step, n, qt_ref, kc_ref, vct_ref, ks_ref, vst_ref, kw_ref, vwt_ref, gt_ref,
            ovl_ref, bias_ref, o_ref, tq))


CHUNK_TILES = 2


def _interleave(*stages):
    live = list(stages)
    while live:
        for st in list(live):
            try:
                next(st)
            except StopIteration:
                live.remove(st)


def _tile_bias(bias_ref, t0, k0, behind):
    if k0 == t0:
        return bias_ref[0]
    if behind is not None and k0 == t0 - behind:
        return bias_ref[1]
    return None


def _key_chunks(k_ref, vt_ref, k0, n_keys, tile, q, bias_fn):
    starts = list(range(k0, k0 + n_keys, tile))
    return [dict(k_ref=k_ref, vt_ref=vt_ref, tile=tile, q=q, bias_fn=bias_fn,
                 starts=starts[i:i + CHUNK_TILES]) for i in range(0, len(starts), CHUNK_TILES)]


def _scores_stage(ch):
    tiles, m = [], None
    for k0 in ch["starts"]:
        s = _dot(ch["k_ref"][k0:k0 + ch["tile"], :], ch["q"])
        bias = ch["bias_fn"](k0)
        if bias is not None:
            s = s + bias
        cm = jnp.max(s, axis=0, keepdims=True)
        m = cm if m is None else jnp.maximum(m, cm)
        tiles.append(s)
        yield
    ch["s"], ch["m"] = tiles, m


def _pv_stage(ch, heads, tq):
    acc = [None] * heads
    for k0, s in zip(ch["starts"], ch["s"]):
        p = jnp.exp2(s - ch["m"]).astype(BF16)
        vt = ch["vt_ref"][:, k0:k0 + ch["tile"]]
        for r in range(heads):
            d = _dot(vt, p[:, r * tq:(r + 1) * tq])
            acc[r] = d if acc[r] is None else acc[r] + d
        yield
    ch["acc"] = acc


def _merge_chunks(chunks, heads, tq):
    if len(chunks) == 1:
        return chunks[0]["acc"]
    m = functools.reduce(jnp.maximum, [ch["m"] for ch in chunks])
    acc = [None] * heads
    for ch in chunks:
        w = jnp.exp2(ch["m"] - m)
        for r in range(heads):
            term = ch["acc"][r] * w[:, r * tq:(r + 1) * tq]
            acc[r] = term if acc[r] is None else acc[r] + term
    return acc


def _rank_stage(out, score, jblk):
    rank = jnp.zeros(score.shape, jnp.int32)
    for kb in range(score.shape[0]):
        sk = score[kb:kb + 1, :]
        ahead = (sk > score) | ((sk == score) & (jblk > kb))
        rank = rank + ahead.astype(jnp.int32)
        if kb % 4 == 3:
            yield
    out["rank"] = rank


def _attn_step(n, qt_ref, kc_ref, vct_ref, ks_ref, vst_ref, kw_ref, vwt_ref, gt_ref, ovl_ref,
               bias_ref, o_ref, tq):
    R = NSA_GROUP
    t0 = n * tq
    n_cmp = kc_ref.shape[0]
    n_blk = ovl_ref.shape[0]
    q_all = jnp.concatenate([qt_ref[r] for r in range(R)], axis=1)
    q_pad = jnp.concatenate([q_all, jnp.zeros((LANES - HEAD_DIM, R * tq), BF16)], axis=0)
    head = lambda a, r: a[:, r * tq:(r + 1) * tq]

    kw0 = max(t0 - WINDOW, 0)
    win = _key_chunks(kw_ref, vwt_ref, kw0, t0 + tq - kw0, tq, q_pad,
                      lambda k0: _tile_bias(bias_ref, t0, k0, WINDOW))
    win_scores = _scores_stage(win[0])
    next(win_scores)

    tpos_c = t0 + (lax.broadcasted_iota(jnp.int32, (n_cmp, R * tq), 1) & (tq - 1))
    cend = lax.broadcasted_iota(jnp.int32, (n_cmp, R * tq), 0) * CMP_STRIDE + (CMP_BLOCK - 1)
    s = _dot(kc_ref[...], q_pad) + jnp.where(cend <= tpos_c, 0.0, NEG)
    e = jnp.exp2(s - jnp.max(s, axis=0, keepdims=True))
    p_cmp = e * (1.0 / jnp.sum(e, axis=0, keepdims=True))
    if t0 < CMP_BLOCK - 1:
        p_cmp = jnp.where(tpos_c >= CMP_BLOCK - 1, p_cmp, 0.0)
    p_cmp_b = p_cmp.astype(BF16)
    vct = vct_ref[...]
    o_cmp = [_dot(vct, head(p_cmp_b, r)) for r in range(R)]
    psum = functools.reduce(jnp.add, [head(p_cmp, r) for r in range(R)])

    p_hi = psum.astype(BF16)
    p_lo = (psum - p_hi.astype(F32)).astype(BF16)
    ovl = ovl_ref[...]
    imp = _dot(ovl, p_hi) + _dot(ovl, p_lo)
    jblk = lax.broadcasted_iota(jnp.int32, (n_blk, tq), 0)
    tpos_b = t0 + lax.broadcasted_iota(jnp.int32, (n_blk, tq), 1)
    blk_valid = jblk * SLC_BLOCK <= tpos_b
    back = (tpos_b >> SLC_SHIFT) - jblk
    forced = (jblk == 0) | ((back >= 0) & (back < N_LOCAL_BLOCKS))
    score = jnp.where(blk_valid, jnp.where(forced, jnp.inf, imp), -jnp.inf)
    sel = {}
    _interleave(_rank_stage(sel, score, jblk), win_scores)
    unsel = (1.0 - (blk_valid & (sel["rank"] < SLC_TOP_N)).astype(F32)).astype(BF16)
    sel_ext = jnp.concatenate(
        [unsel, jnp.zeros((LANES - HEAD_DIM - n_blk, tq), BF16)], axis=0)
    q_sel = jnp.concatenate([q_all, jnp.concatenate([sel_ext] * R, axis=1)], axis=0)

    slc = _key_chunks(ks_ref, vst_ref, 0, t0 + tq, tq, q_sel,
                      lambda k0: _tile_bias(bias_ref, t0, k0, None))
    chunks = win + slc
    for idx, ch in enumerate(chunks):
        stages = [_pv_stage(ch, R, tq)]
        if idx + 1 < len(chunks):
            stages.append(_scores_stage(chunks[idx + 1]))
        _interleave(*stages)
    acc_w = _merge_chunks(win, R, tq)
    acc_s = _merge_chunks(slc, R, tq)

    gt = gt_ref[...]
    outs = []
    for r in range(R):
        g_cmp = gt[3 * r:3 * r + 1]
        g_slc = gt[3 * r + 1:3 * r + 2] * (1.0 / acc_s[r][HEAD_DIM:HEAD_DIM + 1])
        g_win = gt[3 * r + 2:3 * r + 3] * (1.0 / acc_w[r][HEAD_DIM:HEAD_DIM + 1])
        outs.append(g_cmp * o_cmp[r] + g_slc * acc_s[r][:HEAD_DIM] + g_win * acc_w[r][:HEAD_DIM])
    o_ref[...] = jnp.concatenate(outs, axis=0).T


def _overlap_t(n_cmp_pad, n_blk):
    cs = np.arange(n_cmp_pad) * CMP_STRIDE
    ce = cs + CMP_BLOCK - 1
    ss = np.arange(n_blk) * SLC_BLOCK
    se = ss + SLC_BLOCK - 1
    return ((cs[None, :] <= se[:, None]) & (ce[None, :] >= ss[:, None])).astype(np.float32)


def _attention(qt, k, vt, kc, vct, gates_t, B, S, *, tq=256):
    T = qt.shape[2]
    nq = S // tq
    n_cmp = S // CMP_STRIDE
    n_blk = S // SLC_BLOCK
    G = NSA_KV_HEADS
    ovl = jnp.asarray(_overlap_t(n_cmp, n_blk), BF16)
    assert WINDOW % tq == 0 and WINDOW + tq <= S
    assert HEAD_DIM + n_blk <= LANES
    assert tq & (tq - 1) == 0
    key_off = np.arange(tq)[:, None]
    q_off = np.tile(np.arange(tq), NSA_GROUP)[None, :]
    bias = jnp.asarray(np.stack([np.where(key_off <= q_off, 0.0, NEG),
                                 np.where(key_off > q_off, 0.0, NEG)]), F32)
    k_spec = lambda c: pl.BlockSpec((None, S, LANES), lambda b, g, i: (G * c + g, b, 0))
    vt_spec = lambda c: pl.BlockSpec((None, V_ROWS, S), lambda b, g, i: (G * c + g, 0, b))
    return pl.pallas_call(
        functools.partial(_attn_kernel, tq=tq, n_tiles=nq),
        grid=(B, G, nq),
        in_specs=[
            pl.BlockSpec((NSA_GROUP, HEAD_DIM, tq), lambda b, g, i: (g, 0, b * nq + i)),
            pl.BlockSpec((None, None, n_cmp, LANES), lambda b, g, i: (b, g, 0, 0)),
            pl.BlockSpec((None, None, HEAD_DIM, n_cmp), lambda b, g, i: (b, g, 0, 0)),
            k_spec(0), vt_spec(0), k_spec(1), vt_spec(1),
            pl.BlockSpec((None, GATE_ROWS, tq), lambda b, g, i: (g, 0, b * nq + i)),
            _resident((n_blk, n_cmp)),
            _resident((2, tq, NSA_GROUP * tq)),
        ],
        out_specs=pl.BlockSpec((tq, NSA_GROUP * HEAD_DIM), lambda b, g, i: (b * nq + i, g)),
        out_shape=jax.ShapeDtypeStruct((T, NSA_WIDTH), F32),
        compiler_params=pltpu.CompilerParams(
            dimension_semantics=("parallel", "parallel", "arbitrary"),
            vmem_limit_bytes=VMEM_LIMIT),
        name="nsa_attn",
    )(qt, kc, vct, k, vt, k, vt, gates_t, ovl, bias)


def _out_kernel(ylru_ref, ynsa_ref, h_ref, gl_ref, gn_ref, w_ref, g_ref, b_ref, o_ref):
    yl = _rms_norm(ylru_ref[...], gl_ref[...]).astype(BF16)
    yn = _rms_norm(ynsa_ref[...], gn_ref[...]).astype(BF16)
    mix = _dot(yl, w_ref[0:LRU_WIDTH, :]) + _dot(yn, w_ref[LRU_WIDTH:, :])
    o_ref[...] = _layer_norm(ALPHA * h_ref[...] + mix, g_ref[...], b_ref[...])


def _out_proj(y_lru, y_nsa, h, gn_lru, gn_nsa, w_out, g, b, *, tm=512):
    T, D = h.shape
    return pl.pallas_call(
        _out_kernel,
        grid=(T // tm,),
        in_specs=[
            pl.BlockSpec((tm, LRU_WIDTH), lambda i: (i, 0)),
            pl.BlockSpec((tm, NSA_WIDTH), lambda i: (i, 0)),
            pl.BlockSpec((tm, D), lambda i: (i, 0)),
            _resident((1, LRU_WIDTH)), _resident((1, NSA_WIDTH)),
            _resident((LRU_WIDTH + NSA_WIDTH, D)),
            _resident((1, D)), _resident((1, D)),
        ],
        out_specs=pl.BlockSpec((tm, D), lambda i: (i, 0)),
        out_shape=jax.ShapeDtypeStruct((T, D), F32),
        compiler_params=pltpu.CompilerParams(
            dimension_semantics=("parallel",), vmem_limit_bytes=VMEM_LIMIT),
        name="out_proj",
    )(y_lru, y_nsa, h, gn_lru.reshape(1, -1), gn_nsa.reshape(1, -1), w_out.astype(BF16),
      g.reshape(1, D), b.reshape(1, D))


def kernel(x, ffn1_w_in, ffn1_w_out, ln1_g, ln1_b, mix_w_in, conv_w, conv_b, lru_w_a, lru_b_a,
           lru_w_x, lru_b_x, lru_lam, cmp_pe_k, cmp_w1_k, cmp_w2_k, cmp_pe_v, cmp_w1_v,
           cmp_w2_v, gn_lru, gn_nsa, mix_w_out, ln2_g, ln2_b, ffn2_w_in, ffn2_w_out, ln3_g,
           ln3_b):
    B, S, D = x.shape
    h = x.reshape(B * S, D)
    for l in range(DEPTH):
        h = _ffn_ln(h, ffn1_w_in[l], ffn1_w_out[l], ln1_g[l], ln1_b[l])
        lru, qt, cmp_in, k, vt, gates_t = _proj(h, mix_w_in[l], S)
        y_lru = _lru(lru, conv_w[l], conv_b[l], lru_w_a[l], lru_b_a[l], lru_w_x[l], lru_b_x[l],
                     lru_lam[l], B, S)
        kc, vct = _compress(cmp_in, cmp_pe_k[l], cmp_w1_k[l], cmp_w2_k[l],
                            cmp_pe_v[l], cmp_w1_v[l], cmp_w2_v[l], B, S)
        y_nsa = _attention(qt, k, vt, kc, vct, gates_t, B, S)
        h = _out_proj(y_lru, y_nsa, h, gn_lru[l], gn_nsa[l], mix_w_out[l], ln2_g[l], ln2_b[l])
        h = _ffn_ln(h, ffn2_w_in[l], ffn2_w_out[l], ln3_g[l], ln3_b[l])
    return h.reshape(B, S, D)
```

```python
import functools

import numpy as np
import jax
import jax.numpy as jnp
from jax import lax
from jax.experimental import pallas as pl
from jax.experimental.pallas import tpu as pltpu

F32 = jnp.float32
BF16 = jnp.bfloat16

D_MODEL = 1024
LRU_WIDTH = 512
LRU_HEADS = 8
LRU_BLOCK = 64
CONV_WIDTH = 4
LRU_C = 8.0
NSA_Q_HEADS = 8
NSA_KV_HEADS = 2
NSA_GROUP = 4
HEAD_DIM = 64
NSA_WIDTH = 512
KV_WIDTH = 128
CMP_BLOCK = 32
CMP_STRIDE = 16
CMP_HIDDEN = 256
SLC_BLOCK = 64
SLC_SHIFT = 6
SLC_TOP_N = 16
N_LOCAL_BLOCKS = 2
WINDOW = 512
ROPE_THETA = 10000.0
D_FF = 2816
DEPTH = 1
ALPHA = (2.0 * DEPTH) ** 0.25
LN_EPS = 1e-5
RMS_EPS = 1e-6
NEG = -1e30
LOG2E = 1.4426950408889634
F32_TINY = 1.1754943508222875e-38

LANES = 128
BF16_ROWS = 16
VMEM_LIMIT = 48 * 1024 * 1024
N_GATE = 3 * NSA_GROUP
GATE_ROWS = 16
V_ROWS = HEAD_DIM + BF16_ROWS


def _dot(a, b):
    return jnp.dot(a, b, preferred_element_type=F32)


def _dot_nt(a, b):
    return lax.dot_general(a, b, (((1,), (1,)), ((), ())), preferred_element_type=F32)


def _layer_norm(y, g, b):
    mu = jnp.mean(y, axis=-1, keepdims=True)
    d = y - mu
    var = jnp.mean(d * d, axis=-1, keepdims=True)
    return d * lax.rsqrt(var + LN_EPS) * g + b


def _rms_norm(y, g):
    return y * lax.rsqrt(jnp.mean(y * y, axis=-1, keepdims=True) + RMS_EPS) * g


def _silu(x):
    return x * jax.nn.sigmoid(x)


def _resident(shape):
    return pl.BlockSpec(shape, lambda *_: (0,) * len(shape), pipeline_mode=pl.Buffered(1))


def _ffn_half_step(x, wg_ref, wu_ref, wo_ref, g_ref, b_ref):
    xb = x.astype(BF16)
    gate = _dot(xb, wg_ref[...])
    up = _dot(xb, wu_ref[...])
    act = (_silu(gate) * up).astype(BF16)
    y = ALPHA * x + 0.5 * _dot(act, wo_ref[...])
    return _layer_norm(y, g_ref[...], b_ref[...])


def _ffn_ln_kernel(x_ref, wg_ref, wu_ref, wo_ref, g_ref, b_ref, o_ref):
    o_ref[...] = _ffn_half_step(x_ref[...], wg_ref, wu_ref, wo_ref, g_ref, b_ref)


def _ffn_ln(x, w_in, w_out, g, b, *, tm=512):
    T, D = x.shape
    wg = w_in[:, :D_FF].astype(BF16)
    wu = w_in[:, D_FF:].astype(BF16)
    wo = w_out.astype(BF16)
    return pl.pallas_call(
        _ffn_ln_kernel,
        grid=(T // tm,),
        in_specs=[
            pl.BlockSpec((tm, D), lambda i: (i, 0)),
            _resident((D, D_FF)), _resident((D, D_FF)), _resident((D_FF, D)),
            _resident((1, D)), _resident((1, D)),
        ],
        out_specs=pl.BlockSpec((tm, D), lambda i: (i, 0)),
        out_shape=jax.ShapeDtypeStruct((T, D), F32),
        compiler_params=pltpu.CompilerParams(
            dimension_semantics=("parallel",), vmem_limit_bytes=VMEM_LIMIT),
        name="ffn_ln",
    )(x, wg, wu, wo, g.reshape(1, D), b.reshape(1, D))


Q_COL0 = 2 * LRU_WIDTH
KCMP_COL0 = Q_COL0 + NSA_WIDTH
VCMP_COL0 = KCMP_COL0 + KV_WIDTH
KSLC_COL0 = VCMP_COL0 + KV_WIDTH
VSLC_COL0 = KSLC_COL0 + KV_WIDTH
KWIN_COL0 = VSLC_COL0 + KV_WIDTH
VWIN_COL0 = KWIN_COL0 + KV_WIDTH
GATE_COL0 = VWIN_COL0 + KV_WIDTH
VT_ROW0 = NSA_WIDTH
GT_ROW0 = VT_ROW0 + 2 * KV_WIDTH
WT_ROWS = GT_ROW0 + NSA_KV_HEADS * GATE_ROWS


def _rope_chunk(xc, cos, sin_signed):
    lane = lax.broadcasted_iota(jnp.int32, xc.shape, 1)
    first = (lane & (HEAD_DIM - 1)) < (HEAD_DIM // 2)
    partner = jnp.where(first, pltpu.roll(xc, LANES - HEAD_DIM // 2, axis=1),
                        pltpu.roll(xc, HEAD_DIM // 2, axis=1))
    return xc * cos + partner * sin_signed


def _proj_kernel(h_ref, wn_ref, wt_ref, cos_ref, sin_ref, cost_ref, sint_ref,
                 lru_ref, qt_ref, cmp_ref, k_ref, vt_ref, gt_ref, *, seq_tiles):
    tm = h_ref.shape[0]
    half = HEAD_DIM // 2
    hb = h_ref[...].astype(BF16)
    p = _dot(hb, wn_ref[...])
    pt = _dot_nt(wt_ref[...], hb)
    lru_ref[...] = p[:, :Q_COL0]

    cos_t = cost_ref[...]
    sin_t = sint_ref[...]
    scale = HEAD_DIM ** -0.5 * LOG2E
    for hd in range(NSA_Q_HEADS):
        x1 = pt[hd * HEAD_DIM:hd * HEAD_DIM + half]
        x2 = pt[hd * HEAD_DIM + half:(hd + 1) * HEAD_DIM]
        qt_ref[hd] = (jnp.concatenate([x1 * cos_t - x2 * sin_t, x2 * cos_t + x1 * sin_t], axis=0)
                      * scale).astype(BF16)
    ones = jnp.ones((V_ROWS - HEAD_DIM, tm), F32)
    for c in range(2 * NSA_KV_HEADS):
        v = pt[VT_ROW0 + c * HEAD_DIM:VT_ROW0 + (c + 1) * HEAD_DIM]
        vt_ref[c] = jnp.concatenate([v, ones], axis=0).astype(BF16)
    for g in range(NSA_KV_HEADS):
        gt_ref[g] = jax.nn.sigmoid(pt[GT_ROW0 + g * GATE_ROWS:GT_ROW0 + (g + 1) * GATE_ROWS])

    cos = cos_ref[...]
    sin = sin_ref[...]
    kcmp = _rope_chunk(p[:, Q_COL0:Q_COL0 + LANES], cos, sin).astype(BF16)
    vcmp = p[:, Q_COL0 + LANES:Q_COL0 + 2 * LANES].astype(BF16)
    for g in range(NSA_KV_HEADS):
        cmp_ref[g] = kcmp[:, g * HEAD_DIM:(g + 1) * HEAD_DIM]
        cmp_ref[NSA_KV_HEADS + g] = vcmp[:, g * HEAD_DIM:(g + 1) * HEAD_DIM]
    lane = lax.broadcasted_iota(jnp.int32, (tm, LANES), 1)
    low = lane < HEAD_DIM
    pos = (pl.program_id(0) % seq_tiles) * tm + lax.broadcasted_iota(jnp.int32, (tm, LANES), 0)
    ext_blk = jnp.where(lane - HEAD_DIM == (pos >> SLC_SHIFT), NEG, 0.0)
    ext_zero = jnp.zeros((tm, LANES), F32)
    for c, ext in enumerate((ext_blk, ext_zero)):
        xc = _rope_chunk(p[:, Q_COL0 + (2 + c) * LANES:Q_COL0 + (3 + c) * LANES], cos, sin)
        k_ref[2 * c] = jnp.where(low, xc, ext).astype(BF16)
        k_ref[2 * c + 1] = jnp.where(low, pltpu.roll(xc, HEAD_DIM, axis=1), ext).astype(BF16)


def _proj(h, w_in, S, *, tm=512):
    T, D = h.shape
    G = NSA_KV_HEADS
    col = lambda c0, n: w_in[:, c0:c0 + n]
    w_nat = jnp.concatenate(
        [col(0, Q_COL0), col(KCMP_COL0, KV_WIDTH), col(VCMP_COL0, KV_WIDTH),
         col(KSLC_COL0, KV_WIDTH), col(KWIN_COL0, KV_WIDTH)], axis=1).astype(BF16)
    gate_rows = [jnp.pad(col(GATE_COL0 + g * N_GATE, N_GATE).T, ((0, GATE_ROWS - N_GATE), (0, 0)))
                 for g in range(G)]
    w_t = jnp.concatenate(
        [col(Q_COL0, NSA_WIDTH).T, col(VSLC_COL0, KV_WIDTH).T, col(VWIN_COL0, KV_WIDTH).T]
        + gate_rows, axis=0).astype(BF16)
    n_nat = w_nat.shape[1]

    half = HEAD_DIM // 2
    inv = ROPE_THETA ** (-jnp.arange(half, dtype=F32) / half)
    ang = jnp.arange(S, dtype=F32)[:, None] * inv[None, :]
    cos = jnp.cos(ang)
    sin = jnp.sin(ang)
    cos_n = jnp.concatenate([cos, cos, cos, cos], axis=1)
    sin_n = jnp.concatenate([-sin, sin, -sin, sin], axis=1)

    nS = S // tm
    return pl.pallas_call(
        functools.partial(_proj_kernel, seq_tiles=nS),
        grid=(T // tm,),
        in_specs=[
            pl.BlockSpec((tm, D), lambda i: (i, 0)),
            _resident((D, n_nat)), _resident((WT_ROWS, D)),
            pl.BlockSpec((tm, LANES), lambda i: (i % nS, 0)),
            pl.BlockSpec((tm, LANES), lambda i: (i % nS, 0)),
            pl.BlockSpec((half, tm), lambda i: (0, i % nS)),
            pl.BlockSpec((half, tm), lambda i: (0, i % nS)),
        ],
        out_specs=[
            pl.BlockSpec((tm, Q_COL0), lambda i: (i, 0)),
            pl.BlockSpec((NSA_Q_HEADS, HEAD_DIM, tm), lambda i: (0, 0, i)),
            pl.BlockSpec((2 * G, tm, HEAD_DIM), lambda i: (0, i, 0)),
            pl.BlockSpec((2 * G, tm, LANES), lambda i: (0, i, 0)),
            pl.BlockSpec((2 * G, V_ROWS, tm), lambda i: (0, 0, i)),
            pl.BlockSpec((G, GATE_ROWS, tm), lambda i: (0, 0, i)),
        ],
        out_shape=[
            jax.ShapeDtypeStruct((T, Q_COL0), F32),
            jax.ShapeDtypeStruct((NSA_Q_HEADS, HEAD_DIM, T), BF16),
            jax.ShapeDtypeStruct((2 * G, T, HEAD_DIM), BF16),
            jax.ShapeDtypeStruct((2 * G, T, LANES), BF16),
            jax.ShapeDtypeStruct((2 * G, V_ROWS, T), BF16),
            jax.ShapeDtypeStruct((G, GATE_ROWS, T), F32),
        ],
        compiler_params=pltpu.CompilerParams(
            dimension_semantics=("parallel",), vmem_limit_bytes=VMEM_LIMIT),
        name="proj",
    )(h, w_nat, w_t, cos_n, sin_n, cos.T, sin.T)


SUBLANES = 8
SCAN_UNROLL = 4
LRU_ROWS = 512


def _lru_kernel(x_ref, gate_ref, cw_ref, cb_ref, wa_ref, ba_ref, wx_ref, bx_ref, lam_ref,
                y_ref, x_scr, a_scr, b_scr):
    S, C = y_ref.shape
    neg_lam = -lam_ref[...]
    softplus = jnp.maximum(neg_lam, 0.0) + jnp.log1p(jnp.exp(-jnp.abs(neg_lam)))

    x_scr[0:SUBLANES, :] = jnp.zeros((SUBLANES, C), F32)
    x_scr[SUBLANES:, :] = x_ref[...]
    for r0 in range(0, S, LRU_ROWS):
        xc = cb_ref[...]
        for d in range(CONV_WIDTH):
            xs = x_scr[SUBLANES + r0 - d:SUBLANES + r0 - d + LRU_ROWS, :]
            xc = xc + xs * cw_ref[CONV_WIDTH - 1 - d:CONV_WIDTH - d, :]
        xb = xc.astype(BF16)
        gate_pre = lambda w_ref: jnp.concatenate(
            [_dot(xb[:, c * LANES:(c + 1) * LANES], w_ref[c]) for c in range(C // LANES)], axis=1)
        r = jax.nn.sigmoid(gate_pre(wa_ref) + ba_ref[...])
        ig = jax.nn.sigmoid(gate_pre(wx_ref) + bx_ref[...])
        log_a = (-LRU_C) * r * softplus
        a = jnp.exp(log_a)
        z = jnp.tanh(-log_a) * (a * a + 1.0)
        mult = z * lax.rsqrt(jnp.maximum(z, F32_TINY))
        if r0 == 0:
            row = lax.broadcasted_iota(jnp.int32, mult.shape, 0)
            mult = jnp.where(row == 0, 1.0, mult)
        a_scr[r0:r0 + LRU_ROWS, :] = a
        b_scr[r0:r0 + LRU_ROWS, :] = mult * (ig * xc)

    sub = lax.broadcasted_iota(jnp.int32, (SUBLANES, C), 0)

    def scan_tile(g, h_prev):
        rows = pl.ds(pl.multiple_of(g * SUBLANES, SUBLANES), SUBLANES)
        at = a_scr[rows, :]
        bt = b_scr[rows, :]
        d = 1
        while d < SUBLANES:
            keep = sub >= d
            a_sh = jnp.where(keep, pltpu.roll(at, d, axis=0), 1.0)
            b_sh = jnp.where(keep, pltpu.roll(bt, d, axis=0), 0.0)
            bt = at * b_sh + bt
            at = at * a_sh
            d *= 2
        h = bt + at * h_prev
        b_scr[rows, :] = h
        return jnp.broadcast_to(h[SUBLANES - 1:SUBLANES, :], (SUBLANES, C))

    def scan_body(i, h_prev):
        for u in range(SCAN_UNROLL):
            h_prev = scan_tile(i * SCAN_UNROLL + u, h_prev)
        return h_prev

    lax.fori_loop(0, S // (SUBLANES * SCAN_UNROLL), scan_body, jnp.zeros((SUBLANES, C), F32))

    for r0 in range(0, S, LRU_ROWS):
        rows = slice(r0, r0 + LRU_ROWS)
        y_ref[rows, :] = b_scr[rows, :] * jax.nn.gelu(gate_ref[rows, :])


def _lru(lru, conv_w, conv_b, w_a, b_a, w_x, b_x, lam, B, S):
    T = lru.shape[0]
    C = LRU_WIDTH
    n_ch = C // LANES
    assert S % (SUBLANES * SCAN_UNROLL) == 0 and S % LRU_ROWS == 0

    def blockdiag(w):
        w = w.reshape(n_ch, 2, LRU_BLOCK, LRU_BLOCK)
        z = jnp.zeros((n_ch, LRU_BLOCK, LRU_BLOCK), w.dtype)
        top = jnp.concatenate([w[:, 0], z], axis=2)
        bot = jnp.concatenate([z, w[:, 1]], axis=2)
        return jnp.concatenate([top, bot], axis=1).astype(BF16)

    vec = lambda v: v.reshape(1, C)
    return pl.pallas_call(
        _lru_kernel,
        grid=(B,),
        in_specs=[
            pl.BlockSpec((S, C), lambda b: (b, 0)),
            pl.BlockSpec((S, C), lambda b: (b, 1)),
            _resident((CONV_WIDTH, C)), _resident((1, C)),
            _resident((n_ch, LANES, LANES)), _resident((1, C)),
            _resident((n_ch, LANES, LANES)), _resident((1, C)),
            _resident((1, C)),
        ],
        out_specs=pl.BlockSpec((S, C), lambda b: (b, 0)),
        out_shape=jax.ShapeDtypeStruct((T, C), F32),
        scratch_shapes=[pltpu.VMEM((SUBLANES + S, C), F32),
                        pltpu.VMEM((S, C), F32),
                        pltpu.VMEM((S, C), F32)],
        compiler_params=pltpu.CompilerParams(
            dimension_semantics=("parallel",), vmem_limit_bytes=VMEM_LIMIT),
        name="lru",
    )(lru, lru, conv_w, vec(conv_b), blockdiag(w_a), vec(b_a.reshape(-1)),
      blockdiag(w_x), vec(b_x.reshape(-1)), vec(lam))


def _cmp_kernel(seg_ref, pek_ref, w1k_ref, w2k_ref, pev_ref, w1v_ref, w2vt_ref, kc_ref, vct_ref):
    n_seg = seg_ref.shape[2]
    half = CMP_STRIDE * HEAD_DIM

    def hidden(idx, pe_ref, w1_ref):
        seg = seg_ref[idx, 0]
        first = _dot(seg, w1_ref[0:half, :])
        second = _dot(seg, w1_ref[half:2 * half, :])
        bias = _dot(pe_ref[...], w1_ref[...])[0:1, :]
        return _silu(first + pltpu.roll(second, n_seg - 1, axis=0) + bias).astype(BF16)

    for g in range(NSA_KV_HEADS):
        tok = _dot(hidden(g, pek_ref, w1k_ref), w2k_ref[...])
        kc_ref[0, g] = jnp.concatenate([tok, jnp.zeros_like(tok)], axis=1).astype(BF16)
        vct_ref[0, g] = _dot_nt(w2vt_ref[...], hidden(NSA_KV_HEADS + g, pev_ref, w1v_ref)
                                ).astype(BF16)


def _compress(cmp_in, pe_k, w1_k, w2_k, pe_v, w1_v, w2_v, B, S):
    G = NSA_KV_HEADS
    n_seg = S // CMP_STRIDE
    seg = cmp_in.reshape(2 * G, B, n_seg, CMP_STRIDE * HEAD_DIM)
    flat = CMP_BLOCK * HEAD_DIM
    pe8 = lambda pe: jnp.broadcast_to(pe.reshape(1, -1), (8, flat)).astype(BF16)
    return pl.pallas_call(
        _cmp_kernel,
        grid=(B,),
        in_specs=[
            pl.BlockSpec((2 * G, 1, n_seg, CMP_STRIDE * HEAD_DIM), lambda b: (0, b, 0, 0)),
            _resident((8, flat)), _resident((flat, CMP_HIDDEN)), _resident((CMP_HIDDEN, HEAD_DIM)),
            _resident((8, flat)), _resident((flat, CMP_HIDDEN)), _resident((HEAD_DIM, CMP_HIDDEN)),
        ],
        out_specs=[
            pl.BlockSpec((1, G, n_seg, LANES), lambda b: (b, 0, 0, 0)),
            pl.BlockSpec((1, G, HEAD_DIM, n_seg), lambda b: (b, 0, 0, 0)),
        ],
        out_shape=[
            jax.ShapeDtypeStruct((B, G, n_seg, LANES), BF16),
            jax.ShapeDtypeStruct((B, G, HEAD_DIM, n_seg), BF16),
        ],
        compiler_params=pltpu.CompilerParams(
            dimension_semantics=("parallel",), vmem_limit_bytes=VMEM_LIMIT),
        name="compress",
    )(seg, pe8(pe_k), w1_k.astype(BF16), w2_k.astype(BF16),
      pe8(pe_v), w1_v.astype(BF16), w2_v.T.astype(BF16))


def _attn_kernel(qt_ref, kc_ref, vct_ref, ks_ref, vst_ref, kw_ref, vwt_ref, gt_ref, ovl_ref,
                 bias_ref, o_ref, *, tq, n_tiles):
    for n in range(n_tiles):
        pl.when(pl.program_id(2) == n)(functools.partial(
            _attn_step, n, qt_ref, kc_ref, vct_ref, ks_ref, vst_ref, kw_ref, vwt_ref, gt_ref,
            ovl_ref, bias_ref, o_ref, tq))


CHUNK_TILES = 2


def _interleave(*stages):
    live = list(stages)
    while live:
        for st in list(live):
            try:
                next(st)
            except StopIteration:
                live.remove(st)


def _tile_bias(bias_ref, t0, k0, behind):
    if k0 == t0:
        return bias_ref[0]
    if behind is not None and k0 == t0 - behind:
        return bias_ref[1]
    return None


def _key_chunks(k_ref, vt_ref, k0, n_keys, tile, q, bias_fn):
    starts = list(range(k0, k0 + n_keys, tile))
    return [dict(k_ref=k_ref, vt_ref=vt_ref, tile=tile, q=q, bias_fn=bias_fn,
                 starts=starts[i:i + CHUNK_TILES]) for i in range(0, len(starts), CHUNK_TILES)]


def _scores_stage(ch):
    tiles, m = [], None
    for k0 in ch["starts"]:
        s = _dot(ch["k_ref"][k0:k0 + ch["tile"], :], ch["q"])
        bias = ch["bias_fn"](k0)
        if bias is not None:
            s = s + bias
        cm = jnp.max(s, axis=0, keepdims=True)
        m = cm if m is None else jnp.maximum(m, cm)
        tiles.append(s)
        yield
    ch["s"], ch["m"] = tiles, m


def _pv_stage(ch, heads, tq):
    acc = [None] * heads
    for k0, s in zip(ch["starts"], ch["s"]):
        p = jnp.exp2(s - ch["m"]).astype(BF16)
        vt = ch["vt_ref"][:, k0:k0 + ch["tile"]]
        for r in range(heads):
            d = _dot(vt, p[:, r * tq:(r + 1) * tq])
            acc[r] = d if acc[r] is None else acc[r] + d
        yield
    ch["acc"] = acc


def _merge_chunks(chunks, heads, tq):
    if len(chunks) == 1:
        return chunks[0]["acc"]
    m = functools.reduce(jnp.maximum, [ch["m"] for ch in chunks])
    acc = [None] * heads
    for ch in chunks:
        w = jnp.exp2(ch["m"] - m)
        for r in range(heads):
            term = ch["acc"][r] * w[:, r * tq:(r + 1) * tq]
            acc[r] = term if acc[r] is None else acc[r] + term
    return acc


def _rank_stage(out, score, jblk):
    rank = jnp.zeros(score.shape, jnp.int32)
    for kb in range(score.shape[0]):
        sk = score[kb:kb + 1, :]
        ahead = (sk > score) | ((sk == score) & (jblk > kb))
        rank = rank + ahead.astype(jnp.int32)
        if kb % 4 == 3:
            yield
    out["rank"] = rank


def _attn_step(n, qt_ref, kc_ref, vct_ref, ks_ref, vst_ref, kw_ref, vwt_ref, gt_ref, ovl_ref,
               bias_ref, o_ref, tq):
    R = NSA_GROUP
    t0 = n * tq
    n_cmp = kc_ref.shape[0]
    n_blk = ovl_ref.shape[0]
    q_all = jnp.concatenate([qt_ref[r] for r in range(R)], axis=1)
    q_pad = jnp.concatenate([q_all, jnp.zeros((LANES - HEAD_DIM, R * tq), BF16)], axis=0)
    head = lambda a, r: a[:, r * tq:(r + 1) * tq]

    kw0 = max(t0 - WINDOW, 0)
    win = _key_chunks(kw_ref, vwt_ref, kw0, t0 + tq - kw0, tq, q_pad,
                      lambda k0: _tile_bias(bias_ref, t0, k0, WINDOW))
    win_scores = _scores_stage(win[0])
    next(win_scores)

    tpos_c = t0 + (lax.broadcasted_iota(jnp.int32, (n_cmp, R * tq), 1) & (tq - 1))
    cend = lax.broadcasted_iota(jnp.int32, (n_cmp, R * tq), 0) * CMP_STRIDE + (CMP_BLOCK - 1)
    s = _dot(kc_ref[...], q_pad) + jnp.where(cend <= tpos_c, 0.0, NEG)
    e = jnp.exp2(s - jnp.max(s, axis=0, keepdims=True))
    p_cmp = e * (1.0 / jnp.sum(e, axis=0, keepdims=True))
    if t0 < CMP_BLOCK - 1:
        p_cmp = jnp.where(tpos_c >= CMP_BLOCK - 1, p_cmp, 0.0)
    p_cmp_b = p_cmp.astype(BF16)
    vct = vct_ref[...]
    o_cmp = [_dot(vct, head(p_cmp_b, r)) for r in range(R)]
    psum = functools.reduce(jnp.add, [head(p_cmp, r) for r in range(R)])

    p_hi = psum.astype(BF16)
    p_lo = (psum - p_hi.astype(F32)).astype(BF16)
    ovl = ovl_ref[...]
    imp = _dot(ovl, p_hi) + _dot(ovl, p_lo)
    jblk = lax.broadcasted_iota(jnp.int32, (n_blk, tq), 0)
    tpos_b = t0 + lax.broadcasted_iota(jnp.int32, (n_blk, tq), 1)
    blk_valid = jblk * SLC_BLOCK <= tpos_b
    back = (tpos_b >> SLC_SHIFT) - jblk
    forced = (jblk == 0) | ((back >= 0) & (back < N_LOCAL_BLOCKS))
    score = jnp.where(blk_valid, jnp.where(forced, jnp.inf, imp), -jnp.inf)
    sel = {}
    _interleave(_rank_stage(sel, score, jblk), win_scores)
    unsel = (1.0 - (blk_valid & (sel["rank"] < SLC_TOP_N)).astype(F32)).astype(BF16)
    sel_ext = jnp.concatenate(
        [unsel, jnp.zeros((LANES - HEAD_DIM - n_blk, tq), BF16)], axis=0)
    q_sel = jnp.concatenate([q_all, jnp.concatenate([sel_ext] * R, axis=1)], axis=0)

    slc = _key_chunks(ks_ref, vst_ref, 0, t0 + tq, tq, q_sel,
                      lambda k0: _tile_bias(bias_ref, t0, k0, None))
    chunks = win + slc
    for idx, ch in enumerate(chunks):
        stages = [_pv_stage(ch, R, tq)]
        if idx + 1 < len(chunks):
            stages.append(_scores_stage(chunks[idx + 1]))
        _interleave(*stages)
    acc_w = _merge_chunks(win, R, tq)
    acc_s = _merge_chunks(slc, R, tq)

    gt = gt_ref[...]
    outs = []
    for r in range(R):
        g_cmp = gt[3 * r:3 * r + 1]
        g_slc = gt[3 * r + 1:3 * r + 2] * (1.0 / acc_s[r][HEAD_DIM:HEAD_DIM + 1])
        g_win = gt[3 * r + 2:3 * r + 3] * (1.0 / acc_w[r][HEAD_DIM:HEAD_DIM + 1])
        outs.append(g_cmp * o_cmp[r] + g_slc * acc_s[r][:HEAD_DIM] + g_win * acc_w[r][:HEAD_DIM])
    o_ref[...] = jnp.concatenate(outs, axis=0).T


def _overlap_t(n_cmp_pad, n_blk):
    cs = np.arange(n_cmp_pad) * CMP_STRIDE
    ce = cs + CMP_BLOCK - 1
    ss = np.arange(n_blk) * SLC_BLOCK
    se = ss + SLC_BLOCK - 1
    return ((cs[None, :] <= se[:, None]) & (ce[None, :] >= ss[:, None])).astype(np.float32)


def _attention(qt, k, vt, kc, vct, gates_t, B, S, *, tq=256):
    T = qt.shape[2]
    nq = S // tq
    n_cmp = S // CMP_STRIDE
    n_blk = S // SLC_BLOCK
    G = NSA_KV_HEADS
    ovl = jnp.asarray(_overlap_t(n_cmp, n_blk), BF16)
    assert WINDOW % tq == 0 and WINDOW + tq <= S
    assert HEAD_DIM + n_blk <= LANES
    assert tq & (tq - 1) == 0
    key_off = np.arange(tq)[:, None]
    q_off = np.tile(np.arange(tq), NSA_GROUP)[None, :]
    bias = jnp.asarray(np.stack([np.where(key_off <= q_off, 0.0, NEG),
                                 np.where(key_off > q_off, 0.0, NEG)]), F32)
    k_spec = lambda c: pl.BlockSpec((None, S, LANES), lambda b, g, i: (G * c + g, b, 0))
    vt_spec = lambda c: pl.BlockSpec((None, V_ROWS, S), lambda b, g, i: (G * c + g, 0, b))
    return pl.pallas_call(
        functools.partial(_attn_kernel, tq=tq, n_tiles=nq),
        grid=(B, G, nq),
        in_specs=[
            pl.BlockSpec((NSA_GROUP, HEAD_DIM, tq), lambda b, g, i: (g, 0, b * nq + i)),
            pl.BlockSpec((None, None, n_cmp, LANES), lambda b, g, i: (b, g, 0, 0)),
            pl.BlockSpec((None, None, HEAD_DIM, n_cmp), lambda b, g, i: (b, g, 0, 0)),
            k_spec(0), vt_spec(0), k_spec(1), vt_spec(1),
            pl.BlockSpec((None, GATE_ROWS, tq), lambda b, g, i: (g, 0, b * nq + i)),
            _resident((n_blk, n_cmp)),
            _resident((2, tq, NSA_GROUP * tq)),
        ],
        out_specs=pl.BlockSpec((tq, NSA_GROUP * HEAD_DIM), lambda b, g, i: (b * nq + i, g)),
        out_shape=jax.ShapeDtypeStruct((T, NSA_WIDTH), F32),
        compiler_params=pltpu.CompilerParams(
            dimension_semantics=("parallel", "parallel", "arbitrary"),
            vmem_limit_bytes=VMEM_LIMIT),
        name="nsa_attn",
    )(qt, kc, vct, k, vt, k, vt, gates_t, ovl, bias)


def _out_ffn_kernel(ylru_ref, ynsa_ref, h_ref, gl_ref, gn_ref, w_ref, g2_ref, b2_ref,
                    wg_ref, wu_ref, wo_ref, g3_ref, b3_ref, o_ref):
    yl = _rms_norm(ylru_ref[...], gl_ref[...]).astype(BF16)
    yn = _rms_norm(ynsa_ref[...], gn_ref[...]).astype(BF16)
    mix = _dot(yl, w_ref[0:LRU_WIDTH, :]) + _dot(yn, w_ref[LRU_WIDTH:, :])
    h2 = _layer_norm(ALPHA * h_ref[...] + mix, g2_ref[...], b2_ref[...])
    o_ref[...] = _ffn_half_step(h2, wg_ref, wu_ref, wo_ref, g3_ref, b3_ref)


def _out_proj_ffn(y_lru, y_nsa, h, gn_lru, gn_nsa, w_mix, g2, b2, w_in, w_out, g3, b3, *,
                  tm=512):
    T, D = h.shape
    row = lambda v: v.reshape(1, -1)
    return pl.pallas_call(
        _out_ffn_kernel,
        grid=(T // tm,),
        in_specs=[
            pl.BlockSpec((tm, LRU_WIDTH), lambda i: (i, 0)),
            pl.BlockSpec((tm, NSA_WIDTH), lambda i: (i, 0)),
            pl.BlockSpec((tm, D), lambda i: (i, 0)),
            _resident((1, LRU_WIDTH)), _resident((1, NSA_WIDTH)),
            _resident((LRU_WIDTH + NSA_WIDTH, D)),
            _resident((1, D)), _resident((1, D)),
            _resident((D, D_FF)), _resident((D, D_FF)), _resident((D_FF, D)),
            _resident((1, D)), _resident((1, D)),
        ],
        out_specs=pl.BlockSpec((tm, D), lambda i: (i, 0)),
        out_shape=jax.ShapeDtypeStruct((T, D), F32),
        compiler_params=pltpu.CompilerParams(
            dimension_semantics=("parallel",), vmem_limit_bytes=VMEM_LIMIT),
        name="out_proj_ffn",
    )(y_lru, y_nsa, h, row(gn_lru), row(gn_nsa), w_mix.astype(BF16), row(g2), row(b2),
      w_in[:, :D_FF].astype(BF16), w_in[:, D_FF:].astype(BF16), w_out.astype(BF16),
      row(g3), row(b3))


def kernel(x, ffn1_w_in, ffn1_w_out, ln1_g, ln1_b, mix_w_in, conv_w, conv_b, lru_w_a, lru_b_a,
           lru_w_x, lru_b_x, lru_lam, cmp_pe_k, cmp_w1_k, cmp_w2_k, cmp_pe_v, cmp_w1_v,
           cmp_w2_v, gn_lru, gn_nsa, mix_w_out, ln2_g, ln2_b, ffn2_w_in, ffn2_w_out, ln3_g,
           ln3_b):
    B, S, D = x.shape
    h = x.reshape(B * S, D)
    for l in range(DEPTH):
        h = _ffn_ln(h, ffn1_w_in[l], ffn1_w_out[l], ln1_g[l], ln1_b[l])
        lru, qt, cmp_in, k, vt, gates_t = _proj(h, mix_w_in[l], S)
        y_lru = _lru(lru, conv_w[l], conv_b[l], lru_w_a[l], lru_b_a[l], lru_w_x[l], lru_b_x[l],
                     lru_lam[l], B, S)
        kc, vct = _compress(cmp_in, cmp_pe_k[l], cmp_w1_k[l], cmp_w2_k[l],
                            cmp_pe_v[l], cmp_w1_v[l], cmp_w2_v[l], B, S)
        y_nsa = _attention(qt, k, vt, kc, vct, gates_t, B, S)
        h = _out_proj_ffn(y_lru, y_nsa, h, gn_lru[l], gn_nsa[l], mix_w_out[l], ln2_g[l], ln2_b[l],
                          ffn2_w_in[l], ffn2_w_out[l], ln3_g[l], ln3_b[l])
    return h.reshape(B, S, D)
```

```python
import functools

import numpy as np
import jax
import jax.numpy as jnp
from jax import lax
from jax.experimental import pallas as pl
from jax.experimental.pallas import tpu as pltpu

F32 = jnp.float32
BF16 = jnp.bfloat16

D_MODEL = 1024
LRU_WIDTH = 512
LRU_HEADS = 8
LRU_BLOCK = 64
CONV_WIDTH = 4
LRU_C = 8.0
NSA_Q_HEADS = 8
NSA_KV_HEADS = 2
NSA_GROUP = 4
HEAD_DIM = 64
NSA_WIDTH = 512
KV_WIDTH = 128
CMP_BLOCK = 32
CMP_STRIDE = 16
CMP_HIDDEN = 256
SLC_BLOCK = 64
SLC_SHIFT = 6
SLC_TOP_N = 16
N_LOCAL_BLOCKS = 2
WINDOW = 512
ROPE_THETA = 10000.0
D_FF = 2816
DEPTH = 1
ALPHA = (2.0 * DEPTH) ** 0.25
LN_EPS = 1e-5
RMS_EPS = 1e-6
NEG = -1e30
LOG2E = 1.4426950408889634
F32_TINY = 1.1754943508222875e-38

LANES = 128
BF16_ROWS = 16
VMEM_LIMIT = 48 * 1024 * 1024
N_GATE = 3 * NSA_GROUP
GATE_ROWS = 16
V_ROWS = HEAD_DIM + BF16_ROWS


def _dot(a, b):
    return jnp.dot(a, b, preferred_element_type=F32)


def _dot_nt(a, b):
    return lax.dot_general(a, b, (((1,), (1,)), ((), ())), preferred_element_type=F32)


def _layer_norm(y, g, b):
    mu = jnp.mean(y, axis=-1, keepdims=True)
    d = y - mu
    var = jnp.mean(d * d, axis=-1, keepdims=True)
    return d * lax.rsqrt(var + LN_EPS) * g + b


def _rms_norm(y, g):
    return y * lax.rsqrt(jnp.mean(y * y, axis=-1, keepdims=True) + RMS_EPS) * g


def _silu(x):
    return x * jax.nn.sigmoid(x)


def _resident(shape):
    return pl.BlockSpec(shape, lambda *_: (0,) * len(shape), pipeline_mode=pl.Buffered(1))


def _ffn_half_step(x, wg_ref, wu_ref, wo_ref, g_ref, b_ref):
    xb = x.astype(BF16)
    gate = _dot(xb, wg_ref[...])
    up = _dot(xb, wu_ref[...])
    act = (_silu(gate) * up).astype(BF16)
    y = ALPHA * x + 0.5 * _dot(act, wo_ref[...])
    return _layer_norm(y, g_ref[...], b_ref[...])


def _ffn_ln_kernel(x_ref, wg_ref, wu_ref, wo_ref, g_ref, b_ref, o_ref):
    o_ref[...] = _ffn_half_step(x_ref[...], wg_ref, wu_ref, wo_ref, g_ref, b_ref)


def _ffn_ln(x, w_in, w_out, g, b, *, tm=1024):
    T, D = x.shape
    wg = w_in[:, :D_FF].astype(BF16)
    wu = w_in[:, D_FF:].astype(BF16)
    wo = w_out.astype(BF16)
    return pl.pallas_call(
        _ffn_ln_kernel,
        grid=(T // tm,),
        in_specs=[
            pl.BlockSpec((tm, D), lambda i: (i, 0)),
            _resident((D, D_FF)), _resident((D, D_FF)), _resident((D_FF, D)),
            _resident((1, D)), _resident((1, D)),
        ],
        out_specs=pl.BlockSpec((tm, D), lambda i: (i, 0)),
        out_shape=jax.ShapeDtypeStruct((T, D), F32),
        compiler_params=pltpu.CompilerParams(
            dimension_semantics=("parallel",), vmem_limit_bytes=VMEM_LIMIT),
        name="ffn_ln",
    )(x, wg, wu, wo, g.reshape(1, D), b.reshape(1, D))


Q_COL0 = 2 * LRU_WIDTH
KCMP_COL0 = Q_COL0 + NSA_WIDTH
VCMP_COL0 = KCMP_COL0 + KV_WIDTH
KSLC_COL0 = VCMP_COL0 + KV_WIDTH
VSLC_COL0 = KSLC_COL0 + KV_WIDTH
KWIN_COL0 = VSLC_COL0 + KV_WIDTH
VWIN_COL0 = KWIN_COL0 + KV_WIDTH
GATE_COL0 = VWIN_COL0 + KV_WIDTH
VT_ROW0 = NSA_WIDTH
GT_ROW0 = VT_ROW0 + 2 * KV_WIDTH
WT_ROWS = GT_ROW0 + NSA_KV_HEADS * GATE_ROWS


def _rope_chunk(xc, cos, sin_signed):
    lane = lax.broadcasted_iota(jnp.int32, xc.shape, 1)
    first = (lane & (HEAD_DIM - 1)) < (HEAD_DIM // 2)
    partner = jnp.where(first, pltpu.roll(xc, LANES - HEAD_DIM // 2, axis=1),
                        pltpu.roll(xc, HEAD_DIM // 2, axis=1))
    return xc * cos + partner * sin_signed


def _proj_kernel(h_ref, wn_ref, wt_ref, cos_ref, sin_ref, cost_ref, sint_ref,
                 lru_ref, qt_ref, cmp_ref, k_ref, vt_ref, gt_ref, *, seq_tiles):
    tm = h_ref.shape[0]
    half = HEAD_DIM // 2
    hb = h_ref[...].astype(BF16)
    p = _dot(hb, wn_ref[...])
    pt = _dot_nt(wt_ref[...], hb)
    lru_ref[...] = p[:, :Q_COL0]

    cos_t = cost_ref[...]
    sin_t = sint_ref[...]
    scale = HEAD_DIM ** -0.5 * LOG2E
    for hd in range(NSA_Q_HEADS):
        x1 = pt[hd * HEAD_DIM:hd * HEAD_DIM + half]
        x2 = pt[hd * HEAD_DIM + half:(hd + 1) * HEAD_DIM]
        qt_ref[hd] = (jnp.concatenate([x1 * cos_t - x2 * sin_t, x2 * cos_t + x1 * sin_t], axis=0)
                      * scale).astype(BF16)
    ones = jnp.ones((V_ROWS - HEAD_DIM, tm), F32)
    for c in range(2 * NSA_KV_HEADS):
        v = pt[VT_ROW0 + c * HEAD_DIM:VT_ROW0 + (c + 1) * HEAD_DIM]
        vt_ref[c] = jnp.concatenate([v, ones], axis=0).astype(BF16)
    for g in range(NSA_KV_HEADS):
        gt_ref[g] = jax.nn.sigmoid(pt[GT_ROW0 + g * GATE_ROWS:GT_ROW0 + (g + 1) * GATE_ROWS])

    cos = cos_ref[...]
    sin = sin_ref[...]
    kcmp = _rope_chunk(p[:, Q_COL0:Q_COL0 + LANES], cos, sin).astype(BF16)
    vcmp = p[:, Q_COL0 + LANES:Q_COL0 + 2 * LANES].astype(BF16)
    for g in range(NSA_KV_HEADS):
        cmp_ref[g] = kcmp[:, g * HEAD_DIM:(g + 1) * HEAD_DIM]
        cmp_ref[NSA_KV_HEADS + g] = vcmp[:, g * HEAD_DIM:(g + 1) * HEAD_DIM]
    lane = lax.broadcasted_iota(jnp.int32, (tm, LANES), 1)
    low = lane < HEAD_DIM
    pos = (pl.program_id(0) % seq_tiles) * tm + lax.broadcasted_iota(jnp.int32, (tm, LANES), 0)
    ext_blk = jnp.where(lane - HEAD_DIM == (pos >> SLC_SHIFT), NEG, 0.0)
    ext_zero = jnp.zeros((tm, LANES), F32)
    for c, ext in enumerate((ext_blk, ext_zero)):
        xc = _rope_chunk(p[:, Q_COL0 + (2 + c) * LANES:Q_COL0 + (3 + c) * LANES], cos, sin)
        k_ref[2 * c] = jnp.where(low, xc, ext).astype(BF16)
        k_ref[2 * c + 1] = jnp.where(low, pltpu.roll(xc, HEAD_DIM, axis=1), ext).astype(BF16)


def _proj(h, w_in, S, *, tm=512):
    T, D = h.shape
    G = NSA_KV_HEADS
    col = lambda c0, n: w_in[:, c0:c0 + n]
    w_nat = jnp.concatenate(
        [col(0, Q_COL0), col(KCMP_COL0, KV_WIDTH), col(VCMP_COL0, KV_WIDTH),
         col(KSLC_COL0, KV_WIDTH), col(KWIN_COL0, KV_WIDTH)], axis=1).astype(BF16)
    gate_rows = [jnp.pad(col(GATE_COL0 + g * N_GATE, N_GATE).T, ((0, GATE_ROWS - N_GATE), (0, 0)))
                 for g in range(G)]
    w_t = jnp.concatenate(
        [col(Q_COL0, NSA_WIDTH).T, col(VSLC_COL0, KV_WIDTH).T, col(VWIN_COL0, KV_WIDTH).T]
        + gate_rows, axis=0).astype(BF16)
    n_nat = w_nat.shape[1]

    half = HEAD_DIM // 2
    inv = ROPE_THETA ** (-jnp.arange(half, dtype=F32) / half)
    ang = jnp.arange(S, dtype=F32)[:, None] * inv[None, :]
    cos = jnp.cos(ang)
    sin = jnp.sin(ang)
    cos_n = jnp.concatenate([cos, cos, cos, cos], axis=1)
    sin_n = jnp.concatenate([-sin, sin, -sin, sin], axis=1)

    nS = S // tm
    return pl.pallas_call(
        functools.partial(_proj_kernel, seq_tiles=nS),
        grid=(T // tm,),
        in_specs=[
            pl.BlockSpec((tm, D), lambda i: (i, 0)),
            _resident((D, n_nat)), _resident((WT_ROWS, D)),
            pl.BlockSpec((tm, LANES), lambda i: (i % nS, 0)),
            pl.BlockSpec((tm, LANES), lambda i: (i % nS, 0)),
            pl.BlockSpec((half, tm), lambda i: (0, i % nS)),
            pl.BlockSpec((half, tm), lambda i: (0, i % nS)),
        ],
        out_specs=[
            pl.BlockSpec((tm, Q_COL0), lambda i: (i, 0)),
            pl.BlockSpec((NSA_Q_HEADS, HEAD_DIM, tm), lambda i: (0, 0, i)),
            pl.BlockSpec((2 * G, tm, HEAD_DIM), lambda i: (0, i, 0)),
            pl.BlockSpec((2 * G, tm, LANES), lambda i: (0, i, 0)),
            pl.BlockSpec((2 * G, V_ROWS, tm), lambda i: (0, 0, i)),
            pl.BlockSpec((G, GATE_ROWS, tm), lambda i: (0, 0, i)),
        ],
        out_shape=[
            jax.ShapeDtypeStruct((T, Q_COL0), F32),
            jax.ShapeDtypeStruct((NSA_Q_HEADS, HEAD_DIM, T), BF16),
            jax.ShapeDtypeStruct((2 * G, T, HEAD_DIM), BF16),
            jax.ShapeDtypeStruct((2 * G, T, LANES), BF16),
            jax.ShapeDtypeStruct((2 * G, V_ROWS, T), BF16),
            jax.ShapeDtypeStruct((G, GATE_ROWS, T), F32),
        ],
        compiler_params=pltpu.CompilerParams(
            dimension_semantics=("parallel",), vmem_limit_bytes=VMEM_LIMIT),
        name="proj",
    )(h, w_nat, w_t, cos_n, sin_n, cos.T, sin.T)


SUBLANES = 8
SCAN_UNROLL = 4
LRU_ROWS = 512


def _lru_kernel(x_ref, gate_ref, cw_ref, cb_ref, wa_ref, ba_ref, wx_ref, bx_ref, lam_ref,
                y_ref, x_scr, a_scr, b_scr):
    S, C = y_ref.shape
    neg_lam = -lam_ref[...]
    softplus = jnp.maximum(neg_lam, 0.0) + jnp.log1p(jnp.exp(-jnp.abs(neg_lam)))

    x_scr[0:SUBLANES, :] = jnp.zeros((SUBLANES, C), F32)
    x_scr[SUBLANES:, :] = x_ref[...]
    for r0 in range(0, S, LRU_ROWS):
        xc = cb_ref[...]
        for d in range(CONV_WIDTH):
            xs = x_scr[SUBLANES + r0 - d:SUBLANES + r0 - d + LRU_ROWS, :]
            xc = xc + xs * cw_ref[CONV_WIDTH - 1 - d:CONV_WIDTH - d, :]
        xb = xc.astype(BF16)
        gate_pre = lambda w_ref: jnp.concatenate(
            [_dot(xb[:, c * LANES:(c + 1) * LANES], w_ref[c]) for c in range(C // LANES)], axis=1)
        r = jax.nn.sigmoid(gate_pre(wa_ref) + ba_ref[...])
        ig = jax.nn.sigmoid(gate_pre(wx_ref) + bx_ref[...])
        log_a = (-LRU_C) * r * softplus
        a = jnp.exp(log_a)
        z = jnp.tanh(-log_a) * (a * a + 1.0)
        mult = z * lax.rsqrt(jnp.maximum(z, F32_TINY))
        if r0 == 0:
            row = lax.broadcasted_iota(jnp.int32, mult.shape, 0)
            mult = jnp.where(row == 0, 1.0, mult)
        a_scr[r0:r0 + LRU_ROWS, :] = a
        b_scr[r0:r0 + LRU_ROWS, :] = mult * (ig * xc)

    sub = lax.broadcasted_iota(jnp.int32, (SUBLANES, C), 0)

    def scan_tile(g, h_prev):
        rows = pl.ds(pl.multiple_of(g * SUBLANES, SUBLANES), SUBLANES)
        at = a_scr[rows, :]
        bt = b_scr[rows, :]
        d = 1
        while d < SUBLANES:
            keep = sub >= d
            a_sh = jnp.where(keep, pltpu.roll(at, d, axis=0), 1.0)
            b_sh = jnp.where(keep, pltpu.roll(bt, d, axis=0), 0.0)
            bt = at * b_sh + bt
            at = at * a_sh
            d *= 2
        h = bt + at * h_prev
        b_scr[rows, :] = h
        return jnp.broadcast_to(h[SUBLANES - 1:SUBLANES, :], (SUBLANES, C))

    def scan_body(i, h_prev):
        for u in range(SCAN_UNROLL):
            h_prev = scan_tile(i * SCAN_UNROLL + u, h_prev)
        return h_prev

    lax.fori_loop(0, S // (SUBLANES * SCAN_UNROLL), scan_body, jnp.zeros((SUBLANES, C), F32))

    for r0 in range(0, S, LRU_ROWS):
        rows = slice(r0, r0 + LRU_ROWS)
        y_ref[rows, :] = b_scr[rows, :] * jax.nn.gelu(gate_ref[rows, :])


def _lru(lru, conv_w, conv_b, w_a, b_a, w_x, b_x, lam, B, S):
    T = lru.shape[0]
    C = LRU_WIDTH
    n_ch = C // LANES
    assert S % (SUBLANES * SCAN_UNROLL) == 0 and S % LRU_ROWS == 0

    def blockdiag(w):
        w = w.reshape(n_ch, 2, LRU_BLOCK, LRU_BLOCK)
        z = jnp.zeros((n_ch, LRU_BLOCK, LRU_BLOCK), w.dtype)
        top = jnp.concatenate([w[:, 0], z], axis=2)
        bot = jnp.concatenate([z, w[:, 1]], axis=2)
        return jnp.concatenate([top, bot], axis=1).astype(BF16)

    vec = lambda v: v.reshape(1, C)
    return pl.pallas_call(
        _lru_kernel,
        grid=(B,),
        in_specs=[
            pl.BlockSpec((S, C), lambda b: (b, 0)),
            pl.BlockSpec((S, C), lambda b: (b, 1)),
            _resident((CONV_WIDTH, C)), _resident((1, C)),
            _resident((n_ch, LANES, LANES)), _resident((1, C)),
            _resident((n_ch, LANES, LANES)), _resident((1, C)),
            _resident((1, C)),
        ],
        out_specs=pl.BlockSpec((S, C), lambda b: (b, 0)),
        out_shape=jax.ShapeDtypeStruct((T, C), F32),
        scratch_shapes=[pltpu.VMEM((SUBLANES + S, C), F32),
                        pltpu.VMEM((S, C), F32),
                        pltpu.VMEM((S, C), F32)],
        compiler_params=pltpu.CompilerParams(
            dimension_semantics=("parallel",), vmem_limit_bytes=VMEM_LIMIT),
        name="lru",
    )(lru, lru, conv_w, vec(conv_b), blockdiag(w_a), vec(b_a.reshape(-1)),
      blockdiag(w_x), vec(b_x.reshape(-1)), vec(lam))


def _cmp_kernel(seg_ref, pek_ref, w1k_ref, w2k_ref, pev_ref, w1v_ref, w2vt_ref, kc_ref, vct_ref):
    n_seg = seg_ref.shape[2]
    half = CMP_STRIDE * HEAD_DIM

    def hidden(idx, pe_ref, w1_ref):
        seg = seg_ref[idx, 0]
        first = _dot(seg, w1_ref[0:half, :])
        second = _dot(seg, w1_ref[half:2 * half, :])
        bias = _dot(pe_ref[...], w1_ref[...])[0:1, :]
        return _silu(first + pltpu.roll(second, n_seg - 1, axis=0) + bias).astype(BF16)

    for g in range(NSA_KV_HEADS):
        tok = _dot(hidden(g, pek_ref, w1k_ref), w2k_ref[...])
        kc_ref[0, g] = jnp.concatenate([tok, jnp.zeros_like(tok)], axis=1).astype(BF16)
        vct_ref[0, g] = _dot_nt(w2vt_ref[...], hidden(NSA_KV_HEADS + g, pev_ref, w1v_ref)
                                ).astype(BF16)


def _compress(cmp_in, pe_k, w1_k, w2_k, pe_v, w1_v, w2_v, B, S):
    G = NSA_KV_HEADS
    n_seg = S // CMP_STRIDE
    seg = cmp_in.reshape(2 * G, B, n_seg, CMP_STRIDE * HEAD_DIM)
    flat = CMP_BLOCK * HEAD_DIM
    pe8 = lambda pe: jnp.broadcast_to(pe.reshape(1, -1), (8, flat)).astype(BF16)
    return pl.pallas_call(
        _cmp_kernel,
        grid=(B,),
        in_specs=[
            pl.BlockSpec((2 * G, 1, n_seg, CMP_STRIDE * HEAD_DIM), lambda b: (0, b, 0, 0)),
            _resident((8, flat)), _resident((flat, CMP_HIDDEN)), _resident((CMP_HIDDEN, HEAD_DIM)),
            _resident((8, flat)), _resident((flat, CMP_HIDDEN)), _resident((HEAD_DIM, CMP_HIDDEN)),
        ],
        out_specs=[
            pl.BlockSpec((1, G, n_seg, LANES), lambda b: (b, 0, 0, 0)),
            pl.BlockSpec((1, G, HEAD_DIM, n_seg), lambda b: (b, 0, 0, 0)),
        ],
        out_shape=[
            jax.ShapeDtypeStruct((B, G, n_seg, LANES), BF16),
            jax.ShapeDtypeStruct((B, G, HEAD_DIM, n_seg), BF16),
        ],
        compiler_params=pltpu.CompilerParams(
            dimension_semantics=("parallel",), vmem_limit_bytes=VMEM_LIMIT),
        name="compress",
    )(seg, pe8(pe_k), w1_k.astype(BF16), w2_k.astype(BF16),
      pe8(pe_v), w1_v.astype(BF16), w2_v.T.astype(BF16))


def _attn_kernel(qt_ref, kc_ref, vct_ref, ks_ref, vst_ref, kw_ref, vwt_ref, gt_ref, ovl_ref,
                 bias_ref, *rest, tq, tiles):
    o_ref = rest[-1]

    def step(n):
        for _ in _interleave(*[
                _attn_step(n, g, qt_ref, kc_ref.at[g], vct_ref.at[g], ks_ref.at[g], vst_ref.at[g],
                           kw_ref.at[g], vwt_ref.at[g], gt_ref.at[g], ovl_ref, bias_ref, o_ref, tq)
                for g in range(NSA_KV_HEADS)]):
            pass

    for n in range(*tiles):
        pl.when(pl.program_id(1) == n - tiles[0])(functools.partial(step, n))


CHUNK_TILES = 1
ATTN_TILE_SPLITS = ((0, 4), (4, 6), (6, 8))


def _interleave(*stages):
    live = list(stages)
    while live:
        for st in list(live):
            try:
                next(st)
            except StopIteration:
                live.remove(st)
        yield


def _tile_bias(bias_ref, t0, k0, behind):
    if k0 == t0:
        return bias_ref[0]
    if behind is not None and k0 == t0 - behind:
        return bias_ref[1]
    return None


def _key_chunks(k_ref, vt_ref, k0, n_keys, tile, q, bias_fn):
    starts = list(range(k0, k0 + n_keys, tile))
    return [dict(k_ref=k_ref, vt_ref=vt_ref, tile=tile, q=q, bias_fn=bias_fn,
                 starts=starts[i:i + CHUNK_TILES]) for i in range(0, len(starts), CHUNK_TILES)]


def _scores_stage(ch):
    tiles, m = [], None
    for k0 in ch["starts"]:
        s = _dot(ch["k_ref"][k0:k0 + ch["tile"], :], ch["q"])
        bias = ch["bias_fn"](k0)
        if bias is not None:
            s = s + bias
        cm = jnp.max(s, axis=0, keepdims=True)
        m = cm if m is None else jnp.maximum(m, cm)
        tiles.append(s)
        yield
    ch["s"], ch["m"] = tiles, m


def _pv_stage(ch, heads, tq):
    acc = [None] * heads
    for k0, s in zip(ch["starts"], ch["s"]):
        p = jnp.exp2(s - ch["m"]).astype(BF16)
        vt = ch["vt_ref"][:, k0:k0 + ch["tile"]]
        for r in range(heads):
            d = _dot(vt, p[:, r * tq:(r + 1) * tq])
            acc[r] = d if acc[r] is None else acc[r] + d
        yield
    ch["acc"] = acc


def _merge_chunks(chunks, heads, tq):
    if len(chunks) == 1:
        return chunks[0]["acc"]
    m = functools.reduce(jnp.maximum, [ch["m"] for ch in chunks])
    acc = [None] * heads
    for ch in chunks:
        w = jnp.exp2(ch["m"] - m)
        for r in range(heads):
            term = ch["acc"][r] * w[:, r * tq:(r + 1) * tq]
            acc[r] = term if acc[r] is None else acc[r] + term
    return acc


def _rank_stage(out, score, jblk):
    rank = jnp.zeros(score.shape, jnp.int32)
    for kb in range(score.shape[0]):
        sk = score[kb:kb + 1, :]
        ahead = (sk > score) | ((sk == score) & (jblk > kb))
        rank = rank + ahead.astype(jnp.int32)
        if kb % 4 == 3:
            yield
    out["rank"] = rank


def _attn_step(n, g, qt_ref, kc_ref, vct_ref, ks_ref, vst_ref, kw_ref, vwt_ref, gt_ref, ovl_ref,
               bias_ref, o_ref, tq):
    R = NSA_GROUP
    t0 = n * tq
    n_cmp = kc_ref.shape[0]
    n_blk = ovl_ref.shape[0]
    q_all = jnp.concatenate([qt_ref[g * R + r] for r in range(R)], axis=1)
    q_pad = jnp.concatenate([q_all, jnp.zeros((LANES - HEAD_DIM, R * tq), BF16)], axis=0)
    head = lambda a, r: a[:, r * tq:(r + 1) * tq]

    kw0 = max(t0 - WINDOW, 0)
    win = _key_chunks(kw_ref, vwt_ref, kw0, t0 + tq - kw0, tq, q_pad,
                      lambda k0: _tile_bias(bias_ref, t0, k0, WINDOW))
    win_scores = _scores_stage(win[0])
    next(win_scores)
    yield

    tpos_c = t0 + (lax.broadcasted_iota(jnp.int32, (n_cmp, R * tq), 1) & (tq - 1))
    cend = lax.broadcasted_iota(jnp.int32, (n_cmp, R * tq), 0) * CMP_STRIDE + (CMP_BLOCK - 1)
    s = _dot(kc_ref[...], q_pad) + jnp.where(cend <= tpos_c, 0.0, NEG)
    e = jnp.exp2(s - jnp.max(s, axis=0, keepdims=True))
    p_cmp = e * (1.0 / jnp.sum(e, axis=0, keepdims=True))
    if t0 < CMP_BLOCK - 1:
        p_cmp = jnp.where(tpos_c >= CMP_BLOCK - 1, p_cmp, 0.0)
    p_cmp_b = p_cmp.astype(BF16)
    vct = vct_ref[...]
    o_cmp = [_dot(vct, head(p_cmp_b, r)) for r in range(R)]
    psum = functools.reduce(jnp.add, [head(p_cmp, r) for r in range(R)])
    yield

    p_hi = psum.astype(BF16)
    p_lo = (psum - p_hi.astype(F32)).astype(BF16)
    ovl = ovl_ref[...]
    imp = _dot(ovl, p_hi) + _dot(ovl, p_lo)
    jblk = lax.broadcasted_iota(jnp.int32, (n_blk, tq), 0)
    tpos_b = t0 + lax.broadcasted_iota(jnp.int32, (n_blk, tq), 1)
    blk_valid = jblk * SLC_BLOCK <= tpos_b
    back = (tpos_b >> SLC_SHIFT) - jblk
    forced = (jblk == 0) | ((back >= 0) & (back < N_LOCAL_BLOCKS))
    score = jnp.where(blk_valid, jnp.where(forced, jnp.inf, imp), -jnp.inf)
    sel = {}
    yield from _interleave(_rank_stage(sel, score, jblk), win_scores)
    unsel = (1.0 - (blk_valid & (sel["rank"] < SLC_TOP_N)).astype(F32)).astype(BF16)
    sel_ext = jnp.concatenate(
        [unsel, jnp.zeros((LANES - HEAD_DIM - n_blk, tq), BF16)], axis=0)
    q_sel = jnp.concatenate([q_all, jnp.concatenate([sel_ext] * R, axis=1)], axis=0)

    slc = _key_chunks(ks_ref, vst_ref, 0, t0 + tq, tq, q_sel,
                      lambda k0: _tile_bias(bias_ref, t0, k0, None))
    chunks = win + slc
    for idx, ch in enumerate(chunks):
        stages = [_pv_stage(ch, R, tq)]
        if idx + 1 < len(chunks):
            stages.append(_scores_stage(chunks[idx + 1]))
        yield from _interleave(*stages)
    acc_w = _merge_chunks(win, R, tq)
    acc_s = _merge_chunks(slc, R, tq)
    yield

    gt = gt_ref[...]
    outs = []
    for r in range(R):
        g_cmp = gt[3 * r:3 * r + 1]
        g_slc = gt[3 * r + 1:3 * r + 2] * (1.0 / acc_s[r][HEAD_DIM:HEAD_DIM + 1])
        g_win = gt[3 * r + 2:3 * r + 3] * (1.0 / acc_w[r][HEAD_DIM:HEAD_DIM + 1])
        outs.append(g_cmp * o_cmp[r] + g_slc * acc_s[r][:HEAD_DIM] + g_win * acc_w[r][:HEAD_DIM])
    width = R * HEAD_DIM
    o_ref[:, g * width:(g + 1) * width] = jnp.concatenate(outs, axis=0).T


def _overlap_t(n_cmp_pad, n_blk):
    cs = np.arange(n_cmp_pad) * CMP_STRIDE
    ce = cs + CMP_BLOCK - 1
    ss = np.arange(n_blk) * SLC_BLOCK
    se = ss + SLC_BLOCK - 1
    return ((cs[None, :] <= se[:, None]) & (ce[None, :] >= ss[:, None])).astype(np.float32)


def _attention(qt, k, vt, kc, vct, gates_t, B, S, *, tq=256):
    T = qt.shape[2]
    nq = S // tq
    n_cmp = S // CMP_STRIDE
    n_blk = S // SLC_BLOCK
    G = NSA_KV_HEADS
    ovl = jnp.asarray(_overlap_t(n_cmp, n_blk), BF16)
    assert WINDOW % tq == 0 and WINDOW + tq <= S
    assert HEAD_DIM + n_blk <= LANES
    assert tq & (tq - 1) == 0
    key_off = np.arange(tq)[:, None]
    q_off = np.tile(np.arange(tq), NSA_GROUP)[None, :]
    bias = jnp.asarray(np.stack([np.where(key_off <= q_off, 0.0, NEG),
                                 np.where(key_off > q_off, 0.0, NEG)]), F32)
    k_spec = lambda c: pl.BlockSpec((G, S, LANES), lambda b, i: (c, b, 0))
    vt_spec = lambda c: pl.BlockSpec((G, V_ROWS, S), lambda b, i: (c, 0, b))

    def call(tiles, prev):
        lo, hi = tiles
        tile = lambda b, i: b * nq + lo + i
        in_specs = [
            pl.BlockSpec((NSA_Q_HEADS, HEAD_DIM, tq), lambda b, i: (0, 0, tile(b, i))),
            pl.BlockSpec((None, G, n_cmp, LANES), lambda b, i: (b, 0, 0, 0)),
            pl.BlockSpec((None, G, HEAD_DIM, n_cmp), lambda b, i: (b, 0, 0, 0)),
            k_spec(0), vt_spec(0), k_spec(1), vt_spec(1),
            pl.BlockSpec((G, GATE_ROWS, tq), lambda b, i: (0, 0, tile(b, i))),
            _resident((n_blk, n_cmp)),
            _resident((2, tq, NSA_GROUP * tq)),
        ]
        args = [qt, kc, vct, k, vt, k, vt, gates_t, ovl, bias]
        aliases = {}
        if prev is not None:
            in_specs.append(pl.BlockSpec(memory_space=pl.ANY))
            aliases = {len(args): 0}
            args.append(prev)
        return pl.pallas_call(
            functools.partial(_attn_kernel, tq=tq, tiles=tiles),
            grid=(B, hi - lo),
            in_specs=in_specs,
            out_specs=pl.BlockSpec((tq, NSA_WIDTH), lambda b, i: (tile(b, i), 0)),
            out_shape=jax.ShapeDtypeStruct((T, NSA_WIDTH), F32),
            input_output_aliases=aliases,
            compiler_params=pltpu.CompilerParams(
                dimension_semantics=("parallel", "arbitrary"),
                vmem_limit_bytes=VMEM_LIMIT),
            name=f"nsa_attn_{lo}_{hi}",
        )(*args)

    assert [t for lo, hi in ATTN_TILE_SPLITS for t in range(lo, hi)] == list(range(nq))
    out = None
    for tiles in ATTN_TILE_SPLITS:
        out = call(tiles, out)
    return out


def _out_ffn_kernel(ylru_ref, ynsa_ref, h_ref, gl_ref, gn_ref, w_ref, g2_ref, b2_ref,
                    wg_ref, wu_ref, wo_ref, g3_ref, b3_ref, o_ref):
    yl = _rms_norm(ylru_ref[...], gl_ref[...]).astype(BF16)
    yn = _rms_norm(ynsa_ref[...], gn_ref[...]).astype(BF16)
    mix = _dot(yl, w_ref[0:LRU_WIDTH, :]) + _dot(yn, w_ref[LRU_WIDTH:, :])
    h2 = _layer_norm(ALPHA * h_ref[...] + mix, g2_ref[...], b2_ref[...])
    o_ref[...] = _ffn_half_step(h2, wg_ref, wu_ref, wo_ref, g3_ref, b3_ref)


def _out_proj_ffn(y_lru, y_nsa, h, gn_lru, gn_nsa, w_mix, g2, b2, w_in, w_out, g3, b3, *,
                  tm=512):
    T, D = h.shape
    row = lambda v: v.reshape(1, -1)
    return pl.pallas_call(
        _out_ffn_kernel,
        grid=(T // tm,),
        in_specs=[
            pl.BlockSpec((tm, LRU_WIDTH), lambda i: (i, 0)),
            pl.BlockSpec((tm, NSA_WIDTH), lambda i: (i, 0)),
            pl.BlockSpec((tm, D), lambda i: (i, 0)),
            _resident((1, LRU_WIDTH)), _resident((1, NSA_WIDTH)),
            _resident((LRU_WIDTH + NSA_WIDTH, D)),
            _resident((1, D)), _resident((1, D)),
            _resident((D, D_FF)), _resident((D, D_FF)), _resident((D_FF, D)),
            _resident((1, D)), _resident((1, D)),
        ],
        out_specs=pl.BlockSpec((tm, D), lambda i: (i, 0)),
        out_shape=jax.ShapeDtypeStruct((T, D), F32),
        compiler_params=pltpu.CompilerParams(
            dimension_semantics=("parallel",), vmem_limit_bytes=VMEM_LIMIT),
        name="out_proj_ffn",
    )(y_lru, y_nsa, h, row(gn_lru), row(gn_nsa), w_mix.astype(BF16), row(g2), row(b2),
      w_in[:, :D_FF].astype(BF16), w_in[:, D_FF:].astype(BF16), w_out.astype(BF16),
      row(g3), row(b3))


def kernel(x, ffn1_w_in, ffn1_w_out, ln1_g, ln1_b, mix_w_in, conv_w, conv_b, lru_w_a, lru_b_a,
           lru_w_x, lru_b_x, lru_lam, cmp_pe_k, cmp_w1_k, cmp_w2_k, cmp_pe_v, cmp_w1_v,
           cmp_w2_v, gn_lru, gn_nsa, mix_w_out, ln2_g, ln2_b, ffn2_w_in, ffn2_w_out, ln3_g,
           ln3_b):
    B, S, D = x.shape
    h = x.reshape(B * S, D)
    for l in range(DEPTH):
        h = _ffn_ln(h, ffn1_w_in[l], ffn1_w_out[l], ln1_g[l], ln1_b[l])
        lru, qt, cmp_in, k, vt, gates_t = _proj(h, mix_w_in[l], S)
        y_lru = _lru(lru, conv_w[l], conv_b[l], lru_w_a[l], lru_b_a[l], lru_w_x[l], lru_b_x[l],
                     lru_lam[l], B, S)
        kc, vct = _compress(cmp_in, cmp_pe_k[l], cmp_w1_k[l], cmp_w2_k[l],
                            cmp_pe_v[l], cmp_w1_v[l], cmp_w2_v[l], B, S)
        y_nsa = _attention(qt, k, vt, kc, vct, gates_t, B, S)
        h = _out_proj_ffn(y_lru, y_nsa, h, gn_lru[l], gn_nsa[l], mix_w_out[l], ln2_g[l], ln2_b[l],
                          ffn2_w_in[l], ffn2_w_out[l], ln3_g[l], ln3_b[l])
    return h.reshape(B, S, D)
```

```python
import functools

import numpy as np
import jax
import jax.numpy as jnp
from jax import lax
from jax.experimental import pallas as pl
from jax.experimental.pallas import tpu as pltpu

F32 = jnp.float32
BF16 = jnp.bfloat16

D_MODEL = 1024
LRU_WIDTH = 512
LRU_HEADS = 8
LRU_BLOCK = 64
CONV_WIDTH = 4
LRU_C = 8.0
NSA_Q_HEADS = 8
NSA_KV_HEADS = 2
NSA_GROUP = 4
HEAD_DIM = 64
NSA_WIDTH = 512
KV_WIDTH = 128
CMP_BLOCK = 32
CMP_STRIDE = 16
CMP_HIDDEN = 256
SLC_BLOCK = 64
SLC_SHIFT = 6
SLC_TOP_N = 16
N_LOCAL_BLOCKS = 2
WINDOW = 512
ROPE_THETA = 10000.0
D_FF = 2816
DEPTH = 1
ALPHA = (2.0 * DEPTH) ** 0.25
LN_EPS = 1e-5
RMS_EPS = 1e-6
NEG = -1e30
LOG2E = 1.4426950408889634
F32_TINY = 1.1754943508222875e-38

LANES = 128
BF16_ROWS = 16
VMEM_LIMIT = 48 * 1024 * 1024
N_GATE = 3 * NSA_GROUP
GATE_ROWS = 16
V_ROWS = HEAD_DIM + BF16_ROWS


def _dot(a, b):
    return jnp.dot(a, b, preferred_element_type=F32)


def _dot_nt(a, b):
    return lax.dot_general(a, b, (((1,), (1,)), ((), ())), preferred_element_type=F32)


def _layer_norm(y, g, b):
    mu = jnp.mean(y, axis=-1, keepdims=True)
    d = y - mu
    var = jnp.mean(d * d, axis=-1, keepdims=True)
    return d * lax.rsqrt(var + LN_EPS) * g + b


def _rms_norm(y, g):
    return y * lax.rsqrt(jnp.mean(y * y, axis=-1, keepdims=True) + RMS_EPS) * g


def _silu(x):
    return x * jax.nn.sigmoid(x)


def _resident(shape):
    return pl.BlockSpec(shape, lambda *_: (0,) * len(shape), pipeline_mode=pl.Buffered(1))


def _ffn_half_step(x, wg_ref, wu_ref, wo_ref, g_ref, b_ref):
    xb = x.astype(BF16)
    gate = _dot(xb, wg_ref[...])
    up = _dot(xb, wu_ref[...])
    act = (_silu(gate) * up).astype(BF16)
    y = ALPHA * x + 0.5 * _dot(act, wo_ref[...])
    return _layer_norm(y, g_ref[...], b_ref[...])


def _ffn_ln_kernel(x_ref, wg_ref, wu_ref, wo_ref, g_ref, b_ref, o_ref):
    o_ref[...] = _ffn_half_step(x_ref[...], wg_ref, wu_ref, wo_ref, g_ref, b_ref)


def _ffn_ln(x, w_in, w_out, g, b, *, tm=1024):
    T, D = x.shape
    wg = w_in[:, :D_FF].astype(BF16)
    wu = w_in[:, D_FF:].astype(BF16)
    wo = w_out.astype(BF16)
    return pl.pallas_call(
        _ffn_ln_kernel,
        grid=(T // tm,),
        in_specs=[
            pl.BlockSpec((tm, D), lambda i: (i, 0)),
            _resident((D, D_FF)), _resident((D, D_FF)), _resident((D_FF, D)),
            _resident((1, D)), _resident((1, D)),
        ],
        out_specs=pl.BlockSpec((tm, D), lambda i: (i, 0)),
        out_shape=jax.ShapeDtypeStruct((T, D), F32),
        compiler_params=pltpu.CompilerParams(
            dimension_semantics=("parallel",), vmem_limit_bytes=VMEM_LIMIT),
        name="ffn_ln",
    )(x, wg, wu, wo, g.reshape(1, D), b.reshape(1, D))


Q_COL0 = 2 * LRU_WIDTH
KCMP_COL0 = Q_COL0 + NSA_WIDTH
VCMP_COL0 = KCMP_COL0 + KV_WIDTH
KSLC_COL0 = VCMP_COL0 + KV_WIDTH
VSLC_COL0 = KSLC_COL0 + KV_WIDTH
KWIN_COL0 = VSLC_COL0 + KV_WIDTH
VWIN_COL0 = KWIN_COL0 + KV_WIDTH
GATE_COL0 = VWIN_COL0 + KV_WIDTH
VT_ROW0 = NSA_WIDTH
GT_ROW0 = VT_ROW0 + 2 * KV_WIDTH
WT_ROWS = GT_ROW0 + NSA_KV_HEADS * GATE_ROWS


def _rope_chunk(xc, cos, sin_signed):
    lane = lax.broadcasted_iota(jnp.int32, xc.shape, 1)
    first = (lane & (HEAD_DIM - 1)) < (HEAD_DIM // 2)
    partner = jnp.where(first, pltpu.roll(xc, LANES - HEAD_DIM // 2, axis=1),
                        pltpu.roll(xc, HEAD_DIM // 2, axis=1))
    return xc * cos + partner * sin_signed


def _proj_kernel(h_ref, wn_ref, wt_ref, cos_ref, sin_ref, cost_ref, sint_ref,
                 lru_ref, qt_ref, cmp_ref, k_ref, vt_ref, gt_ref, seg_scr, *, seq_tiles):
    tm = h_ref.shape[0]
    half = HEAD_DIM // 2
    hb = h_ref[...].astype(BF16)
    p = _dot(hb, wn_ref[...])
    pt = _dot_nt(wt_ref[...], hb)
    lru_ref[...] = p[:, :Q_COL0]

    cos_t = cost_ref[...]
    sin_t = sint_ref[...]
    scale = HEAD_DIM ** -0.5 * LOG2E
    for hd in range(NSA_Q_HEADS):
        x1 = pt[hd * HEAD_DIM:hd * HEAD_DIM + half]
        x2 = pt[hd * HEAD_DIM + half:(hd + 1) * HEAD_DIM]
        qt_ref[hd] = (jnp.concatenate([x1 * cos_t - x2 * sin_t, x2 * cos_t + x1 * sin_t], axis=0)
                      * scale).astype(BF16)
    ones = jnp.ones((V_ROWS - HEAD_DIM, tm), F32)
    for c in range(2 * NSA_KV_HEADS):
        v = pt[VT_ROW0 + c * HEAD_DIM:VT_ROW0 + (c + 1) * HEAD_DIM]
        vt_ref[c] = jnp.concatenate([v, ones], axis=0).astype(BF16)
    for g in range(NSA_KV_HEADS):
        gt_ref[g] = jax.nn.sigmoid(pt[GT_ROW0 + g * GATE_ROWS:GT_ROW0 + (g + 1) * GATE_ROWS])

    cos = cos_ref[...]
    sin = sin_ref[...]
    low_seg = lax.broadcasted_iota(jnp.int32, (tm // CMP_STRIDE, LANES), 1) < HEAD_DIM
    for c, xc in enumerate((_rope_chunk(p[:, Q_COL0:Q_COL0 + LANES], cos, sin),
                            p[:, Q_COL0 + LANES:Q_COL0 + 2 * LANES])):
        seg_scr[...] = xc
        tok = [seg_scr[pl.ds(j, tm // CMP_STRIDE, stride=CMP_STRIDE), :]
               for j in range(CMP_STRIDE)]
        g0, g1 = [], []
        for j in range(0, CMP_STRIDE, 2):
            g0.append(jnp.where(low_seg, tok[j], pltpu.roll(tok[j + 1], HEAD_DIM, axis=1)))
            g1.append(jnp.where(low_seg, pltpu.roll(tok[j], HEAD_DIM, axis=1), tok[j + 1]))
        cmp_ref[NSA_KV_HEADS * c] = jnp.concatenate(g0, axis=1).astype(BF16)
        cmp_ref[NSA_KV_HEADS * c + 1] = jnp.concatenate(g1, axis=1).astype(BF16)
    lane = lax.broadcasted_iota(jnp.int32, (tm, LANES), 1)
    low = lane < HEAD_DIM
    pos = (pl.program_id(0) % seq_tiles) * tm + lax.broadcasted_iota(jnp.int32, (tm, LANES), 0)
    ext_blk = jnp.where(lane - HEAD_DIM == (pos >> SLC_SHIFT), NEG, 0.0)
    ext_zero = jnp.zeros((tm, LANES), F32)
    for c, ext in enumerate((ext_blk, ext_zero)):
        xc = _rope_chunk(p[:, Q_COL0 + (2 + c) * LANES:Q_COL0 + (3 + c) * LANES], cos, sin)
        k_ref[2 * c] = jnp.where(low, xc, ext).astype(BF16)
        k_ref[2 * c + 1] = jnp.where(low, pltpu.roll(xc, HEAD_DIM, axis=1), ext).astype(BF16)


def _proj(h, w_in, S, *, tm=512):
    T, D = h.shape
    G = NSA_KV_HEADS
    col = lambda c0, n: w_in[:, c0:c0 + n]
    w_nat = jnp.concatenate(
        [col(0, Q_COL0), col(KCMP_COL0, KV_WIDTH), col(VCMP_COL0, KV_WIDTH),
         col(KSLC_COL0, KV_WIDTH), col(KWIN_COL0, KV_WIDTH)], axis=1).astype(BF16)
    gate_rows = [jnp.pad(col(GATE_COL0 + g * N_GATE, N_GATE).T, ((0, GATE_ROWS - N_GATE), (0, 0)))
                 for g in range(G)]
    w_t = jnp.concatenate(
        [col(Q_COL0, NSA_WIDTH).T, col(VSLC_COL0, KV_WIDTH).T, col(VWIN_COL0, KV_WIDTH).T]
        + gate_rows, axis=0).astype(BF16)
    n_nat = w_nat.shape[1]

    half = HEAD_DIM // 2
    inv = ROPE_THETA ** (-jnp.arange(half, dtype=F32) / half)
    ang = jnp.arange(S, dtype=F32)[:, None] * inv[None, :]
    cos = jnp.cos(ang)
    sin = jnp.sin(ang)
    cos_n = jnp.concatenate([cos, cos, cos, cos], axis=1)
    sin_n = jnp.concatenate([-sin, sin, -sin, sin], axis=1)

    nS = S // tm
    seg_w = CMP_STRIDE * HEAD_DIM
    assert G * HEAD_DIM == LANES
    return pl.pallas_call(
        functools.partial(_proj_kernel, seq_tiles=nS),
        grid=(T // tm,),
        in_specs=[
            pl.BlockSpec((tm, D), lambda i: (i, 0)),
            _resident((D, n_nat)), _resident((WT_ROWS, D)),
            pl.BlockSpec((tm, LANES), lambda i: (i % nS, 0)),
            pl.BlockSpec((tm, LANES), lambda i: (i % nS, 0)),
            pl.BlockSpec((half, tm), lambda i: (0, i % nS)),
            pl.BlockSpec((half, tm), lambda i: (0, i % nS)),
        ],
        out_specs=[
            pl.BlockSpec((tm, Q_COL0), lambda i: (i, 0)),
            pl.BlockSpec((NSA_Q_HEADS, HEAD_DIM, tm), lambda i: (0, 0, i)),
            pl.BlockSpec((2 * G, tm // CMP_STRIDE, seg_w), lambda i: (0, i, 0)),
            pl.BlockSpec((2 * G, tm, LANES), lambda i: (0, i, 0)),
            pl.BlockSpec((2 * G, V_ROWS, tm), lambda i: (0, 0, i)),
            pl.BlockSpec((G, GATE_ROWS, tm), lambda i: (0, 0, i)),
        ],
        out_shape=[
            jax.ShapeDtypeStruct((T, Q_COL0), F32),
            jax.ShapeDtypeStruct((NSA_Q_HEADS, HEAD_DIM, T), BF16),
            jax.ShapeDtypeStruct((2 * G, T // CMP_STRIDE, seg_w), BF16),
            jax.ShapeDtypeStruct((2 * G, T, LANES), BF16),
            jax.ShapeDtypeStruct((2 * G, V_ROWS, T), BF16),
            jax.ShapeDtypeStruct((G, GATE_ROWS, T), F32),
        ],
        scratch_shapes=[pltpu.VMEM((tm, LANES), F32)],
        compiler_params=pltpu.CompilerParams(
            dimension_semantics=("parallel",), vmem_limit_bytes=VMEM_LIMIT),
        name="proj",
    )(h, w_nat, w_t, cos_n, sin_n, cos.T, sin.T)


SUBLANES = 8
SCAN_UNROLL = 8
LRU_ROWS = 512


def _lru_kernel(x_ref, gate_ref, cw_ref, cb_ref, wa_ref, ba_ref, wx_ref, bx_ref, lam_ref,
                y_ref, x_scr, a_scr, b_scr):
    S, C = y_ref.shape
    neg_lam = -lam_ref[...]
    softplus = jnp.maximum(neg_lam, 0.0) + jnp.log1p(jnp.exp(-jnp.abs(neg_lam)))

    x_scr[0:SUBLANES, :] = jnp.zeros((SUBLANES, C), F32)
    x_scr[SUBLANES:, :] = x_ref[...]
    for r0 in range(0, S, LRU_ROWS):
        xc = cb_ref[...]
        for d in range(CONV_WIDTH):
            xs = x_scr[SUBLANES + r0 - d:SUBLANES + r0 - d + LRU_ROWS, :]
            xc = xc + xs * cw_ref[CONV_WIDTH - 1 - d:CONV_WIDTH - d, :]
        xb = xc.astype(BF16)
        gate_pre = lambda w_ref: jnp.concatenate(
            [_dot(xb[:, c * LANES:(c + 1) * LANES], w_ref[c]) for c in range(C // LANES)], axis=1)
        r = jax.nn.sigmoid(gate_pre(wa_ref) + ba_ref[...])
        ig = jax.nn.sigmoid(gate_pre(wx_ref) + bx_ref[...])
        log_a = (-LRU_C) * r * softplus
        a = jnp.exp(log_a)
        z = jnp.tanh(-log_a) * (a * a + 1.0)
        mult = z * lax.rsqrt(jnp.maximum(z, F32_TINY))
        if r0 == 0:
            row = lax.broadcasted_iota(jnp.int32, mult.shape, 0)
            mult = jnp.where(row == 0, 1.0, mult)
        a_scr[r0:r0 + LRU_ROWS, :] = a
        b_scr[r0:r0 + LRU_ROWS, :] = mult * (ig * xc)

    sub = lax.broadcasted_iota(jnp.int32, (SUBLANES, C), 0)

    def scan_tile(g, h_prev):
        rows = pl.ds(pl.multiple_of(g * SUBLANES, SUBLANES), SUBLANES)
        at = a_scr[rows, :]
        bt = b_scr[rows, :]
        d = 1
        while d < SUBLANES:
            keep = sub >= d
            a_sh = jnp.where(keep, pltpu.roll(at, d, axis=0), 1.0)
            b_sh = jnp.where(keep, pltpu.roll(bt, d, axis=0), 0.0)
            bt = at * b_sh + bt
            at = at * a_sh
            d *= 2
        h = bt + at * h_prev
        b_scr[rows, :] = h
        return jnp.broadcast_to(h[SUBLANES - 1:SUBLANES, :], (SUBLANES, C))

    def scan_body(i, h_prev):
        for u in range(SCAN_UNROLL):
            h_prev = scan_tile(i * SCAN_UNROLL + u, h_prev)
        return h_prev

    lax.fori_loop(0, S // (SUBLANES * SCAN_UNROLL), scan_body, jnp.zeros((SUBLANES, C), F32))

    for r0 in range(0, S, LRU_ROWS):
        rows = slice(r0, r0 + LRU_ROWS)
        y_ref[rows, :] = b_scr[rows, :] * jax.nn.gelu(gate_ref[rows, :])


def _lru(lru, conv_w, conv_b, w_a, b_a, w_x, b_x, lam, B, S):
    T = lru.shape[0]
    C = LRU_WIDTH
    n_ch = C // LANES
    assert S % (SUBLANES * SCAN_UNROLL) == 0 and S % LRU_ROWS == 0

    def blockdiag(w):
        w = w.reshape(n_ch, 2, LRU_BLOCK, LRU_BLOCK)
        z = jnp.zeros((n_ch, LRU_BLOCK, LRU_BLOCK), w.dtype)
        top = jnp.concatenate([w[:, 0], z], axis=2)
        bot = jnp.concatenate([z, w[:, 1]], axis=2)
        return jnp.concatenate([top, bot], axis=1).astype(BF16)

    vec = lambda v: v.reshape(1, C)
    return pl.pallas_call(
        _lru_kernel,
        grid=(B,),
        in_specs=[
            pl.BlockSpec((S, C), lambda b: (b, 0)),
            pl.BlockSpec((S, C), lambda b: (b, 1)),
            _resident((CONV_WIDTH, C)), _resident((1, C)),
            _resident((n_ch, LANES, LANES)), _resident((1, C)),
            _resident((n_ch, LANES, LANES)), _resident((1, C)),
            _resident((1, C)),
        ],
        out_specs=pl.BlockSpec((S, C), lambda b: (b, 0)),
        out_shape=jax.ShapeDtypeStruct((T, C), F32),
        scratch_shapes=[pltpu.VMEM((SUBLANES + S, C), F32),
                        pltpu.VMEM((S, C), F32),
                        pltpu.VMEM((S, C), F32)],
        compiler_params=pltpu.CompilerParams(
            dimension_semantics=("parallel",), vmem_limit_bytes=VMEM_LIMIT),
        name="lru",
    )(lru, lru, conv_w, vec(conv_b), blockdiag(w_a), vec(b_a.reshape(-1)),
      blockdiag(w_x), vec(b_x.reshape(-1)), vec(lam))


def _cmp_kernel(seg_ref, pek_ref, w1k_ref, w2k_ref, pev_ref, w1v_ref, w2vt_ref, kc_ref, vct_ref):
    n_seg = seg_ref.shape[2]
    half = CMP_STRIDE * HEAD_DIM

    def hidden(idx, pe_ref, w1_ref):
        seg = seg_ref[idx, 0]
        first = _dot(seg, w1_ref[0:half, :])
        second = _dot(seg, w1_ref[half:2 * half, :])
        bias = _dot(pe_ref[...], w1_ref[...])[0:1, :]
        return _silu(first + pltpu.roll(second, n_seg - 1, axis=0) + bias).astype(BF16)

    for g in range(NSA_KV_HEADS):
        tok = _dot(hidden(g, pek_ref, w1k_ref), w2k_ref[...])
        kc_ref[0, g] = jnp.concatenate([tok, jnp.zeros_like(tok)], axis=1).astype(BF16)
        vct_ref[0, g] = _dot_nt(w2vt_ref[...], hidden(NSA_KV_HEADS + g, pev_ref, w1v_ref)
                                ).astype(BF16)


def _compress(cmp_in, pe_k, w1_k, w2_k, pe_v, w1_v, w2_v, B, S):
    G = NSA_KV_HEADS
    n_seg = S // CMP_STRIDE
    seg = cmp_in.reshape(2 * G, B, n_seg, CMP_STRIDE * HEAD_DIM)
    flat = CMP_BLOCK * HEAD_DIM
    pe8 = lambda pe: jnp.broadcast_to(pe.reshape(1, -1), (8, flat)).astype(BF16)
    return pl.pallas_call(
        _cmp_kernel,
        grid=(B,),
        in_specs=[
            pl.BlockSpec((2 * G, 1, n_seg, CMP_STRIDE * HEAD_DIM), lambda b: (0, b, 0, 0)),
            _resident((8, flat)), _resident((flat, CMP_HIDDEN)), _resident((CMP_HIDDEN, HEAD_DIM)),
            _resident((8, flat)), _resident((flat, CMP_HIDDEN)), _resident((HEAD_DIM, CMP_HIDDEN)),
        ],
        out_specs=[
            pl.BlockSpec((1, G, n_seg, LANES), lambda b: (b, 0, 0, 0)),
            pl.BlockSpec((1, G, HEAD_DIM, n_seg), lambda b: (b, 0, 0, 0)),
        ],
        out_shape=[
            jax.ShapeDtypeStruct((B, G, n_seg, LANES), BF16),
            jax.ShapeDtypeStruct((B, G, HEAD_DIM, n_seg), BF16),
        ],
        compiler_params=pltpu.CompilerParams(
            dimension_semantics=("parallel",), vmem_limit_bytes=VMEM_LIMIT),
        name="compress",
    )(seg, pe8(pe_k), w1_k.astype(BF16), w2_k.astype(BF16),
      pe8(pe_v), w1_v.astype(BF16), w2_v.T.astype(BF16))


def _attn_kernel(qt_ref, kc_ref, vct_ref, ks_ref, vst_ref, kw_ref, vwt_ref, gt_ref, ovl_ref,
                 bias_ref, *rest, tq, tiles):
    o_ref = rest[-1]

    def step(n):
        for _ in _interleave(*[
                _attn_step(n, g, qt_ref, kc_ref.at[g], vct_ref.at[g], ks_ref.at[g], vst_ref.at[g],
                           kw_ref.at[g], vwt_ref.at[g], gt_ref.at[g], ovl_ref, bias_ref, o_ref, tq)
                for g in range(NSA_KV_HEADS)]):
            pass

    for n in range(*tiles):
        pl.when(pl.program_id(1) == n - tiles[0])(functools.partial(step, n))


CHUNK_TILES = 1
SCORES_AHEAD = 1
ATTN_TILE_SPLITS = ((0, 4), (4, 6), (6, 8))


def _interleave(*stages):
    live = list(stages)
    while live:
        for st in list(live):
            try:
                next(st)
            except StopIteration:
                live.remove(st)
        yield


def _tile_bias(bias_ref, t0, k0, behind):
    if k0 == t0:
        return bias_ref[0]
    if behind is not None and k0 == t0 - behind:
        return bias_ref[1]
    return None


def _key_chunks(k_ref, vt_ref, k0, n_keys, tile, q, bias_fn):
    starts = list(range(k0, k0 + n_keys, tile))
    return [dict(k_ref=k_ref, vt_ref=vt_ref, tile=tile, q=q, bias_fn=bias_fn,
                 starts=starts[i:i + CHUNK_TILES]) for i in range(0, len(starts), CHUNK_TILES)]


def _scores_stage(ch):
    tiles, m = [], None
    for k0 in ch["starts"]:
        s = _dot(ch["k_ref"][k0:k0 + ch["tile"], :], ch["q"])
        bias = ch["bias_fn"](k0)
        if bias is not None:
            s = s + bias
        cm = jnp.max(s, axis=0, keepdims=True)
        m = cm if m is None else jnp.maximum(m, cm)
        tiles.append(s)
        yield
    ch["s"], ch["m"] = tiles, m


def _pv_stage(ch, heads, tq):
    acc = [None] * heads
    for k0, s in zip(ch["starts"], ch["s"]):
        p = jnp.exp2(s - ch["m"]).astype(BF16)
        vt = ch["vt_ref"][:, k0:k0 + ch["tile"]]
        for r in range(heads):
            d = _dot(vt, p[:, r * tq:(r + 1) * tq])
            acc[r] = d if acc[r] is None else acc[r] + d
        yield
    ch["acc"] = acc


def _merge_chunks(chunks, heads, tq):
    if len(chunks) == 1:
        return chunks[0]["acc"]
    m = functools.reduce(jnp.maximum, [ch["m"] for ch in chunks])
    acc = [None] * heads
    for ch in chunks:
        w = jnp.exp2(ch["m"] - m)
        for r in range(heads):
            term = ch["acc"][r] * w[:, r * tq:(r + 1) * tq]
            acc[r] = term if acc[r] is None else acc[r] + term
    return acc


def _rank_stage(out, score, jblk):
    rank = jnp.zeros(score.shape, jnp.int32)
    for kb in range(score.shape[0]):
        sk = score[kb:kb + 1, :]
        ahead = (sk > score) | ((sk == score) & (jblk > kb))
        rank = rank + ahead.astype(jnp.int32)
        if kb % 4 == 3:
            yield
    out["rank"] = rank


def _attn_step(n, g, qt_ref, kc_ref, vct_ref, ks_ref, vst_ref, kw_ref, vwt_ref, gt_ref, ovl_ref,
               bias_ref, o_ref, tq):
    R = NSA_GROUP
    t0 = n * tq
    n_cmp = kc_ref.shape[0]
    n_blk = ovl_ref.shape[0]
    q_all = jnp.concatenate([qt_ref[g * R + r] for r in range(R)], axis=1)
    q_pad = jnp.concatenate([q_all, jnp.zeros((LANES - HEAD_DIM, R * tq), BF16)], axis=0)
    head = lambda a, r: a[:, r * tq:(r + 1) * tq]

    kw0 = max(t0 - WINDOW, 0)
    win = _key_chunks(kw_ref, vwt_ref, kw0, t0 + tq - kw0, tq, q_pad,
                      lambda k0: _tile_bias(bias_ref, t0, k0, WINDOW))
    win_scores = _scores_stage(win[0])
    next(win_scores)
    yield

    tpos_c = t0 + (lax.broadcasted_iota(jnp.int32, (n_cmp, R * tq), 1) & (tq - 1))
    cend = lax.broadcasted_iota(jnp.int32, (n_cmp, R * tq), 0) * CMP_STRIDE + (CMP_BLOCK - 1)
    s = _dot(kc_ref[...], q_pad) + jnp.where(cend <= tpos_c, 0.0, NEG)
    e = jnp.exp2(s - jnp.max(s, axis=0, keepdims=True))
    p_cmp = e * (1.0 / jnp.sum(e, axis=0, keepdims=True))
    if t0 < CMP_BLOCK - 1:
        p_cmp = jnp.where(tpos_c >= CMP_BLOCK - 1, p_cmp, 0.0)
    p_cmp_b = p_cmp.astype(BF16)
    vct = vct_ref[...]
    o_cmp = [_dot(vct, head(p_cmp_b, r)) for r in range(R)]
    psum = functools.reduce(jnp.add, [head(p_cmp, r) for r in range(R)])
    yield

    p_hi = psum.astype(BF16)
    p_lo = (psum - p_hi.astype(F32)).astype(BF16)
    ovl = ovl_ref[...]
    imp = _dot(ovl, p_hi) + _dot(ovl, p_lo)
    jblk = lax.broadcasted_iota(jnp.int32, (n_blk, tq), 0)
    tpos_b = t0 + lax.broadcasted_iota(jnp.int32, (n_blk, tq), 1)
    blk_valid = jblk * SLC_BLOCK <= tpos_b
    back = (tpos_b >> SLC_SHIFT) - jblk
    forced = (jblk == 0) | ((back >= 0) & (back < N_LOCAL_BLOCKS))
    score = jnp.where(blk_valid, jnp.where(forced, jnp.inf, imp), -jnp.inf)
    sel = {}
    yield from _interleave(_rank_stage(sel, score, jblk), win_scores)
    unsel = (1.0 - (blk_valid & (sel["rank"] < SLC_TOP_N)).astype(F32)).astype(BF16)
    sel_ext = jnp.concatenate(
        [unsel, jnp.zeros((LANES - HEAD_DIM - n_blk, tq), BF16)], axis=0)
    q_sel = jnp.concatenate([q_all, jnp.concatenate([sel_ext] * R, axis=1)], axis=0)

    slc = _key_chunks(ks_ref, vst_ref, 0, t0 + tq, tq, q_sel,
                      lambda k0: _tile_bias(bias_ref, t0, k0, None))
    chunks = win + slc
    for ch in chunks[1:SCORES_AHEAD]:
        yield from _scores_stage(ch)
    for idx, ch in enumerate(chunks):
        stages = [_pv_stage(ch, R, tq)]
        if idx + SCORES_AHEAD < len(chunks):
            stages.append(_scores_stage(chunks[idx + SCORES_AHEAD]))
        yield from _interleave(*stages)
    acc_w = _merge_chunks(win, R, tq)
    acc_s = _merge_chunks(slc, R, tq)
    yield

    gt = gt_ref[...]
    outs = []
    for r in range(R):
        g_cmp = gt[3 * r:3 * r + 1]
        g_slc = gt[3 * r + 1:3 * r + 2] * (1.0 / acc_s[r][HEAD_DIM:HEAD_DIM + 1])
        g_win = gt[3 * r + 2:3 * r + 3] * (1.0 / acc_w[r][HEAD_DIM:HEAD_DIM + 1])
        outs.append(g_cmp * o_cmp[r] + g_slc * acc_s[r][:HEAD_DIM] + g_win * acc_w[r][:HEAD_DIM])
    width = R * HEAD_DIM
    o_ref[:, g * width:(g + 1) * width] = jnp.concatenate(outs, axis=0).T


def _overlap_t(n_cmp_pad, n_blk):
    cs = np.arange(n_cmp_pad) * CMP_STRIDE
    ce = cs + CMP_BLOCK - 1
    ss = np.arange(n_blk) * SLC_BLOCK
    se = ss + SLC_BLOCK - 1
    return ((cs[None, :] <= se[:, None]) & (ce[None, :] >= ss[:, None])).astype(np.float32)


def _attention(qt, k, vt, kc, vct, gates_t, B, S, *, tq=256):
    T = qt.shape[2]
    nq = S // tq
    n_cmp = S // CMP_STRIDE
    n_blk = S // SLC_BLOCK
    G = NSA_KV_HEADS
    ovl = jnp.asarray(_overlap_t(n_cmp, n_blk), BF16)
    assert WINDOW % tq == 0 and WINDOW + tq <= S
    assert HEAD_DIM + n_blk <= LANES
    assert tq & (tq - 1) == 0
    key_off = np.arange(tq)[:, None]
    q_off = np.tile(np.arange(tq), NSA_GROUP)[None, :]
    bias = jnp.asarray(np.stack([np.where(key_off <= q_off, 0.0, NEG),
                                 np.where(key_off > q_off, 0.0, NEG)]), F32)
    k_spec = lambda c: pl.BlockSpec((G, S, LANES), lambda b, i: (c, b, 0))
    vt_spec = lambda c: pl.BlockSpec((G, V_ROWS, S), lambda b, i: (c, 0, b))

    def call(tiles, prev):
        lo, hi = tiles
        tile = lambda b, i: b * nq + lo + i
        in_specs = [
            pl.BlockSpec((NSA_Q_HEADS, HEAD_DIM, tq), lambda b, i: (0, 0, tile(b, i))),
            pl.BlockSpec((None, G, n_cmp, LANES), lambda b, i: (b, 0, 0, 0)),
            pl.BlockSpec((None, G, HEAD_DIM, n_cmp), lambda b, i: (b, 0, 0, 0)),
            k_spec(0), vt_spec(0), k_spec(1), vt_spec(1),
            pl.BlockSpec((G, GATE_ROWS, tq), lambda b, i: (0, 0, tile(b, i))),
            _resident((n_blk, n_cmp)),
            _resident((2, tq, NSA_GROUP * tq)),
        ]
        args = [qt, kc, vct, k, vt, k, vt, gates_t, ovl, bias]
        aliases = {}
        if prev is not None:
            in_specs.append(pl.BlockSpec(memory_space=pl.ANY))
            aliases = {len(args): 0}
            args.append(prev)
        return pl.pallas_call(
            functools.partial(_attn_kernel, tq=tq, tiles=tiles),
            grid=(B, hi - lo),
            in_specs=in_specs,
            out_specs=pl.BlockSpec((tq, NSA_WIDTH), lambda b, i: (tile(b, i), 0)),
            out_shape=jax.ShapeDtypeStruct((T, NSA_WIDTH), F32),
            input_output_aliases=aliases,
            compiler_params=pltpu.CompilerParams(
                dimension_semantics=("parallel", "arbitrary"),
                vmem_limit_bytes=VMEM_LIMIT),
            name=f"nsa_attn_{lo}_{hi}",
        )(*args)

    assert [t for lo, hi in ATTN_TILE_SPLITS for t in range(lo, hi)] == list(range(nq))
    out = None
    for tiles in ATTN_TILE_SPLITS:
        out = call(tiles, out)
    return out


def _out_ffn_kernel(ylru_ref, ynsa_ref, h_ref, gl_ref, gn_ref, w_ref, g2_ref, b2_ref,
                    wg_ref, wu_ref, wo_ref, g3_ref, b3_ref, o_ref):
    yl = _rms_norm(ylru_ref[...], gl_ref[...]).astype(BF16)
    yn = _rms_norm(ynsa_ref[...], gn_ref[...]).astype(BF16)
    mix = _dot(yl, w_ref[0:LRU_WIDTH, :]) + _dot(yn, w_ref[LRU_WIDTH:, :])
    h2 = _layer_norm(ALPHA * h_ref[...] + mix, g2_ref[...], b2_ref[...])
    o_ref[...] = _ffn_half_step(h2, wg_ref, wu_ref, wo_ref, g3_ref, b3_ref)


def _out_proj_ffn(y_lru, y_nsa, h, gn_lru, gn_nsa, w_mix, g2, b2, w_in, w_out, g3, b3, *,
                  tm=512):
    T, D = h.shape
    row = lambda v: v.reshape(1, -1)
    return pl.pallas_call(
        _out_ffn_kernel,
        grid=(T // tm,),
        in_specs=[
            pl.BlockSpec((tm, LRU_WIDTH), lambda i: (i, 0)),
            pl.BlockSpec((tm, NSA_WIDTH), lambda i: (i, 0)),
            pl.BlockSpec((tm, D), lambda i: (i, 0)),
            _resident((1, LRU_WIDTH)), _resident((1, NSA_WIDTH)),
            _resident((LRU_WIDTH + NSA_WIDTH, D)),
            _resident((1, D)), _resident((1, D)),
            _resident((D, D_FF)), _resident((D, D_FF)), _resident((D_FF, D)),
            _resident((1, D)), _resident((1, D)),
        ],
        out_specs=pl.BlockSpec((tm, D), lambda i: (i, 0)),
        out_shape=jax.ShapeDtypeStruct((T, D), F32),
        compiler_params=pltpu.CompilerParams(
            dimension_semantics=("parallel",), vmem_limit_bytes=VMEM_LIMIT),
        name="out_proj_ffn",
    )(y_lru, y_nsa, h, row(gn_lru), row(gn_nsa), w_mix.astype(BF16), row(g2), row(b2),
      w_in[:, :D_FF].astype(BF16), w_in[:, D_FF:].astype(BF16), w_out.astype(BF16),
      row(g3), row(b3))


def kernel(x, ffn1_w_in, ffn1_w_out, ln1_g, ln1_b, mix_w_in, conv_w, conv_b, lru_w_a, lru_b_a,
           lru_w_x, lru_b_x, lru_lam, cmp_pe_k, cmp_w1_k, cmp_w2_k, cmp_pe_v, cmp_w1_v,
           cmp_w2_v, gn_lru, gn_nsa, mix_w_out, ln2_g, ln2_b, ffn2_w_in, ffn2_w_out, ln3_g,
           ln3_b):
    B, S, D = x.shape
    h = x.reshape(B * S, D)
    for l in range(DEPTH):
        h = _ffn_ln(h, ffn1_w_in[l], ffn1_w_out[l], ln1_g[l], ln1_b[l])
        lru, qt, cmp_in, k, vt, gates_t = _proj(h, mix_w_in[l], S)
        y_lru = _lru(lru, conv_w[l], conv_b[l], lru_w_a[l], lru_b_a[l], lru_w_x[l], lru_b_x[l],
                     lru_lam[l], B, S)
        kc, vct = _compress(cmp_in, cmp_pe_k[l], cmp_w1_k[l], cmp_w2_k[l],
                            cmp_pe_v[l], cmp_w1_v[l], cmp_w2_v[l], B, S)
        y_nsa = _attention(qt, k, vt, kc, vct, gates_t, B, S)
        h = _out_proj_ffn(y_lru, y_nsa, h, gn_lru[l], gn_nsa[l], mix_w_out[l], ln2_g[l], ln2_b[l],
                          ffn2_w_in[l], ffn2_w_out[l], ln3_g[l], ln3_b[l])
    return h.reshape(B, S, D)
```

```python
import functools

import numpy as np
import jax
import jax.numpy as jnp
from jax import lax
from jax.experimental import pallas as pl
from jax.experimental.pallas import tpu as pltpu

F32 = jnp.float32
BF16 = jnp.bfloat16

D_MODEL = 1024
LRU_WIDTH = 512
LRU_HEADS = 8
LRU_BLOCK = 64
CONV_WIDTH = 4
LRU_C = 8.0
NSA_Q_HEADS = 8
NSA_KV_HEADS = 2
NSA_GROUP = 4
HEAD_DIM = 64
NSA_WIDTH = 512
KV_WIDTH = 128
CMP_BLOCK = 32
CMP_STRIDE = 16
CMP_HIDDEN = 256
SLC_BLOCK = 64
SLC_SHIFT = 6
SLC_TOP_N = 16
N_LOCAL_BLOCKS = 2
WINDOW = 512
ROPE_THETA = 10000.0
D_FF = 2816
DEPTH = 1
ALPHA = (2.0 * DEPTH) ** 0.25
LN_EPS = 1e-5
RMS_EPS = 1e-6
NEG = -1e30
LOG2E = 1.4426950408889634
F32_TINY = 1.1754943508222875e-38

LANES = 128
BF16_ROWS = 16
VMEM_LIMIT = 48 * 1024 * 1024
N_GATE = 3 * NSA_GROUP
GATE_ROWS = 16
V_ROWS = HEAD_DIM + BF16_ROWS


def _dot(a, b):
    return jnp.dot(a, b, preferred_element_type=F32)


def _dot_nt(a, b):
    return lax.dot_general(a, b, (((1,), (1,)), ((), ())), preferred_element_type=F32)


def _layer_norm(y, g, b):
    mu = jnp.mean(y, axis=-1, keepdims=True)
    d = y - mu
    var = jnp.mean(d * d, axis=-1, keepdims=True)
    return d * lax.rsqrt(var + LN_EPS) * g + b


def _rms_norm(y, g):
    return y * lax.rsqrt(jnp.mean(y * y, axis=-1, keepdims=True) + RMS_EPS) * g


def _silu(x):
    return x * jax.nn.sigmoid(x)


def _resident(shape):
    return pl.BlockSpec(shape, lambda *_: (0,) * len(shape), pipeline_mode=pl.Buffered(1))


def _ffn_half_step(x, wg_ref, wu_ref, wo_ref, g_ref, b_ref):
    xb = x.astype(BF16)
    gate = _dot(xb, wg_ref[...])
    up = _dot(xb, wu_ref[...])
    act = (_silu(gate) * up).astype(BF16)
    y = ALPHA * x + 0.5 * _dot(act, wo_ref[...])
    return _layer_norm(y, g_ref[...], b_ref[...])


def _ffn_ln_kernel(x_ref, wg_ref, wu_ref, wo_ref, g_ref, b_ref, o_ref):
    o_ref[...] = _ffn_half_step(x_ref[...], wg_ref, wu_ref, wo_ref, g_ref, b_ref)


def _ffn_ln(x, w_in, w_out, g, b, *, tm=1024):
    T, D = x.shape
    wg = w_in[:, :D_FF].astype(BF16)
    wu = w_in[:, D_FF:].astype(BF16)
    wo = w_out.astype(BF16)
    return pl.pallas_call(
        _ffn_ln_kernel,
        grid=(T // tm,),
        in_specs=[
            pl.BlockSpec((tm, D), lambda i: (i, 0)),
            _resident((D, D_FF)), _resident((D, D_FF)), _resident((D_FF, D)),
            _resident((1, D)), _resident((1, D)),
        ],
        out_specs=pl.BlockSpec((tm, D), lambda i: (i, 0)),
        out_shape=jax.ShapeDtypeStruct((T, D), F32),
        compiler_params=pltpu.CompilerParams(
            dimension_semantics=("parallel",), vmem_limit_bytes=VMEM_LIMIT),
        name="ffn_ln",
    )(x, wg, wu, wo, g.reshape(1, D), b.reshape(1, D))


Q_COL0 = 2 * LRU_WIDTH
KCMP_COL0 = Q_COL0 + NSA_WIDTH
VCMP_COL0 = KCMP_COL0 + KV_WIDTH
KSLC_COL0 = VCMP_COL0 + KV_WIDTH
VSLC_COL0 = KSLC_COL0 + KV_WIDTH
KWIN_COL0 = VSLC_COL0 + KV_WIDTH
VWIN_COL0 = KWIN_COL0 + KV_WIDTH
GATE_COL0 = VWIN_COL0 + KV_WIDTH
VT_ROW0 = NSA_WIDTH
GT_ROW0 = VT_ROW0 + 2 * KV_WIDTH
WT_ROWS = GT_ROW0 + NSA_KV_HEADS * GATE_ROWS


def _rope_chunk(xc, cos, sin_signed):
    lane = lax.broadcasted_iota(jnp.int32, xc.shape, 1)
    first = (lane & (HEAD_DIM - 1)) < (HEAD_DIM // 2)
    partner = jnp.where(first, pltpu.roll(xc, LANES - HEAD_DIM // 2, axis=1),
                        pltpu.roll(xc, HEAD_DIM // 2, axis=1))
    return xc * cos + partner * sin_signed


def _proj_kernel(h_ref, wn_ref, wt_ref, cos_ref, sin_ref, cost_ref, sint_ref,
                 lru_ref, qt_ref, cmp_ref, k_ref, vt_ref, gt_ref, seg_scr, *, seq_tiles):
    tm = h_ref.shape[0]
    half = HEAD_DIM // 2
    hb = h_ref[...].astype(BF16)
    p = _dot(hb, wn_ref[...])
    pt = _dot_nt(wt_ref[...], hb)
    lru_ref[...] = p[:, :Q_COL0]

    cos_t = cost_ref[...]
    sin_t = sint_ref[...]
    scale = HEAD_DIM ** -0.5 * LOG2E
    for hd in range(NSA_Q_HEADS):
        x1 = pt[hd * HEAD_DIM:hd * HEAD_DIM + half]
        x2 = pt[hd * HEAD_DIM + half:(hd + 1) * HEAD_DIM]
        qt_ref[hd] = (jnp.concatenate([x1 * cos_t - x2 * sin_t, x2 * cos_t + x1 * sin_t], axis=0)
                      * scale).astype(BF16)
    ones = jnp.ones((V_ROWS - HEAD_DIM, tm), F32)
    for c in range(2 * NSA_KV_HEADS):
        v = pt[VT_ROW0 + c * HEAD_DIM:VT_ROW0 + (c + 1) * HEAD_DIM]
        vt_ref[c] = jnp.concatenate([v, ones], axis=0).astype(BF16)
    for g in range(NSA_KV_HEADS):
        gt_ref[g] = jax.nn.sigmoid(pt[GT_ROW0 + g * GATE_ROWS:GT_ROW0 + (g + 1) * GATE_ROWS])

    cos = cos_ref[...]
    sin = sin_ref[...]
    low_seg = lax.broadcasted_iota(jnp.int32, (tm // CMP_STRIDE, LANES), 1) < HEAD_DIM
    for c, xc in enumerate((_rope_chunk(p[:, Q_COL0:Q_COL0 + LANES], cos, sin),
                            p[:, Q_COL0 + LANES:Q_COL0 + 2 * LANES])):
        seg_scr[...] = xc
        tok = [seg_scr[pl.ds(j, tm // CMP_STRIDE, stride=CMP_STRIDE), :]
               for j in range(CMP_STRIDE)]
        g0, g1 = [], []
        for j in range(0, CMP_STRIDE, 2):
            g0.append(jnp.where(low_seg, tok[j], pltpu.roll(tok[j + 1], HEAD_DIM, axis=1)))
            g1.append(jnp.where(low_seg, pltpu.roll(tok[j], HEAD_DIM, axis=1), tok[j + 1]))
        cmp_ref[NSA_KV_HEADS * c] = jnp.concatenate(g0, axis=1).astype(BF16)
        cmp_ref[NSA_KV_HEADS * c + 1] = jnp.concatenate(g1, axis=1).astype(BF16)
    lane = lax.broadcasted_iota(jnp.int32, (tm, LANES), 1)
    low = lane < HEAD_DIM
    pos = (pl.program_id(0) % seq_tiles) * tm + lax.broadcasted_iota(jnp.int32, (tm, LANES), 0)
    ext_blk = jnp.where(lane - HEAD_DIM == (pos >> SLC_SHIFT), NEG, 0.0)
    ext_zero = jnp.zeros((tm, LANES), F32)
    for c, ext in enumerate((ext_blk, ext_zero)):
        xc = _rope_chunk(p[:, Q_COL0 + (2 + c) * LANES:Q_COL0 + (3 + c) * LANES], cos, sin)
        k_ref[2 * c] = jnp.where(low, xc, ext).astype(BF16)
        k_ref[2 * c + 1] = jnp.where(low, pltpu.roll(xc, HEAD_DIM, axis=1), ext).astype(BF16)


def _proj(h, w_in, S, *, tm=512):
    T, D = h.shape
    G = NSA_KV_HEADS
    col = lambda c0, n: w_in[:, c0:c0 + n]
    w_nat = jnp.concatenate(
        [col(0, Q_COL0), col(KCMP_COL0, KV_WIDTH), col(VCMP_COL0, KV_WIDTH),
         col(KSLC_COL0, KV_WIDTH), col(KWIN_COL0, KV_WIDTH)], axis=1).astype(BF16)
    gate_rows = [jnp.pad(col(GATE_COL0 + g * N_GATE, N_GATE).T, ((0, GATE_ROWS - N_GATE), (0, 0)))
                 for g in range(G)]
    w_t = jnp.concatenate(
        [col(Q_COL0, NSA_WIDTH).T, col(VSLC_COL0, KV_WIDTH).T, col(VWIN_COL0, KV_WIDTH).T]
        + gate_rows, axis=0).astype(BF16)
    n_nat = w_nat.shape[1]

    half = HEAD_DIM // 2
    inv = ROPE_THETA ** (-jnp.arange(half, dtype=F32) / half)
    ang = jnp.arange(S, dtype=F32)[:, None] * inv[None, :]
    cos = jnp.cos(ang)
    sin = jnp.sin(ang)
    cos_n = jnp.concatenate([cos, cos, cos, cos], axis=1)
    sin_n = jnp.concatenate([-sin, sin, -sin, sin], axis=1)

    nS = S // tm
    seg_w = CMP_STRIDE * HEAD_DIM
    assert G * HEAD_DIM == LANES
    return pl.pallas_call(
        functools.partial(_proj_kernel, seq_tiles=nS),
        grid=(T // tm,),
        in_specs=[
            pl.BlockSpec((tm, D), lambda i: (i, 0)),
            _resident((D, n_nat)), _resident((WT_ROWS, D)),
            pl.BlockSpec((tm, LANES), lambda i: (i % nS, 0)),
            pl.BlockSpec((tm, LANES), lambda i: (i % nS, 0)),
            pl.BlockSpec((half, tm), lambda i: (0, i % nS)),
            pl.BlockSpec((half, tm), lambda i: (0, i % nS)),
        ],
        out_specs=[
            pl.BlockSpec((tm, Q_COL0), lambda i: (i, 0)),
            pl.BlockSpec((NSA_Q_HEADS, HEAD_DIM, tm), lambda i: (0, 0, i)),
            pl.BlockSpec((2 * G, tm // CMP_STRIDE, seg_w), lambda i: (0, i, 0)),
            pl.BlockSpec((2 * G, tm, LANES), lambda i: (0, i, 0)),
            pl.BlockSpec((2 * G, V_ROWS, tm), lambda i: (0, 0, i)),
            pl.BlockSpec((G, GATE_ROWS, tm), lambda i: (0, 0, i)),
        ],
        out_shape=[
            jax.ShapeDtypeStruct((T, Q_COL0), F32),
            jax.ShapeDtypeStruct((NSA_Q_HEADS, HEAD_DIM, T), BF16),
            jax.ShapeDtypeStruct((2 * G, T // CMP_STRIDE, seg_w), BF16),
            jax.ShapeDtypeStruct((2 * G, T, LANES), BF16),
            jax.ShapeDtypeStruct((2 * G, V_ROWS, T), BF16),
            jax.ShapeDtypeStruct((G, GATE_ROWS, T), F32),
        ],
        scratch_shapes=[pltpu.VMEM((tm, LANES), F32)],
        compiler_params=pltpu.CompilerParams(
            dimension_semantics=("parallel",), vmem_limit_bytes=VMEM_LIMIT),
        name="proj",
    )(h, w_nat, w_t, cos_n, sin_n, cos.T, sin.T)


SUBLANES = 8
SCAN_UNROLL = 8
LRU_ROWS = 512


def _lru_kernel(x_ref, gate_ref, cw_ref, cb_ref, wa_ref, ba_ref, wx_ref, bx_ref, lam_ref,
                y_ref, x_scr, a_scr, b_scr):
    S, C = y_ref.shape
    neg_lam = -lam_ref[...]
    softplus = jnp.maximum(neg_lam, 0.0) + jnp.log1p(jnp.exp(-jnp.abs(neg_lam)))

    x_scr[0:SUBLANES, :] = jnp.zeros((SUBLANES, C), F32)
    x_scr[SUBLANES:, :] = x_ref[...]
    for r0 in range(0, S, LRU_ROWS):
        xc = cb_ref[...]
        for d in range(CONV_WIDTH):
            xs = x_scr[SUBLANES + r0 - d:SUBLANES + r0 - d + LRU_ROWS, :]
            xc = xc + xs * cw_ref[CONV_WIDTH - 1 - d:CONV_WIDTH - d, :]
        xb = xc.astype(BF16)
        gate_pre = lambda w_ref: jnp.concatenate(
            [_dot(xb[:, c * LANES:(c + 1) * LANES], w_ref[c]) for c in range(C // LANES)], axis=1)
        r = jax.nn.sigmoid(gate_pre(wa_ref) + ba_ref[...])
        ig = jax.nn.sigmoid(gate_pre(wx_ref) + bx_ref[...])
        log_a = (-LRU_C) * r * softplus
        a = jnp.exp(log_a)
        z = jnp.tanh(-log_a) * (a * a + 1.0)
        mult = z * lax.rsqrt(jnp.maximum(z, F32_TINY))
        if r0 == 0:
            row = lax.broadcasted_iota(jnp.int32, mult.shape, 0)
            mult = jnp.where(row == 0, 1.0, mult)
        a_scr[r0:r0 + LRU_ROWS, :] = a
        b_scr[r0:r0 + LRU_ROWS, :] = mult * (ig * xc)

    sub = lax.broadcasted_iota(jnp.int32, (SUBLANES, C), 0)

    def scan_tile(g, h_prev):
        rows = pl.ds(pl.multiple_of(g * SUBLANES, SUBLANES), SUBLANES)
        at = a_scr[rows, :]
        bt = b_scr[rows, :]
        d = 1
        while d < SUBLANES:
            keep = sub >= d
            a_sh = jnp.where(keep, pltpu.roll(at, d, axis=0), 1.0)
            b_sh = jnp.where(keep, pltpu.roll(bt, d, axis=0), 0.0)
            bt = at * b_sh + bt
            at = at * a_sh
            d *= 2
        h = bt + at * h_prev
        b_scr[rows, :] = h
        return jnp.broadcast_to(h[SUBLANES - 1:SUBLANES, :], (SUBLANES, C))

    def scan_body(i, h_prev):
        for u in range(SCAN_UNROLL):
            h_prev = scan_tile(i * SCAN_UNROLL + u, h_prev)
        return h_prev

    lax.fori_loop(0, S // (SUBLANES * SCAN_UNROLL), scan_body, jnp.zeros((SUBLANES, C), F32))

    for r0 in range(0, S, LRU_ROWS):
        rows = slice(r0, r0 + LRU_ROWS)
        y_ref[rows, :] = b_scr[rows, :] * jax.nn.gelu(gate_ref[rows, :])


def _lru(lru, conv_w, conv_b, w_a, b_a, w_x, b_x, lam, B, S):
    T = lru.shape[0]
    C = LRU_WIDTH
    n_ch = C // LANES
    assert S % (SUBLANES * SCAN_UNROLL) == 0 and S % LRU_ROWS == 0

    def blockdiag(w):
        w = w.reshape(n_ch, 2, LRU_BLOCK, LRU_BLOCK)
        z = jnp.zeros((n_ch, LRU_BLOCK, LRU_BLOCK), w.dtype)
        top = jnp.concatenate([w[:, 0], z], axis=2)
        bot = jnp.concatenate([z, w[:, 1]], axis=2)
        return jnp.concatenate([top, bot], axis=1).astype(BF16)

    vec = lambda v: v.reshape(1, C)
    return pl.pallas_call(
        _lru_kernel,
        grid=(B,),
        in_specs=[
            pl.BlockSpec((S, C), lambda b: (b, 0)),
            pl.BlockSpec((S, C), lambda b: (b, 1)),
            _resident((CONV_WIDTH, C)), _resident((1, C)),
            _resident((n_ch, LANES, LANES)), _resident((1, C)),
            _resident((n_ch, LANES, LANES)), _resident((1, C)),
            _resident((1, C)),
        ],
        out_specs=pl.BlockSpec((S, C), lambda b: (b, 0)),
        out_shape=jax.ShapeDtypeStruct((T, C), F32),
        scratch_shapes=[pltpu.VMEM((SUBLANES + S, C), F32),
                        pltpu.VMEM((S, C), F32),
                        pltpu.VMEM((S, C), F32)],
        compiler_params=pltpu.CompilerParams(
            dimension_semantics=("parallel",), vmem_limit_bytes=VMEM_LIMIT),
        name="lru",
    )(lru, lru, conv_w, vec(conv_b), blockdiag(w_a), vec(b_a.reshape(-1)),
      blockdiag(w_x), vec(b_x.reshape(-1)), vec(lam))


def _cmp_kernel(seg_ref, pek_ref, w1k_ref, w2k_ref, pev_ref, w1v_ref, w2vt_ref, kc_ref, vct_ref):
    n_seg = seg_ref.shape[2]
    half = CMP_STRIDE * HEAD_DIM

    def hidden(idx, pe_ref, w1_ref):
        seg = seg_ref[idx, 0]
        first = _dot(seg, w1_ref[0:half, :])
        second = _dot(seg, w1_ref[half:2 * half, :])
        bias = _dot(pe_ref[...], w1_ref[...])[0:1, :]
        return _silu(first + pltpu.roll(second, n_seg - 1, axis=0) + bias).astype(BF16)

    for g in range(NSA_KV_HEADS):
        tok = _dot(hidden(g, pek_ref, w1k_ref), w2k_ref[...])
        kc_ref[0, g] = jnp.concatenate([tok, jnp.zeros_like(tok)], axis=1).astype(BF16)
        vct_ref[0, g] = _dot_nt(w2vt_ref[...], hidden(NSA_KV_HEADS + g, pev_ref, w1v_ref)
                                ).astype(BF16)


def _compress(cmp_in, pe_k, w1_k, w2_k, pe_v, w1_v, w2_v, B, S):
    G = NSA_KV_HEADS
    n_seg = S // CMP_STRIDE
    seg = cmp_in.reshape(2 * G, B, n_seg, CMP_STRIDE * HEAD_DIM)
    flat = CMP_BLOCK * HEAD_DIM
    pe8 = lambda pe: jnp.broadcast_to(pe.reshape(1, -1), (8, flat)).astype(BF16)
    return pl.pallas_call(
        _cmp_kernel,
        grid=(B,),
        in_specs=[
            pl.BlockSpec((2 * G, 1, n_seg, CMP_STRIDE * HEAD_DIM), lambda b: (0, b, 0, 0)),
            _resident((8, flat)), _resident((flat, CMP_HIDDEN)), _resident((CMP_HIDDEN, HEAD_DIM)),
            _resident((8, flat)), _resident((flat, CMP_HIDDEN)), _resident((HEAD_DIM, CMP_HIDDEN)),
        ],
        out_specs=[
            pl.BlockSpec((1, G, n_seg, LANES), lambda b: (b, 0, 0, 0)),
            pl.BlockSpec((1, G, HEAD_DIM, n_seg), lambda b: (b, 0, 0, 0)),
        ],
        out_shape=[
            jax.ShapeDtypeStruct((B, G, n_seg, LANES), BF16),
            jax.ShapeDtypeStruct((B, G, HEAD_DIM, n_seg), BF16),
        ],
        compiler_params=pltpu.CompilerParams(
            dimension_semantics=("parallel",), vmem_limit_bytes=VMEM_LIMIT),
        name="compress",
    )(seg, pe8(pe_k), w1_k.astype(BF16), w2_k.astype(BF16),
      pe8(pe_v), w1_v.astype(BF16), w2_v.T.astype(BF16))


def _attn_kernel(qt_ref, kc_ref, vct_ref, ks_ref, vst_ref, kw_ref, vwt_ref, gt_ref, ovl_ref,
                 bias_ref, *rest, tq, tiles):
    o_ref = rest[-1]

    def step(n):
        for _ in _interleave(*[
                _attn_step(n, g, qt_ref, kc_ref.at[g], vct_ref.at[g], ks_ref.at[g], vst_ref.at[g],
                           kw_ref.at[g], vwt_ref.at[g], gt_ref.at[g], ovl_ref, bias_ref, o_ref, tq)
                for g in range(NSA_KV_HEADS)]):
            pass

    for n in range(*tiles):
        pl.when(pl.program_id(1) == n - tiles[0])(functools.partial(step, n))


CHUNK_TILES = 1
SCORES_AHEAD = 1
ATTN_TILE_SPLITS = ((0, 4), (4, 6), (6, 8))


def _interleave(*stages):
    live = list(stages)
    while live:
        for st in list(live):
            try:
                next(st)
            except StopIteration:
                live.remove(st)
        yield


def _tile_bias(bias_ref, t0, k0, behind):
    if k0 == t0:
        return bias_ref[0]
    if behind is not None and k0 == t0 - behind:
        return bias_ref[1]
    return None


def _key_chunks(k_ref, vt_ref, k0, n_keys, tile, q, bias_fn):
    starts = list(range(k0, k0 + n_keys, tile))
    return [dict(k_ref=k_ref, vt_ref=vt_ref, tile=tile, q=q, bias_fn=bias_fn,
                 starts=starts[i:i + CHUNK_TILES]) for i in range(0, len(starts), CHUNK_TILES)]


def _scores_stage(ch):
    tiles, m = [], None
    for k0 in ch["starts"]:
        s = _dot(ch["k_ref"][k0:k0 + ch["tile"], :], ch["q"])
        bias = ch["bias_fn"](k0)
        if bias is not None:
            s = s + bias
        cm = jnp.max(s, axis=0, keepdims=True)
        m = cm if m is None else jnp.maximum(m, cm)
        tiles.append(s)
        yield
    ch["s"], ch["m"] = tiles, m


def _pv_stage(ch, heads, tq):
    acc = [None] * heads
    for k0, s in zip(ch["starts"], ch["s"]):
        p = jnp.exp2(s - ch["m"]).astype(BF16)
        vt = ch["vt_ref"][:, k0:k0 + ch["tile"]]
        for r in range(heads):
            d = _dot(vt, p[:, r * tq:(r + 1) * tq])
            acc[r] = d if acc[r] is None else acc[r] + d
        yield
    ch["acc"] = acc


def _merge_chunks(chunks, heads, tq):
    if len(chunks) == 1:
        return chunks[0]["acc"]
    m = functools.reduce(jnp.maximum, [ch["m"] for ch in chunks])
    acc = [None] * heads
    for ch in chunks:
        w = jnp.exp2(ch["m"] - m)
        for r in range(heads):
            term = ch["acc"][r] * w[:, r * tq:(r + 1) * tq]
            acc[r] = term if acc[r] is None else acc[r] + term
    return acc


def _rank_stage(out, score, jblk, n_live):
    rank = jnp.zeros(score.shape, jnp.int32)
    for kb in range(n_live):
        sk = score[kb:kb + 1, :]
        ahead = (sk > score) | ((sk == score) & (jblk > kb))
        rank = rank + ahead.astype(jnp.int32)
        if kb % 4 == 3:
            yield
    out["rank"] = rank


def _attn_step(n, g, qt_ref, kc_ref, vct_ref, ks_ref, vst_ref, kw_ref, vwt_ref, gt_ref, ovl_ref,
               bias_ref, o_ref, tq):
    R = NSA_GROUP
    t0 = n * tq
    n_cmp = kc_ref.shape[0]
    n_blk = ovl_ref.shape[0]
    q_all = jnp.concatenate([qt_ref[g * R + r] for r in range(R)], axis=1)
    q_pad = jnp.concatenate([q_all, jnp.zeros((LANES - HEAD_DIM, R * tq), BF16)], axis=0)
    head = lambda a, r: a[:, r * tq:(r + 1) * tq]

    kw0 = max(t0 - WINDOW, 0)
    win = _key_chunks(kw_ref, vwt_ref, kw0, t0 + tq - kw0, tq, q_pad,
                      lambda k0: _tile_bias(bias_ref, t0, k0, WINDOW))
    win_scores = _scores_stage(win[0])
    next(win_scores)
    yield

    n_vis = (t0 + tq - CMP_BLOCK) // CMP_STRIDE + 1
    n_vis = min(n_cmp, -(-n_vis // BF16_ROWS) * BF16_ROWS)
    tpos_c = t0 + (lax.broadcasted_iota(jnp.int32, (n_vis, R * tq), 1) & (tq - 1))
    cend = lax.broadcasted_iota(jnp.int32, (n_vis, R * tq), 0) * CMP_STRIDE + (CMP_BLOCK - 1)
    s = _dot(kc_ref[0:n_vis, :], q_pad) + jnp.where(cend <= tpos_c, 0.0, NEG)
    e = jnp.exp2(s - jnp.max(s, axis=0, keepdims=True))
    p_cmp = e * (1.0 / jnp.sum(e, axis=0, keepdims=True))
    if t0 < CMP_BLOCK - 1:
        p_cmp = jnp.where(tpos_c >= CMP_BLOCK - 1, p_cmp, 0.0)
    if n_vis < n_cmp:
        p_cmp = jnp.concatenate([p_cmp, jnp.zeros((n_cmp - n_vis, R * tq), F32)], axis=0)
    p_cmp_b = p_cmp.astype(BF16)
    vct = vct_ref[...]
    o_cmp = [_dot(vct, head(p_cmp_b, r)) for r in range(R)]
    psum = functools.reduce(jnp.add, [head(p_cmp, r) for r in range(R)])
    yield

    p_hi = psum.astype(BF16)
    p_lo = (psum - p_hi.astype(F32)).astype(BF16)
    ovl = ovl_ref[...]
    imp = _dot(ovl, p_hi) + _dot(ovl, p_lo)
    n_live = min(n_blk, (t0 + tq - 1) // SLC_BLOCK + 1)
    rows = -(-n_live // SUBLANES) * SUBLANES
    jblk = lax.broadcasted_iota(jnp.int32, (rows, tq), 0)
    tpos_b = t0 + lax.broadcasted_iota(jnp.int32, (rows, tq), 1)
    blk_valid = jblk * SLC_BLOCK <= tpos_b
    back = (tpos_b >> SLC_SHIFT) - jblk
    forced = (jblk == 0) | ((back >= 0) & (back < N_LOCAL_BLOCKS))
    score = jnp.where(blk_valid, jnp.where(forced, jnp.inf, imp[0:rows]), -jnp.inf)
    sel = {}
    yield from _interleave(_rank_stage(sel, score, jblk, n_live), win_scores)
    unsel = 1.0 - (blk_valid & (sel["rank"] < SLC_TOP_N)).astype(F32)
    dead = [jnp.ones((n_blk - rows, tq), F32)] if rows < n_blk else []
    sel_ext = jnp.concatenate(
        [unsel] + dead + [jnp.zeros((LANES - HEAD_DIM - n_blk, tq), F32)],
        axis=0).astype(BF16)
    q_sel = jnp.concatenate([q_all, jnp.concatenate([sel_ext] * R, axis=1)], axis=0)

    slc = _key_chunks(ks_ref, vst_ref, 0, t0 + tq, tq, q_sel,
                      lambda k0: _tile_bias(bias_ref, t0, k0, None))
    chunks = win + slc
    for ch in chunks[1:SCORES_AHEAD]:
        yield from _scores_stage(ch)
    for idx, ch in enumerate(chunks):
        stages = [_pv_stage(ch, R, tq)]
        if idx + SCORES_AHEAD < len(chunks):
            stages.append(_scores_stage(chunks[idx + SCORES_AHEAD]))
        yield from _interleave(*stages)
    acc_w = _merge_chunks(win, R, tq)
    acc_s = _merge_chunks(slc, R, tq)
    yield

    gt = gt_ref[...]
    outs = []
    for r in range(R):
        g_cmp = gt[3 * r:3 * r + 1]
        g_slc = gt[3 * r + 1:3 * r + 2] * (1.0 / acc_s[r][HEAD_DIM:HEAD_DIM + 1])
        g_win = gt[3 * r + 2:3 * r + 3] * (1.0 / acc_w[r][HEAD_DIM:HEAD_DIM + 1])
        outs.append(g_cmp * o_cmp[r] + g_slc * acc_s[r][:HEAD_DIM] + g_win * acc_w[r][:HEAD_DIM])
    width = R * HEAD_DIM
    o_ref[:, g * width:(g + 1) * width] = jnp.concatenate(outs, axis=0).T


def _overlap_t(n_cmp_pad, n_blk):
    cs = np.arange(n_cmp_pad) * CMP_STRIDE
    ce = cs + CMP_BLOCK - 1
    ss = np.arange(n_blk) * SLC_BLOCK
    se = ss + SLC_BLOCK - 1
    return ((cs[None, :] <= se[:, None]) & (ce[None, :] >= ss[:, None])).astype(np.float32)


def _attention(qt, k, vt, kc, vct, gates_t, B, S, *, tq=256):
    T = qt.shape[2]
    nq = S // tq
    n_cmp = S // CMP_STRIDE
    n_blk = S // SLC_BLOCK
    G = NSA_KV_HEADS
    ovl = jnp.asarray(_overlap_t(n_cmp, n_blk), BF16)
    assert WINDOW % tq == 0 and WINDOW + tq <= S
    assert HEAD_DIM + n_blk <= LANES
    assert tq & (tq - 1) == 0
    key_off = np.arange(tq)[:, None]
    q_off = np.tile(np.arange(tq), NSA_GROUP)[None, :]
    bias = jnp.asarray(np.stack([np.where(key_off <= q_off, 0.0, NEG),
                                 np.where(key_off > q_off, 0.0, NEG)]), F32)
    k_spec = lambda c: pl.BlockSpec((G, S, LANES), lambda b, i: (c, b, 0))
    vt_spec = lambda c: pl.BlockSpec((G, V_ROWS, S), lambda b, i: (c, 0, b))

    def call(tiles, prev):
        lo, hi = tiles
        tile = lambda b, i: b * nq + lo + i
        in_specs = [
            pl.BlockSpec((NSA_Q_HEADS, HEAD_DIM, tq), lambda b, i: (0, 0, tile(b, i))),
            pl.BlockSpec((None, G, n_cmp, LANES), lambda b, i: (b, 0, 0, 0)),
            pl.BlockSpec((None, G, HEAD_DIM, n_cmp), lambda b, i: (b, 0, 0, 0)),
            k_spec(0), vt_spec(0), k_spec(1), vt_spec(1),
            pl.BlockSpec((G, GATE_ROWS, tq), lambda b, i: (0, 0, tile(b, i))),
            _resident((n_blk, n_cmp)),
            _resident((2, tq, NSA_GROUP * tq)),
        ]
        args = [qt, kc, vct, k, vt, k, vt, gates_t, ovl, bias]
        aliases = {}
        if prev is not None:
            in_specs.append(pl.BlockSpec(memory_space=pl.ANY))
            aliases = {len(args): 0}
            args.append(prev)
        return pl.pallas_call(
            functools.partial(_attn_kernel, tq=tq, tiles=tiles),
            grid=(B, hi - lo),
            in_specs=in_specs,
            out_specs=pl.BlockSpec((tq, NSA_WIDTH), lambda b, i: (tile(b, i), 0)),
            out_shape=jax.ShapeDtypeStruct((T, NSA_WIDTH), F32),
            input_output_aliases=aliases,
            compiler_params=pltpu.CompilerParams(
                dimension_semantics=("parallel", "arbitrary"),
                vmem_limit_bytes=VMEM_LIMIT),
            name=f"nsa_attn_{lo}_{hi}",
        )(*args)

    assert [t for lo, hi in ATTN_TILE_SPLITS for t in range(lo, hi)] == list(range(nq))
    out = None
    for tiles in ATTN_TILE_SPLITS:
        out = call(tiles, out)
    return out


def _out_ffn_kernel(ylru_ref, ynsa_ref, h_ref, gl_ref, gn_ref, w_ref, g2_ref, b2_ref,
                    wg_ref, wu_ref, wo_ref, g3_ref, b3_ref, o_ref):
    yl = _rms_norm(ylru_ref[...], gl_ref[...]).astype(BF16)
    yn = _rms_norm(ynsa_ref[...], gn_ref[...]).astype(BF16)
    mix = _dot(yl, w_ref[0:LRU_WIDTH, :]) + _dot(yn, w_ref[LRU_WIDTH:, :])
    h2 = _layer_norm(ALPHA * h_ref[...] + mix, g2_ref[...], b2_ref[...])
    o_ref[...] = _ffn_half_step(h2, wg_ref, wu_ref, wo_ref, g3_ref, b3_ref)


def _out_proj_ffn(y_lru, y_nsa, h, gn_lru, gn_nsa, w_mix, g2, b2, w_in, w_out, g3, b3, *,
                  tm=512):
    T, D = h.shape
    row = lambda v: v.reshape(1, -1)
    return pl.pallas_call(
        _out_ffn_kernel,
        grid=(T // tm,),
        in_specs=[
            pl.BlockSpec((tm, LRU_WIDTH), lambda i: (i, 0)),
            pl.BlockSpec((tm, NSA_WIDTH), lambda i: (i, 0)),
            pl.BlockSpec((tm, D), lambda i: (i, 0)),
            _resident((1, LRU_WIDTH)), _resident((1, NSA_WIDTH)),
            _resident((LRU_WIDTH + NSA_WIDTH, D)),
            _resident((1, D)), _resident((1, D)),
            _resident((D, D_FF)), _resident((D, D_FF)), _resident((D_FF, D)),
            _resident((1, D)), _resident((1, D)),
        ],
        out_specs=pl.BlockSpec((tm, D), lambda i: (i, 0)),
        out_shape=jax.ShapeDtypeStruct((T, D), F32),
        compiler_params=pltpu.CompilerParams(
            dimension_semantics=("parallel",), vmem_limit_bytes=VMEM_LIMIT),
        name="out_proj_ffn",
    )(y_lru, y_nsa, h, row(gn_lru), row(gn_nsa), w_mix.astype(BF16), row(g2), row(b2),
      w_in[:, :D_FF].astype(BF16), w_in[:, D_FF:].astype(BF16), w_out.astype(BF16),
      row(g3), row(b3))


def kernel(x, ffn1_w_in, ffn1_w_out, ln1_g, ln1_b, mix_w_in, conv_w, conv_b, lru_w_a, lru_b_a,
           lru_w_x, lru_b_x, lru_lam, cmp_pe_k, cmp_w1_k, cmp_w2_k, cmp_pe_v, cmp_w1_v,
           cmp_w2_v, gn_lru, gn_nsa, mix_w_out, ln2_g, ln2_b, ffn2_w_in, ffn2_w_out, ln3_g,
           ln3_b):
    B, S, D = x.shape
    h = x.reshape(B * S, D)
    for l in range(DEPTH):
        h = _ffn_ln(h, ffn1_w_in[l], ffn1_w_out[l], ln1_g[l], ln1_b[l])
        lru, qt, cmp_in, k, vt, gates_t = _proj(h, mix_w_in[l], S)
        y_lru = _lru(lru, conv_w[l], conv_b[l], lru_w_a[l], lru_b_a[l], lru_w_x[l], lru_b_x[l],
                     lru_lam[l], B, S)
        kc, vct = _compress(cmp_in, cmp_pe_k[l], cmp_w1_k[l], cmp_w2_k[l],
                            cmp_pe_v[l], cmp_w1_v[l], cmp_w2_v[l], B, S)
        y_nsa = _attention(qt, k, vt, kc, vct, gates_t, B, S)
        h = _out_proj_ffn(y_lru, y_nsa, h, gn_lru[l], gn_nsa[l], mix_w_out[l], ln2_g[l], ln2_b[l],
                          ffn2_w_in[l], ffn2_w_out[l], ln3_g[l], ln3_b[l])
    return h.reshape(B, S, D)
```

```python
import functools

import numpy as np
import jax
import jax.numpy as jnp
from jax import lax
from jax.experimental import pallas as pl
from jax.experimental.pallas import tpu as pltpu

F32 = jnp.float32
BF16 = jnp.bfloat16

D_MODEL = 1024
LRU_WIDTH = 512
LRU_HEADS = 8
LRU_BLOCK = 64
CONV_WIDTH = 4
LRU_C = 8.0
NSA_Q_HEADS = 8
NSA_KV_HEADS = 2
NSA_GROUP = 4
HEAD_DIM = 64
NSA_WIDTH = 512
KV_WIDTH = 128
CMP_BLOCK = 32
CMP_STRIDE = 16
CMP_HIDDEN = 256
SLC_BLOCK = 64
SLC_SHIFT = 6
SLC_TOP_N = 16
N_LOCAL_BLOCKS = 2
WINDOW = 512
ROPE_THETA = 10000.0
D_FF = 2816
DEPTH = 1
ALPHA = (2.0 * DEPTH) ** 0.25
LN_EPS = 1e-5
RMS_EPS = 1e-6
NEG = -1e30
LOG2E = 1.4426950408889634
F32_TINY = 1.1754943508222875e-38

LANES = 128
BF16_ROWS = 16
VMEM_LIMIT = 48 * 1024 * 1024
N_GATE = 3 * NSA_GROUP
GATE_ROWS = 16
V_ROWS = HEAD_DIM + BF16_ROWS


def _dot(a, b):
    return jnp.dot(a, b, preferred_element_type=F32)


def _dot_nt(a, b):
    return lax.dot_general(a, b, (((1,), (1,)), ((), ())), preferred_element_type=F32)


def _layer_norm(y, g, b):
    mu = jnp.mean(y, axis=-1, keepdims=True)
    d = y - mu
    var = jnp.mean(d * d, axis=-1, keepdims=True)
    return d * lax.rsqrt(var + LN_EPS) * g + b


def _rms_norm(y, g):
    return y * lax.rsqrt(jnp.mean(y * y, axis=-1, keepdims=True) + RMS_EPS) * g


def _silu(x):
    return x * jax.nn.sigmoid(x)


def _resident(shape):
    return pl.BlockSpec(shape, lambda *_: (0,) * len(shape), pipeline_mode=pl.Buffered(1))


def _ffn_half_step(x, wg_ref, wu_ref, wo_ref, g_ref, b_ref):
    xb = x.astype(BF16)
    gate = _dot(xb, wg_ref[...])
    up = _dot(xb, wu_ref[...])
    act = (_silu(gate) * up).astype(BF16)
    y = ALPHA * x + 0.5 * _dot(act, wo_ref[...])
    return _layer_norm(y, g_ref[...], b_ref[...])


def _ffn_ln_kernel(x_ref, wg_ref, wu_ref, wo_ref, g_ref, b_ref, o_ref):
    o_ref[...] = _ffn_half_step(x_ref[...], wg_ref, wu_ref, wo_ref, g_ref, b_ref)


def _ffn_ln(x, w_in, w_out, g, b, *, tm=1024):
    T, D = x.shape
    wg = w_in[:, :D_FF].astype(BF16)
    wu = w_in[:, D_FF:].astype(BF16)
    wo = w_out.astype(BF16)
    return pl.pallas_call(
        _ffn_ln_kernel,
        grid=(T // tm,),
        in_specs=[
            pl.BlockSpec((tm, D), lambda i: (i, 0)),
            _resident((D, D_FF)), _resident((D, D_FF)), _resident((D_FF, D)),
            _resident((1, D)), _resident((1, D)),
        ],
        out_specs=pl.BlockSpec((tm, D), lambda i: (i, 0)),
        out_shape=jax.ShapeDtypeStruct((T, D), F32),
        compiler_params=pltpu.CompilerParams(
            dimension_semantics=("parallel",), vmem_limit_bytes=VMEM_LIMIT),
        name="ffn_ln",
    )(x, wg, wu, wo, g.reshape(1, D), b.reshape(1, D))


Q_COL0 = 2 * LRU_WIDTH
KCMP_COL0 = Q_COL0 + NSA_WIDTH
VCMP_COL0 = KCMP_COL0 + KV_WIDTH
KSLC_COL0 = VCMP_COL0 + KV_WIDTH
VSLC_COL0 = KSLC_COL0 + KV_WIDTH
KWIN_COL0 = VSLC_COL0 + KV_WIDTH
VWIN_COL0 = KWIN_COL0 + KV_WIDTH
GATE_COL0 = VWIN_COL0 + KV_WIDTH
VT_ROW0 = NSA_WIDTH
GT_ROW0 = VT_ROW0 + 2 * KV_WIDTH
WT_ROWS = GT_ROW0 + NSA_KV_HEADS * GATE_ROWS


def _rope_chunk(xc, cos, sin_signed):
    lane = lax.broadcasted_iota(jnp.int32, xc.shape, 1)
    first = (lane & (HEAD_DIM - 1)) < (HEAD_DIM // 2)
    partner = jnp.where(first, pltpu.roll(xc, LANES - HEAD_DIM // 2, axis=1),
                        pltpu.roll(xc, HEAD_DIM // 2, axis=1))
    return xc * cos + partner * sin_signed


def _proj_kernel(h_ref, wn_ref, wt_ref, cos_ref, sin_ref, cost_ref, sint_ref,
                 lru_ref, qt_ref, cmp_ref, k_ref, vt_ref, gt_ref, seg_scr, *, seq_tiles):
    tm = h_ref.shape[0]
    half = HEAD_DIM // 2
    hb = h_ref[...].astype(BF16)
    p = _dot(hb, wn_ref[...])
    pt = _dot_nt(wt_ref[...], hb)
    lru_ref[...] = p[:, :Q_COL0]

    cos_t = cost_ref[...]
    sin_t = sint_ref[...]
    scale = HEAD_DIM ** -0.5 * LOG2E
    for hd in range(NSA_Q_HEADS):
        x1 = pt[hd * HEAD_DIM:hd * HEAD_DIM + half]
        x2 = pt[hd * HEAD_DIM + half:(hd + 1) * HEAD_DIM]
        qt_ref[hd] = (jnp.concatenate([x1 * cos_t - x2 * sin_t, x2 * cos_t + x1 * sin_t], axis=0)
                      * scale).astype(BF16)
    ones = jnp.ones((V_ROWS - HEAD_DIM, tm), F32)
    for c in range(2 * NSA_KV_HEADS):
        v = pt[VT_ROW0 + c * HEAD_DIM:VT_ROW0 + (c + 1) * HEAD_DIM]
        vt_ref[c] = jnp.concatenate([v, ones], axis=0).astype(BF16)
    for g in range(NSA_KV_HEADS):
        gt_ref[g] = jax.nn.sigmoid(pt[GT_ROW0 + g * GATE_ROWS:GT_ROW0 + (g + 1) * GATE_ROWS])

    cos = cos_ref[...]
    sin = sin_ref[...]
    low_seg = lax.broadcasted_iota(jnp.int32, (tm // CMP_STRIDE, LANES), 1) < HEAD_DIM
    for c, xc in enumerate((_rope_chunk(p[:, Q_COL0:Q_COL0 + LANES], cos, sin),
                            p[:, Q_COL0 + LANES:Q_COL0 + 2 * LANES])):
        seg_scr[...] = xc
        tok = [seg_scr[pl.ds(j, tm // CMP_STRIDE, stride=CMP_STRIDE), :]
               for j in range(CMP_STRIDE)]
        g0, g1 = [], []
        for j in range(0, CMP_STRIDE, 2):
            g0.append(jnp.where(low_seg, tok[j], pltpu.roll(tok[j + 1], HEAD_DIM, axis=1)))
            g1.append(jnp.where(low_seg, pltpu.roll(tok[j], HEAD_DIM, axis=1), tok[j + 1]))
        cmp_ref[NSA_KV_HEADS * c] = jnp.concatenate(g0, axis=1).astype(BF16)
        cmp_ref[NSA_KV_HEADS * c + 1] = jnp.concatenate(g1, axis=1).astype(BF16)
    lane = lax.broadcasted_iota(jnp.int32, (tm, LANES), 1)
    low = lane < HEAD_DIM
    pos = (pl.program_id(0) % seq_tiles) * tm + lax.broadcasted_iota(jnp.int32, (tm, LANES), 0)
    ext_blk = jnp.where(lane - HEAD_DIM == (pos >> SLC_SHIFT), NEG, 0.0)
    ext_zero = jnp.zeros((tm, LANES), F32)
    for c, ext in enumerate((ext_blk, ext_zero)):
        xc = _rope_chunk(p[:, Q_COL0 + (2 + c) * LANES:Q_COL0 + (3 + c) * LANES], cos, sin)
        k_ref[2 * c] = jnp.where(low, xc, ext).astype(BF16)
        k_ref[2 * c + 1] = jnp.where(low, pltpu.roll(xc, HEAD_DIM, axis=1), ext).astype(BF16)


def _proj(h, w_in, S, *, tm=1024):
    T, D = h.shape
    G = NSA_KV_HEADS
    col = lambda c0, n: w_in[:, c0:c0 + n]
    w_nat = jnp.concatenate(
        [col(0, Q_COL0), col(KCMP_COL0, KV_WIDTH), col(VCMP_COL0, KV_WIDTH),
         col(KSLC_COL0, KV_WIDTH), col(KWIN_COL0, KV_WIDTH)], axis=1).astype(BF16)
    gate_rows = [jnp.pad(col(GATE_COL0 + g * N_GATE, N_GATE).T, ((0, GATE_ROWS - N_GATE), (0, 0)))
                 for g in range(G)]
    w_t = jnp.concatenate(
        [col(Q_COL0, NSA_WIDTH).T, col(VSLC_COL0, KV_WIDTH).T, col(VWIN_COL0, KV_WIDTH).T]
        + gate_rows, axis=0).astype(BF16)
    n_nat = w_nat.shape[1]

    half = HEAD_DIM // 2
    inv = ROPE_THETA ** (-jnp.arange(half, dtype=F32) / half)
    ang = jnp.arange(S, dtype=F32)[:, None] * inv[None, :]
    cos = jnp.cos(ang)
    sin = jnp.sin(ang)
    cos_n = jnp.concatenate([cos, cos, cos, cos], axis=1)
    sin_n = jnp.concatenate([-sin, sin, -sin, sin], axis=1)

    nS = S // tm
    seg_w = CMP_STRIDE * HEAD_DIM
    assert G * HEAD_DIM == LANES
    return pl.pallas_call(
        functools.partial(_proj_kernel, seq_tiles=nS),
        grid=(T // tm,),
        in_specs=[
            pl.BlockSpec((tm, D), lambda i: (i, 0)),
            _resident((D, n_nat)), _resident((WT_ROWS, D)),
            pl.BlockSpec((tm, LANES), lambda i: (i % nS, 0)),
            pl.BlockSpec((tm, LANES), lambda i: (i % nS, 0)),
            pl.BlockSpec((half, tm), lambda i: (0, i % nS)),
            pl.BlockSpec((half, tm), lambda i: (0, i % nS)),
        ],
        out_specs=[
            pl.BlockSpec((tm, Q_COL0), lambda i: (i, 0)),
            pl.BlockSpec((NSA_Q_HEADS, HEAD_DIM, tm), lambda i: (0, 0, i)),
            pl.BlockSpec((2 * G, tm // CMP_STRIDE, seg_w), lambda i: (0, i, 0)),
            pl.BlockSpec((2 * G, tm, LANES), lambda i: (0, i, 0)),
            pl.BlockSpec((2 * G, V_ROWS, tm), lambda i: (0, 0, i)),
            pl.BlockSpec((G, GATE_ROWS, tm), lambda i: (0, 0, i)),
        ],
        out_shape=[
            jax.ShapeDtypeStruct((T, Q_COL0), F32),
            jax.ShapeDtypeStruct((NSA_Q_HEADS, HEAD_DIM, T), BF16),
            jax.ShapeDtypeStruct((2 * G, T // CMP_STRIDE, seg_w), BF16),
            jax.ShapeDtypeStruct((2 * G, T, LANES), BF16),
            jax.ShapeDtypeStruct((2 * G, V_ROWS, T), BF16),
            jax.ShapeDtypeStruct((G, GATE_ROWS, T), F32),
        ],
        scratch_shapes=[pltpu.VMEM((tm, LANES), F32)],
        compiler_params=pltpu.CompilerParams(
            dimension_semantics=("parallel",), vmem_limit_bytes=VMEM_LIMIT),
        name="proj",
    )(h, w_nat, w_t, cos_n, sin_n, cos.T, sin.T)


SUBLANES = 8
SCAN_UNROLL = 8
LRU_ROWS = 512


def _lru_kernel(x_ref, gate_ref, cw_ref, cb_ref, wa_ref, ba_ref, wx_ref, bx_ref, lam_ref,
                y_ref, x_scr, a_scr, b_scr):
    S, C = y_ref.shape
    neg_lam = -lam_ref[...]
    softplus = jnp.maximum(neg_lam, 0.0) + jnp.log1p(jnp.exp(-jnp.abs(neg_lam)))

    x_scr[0:SUBLANES, :] = jnp.zeros((SUBLANES, C), F32)
    x_scr[SUBLANES:, :] = x_ref[...]
    for r0 in range(0, S, LRU_ROWS):
        xc = cb_ref[...]
        for d in range(CONV_WIDTH):
            xs = x_scr[SUBLANES + r0 - d:SUBLANES + r0 - d + LRU_ROWS, :]
            xc = xc + xs * cw_ref[CONV_WIDTH - 1 - d:CONV_WIDTH - d, :]
        xb = xc.astype(BF16)
        gate_pre = lambda w_ref: jnp.concatenate(
            [_dot(xb[:, c * LANES:(c + 1) * LANES], w_ref[c]) for c in range(C // LANES)], axis=1)
        r = 0.5 * jnp.tanh(0.5 * (gate_pre(wa_ref) + ba_ref[...])) + 0.5
        ig = 0.5 * jnp.tanh(0.5 * (gate_pre(wx_ref) + bx_ref[...])) + 0.5
        log_a = (-LRU_C) * r * softplus
        a = jnp.exp(log_a)
        z = jnp.tanh(-log_a) * (a * a + 1.0)
        mult = z * lax.rsqrt(jnp.maximum(z, F32_TINY))
        if r0 == 0:
            row = lax.broadcasted_iota(jnp.int32, mult.shape, 0)
            mult = jnp.where(row == 0, 1.0, mult)
        a_scr[r0:r0 + LRU_ROWS, :] = a
        b_scr[r0:r0 + LRU_ROWS, :] = mult * (ig * xc)

    sub = lax.broadcasted_iota(jnp.int32, (SUBLANES, C), 0)

    def scan_tile(g, h_prev):
        rows = pl.ds(pl.multiple_of(g * SUBLANES, SUBLANES), SUBLANES)
        at = a_scr[rows, :]
        bt = b_scr[rows, :]
        d = 1
        while d < SUBLANES:
            keep = sub >= d
            a_sh = jnp.where(keep, pltpu.roll(at, d, axis=0), 1.0)
            b_sh = jnp.where(keep, pltpu.roll(bt, d, axis=0), 0.0)
            bt = at * b_sh + bt
            at = at * a_sh
            d *= 2
        h = bt + at * h_prev
        b_scr[rows, :] = h
        return jnp.broadcast_to(h[SUBLANES - 1:SUBLANES, :], (SUBLANES, C))

    def scan_body(i, h_prev):
        for u in range(SCAN_UNROLL):
            h_prev = scan_tile(i * SCAN_UNROLL + u, h_prev)
        return h_prev

    lax.fori_loop(0, S // (SUBLANES * SCAN_UNROLL), scan_body, jnp.zeros((SUBLANES, C), F32))

    for r0 in range(0, S, LRU_ROWS):
        rows = slice(r0, r0 + LRU_ROWS)
        y_ref[rows, :] = b_scr[rows, :] * jax.nn.gelu(gate_ref[rows, :])


def _lru(lru, conv_w, conv_b, w_a, b_a, w_x, b_x, lam, B, S):
    T = lru.shape[0]
    C = LRU_WIDTH
    n_ch = C // LANES
    assert S % (SUBLANES * SCAN_UNROLL) == 0 and S % LRU_ROWS == 0

    def blockdiag(w):
        w = w.reshape(n_ch, 2, LRU_BLOCK, LRU_BLOCK)
        z = jnp.zeros((n_ch, LRU_BLOCK, LRU_BLOCK), w.dtype)
        top = jnp.concatenate([w[:, 0], z], axis=2)
        bot = jnp.concatenate([z, w[:, 1]], axis=2)
        return jnp.concatenate([top, bot], axis=1).astype(BF16)

    vec = lambda v: v.reshape(1, C)
    return pl.pallas_call(
        _lru_kernel,
        grid=(B,),
        in_specs=[
            pl.BlockSpec((S, C), lambda b: (b, 0)),
            pl.BlockSpec((S, C), lambda b: (b, 1)),
            _resident((CONV_WIDTH, C)), _resident((1, C)),
            _resident((n_ch, LANES, LANES)), _resident((1, C)),
            _resident((n_ch, LANES, LANES)), _resident((1, C)),
            _resident((1, C)),
        ],
        out_specs=pl.BlockSpec((S, C), lambda b: (b, 0)),
        out_shape=jax.ShapeDtypeStruct((T, C), F32),
        scratch_shapes=[pltpu.VMEM((SUBLANES + S, C), F32),
                        pltpu.VMEM((S, C), F32),
                        pltpu.VMEM((S, C), F32)],
        compiler_params=pltpu.CompilerParams(
            dimension_semantics=("parallel",), vmem_limit_bytes=VMEM_LIMIT),
        name="lru",
    )(lru, lru, conv_w, vec(conv_b), blockdiag(w_a), vec(b_a.reshape(-1)),
      blockdiag(w_x), vec(b_x.reshape(-1)), vec(lam))


def _cmp_kernel(seg_ref, pek_ref, w1k_ref, w2k_ref, pev_ref, w1v_ref, w2vt_ref, kc_ref, vct_ref):
    n_seg = seg_ref.shape[2]
    half = CMP_STRIDE * HEAD_DIM

    def hidden(idx, pe_ref, w1_ref):
        seg = seg_ref[idx, 0]
        first = _dot(seg, w1_ref[0:half, :])
        second = _dot(seg, w1_ref[half:2 * half, :])
        bias = _dot(pe_ref[...], w1_ref[...])[0:1, :]
        return _silu(first + pltpu.roll(second, n_seg - 1, axis=0) + bias).astype(BF16)

    for g in range(NSA_KV_HEADS):
        tok = _dot(hidden(g, pek_ref, w1k_ref), w2k_ref[...])
        kc_ref[0, g] = jnp.concatenate([tok, jnp.zeros_like(tok)], axis=1).astype(BF16)
        vct_ref[0, g] = _dot_nt(w2vt_ref[...], hidden(NSA_KV_HEADS + g, pev_ref, w1v_ref)
                                ).astype(BF16)


def _compress(cmp_in, pe_k, w1_k, w2_k, pe_v, w1_v, w2_v, B, S):
    G = NSA_KV_HEADS
    n_seg = S // CMP_STRIDE
    seg = cmp_in.reshape(2 * G, B, n_seg, CMP_STRIDE * HEAD_DIM)
    flat = CMP_BLOCK * HEAD_DIM
    pe8 = lambda pe: jnp.broadcast_to(pe.reshape(1, -1), (8, flat)).astype(BF16)
    return pl.pallas_call(
        _cmp_kernel,
        grid=(B,),
        in_specs=[
            pl.BlockSpec((2 * G, 1, n_seg, CMP_STRIDE * HEAD_DIM), lambda b: (0, b, 0, 0)),
            _resident((8, flat)), _resident((flat, CMP_HIDDEN)), _resident((CMP_HIDDEN, HEAD_DIM)),
            _resident((8, flat)), _resident((flat, CMP_HIDDEN)), _resident((HEAD_DIM, CMP_HIDDEN)),
        ],
        out_specs=[
            pl.BlockSpec((1, G, n_seg, LANES), lambda b: (b, 0, 0, 0)),
            pl.BlockSpec((1, G, HEAD_DIM, n_seg), lambda b: (b, 0, 0, 0)),
        ],
        out_shape=[
            jax.ShapeDtypeStruct((B, G, n_seg, LANES), BF16),
            jax.ShapeDtypeStruct((B, G, HEAD_DIM, n_seg), BF16),
        ],
        compiler_params=pltpu.CompilerParams(
            dimension_semantics=("parallel",), vmem_limit_bytes=VMEM_LIMIT),
        name="compress",
    )(seg, pe8(pe_k), w1_k.astype(BF16), w2_k.astype(BF16),
      pe8(pe_v), w1_v.astype(BF16), w2_v.T.astype(BF16))


def _attn_kernel(qt_ref, kc_ref, vct_ref, ks_ref, vst_ref, kw_ref, vwt_ref, gt_ref, ovl_ref,
                 bias_ref, *rest, tq, tiles):
    o_ref = rest[-1]

    def step(n):
        for _ in _interleave(*[
                _attn_step(n, g, qt_ref, kc_ref.at[g], vct_ref.at[g], ks_ref.at[g], vst_ref.at[g],
                           kw_ref.at[g], vwt_ref.at[g], gt_ref.at[g], ovl_ref, bias_ref, o_ref, tq)
                for g in range(NSA_KV_HEADS)]):
            pass

    for n in range(*tiles):
        pl.when(pl.program_id(1) == n - tiles[0])(functools.partial(step, n))


CHUNK_TILES = 1
SCORES_AHEAD = 1
ATTN_TILE_SPLITS = ((0, 4), (4, 6), (6, 8))


def _interleave(*stages):
    live = list(stages)
    while live:
        for st in list(live):
            try:
                next(st)
            except StopIteration:
                live.remove(st)
        yield


def _tile_bias(bias_ref, t0, k0, behind):
    if k0 == t0:
        return bias_ref[0]
    if behind is not None and k0 == t0 - behind:
        return bias_ref[1]
    return None


def _key_chunks(k_ref, vt_ref, k0, n_keys, tile, q, bias_fn):
    starts = list(range(k0, k0 + n_keys, tile))
    return [dict(k_ref=k_ref, vt_ref=vt_ref, tile=tile, q=q, bias_fn=bias_fn,
                 starts=starts[i:i + CHUNK_TILES]) for i in range(0, len(starts), CHUNK_TILES)]


def _scores_stage(ch):
    tiles, m = [], None
    for k0 in ch["starts"]:
        s = _dot(ch["k_ref"][k0:k0 + ch["tile"], :], ch["q"])
        bias = ch["bias_fn"](k0)
        if bias is not None:
            s = s + bias
        cm = jnp.max(s, axis=0, keepdims=True)
        m = cm if m is None else jnp.maximum(m, cm)
        tiles.append(s)
        yield
    ch["s"], ch["m"] = tiles, m


def _pv_stage(ch, heads, tq):
    acc = [None] * heads
    for k0, s in zip(ch["starts"], ch["s"]):
        p = jnp.exp2(s - ch["m"]).astype(BF16)
        vt = ch["vt_ref"][:, k0:k0 + ch["tile"]]
        for r in range(heads):
            d = _dot(vt, p[:, r * tq:(r + 1) * tq])
            acc[r] = d if acc[r] is None else acc[r] + d
        yield
    ch["acc"] = acc


def _merge_chunks(chunks, heads, tq):
    if len(chunks) == 1:
        return chunks[0]["acc"]
    m = functools.reduce(jnp.maximum, [ch["m"] for ch in chunks])
    acc = [None] * heads
    for ch in chunks:
        w = jnp.exp2(ch["m"] - m)
        for r in range(heads):
            term = ch["acc"][r] * w[:, r * tq:(r + 1) * tq]
            acc[r] = term if acc[r] is None else acc[r] + term
    return acc


def _rank_stage(out, score, jblk, n_live):
    rank = jnp.zeros(score.shape, jnp.int32)
    for kb in range(n_live):
        sk = score[kb:kb + 1, :]
        ahead = (sk > score) | ((sk == score) & (jblk > kb))
        rank = rank + ahead.astype(jnp.int32)
        if kb % 4 == 3:
            yield
    out["rank"] = rank


def _attn_step(n, g, qt_ref, kc_ref, vct_ref, ks_ref, vst_ref, kw_ref, vwt_ref, gt_ref, ovl_ref,
               bias_ref, o_ref, tq):
    R = NSA_GROUP
    t0 = n * tq
    n_cmp = kc_ref.shape[0]
    n_blk = ovl_ref.shape[0]
    q_all = jnp.concatenate([qt_ref[g * R + r] for r in range(R)], axis=1)
    q_pad = jnp.concatenate([q_all, jnp.zeros((LANES - HEAD_DIM, R * tq), BF16)], axis=0)
    head = lambda a, r: a[:, r * tq:(r + 1) * tq]

    kw0 = max(t0 - WINDOW, 0)
    win = _key_chunks(kw_ref, vwt_ref, kw0, t0 + tq - kw0, tq, q_pad,
                      lambda k0: _tile_bias(bias_ref, t0, k0, WINDOW))
    win_scores = _scores_stage(win[0])
    next(win_scores)
    yield

    n_vis = (t0 + tq - CMP_BLOCK) // CMP_STRIDE + 1
    n_vis = min(n_cmp, -(-n_vis // BF16_ROWS) * BF16_ROWS)
    tpos_c = t0 + (lax.broadcasted_iota(jnp.int32, (n_vis, R * tq), 1) & (tq - 1))
    cend = lax.broadcasted_iota(jnp.int32, (n_vis, R * tq), 0) * CMP_STRIDE + (CMP_BLOCK - 1)
    s = _dot(kc_ref[0:n_vis, :], q_pad) + jnp.where(cend <= tpos_c, 0.0, NEG)
    e = jnp.exp2(s - jnp.max(s, axis=0, keepdims=True))
    p_cmp = e * (1.0 / jnp.sum(e, axis=0, keepdims=True))
    if t0 < CMP_BLOCK - 1:
        p_cmp = jnp.where(tpos_c >= CMP_BLOCK - 1, p_cmp, 0.0)
    if n_vis < n_cmp:
        p_cmp = jnp.concatenate([p_cmp, jnp.zeros((n_cmp - n_vis, R * tq), F32)], axis=0)
    p_cmp_b = p_cmp.astype(BF16)
    vct = vct_ref[...]
    o_cmp = [_dot(vct, head(p_cmp_b, r)) for r in range(R)]
    psum = functools.reduce(jnp.add, [head(p_cmp, r) for r in range(R)])
    yield

    p_hi = psum.astype(BF16)
    p_lo = (psum - p_hi.astype(F32)).astype(BF16)
    ovl = ovl_ref[...]
    imp = _dot(ovl, p_hi) + _dot(ovl, p_lo)
    n_live = min(n_blk, (t0 + tq - 1) // SLC_BLOCK + 1)
    rows = -(-n_live // SUBLANES) * SUBLANES
    jblk = lax.broadcasted_iota(jnp.int32, (rows, tq), 0)
    tpos_b = t0 + lax.broadcasted_iota(jnp.int32, (rows, tq), 1)
    blk_valid = jblk * SLC_BLOCK <= tpos_b
    back = (tpos_b >> SLC_SHIFT) - jblk
    forced = (jblk == 0) | ((back >= 0) & (back < N_LOCAL_BLOCKS))
    score = jnp.where(blk_valid, jnp.where(forced, jnp.inf, imp[0:rows]), -jnp.inf)
    sel = {}
    yield from _interleave(_rank_stage(sel, score, jblk, n_live), win_scores)
    unsel = 1.0 - (blk_valid & (sel["rank"] < SLC_TOP_N)).astype(F32)
    dead = [jnp.ones((n_blk - rows, tq), F32)] if rows < n_blk else []
    sel_ext = jnp.concatenate(
        [unsel] + dead + [jnp.zeros((LANES - HEAD_DIM - n_blk, tq), F32)],
        axis=0).astype(BF16)
    q_sel = jnp.concatenate([q_all, jnp.concatenate([sel_ext] * R, axis=1)], axis=0)

    slc = _key_chunks(ks_ref, vst_ref, 0, t0 + tq, tq, q_sel,
                      lambda k0: _tile_bias(bias_ref, t0, k0, None))
    chunks = win + slc
    for ch in chunks[1:SCORES_AHEAD]:
        yield from _scores_stage(ch)
    for idx, ch in enumerate(chunks):
        stages = [_pv_stage(ch, R, tq)]
        if idx + SCORES_AHEAD < len(chunks):
            stages.append(_scores_stage(chunks[idx + SCORES_AHEAD]))
        yield from _interleave(*stages)
    acc_w = _merge_chunks(win, R, tq)
    acc_s = _merge_chunks(slc, R, tq)
    yield

    gt = gt_ref[...]
    outs = []
    for r in range(R):
        g_cmp = gt[3 * r:3 * r + 1]
        g_slc = gt[3 * r + 1:3 * r + 2] * (1.0 / acc_s[r][HEAD_DIM:HEAD_DIM + 1])
        g_win = gt[3 * r + 2:3 * r + 3] * (1.0 / acc_w[r][HEAD_DIM:HEAD_DIM + 1])
        outs.append(g_cmp * o_cmp[r] + g_slc * acc_s[r][:HEAD_DIM] + g_win * acc_w[r][:HEAD_DIM])
    width = R * HEAD_DIM
    o_ref[:, g * width:(g + 1) * width] = jnp.concatenate(outs, axis=0).T


def _overlap_t(n_cmp_pad, n_blk):
    cs = np.arange(n_cmp_pad) * CMP_STRIDE
    ce = cs + CMP_BLOCK - 1
    ss = np.arange(n_blk) * SLC_BLOCK
    se = ss + SLC_BLOCK - 1
    return ((cs[None, :] <= se[:, None]) & (ce[None, :] >= ss[:, None])).astype(np.float32)


def _attention(qt, k, vt, kc, vct, gates_t, B, S, *, tq=256):
    T = qt.shape[2]
    nq = S // tq
    n_cmp = S // CMP_STRIDE
    n_blk = S // SLC_BLOCK
    G = NSA_KV_HEADS
    ovl = jnp.asarray(_overlap_t(n_cmp, n_blk), BF16)
    assert WINDOW % tq == 0 and WINDOW + tq <= S
    assert HEAD_DIM + n_blk <= LANES
    assert tq & (tq - 1) == 0
    key_off = np.arange(tq)[:, None]
    q_off = np.tile(np.arange(tq), NSA_GROUP)[None, :]
    bias = jnp.asarray(np.stack([np.where(key_off <= q_off, 0.0, NEG),
                                 np.where(key_off > q_off, 0.0, NEG)]), F32)
    k_spec = lambda c: pl.BlockSpec((G, S, LANES), lambda b, i: (c, b, 0))
    vt_spec = lambda c: pl.BlockSpec((G, V_ROWS, S), lambda b, i: (c, 0, b))

    def call(tiles, prev):
        lo, hi = tiles
        tile = lambda b, i: b * nq + lo + i
        in_specs = [
            pl.BlockSpec((NSA_Q_HEADS, HEAD_DIM, tq), lambda b, i: (0, 0, tile(b, i))),
            pl.BlockSpec((None, G, n_cmp, LANES), lambda b, i: (b, 0, 0, 0)),
            pl.BlockSpec((None, G, HEAD_DIM, n_cmp), lambda b, i: (b, 0, 0, 0)),
            k_spec(0), vt_spec(0), k_spec(1), vt_spec(1),
            pl.BlockSpec((G, GATE_ROWS, tq), lambda b, i: (0, 0, tile(b, i))),
            _resident((n_blk, n_cmp)),
            _resident((2, tq, NSA_GROUP * tq)),
        ]
        args = [qt, kc, vct, k, vt, k, vt, gates_t, ovl, bias]
        aliases = {}
        if prev is not None:
            in_specs.append(pl.BlockSpec(memory_space=pl.ANY))
            aliases = {len(args): 0}
            args.append(prev)
        return pl.pallas_call(
            functools.partial(_attn_kernel, tq=tq, tiles=tiles),
            grid=(B, hi - lo),
            in_specs=in_specs,
            out_specs=pl.BlockSpec((tq, NSA_WIDTH), lambda b, i: (tile(b, i), 0)),
            out_shape=jax.ShapeDtypeStruct((T, NSA_WIDTH), F32),
            input_output_aliases=aliases,
            compiler_params=pltpu.CompilerParams(
                dimension_semantics=("parallel", "arbitrary"),
                vmem_limit_bytes=VMEM_LIMIT),
            name=f"nsa_attn_{lo}_{hi}",
        )(*args)

    assert [t for lo, hi in ATTN_TILE_SPLITS for t in range(lo, hi)] == list(range(nq))
    out = None
    for tiles in ATTN_TILE_SPLITS:
        out = call(tiles, out)
    return out


def _out_ffn_kernel(ylru_ref, ynsa_ref, h_ref, gl_ref, gn_ref, w_ref, g2_ref, b2_ref,
                    wg_ref, wu_ref, wo_ref, g3_ref, b3_ref, o_ref):
    yl = _rms_norm(ylru_ref[...], gl_ref[...]).astype(BF16)
    yn = _rms_norm(ynsa_ref[...], gn_ref[...]).astype(BF16)
    mix = _dot(yl, w_ref[0:LRU_WIDTH, :]) + _dot(yn, w_ref[LRU_WIDTH:, :])
    h2 = _layer_norm(ALPHA * h_ref[...] + mix, g2_ref[...], b2_ref[...])
    o_ref[...] = _ffn_half_step(h2, wg_ref, wu_ref, wo_ref, g3_ref, b3_ref)


def _out_proj_ffn(y_lru, y_nsa, h, gn_lru, gn_nsa, w_mix, g2, b2, w_in, w_out, g3, b3, *,
                  tm=512):
    T, D = h.shape
    row = lambda v: v.reshape(1, -1)
    return pl.pallas_call(
        _out_ffn_kernel,
        grid=(T // tm,),
        in_specs=[
            pl.BlockSpec((tm, LRU_WIDTH), lambda i: (i, 0)),
            pl.BlockSpec((tm, NSA_WIDTH), lambda i: (i, 0)),
            pl.BlockSpec((tm, D), lambda i: (i, 0)),
            _resident((1, LRU_WIDTH)), _resident((1, NSA_WIDTH)),
            _resident((LRU_WIDTH + NSA_WIDTH, D)),
            _resident((1, D)), _resident((1, D)),
            _resident((D, D_FF)), _resident((D, D_FF)), _resident((D_FF, D)),
            _resident((1, D)), _resident((1, D)),
        ],
        out_specs=pl.BlockSpec((tm, D), lambda i: (i, 0)),
        out_shape=jax.ShapeDtypeStruct((T, D), F32),
        compiler_params=pltpu.CompilerParams(
            dimension_semantics=("parallel",), vmem_limit_bytes=VMEM_LIMIT),
        name="out_proj_ffn",
    )(y_lru, y_nsa, h, row(gn_lru), row(gn_nsa), w_mix.astype(BF16), row(g2), row(b2),
      w_in[:, :D_FF].astype(BF16), w_in[:, D_FF:].astype(BF16), w_out.astype(BF16),
      row(g3), row(b3))


def kernel(x, ffn1_w_in, ffn1_w_out, ln1_g, ln1_b, mix_w_in, conv_w, conv_b, lru_w_a, lru_b_a,
           lru_w_x, lru_b_x, lru_lam, cmp_pe_k, cmp_w1_k, cmp_w2_k, cmp_pe_v, cmp_w1_v,
           cmp_w2_v, gn_lru, gn_nsa, mix_w_out, ln2_g, ln2_b, ffn2_w_in, ffn2_w_out, ln3_g,
           ln3_b):
    B, S, D = x.shape
    h = x.reshape(B * S, D)
    for l in range(DEPTH):
        h = _ffn_ln(h, ffn1_w_in[l], ffn1_w_out[l], ln1_g[l], ln1_b[l])
        lru, qt, cmp_in, k, vt, gates_t = _proj(h, mix_w_in[l], S)
        y_lru = _lru(lru, conv_w[l], conv_b[l], lru_w_a[l], lru_b_a[l], lru_w_x[l], lru_b_x[l],
                     lru_lam[l], B, S)
        kc, vct = _compress(cmp_in, cmp_pe_k[l], cmp_w1_k[l], cmp_w2_k[l],
                            cmp_pe_v[l], cmp_w1_v[l], cmp_w2_v[l], B, S)
        y_nsa = _attention(qt, k, vt, kc, vct, gates_t, B, S)
        h = _out_proj_ffn(y_lru, y_nsa, h, gn_lru[l], gn_nsa[l], mix_w_out[l], ln2_g[l], ln2_b[l],
                          ffn2_w_in[l], ffn2_w_out[l], ln3_g[l], ln3_b[l])
    return h.reshape(B, S, D)
```

```python
import functools

import numpy as np
import jax
import jax.numpy as jnp
from jax import lax
from jax.experimental import pallas as pl
from jax.experimental.pallas import tpu as pltpu

F32 = jnp.float32
BF16 = jnp.bfloat16

D_MODEL = 1024
LRU_WIDTH = 512
LRU_HEADS = 8
LRU_BLOCK = 64
CONV_WIDTH = 4
LRU_C = 8.0
NSA_Q_HEADS = 8
NSA_KV_HEADS = 2
NSA_GROUP = 4
HEAD_DIM = 64
NSA_WIDTH = 512
KV_WIDTH = 128
CMP_BLOCK = 32
CMP_STRIDE = 16
CMP_HIDDEN = 256
SLC_BLOCK = 64
SLC_SHIFT = 6
SLC_TOP_N = 16
N_LOCAL_BLOCKS = 2
WINDOW = 512
ROPE_THETA = 10000.0
D_FF = 2816
DEPTH = 1
ALPHA = (2.0 * DEPTH) ** 0.25
LN_EPS = 1e-5
RMS_EPS = 1e-6
NEG = -1e30
LOG2E = 1.4426950408889634
F32_TINY = 1.1754943508222875e-38

LANES = 128
SUBLANES = 8
BF16_ROWS = 16
VMEM_LIMIT = 48 * 1024 * 1024
N_GATE = 3 * NSA_GROUP
GATE_ROWS = 16
V_ROWS = HEAD_DIM + BF16_ROWS


def _dot(a, b):
    return jnp.dot(a, b, preferred_element_type=F32)


def _dot_nt(a, b):
    return lax.dot_general(a, b, (((1,), (1,)), ((), ())), preferred_element_type=F32)


def _layer_norm(y, g, b):
    mu = jnp.mean(y, axis=-1, keepdims=True)
    d = y - mu
    var = jnp.mean(d * d, axis=-1, keepdims=True)
    return d * lax.rsqrt(var + LN_EPS) * g + b


def _rms_norm(y, g):
    return y * lax.rsqrt(jnp.mean(y * y, axis=-1, keepdims=True) + RMS_EPS) * g


def _silu(x):
    return x * jax.nn.sigmoid(x)


def _resident(shape):
    return pl.BlockSpec(shape, lambda *_: (0,) * len(shape), pipeline_mode=pl.Buffered(1))


def _ffn_half_step(x, wg_ref, wu_ref, wo_ref, g_ref, b_ref):
    xb = x.astype(BF16)
    gate = _dot(xb, wg_ref[...])
    up = _dot(xb, wu_ref[...])
    act = (_silu(gate) * up).astype(BF16)
    y = ALPHA * x + 0.5 * _dot(act, wo_ref[...])
    return _layer_norm(y, g_ref[...], b_ref[...])


def _ffn_ln_kernel(x_ref, wg_ref, wu_ref, wo_ref, g_ref, b_ref, o_ref):
    o_ref[...] = _ffn_half_step(x_ref[...], wg_ref, wu_ref, wo_ref, g_ref, b_ref)


def _ffn_ln(x, w_in, w_out, g, b, *, tm=1024):
    T, D = x.shape
    wg = w_in[:, :D_FF].astype(BF16)
    wu = w_in[:, D_FF:].astype(BF16)
    wo = w_out.astype(BF16)
    return pl.pallas_call(
        _ffn_ln_kernel,
        grid=(T // tm,),
        in_specs=[
            pl.BlockSpec((tm, D), lambda i: (i, 0)),
            _resident((D, D_FF)), _resident((D, D_FF)), _resident((D_FF, D)),
            _resident((1, D)), _resident((1, D)),
        ],
        out_specs=pl.BlockSpec((tm, D), lambda i: (i, 0)),
        out_shape=jax.ShapeDtypeStruct((T, D), F32),
        compiler_params=pltpu.CompilerParams(
            dimension_semantics=("parallel",), vmem_limit_bytes=VMEM_LIMIT),
        name="ffn_ln",
    )(x, wg, wu, wo, g.reshape(1, D), b.reshape(1, D))


Q_COL0 = 2 * LRU_WIDTH
KCMP_COL0 = Q_COL0 + NSA_WIDTH
VCMP_COL0 = KCMP_COL0 + KV_WIDTH
KSLC_COL0 = VCMP_COL0 + KV_WIDTH
VSLC_COL0 = KSLC_COL0 + KV_WIDTH
KWIN_COL0 = VSLC_COL0 + KV_WIDTH
VWIN_COL0 = KWIN_COL0 + KV_WIDTH
GATE_COL0 = VWIN_COL0 + KV_WIDTH
VT_ROW0 = NSA_WIDTH
GT_ROW0 = VT_ROW0 + 2 * KV_WIDTH
WT_ROWS = GT_ROW0 + NSA_KV_HEADS * GATE_ROWS


def _rope_chunk(xc, cos, sin_signed):
    lane = lax.broadcasted_iota(jnp.int32, xc.shape, 1)
    first = (lane & (HEAD_DIM - 1)) < (HEAD_DIM // 2)
    partner = jnp.where(first, pltpu.roll(xc, LANES - HEAD_DIM // 2, axis=1),
                        pltpu.roll(xc, HEAD_DIM // 2, axis=1))
    return xc * cos + partner * sin_signed


def _proj_kernel(h_ref, wn_ref, wt_ref, cos_ref, sin_ref, cost_ref, sint_ref, cw_ref, cb_ref,
                 lru_ref, qt_ref, cmp_ref, k_ref, vt_ref, gt_ref, seg_scr, conv_scr, *, seq_tiles):
    tm = h_ref.shape[0]
    half = HEAD_DIM // 2

    @pl.when(pl.program_id(0) % seq_tiles == 0)
    def _():
        conv_scr[0:SUBLANES, :] = jnp.zeros((SUBLANES, LRU_WIDTH), F32)

    hb = h_ref[...].astype(BF16)
    p = _dot(hb, wn_ref[...])
    pt = _dot_nt(wt_ref[...], hb)

    x = p[:, :LRU_WIDTH]
    conv_scr[SUBLANES:, :] = x
    xc = cb_ref[...]
    for d in range(CONV_WIDTH):
        xs = conv_scr[SUBLANES - d:SUBLANES - d + tm, :]
        xc = xc + xs * cw_ref[CONV_WIDTH - 1 - d:CONV_WIDTH - d, :]
    conv_scr[0:SUBLANES, :] = x[tm - SUBLANES:, :]
    lru_ref[:, :LRU_WIDTH] = xc
    lru_ref[:, LRU_WIDTH:] = jax.nn.gelu(p[:, LRU_WIDTH:Q_COL0])

    cos_t = cost_ref[...]
    sin_t = sint_ref[...]
    scale = HEAD_DIM ** -0.5 * LOG2E
    for hd in range(NSA_Q_HEADS):
        x1 = pt[hd * HEAD_DIM:hd * HEAD_DIM + half]
        x2 = pt[hd * HEAD_DIM + half:(hd + 1) * HEAD_DIM]
        qt_ref[hd] = (jnp.concatenate([x1 * cos_t - x2 * sin_t, x2 * cos_t + x1 * sin_t], axis=0)
                      * scale).astype(BF16)
    ones = jnp.ones((V_ROWS - HEAD_DIM, tm), F32)
    for c in range(2 * NSA_KV_HEADS):
        v = pt[VT_ROW0 + c * HEAD_DIM:VT_ROW0 + (c + 1) * HEAD_DIM]
        vt_ref[c] = jnp.concatenate([v, ones], axis=0).astype(BF16)
    for g in range(NSA_KV_HEADS):
        gt_ref[g] = jax.nn.sigmoid(pt[GT_ROW0 + g * GATE_ROWS:GT_ROW0 + (g + 1) * GATE_ROWS])

    cos = cos_ref[...]
    sin = sin_ref[...]
    low_seg = lax.broadcasted_iota(jnp.int32, (tm // CMP_STRIDE, LANES), 1) < HEAD_DIM
    for c, xc in enumerate((_rope_chunk(p[:, Q_COL0:Q_COL0 + LANES], cos, sin),
                            p[:, Q_COL0 + LANES:Q_COL0 + 2 * LANES])):
        seg_scr[...] = xc
        tok = [seg_scr[pl.ds(j, tm // CMP_STRIDE, stride=CMP_STRIDE), :]
               for j in range(CMP_STRIDE)]
        g0, g1 = [], []
        for j in range(0, CMP_STRIDE, 2):
            g0.append(jnp.where(low_seg, tok[j], pltpu.roll(tok[j + 1], HEAD_DIM, axis=1)))
            g1.append(jnp.where(low_seg, pltpu.roll(tok[j], HEAD_DIM, axis=1), tok[j + 1]))
        cmp_ref[NSA_KV_HEADS * c] = jnp.concatenate(g0, axis=1).astype(BF16)
        cmp_ref[NSA_KV_HEADS * c + 1] = jnp.concatenate(g1, axis=1).astype(BF16)
    lane = lax.broadcasted_iota(jnp.int32, (tm, LANES), 1)
    low = lane < HEAD_DIM
    pos = (pl.program_id(0) % seq_tiles) * tm + lax.broadcasted_iota(jnp.int32, (tm, LANES), 0)
    ext_blk = jnp.where(lane - HEAD_DIM == (pos >> SLC_SHIFT), NEG, 0.0)
    ext_zero = jnp.zeros((tm, LANES), F32)
    for c, ext in enumerate((ext_blk, ext_zero)):
        xc = _rope_chunk(p[:, Q_COL0 + (2 + c) * LANES:Q_COL0 + (3 + c) * LANES], cos, sin)
        k_ref[2 * c] = jnp.where(low, xc, ext).astype(BF16)
        k_ref[2 * c + 1] = jnp.where(low, pltpu.roll(xc, HEAD_DIM, axis=1), ext).astype(BF16)


def _proj(h, w_in, conv_w, conv_b, S, *, tm=1024):
    T, D = h.shape
    G = NSA_KV_HEADS
    col = lambda c0, n: w_in[:, c0:c0 + n]
    w_nat = jnp.concatenate(
        [col(0, Q_COL0), col(KCMP_COL0, KV_WIDTH), col(VCMP_COL0, KV_WIDTH),
         col(KSLC_COL0, KV_WIDTH), col(KWIN_COL0, KV_WIDTH)], axis=1).astype(BF16)
    gate_rows = [jnp.pad(col(GATE_COL0 + g * N_GATE, N_GATE).T, ((0, GATE_ROWS - N_GATE), (0, 0)))
                 for g in range(G)]
    w_t = jnp.concatenate(
        [col(Q_COL0, NSA_WIDTH).T, col(VSLC_COL0, KV_WIDTH).T, col(VWIN_COL0, KV_WIDTH).T]
        + gate_rows, axis=0).astype(BF16)
    n_nat = w_nat.shape[1]

    half = HEAD_DIM // 2
    inv = ROPE_THETA ** (-jnp.arange(half, dtype=F32) / half)
    ang = jnp.arange(S, dtype=F32)[:, None] * inv[None, :]
    cos = jnp.cos(ang)
    sin = jnp.sin(ang)
    cos_n = jnp.concatenate([cos, cos, cos, cos], axis=1)
    sin_n = jnp.concatenate([-sin, sin, -sin, sin], axis=1)

    nS = S // tm
    seg_w = CMP_STRIDE * HEAD_DIM
    assert G * HEAD_DIM == LANES
    return pl.pallas_call(
        functools.partial(_proj_kernel, seq_tiles=nS),
        grid=(T // tm,),
        in_specs=[
            pl.BlockSpec((tm, D), lambda i: (i, 0)),
            _resident((D, n_nat)), _resident((WT_ROWS, D)),
            pl.BlockSpec((tm, LANES), lambda i: (i % nS, 0)),
            pl.BlockSpec((tm, LANES), lambda i: (i % nS, 0)),
            pl.BlockSpec((half, tm), lambda i: (0, i % nS)),
            pl.BlockSpec((half, tm), lambda i: (0, i % nS)),
            _resident((CONV_WIDTH, LRU_WIDTH)), _resident((1, LRU_WIDTH)),
        ],
        out_specs=[
            pl.BlockSpec((tm, Q_COL0), lambda i: (i, 0)),
            pl.BlockSpec((NSA_Q_HEADS, HEAD_DIM, tm), lambda i: (0, 0, i)),
            pl.BlockSpec((2 * G, tm // CMP_STRIDE, seg_w), lambda i: (0, i, 0)),
            pl.BlockSpec((2 * G, tm, LANES), lambda i: (0, i, 0)),
            pl.BlockSpec((2 * G, V_ROWS, tm), lambda i: (0, 0, i)),
            pl.BlockSpec((G, GATE_ROWS, tm), lambda i: (0, 0, i)),
        ],
        out_shape=[
            jax.ShapeDtypeStruct((T, Q_COL0), F32),
            jax.ShapeDtypeStruct((NSA_Q_HEADS, HEAD_DIM, T), BF16),
            jax.ShapeDtypeStruct((2 * G, T // CMP_STRIDE, seg_w), BF16),
            jax.ShapeDtypeStruct((2 * G, T, LANES), BF16),
            jax.ShapeDtypeStruct((2 * G, V_ROWS, T), BF16),
            jax.ShapeDtypeStruct((G, GATE_ROWS, T), F32),
        ],
        scratch_shapes=[pltpu.VMEM((tm, LANES), F32),
                        pltpu.VMEM((SUBLANES + tm, LRU_WIDTH), F32)],
        compiler_params=pltpu.CompilerParams(
            dimension_semantics=("arbitrary",), vmem_limit_bytes=VMEM_LIMIT),
        name="proj",
    )(h, w_nat, w_t, cos_n, sin_n, cos.T, sin.T, conv_w, conv_b.reshape(1, LRU_WIDTH))


SCAN_UNROLL = 8
LRU_ROWS = 512


def _lru_kernel(x_ref, gate_ref, wa_ref, ba_ref, wx_ref, bx_ref, lam_ref, y_ref, a_scr, b_scr):
    S, C = y_ref.shape
    neg_lam = -lam_ref[...]
    softplus = jnp.maximum(neg_lam, 0.0) + jnp.log1p(jnp.exp(-jnp.abs(neg_lam)))

    for r0 in range(0, S, LRU_ROWS):
        xc = x_ref[r0:r0 + LRU_ROWS, :]
        xb = xc.astype(BF16)
        gate_pre = lambda w_ref: jnp.concatenate(
            [_dot(xb[:, c * LANES:(c + 1) * LANES], w_ref[c]) for c in range(C // LANES)], axis=1)
        r = 0.5 * jnp.tanh(0.5 * (gate_pre(wa_ref) + ba_ref[...])) + 0.5
        ig = 0.5 * jnp.tanh(0.5 * (gate_pre(wx_ref) + bx_ref[...])) + 0.5
        log_a = (-LRU_C) * r * softplus
        a = jnp.exp(log_a)
        z = jnp.tanh(-log_a) * (a * a + 1.0)
        mult = z * lax.rsqrt(jnp.maximum(z, F32_TINY))
        if r0 == 0:
            row = lax.broadcasted_iota(jnp.int32, mult.shape, 0)
            mult = jnp.where(row == 0, 1.0, mult)
        a_scr[r0:r0 + LRU_ROWS, :] = a
        b_scr[r0:r0 + LRU_ROWS, :] = mult * (ig * xc)

    sub = lax.broadcasted_iota(jnp.int32, (SUBLANES, C), 0)

    def scan_tile(g, h_prev):
        rows = pl.ds(pl.multiple_of(g * SUBLANES, SUBLANES), SUBLANES)
        at = a_scr[rows, :]
        bt = b_scr[rows, :]
        d = 1
        while d < SUBLANES:
            keep = sub >= d
            a_sh = jnp.where(keep, pltpu.roll(at, d, axis=0), 1.0)
            b_sh = jnp.where(keep, pltpu.roll(bt, d, axis=0), 0.0)
            bt = at * b_sh + bt
            at = at * a_sh
            d *= 2
        h = bt + at * h_prev
        b_scr[rows, :] = h
        return jnp.broadcast_to(h[SUBLANES - 1:SUBLANES, :], (SUBLANES, C))

    def scan_body(i, h_prev):
        for u in range(SCAN_UNROLL):
            h_prev = scan_tile(i * SCAN_UNROLL + u, h_prev)
        return h_prev

    lax.fori_loop(0, S // (SUBLANES * SCAN_UNROLL), scan_body, jnp.zeros((SUBLANES, C), F32))

    for r0 in range(0, S, LRU_ROWS):
        rows = slice(r0, r0 + LRU_ROWS)
        y_ref[rows, :] = b_scr[rows, :] * gate_ref[rows, :]


def _lru(lru, w_a, b_a, w_x, b_x, lam, B, S):
    T = lru.shape[0]
    C = LRU_WIDTH
    n_ch = C // LANES
    assert S % (SUBLANES * SCAN_UNROLL) == 0 and S % LRU_ROWS == 0

    def blockdiag(w):
        w = w.reshape(n_ch, 2, LRU_BLOCK, LRU_BLOCK)
        z = jnp.zeros((n_ch, LRU_BLOCK, LRU_BLOCK), w.dtype)
        top = jnp.concatenate([w[:, 0], z], axis=2)
        bot = jnp.concatenate([z, w[:, 1]], axis=2)
        return jnp.concatenate([top, bot], axis=1).astype(BF16)

    vec = lambda v: v.reshape(1, C)
    return pl.pallas_call(
        _lru_kernel,
        grid=(B,),
        in_specs=[
            pl.BlockSpec((S, C), lambda b: (b, 0)),
            pl.BlockSpec((S, C), lambda b: (b, 1)),
            _resident((n_ch, LANES, LANES)), _resident((1, C)),
            _resident((n_ch, LANES, LANES)), _resident((1, C)),
            _resident((1, C)),
        ],
        out_specs=pl.BlockSpec((S, C), lambda b: (b, 0)),
        out_shape=jax.ShapeDtypeStruct((T, C), F32),
        scratch_shapes=[pltpu.VMEM((S, C), F32),
                        pltpu.VMEM((S, C), F32)],
        compiler_params=pltpu.CompilerParams(
            dimension_semantics=("parallel",), vmem_limit_bytes=VMEM_LIMIT),
        name="lru",
    )(lru, lru, blockdiag(w_a), vec(b_a.reshape(-1)),
      blockdiag(w_x), vec(b_x.reshape(-1)), vec(lam))


def _cmp_kernel(seg_ref, pek_ref, w1k_ref, w2k_ref, pev_ref, w1v_ref, w2vt_ref, kc_ref, vct_ref):
    n_seg = seg_ref.shape[2]
    half = CMP_STRIDE * HEAD_DIM

    def hidden(idx, pe_ref, w1_ref):
        seg = seg_ref[idx, 0]
        first = _dot(seg, w1_ref[0:half, :])
        second = _dot(seg, w1_ref[half:2 * half, :])
        bias = _dot(pe_ref[...], w1_ref[...])[0:1, :]
        return _silu(first + pltpu.roll(second, n_seg - 1, axis=0) + bias).astype(BF16)

    for g in range(NSA_KV_HEADS):
        tok = _dot(hidden(g, pek_ref, w1k_ref), w2k_ref[...])
        kc_ref[0, g] = jnp.concatenate([tok, jnp.zeros_like(tok)], axis=1).astype(BF16)
        vct_ref[0, g] = _dot_nt(w2vt_ref[...], hidden(NSA_KV_HEADS + g, pev_ref, w1v_ref)
                                ).astype(BF16)


def _compress(cmp_in, pe_k, w1_k, w2_k, pe_v, w1_v, w2_v, B, S):
    G = NSA_KV_HEADS
    n_seg = S // CMP_STRIDE
    seg = cmp_in.reshape(2 * G, B, n_seg, CMP_STRIDE * HEAD_DIM)
    flat = CMP_BLOCK * HEAD_DIM
    pe8 = lambda pe: jnp.broadcast_to(pe.reshape(1, -1), (8, flat)).astype(BF16)
    return pl.pallas_call(
        _cmp_kernel,
        grid=(B,),
        in_specs=[
            pl.BlockSpec((2 * G, 1, n_seg, CMP_STRIDE * HEAD_DIM), lambda b: (0, b, 0, 0)),
            _resident((8, flat)), _resident((flat, CMP_HIDDEN)), _resident((CMP_HIDDEN, HEAD_DIM)),
            _resident((8, flat)), _resident((flat, CMP_HIDDEN)), _resident((HEAD_DIM, CMP_HIDDEN)),
        ],
        out_specs=[
            pl.BlockSpec((1, G, n_seg, LANES), lambda b: (b, 0, 0, 0)),
            pl.BlockSpec((1, G, HEAD_DIM, n_seg), lambda b: (b, 0, 0, 0)),
        ],
        out_shape=[
            jax.ShapeDtypeStruct((B, G, n_seg, LANES), BF16),
            jax.ShapeDtypeStruct((B, G, HEAD_DIM, n_seg), BF16),
        ],
        compiler_params=pltpu.CompilerParams(
            dimension_semantics=("parallel",), vmem_limit_bytes=VMEM_LIMIT),
        name="compress",
    )(seg, pe8(pe_k), w1_k.astype(BF16), w2_k.astype(BF16),
      pe8(pe_v), w1_v.astype(BF16), w2_v.T.astype(BF16))


def _attn_kernel(qt_ref, kc_ref, vct_ref, ks_ref, vst_ref, kw_ref, vwt_ref, gt_ref, ovl_ref,
                 bias_ref, *rest, tq, tiles):
    o_ref = rest[-1]

    def step(n):
        for _ in _interleave(*[
                _attn_step(n, g, qt_ref, kc_ref.at[g], vct_ref.at[g], ks_ref.at[g], vst_ref.at[g],
                           kw_ref.at[g], vwt_ref.at[g], gt_ref.at[g], ovl_ref, bias_ref, o_ref, tq)
                for g in range(NSA_KV_HEADS)]):
            pass

    for n in range(*tiles):
        pl.when(pl.program_id(1) == n - tiles[0])(functools.partial(step, n))


CHUNK_TILES = 1
SCORES_AHEAD = 1
ATTN_TILE_SPLITS = ((0, 4), (4, 6), (6, 8))


def _interleave(*stages):
    live = list(stages)
    while live:
        for st in list(live):
            try:
                next(st)
            except StopIteration:
                live.remove(st)
        yield


def _tile_bias(bias_ref, t0, k0, behind):
    if k0 == t0:
        return bias_ref[0]
    if behind is not None and k0 == t0 - behind:
        return bias_ref[1]
    return None


def _key_chunks(k_ref, vt_ref, k0, n_keys, tile, q, bias_fn):
    starts = list(range(k0, k0 + n_keys, tile))
    return [dict(k_ref=k_ref, vt_ref=vt_ref, tile=tile, q=q, bias_fn=bias_fn,
                 starts=starts[i:i + CHUNK_TILES]) for i in range(0, len(starts), CHUNK_TILES)]


def _scores_stage(ch):
    tiles, m = [], None
    for k0 in ch["starts"]:
        s = _dot(ch["k_ref"][k0:k0 + ch["tile"], :], ch["q"])
        bias = ch["bias_fn"](k0)
        if bias is not None:
            s = s + bias
        cm = jnp.max(s, axis=0, keepdims=True)
        m = cm if m is None else jnp.maximum(m, cm)
        tiles.append(s)
        yield
    ch["s"], ch["m"] = tiles, m


def _pv_stage(ch, heads, tq):
    acc = [None] * heads
    for k0, s in zip(ch["starts"], ch["s"]):
        p = jnp.exp2(s - ch["m"]).astype(BF16)
        vt = ch["vt_ref"][:, k0:k0 + ch["tile"]]
        for r in range(heads):
            d = _dot(vt, p[:, r * tq:(r + 1) * tq])
            acc[r] = d if acc[r] is None else acc[r] + d
        yield
    ch["acc"] = acc


def _merge_chunks(chunks, heads, tq):
    if len(chunks) == 1:
        return chunks[0]["acc"]
    m = functools.reduce(jnp.maximum, [ch["m"] for ch in chunks])
    acc = [None] * heads
    for ch in chunks:
        w = jnp.exp2(ch["m"] - m)
        for r in range(heads):
            term = ch["acc"][r] * w[:, r * tq:(r + 1) * tq]
            acc[r] = term if acc[r] is None else acc[r] + term
    return acc


def _rank_stage(out, score, jblk, n_live):
    rank = jnp.zeros(score.shape, jnp.int32)
    for kb in range(n_live):
        sk = score[kb:kb + 1, :]
        ahead = (sk > score) | ((sk == score) & (jblk > kb))
        rank = rank + ahead.astype(jnp.int32)
        if kb % 4 == 3:
            yield
    out["rank"] = rank


def _attn_step(n, g, qt_ref, kc_ref, vct_ref, ks_ref, vst_ref, kw_ref, vwt_ref, gt_ref, ovl_ref,
               bias_ref, o_ref, tq):
    R = NSA_GROUP
    t0 = n * tq
    n_cmp = kc_ref.shape[0]
    n_blk = ovl_ref.shape[0]
    q_all = jnp.concatenate([qt_ref[g * R + r] for r in range(R)], axis=1)
    q_pad = jnp.concatenate([q_all, jnp.zeros((LANES - HEAD_DIM, R * tq), BF16)], axis=0)
    head = lambda a, r: a[:, r * tq:(r + 1) * tq]

    kw0 = max(t0 - WINDOW, 0)
    win = _key_chunks(kw_ref, vwt_ref, kw0, t0 + tq - kw0, tq, q_pad,
                      lambda k0: _tile_bias(bias_ref, t0, k0, WINDOW))
    win_scores = _scores_stage(win[0])
    next(win_scores)
    yield

    n_vis = (t0 + tq - CMP_BLOCK) // CMP_STRIDE + 1
    n_vis = min(n_cmp, -(-n_vis // BF16_ROWS) * BF16_ROWS)
    tpos_c = t0 + (lax.broadcasted_iota(jnp.int32, (n_vis, R * tq), 1) & (tq - 1))
    cend = lax.broadcasted_iota(jnp.int32, (n_vis, R * tq), 0) * CMP_STRIDE + (CMP_BLOCK - 1)
    s = _dot(kc_ref[0:n_vis, :], q_pad) + jnp.where(cend <= tpos_c, 0.0, NEG)
    e = jnp.exp2(s - jnp.max(s, axis=0, keepdims=True))
    p_cmp = e * (1.0 / jnp.sum(e, axis=0, keepdims=True))
    if t0 < CMP_BLOCK - 1:
        p_cmp = jnp.where(tpos_c >= CMP_BLOCK - 1, p_cmp, 0.0)
    if n_vis < n_cmp:
        p_cmp = jnp.concatenate([p_cmp, jnp.zeros((n_cmp - n_vis, R * tq), F32)], axis=0)
    p_cmp_b = p_cmp.astype(BF16)
    vct = vct_ref[...]
    o_cmp = [_dot(vct, head(p_cmp_b, r)) for r in range(R)]
    psum = functools.reduce(jnp.add, [head(p_cmp, r) for r in range(R)])
    yield

    p_hi = psum.astype(BF16)
    p_lo = (psum - p_hi.astype(F32)).astype(BF16)
    ovl = ovl_ref[...]
    imp = _dot(ovl, p_hi) + _dot(ovl, p_lo)
    n_live = min(n_blk, (t0 + tq - 1) // SLC_BLOCK + 1)
    rows = -(-n_live // SUBLANES) * SUBLANES
    jblk = lax.broadcasted_iota(jnp.int32, (rows, tq), 0)
    tpos_b = t0 + lax.broadcasted_iota(jnp.int32, (rows, tq), 1)
    blk_valid = jblk * SLC_BLOCK <= tpos_b
    back = (tpos_b >> SLC_SHIFT) - jblk
    forced = (jblk == 0) | ((back >= 0) & (back < N_LOCAL_BLOCKS))
    score = jnp.where(blk_valid, jnp.where(forced, jnp.inf, imp[0:rows]), -jnp.inf)
    sel = {}
    yield from _interleave(_rank_stage(sel, score, jblk, n_live), win_scores)
    unsel = 1.0 - (blk_valid & (sel["rank"] < SLC_TOP_N)).astype(F32)
    dead = [jnp.ones((n_blk - rows, tq), F32)] if rows < n_blk else []
    sel_ext = jnp.concatenate(
        [unsel] + dead + [jnp.zeros((LANES - HEAD_DIM - n_blk, tq), F32)],
        axis=0).astype(BF16)
    q_sel = jnp.concatenate([q_all, jnp.concatenate([sel_ext] * R, axis=1)], axis=0)

    slc = _key_chunks(ks_ref, vst_ref, 0, t0 + tq, tq, q_sel,
                      lambda k0: _tile_bias(bias_ref, t0, k0, None))
    chunks = win + slc
    for ch in chunks[1:SCORES_AHEAD]:
        yield from _scores_stage(ch)
    for idx, ch in enumerate(chunks):
        stages = [_pv_stage(ch, R, tq)]
        if idx + SCORES_AHEAD < len(chunks):
            stages.append(_scores_stage(chunks[idx + SCORES_AHEAD]))
        yield from _interleave(*stages)
    acc_w = _merge_chunks(win, R, tq)
    acc_s = _merge_chunks(slc, R, tq)
    yield

    gt = gt_ref[...]
    outs = []
    for r in range(R):
        g_cmp = gt[3 * r:3 * r + 1]
        g_slc = gt[3 * r + 1:3 * r + 2] * (1.0 / acc_s[r][HEAD_DIM:HEAD_DIM + 1])
        g_win = gt[3 * r + 2:3 * r + 3] * (1.0 / acc_w[r][HEAD_DIM:HEAD_DIM + 1])
        outs.append(g_cmp * o_cmp[r] + g_slc * acc_s[r][:HEAD_DIM] + g_win * acc_w[r][:HEAD_DIM])
    width = R * HEAD_DIM
    o_ref[:, g * width:(g + 1) * width] = jnp.concatenate(outs, axis=0).T


def _overlap_t(n_cmp_pad, n_blk):
    cs = np.arange(n_cmp_pad) * CMP_STRIDE
    ce = cs + CMP_BLOCK - 1
    ss = np.arange(n_blk) * SLC_BLOCK
    se = ss + SLC_BLOCK - 1
    return ((cs[None, :] <= se[:, None]) & (ce[None, :] >= ss[:, None])).astype(np.float32)


def _attention(qt, k, vt, kc, vct, gates_t, B, S, *, tq=256):
    T = qt.shape[2]
    nq = S // tq
    n_cmp = S // CMP_STRIDE
    n_blk = S // SLC_BLOCK
    G = NSA_KV_HEADS
    ovl = jnp.asarray(_overlap_t(n_cmp, n_blk), BF16)
    assert WINDOW % tq == 0 and WINDOW + tq <= S
    assert HEAD_DIM + n_blk <= LANES
    assert tq & (tq - 1) == 0
    key_off = np.arange(tq)[:, None]
    q_off = np.tile(np.arange(tq), NSA_GROUP)[None, :]
    bias = jnp.asarray(np.stack([np.where(key_off <= q_off, 0.0, NEG),
                                 np.where(key_off > q_off, 0.0, NEG)]), F32)
    k_spec = lambda c: pl.BlockSpec((G, S, LANES), lambda b, i: (c, b, 0))
    vt_spec = lambda c: pl.BlockSpec((G, V_ROWS, S), lambda b, i: (c, 0, b))

    def call(tiles, prev):
        lo, hi = tiles
        tile = lambda b, i: b * nq + lo + i
        in_specs = [
            pl.BlockSpec((NSA_Q_HEADS, HEAD_DIM, tq), lambda b, i: (0, 0, tile(b, i))),
            pl.BlockSpec((None, G, n_cmp, LANES), lambda b, i: (b, 0, 0, 0)),
            pl.BlockSpec((None, G, HEAD_DIM, n_cmp), lambda b, i: (b, 0, 0, 0)),
            k_spec(0), vt_spec(0), k_spec(1), vt_spec(1),
            pl.BlockSpec((G, GATE_ROWS, tq), lambda b, i: (0, 0, tile(b, i))),
            _resident((n_blk, n_cmp)),
            _resident((2, tq, NSA_GROUP * tq)),
        ]
        args = [qt, kc, vct, k, vt, k, vt, gates_t, ovl, bias]
        aliases = {}
        if prev is not None:
            in_specs.append(pl.BlockSpec(memory_space=pl.ANY))
            aliases = {len(args): 0}
            args.append(prev)
        return pl.pallas_call(
            functools.partial(_attn_kernel, tq=tq, tiles=tiles),
            grid=(B, hi - lo),
            in_specs=in_specs,
            out_specs=pl.BlockSpec((tq, NSA_WIDTH), lambda b, i: (tile(b, i), 0)),
            out_shape=jax.ShapeDtypeStruct((T, NSA_WIDTH), F32),
            input_output_aliases=aliases,
            compiler_params=pltpu.CompilerParams(
                dimension_semantics=("parallel", "arbitrary"),
                vmem_limit_bytes=VMEM_LIMIT),
            name=f"nsa_attn_{lo}_{hi}",
        )(*args)

    assert [t for lo, hi in ATTN_TILE_SPLITS for t in range(lo, hi)] == list(range(nq))
    out = None
    for tiles in ATTN_TILE_SPLITS:
        out = call(tiles, out)
    return out


def _out_ffn_kernel(ylru_ref, ynsa_ref, h_ref, gl_ref, gn_ref, w_ref, g2_ref, b2_ref,
                    wg_ref, wu_ref, wo_ref, g3_ref, b3_ref, o_ref):
    yl = _rms_norm(ylru_ref[...], gl_ref[...]).astype(BF16)
    yn = _rms_norm(ynsa_ref[...], gn_ref[...]).astype(BF16)
    mix = _dot(yl, w_ref[0:LRU_WIDTH, :]) + _dot(yn, w_ref[LRU_WIDTH:, :])
    h2 = _layer_norm(ALPHA * h_ref[...] + mix, g2_ref[...], b2_ref[...])
    o_ref[...] = _ffn_half_step(h2, wg_ref, wu_ref, wo_ref, g3_ref, b3_ref)


def _out_proj_ffn(y_lru, y_nsa, h, gn_lru, gn_nsa, w_mix, g2, b2, w_in, w_out, g3, b3, *,
                  tm=512):
    T, D = h.shape
    row = lambda v: v.reshape(1, -1)
    return pl.pallas_call(
        _out_ffn_kernel,
        grid=(T // tm,),
        in_specs=[
            pl.BlockSpec((tm, LRU_WIDTH), lambda i: (i, 0)),
            pl.BlockSpec((tm, NSA_WIDTH), lambda i: (i, 0)),
            pl.BlockSpec((tm, D), lambda i: (i, 0)),
            _resident((1, LRU_WIDTH)), _resident((1, NSA_WIDTH)),
            _resident((LRU_WIDTH + NSA_WIDTH, D)),
            _resident((1, D)), _resident((1, D)),
            _resident((D, D_FF)), _resident((D, D_FF)), _resident((D_FF, D)),
            _resident((1, D)), _resident((1, D)),
        ],
        out_specs=pl.BlockSpec((tm, D), lambda i: (i, 0)),
        out_shape=jax.ShapeDtypeStruct((T, D), F32),
        compiler_params=pltpu.CompilerParams(
            dimension_semantics=("parallel",), vmem_limit_bytes=VMEM_LIMIT),
        name="out_proj_ffn",
    )(y_lru, y_nsa, h, row(gn_lru), row(gn_nsa), w_mix.astype(BF16), row(g2), row(b2),
      w_in[:, :D_FF].astype(BF16), w_in[:, D_FF:].astype(BF16), w_out.astype(BF16),
      row(g3), row(b3))


def kernel(x, ffn1_w_in, ffn1_w_out, ln1_g, ln1_b, mix_w_in, conv_w, conv_b, lru_w_a, lru_b_a,
           lru_w_x, lru_b_x, lru_lam, cmp_pe_k, cmp_w1_k, cmp_w2_k, cmp_pe_v, cmp_w1_v,
           cmp_w2_v, gn_lru, gn_nsa, mix_w_out, ln2_g, ln2_b, ffn2_w_in, ffn2_w_out, ln3_g,
           ln3_b):
    B, S, D = x.shape
    h = x.reshape(B * S, D)
    for l in range(DEPTH):
        h = _ffn_ln(h, ffn1_w_in[l], ffn1_w_out[l], ln1_g[l], ln1_b[l])
        lru, qt, cmp_in, k, vt, gates_t = _proj(h, mix_w_in[l], conv_w[l], conv_b[l], S)
        y_lru = _lru(lru, lru_w_a[l], lru_b_a[l], lru_w_x[l], lru_b_x[l], lru_lam[l], B, S)
        kc, vct = _compress(cmp_in, cmp_pe_k[l], cmp_w1_k[l], cmp_w2_k[l],
                            cmp_pe_v[l], cmp_w1_v[l], cmp_w2_v[l], B, S)
        y_nsa = _attention(qt, k, vt, kc, vct, gates_t, B, S)
        h = _out_proj_ffn(y_lru, y_nsa, h, gn_lru[l], gn_nsa[l], mix_w_out[l], ln2_g[l], ln2_b[l],
                          ffn2_w_in[l], ffn2_w_out[l], ln3_g[l], ln3_b[l])
    return h.reshape(B, S, D)
```

```python
import functools

import numpy as np
import jax
import jax.numpy as jnp
from jax import lax
from jax.experimental import pallas as pl
from jax.experimental.pallas import tpu as pltpu

F32 = jnp.float32
BF16 = jnp.bfloat16

D_MODEL = 1024
LRU_WIDTH = 512
LRU_HEADS = 8
LRU_BLOCK = 64
CONV_WIDTH = 4
LRU_C = 8.0
NSA_Q_HEADS = 8
NSA_KV_HEADS = 2
NSA_GROUP = 4
HEAD_DIM = 64
NSA_WIDTH = 512
KV_WIDTH = 128
CMP_BLOCK = 32
CMP_STRIDE = 16
CMP_HIDDEN = 256
SLC_BLOCK = 64
SLC_SHIFT = 6
SLC_TOP_N = 16
N_LOCAL_BLOCKS = 2
WINDOW = 512
ROPE_THETA = 10000.0
D_FF = 2816
DEPTH = 1
ALPHA = (2.0 * DEPTH) ** 0.25
LN_EPS = 1e-5
RMS_EPS = 1e-6
NEG = -1e30
LOG2E = 1.4426950408889634
F32_TINY = 1.1754943508222875e-38

LANES = 128
SUBLANES = 8
BF16_ROWS = 16
VMEM_LIMIT = 48 * 1024 * 1024
N_GATE = 3 * NSA_GROUP
GATE_ROWS = 16
V_ROWS = HEAD_DIM + BF16_ROWS


def _dot(a, b):
    return jnp.dot(a, b, preferred_element_type=F32)


def _dot_nt(a, b):
    return lax.dot_general(a, b, (((1,), (1,)), ((), ())), preferred_element_type=F32)


def _layer_norm(y, g, b):
    mu = jnp.mean(y, axis=-1, keepdims=True)
    d = y - mu
    var = jnp.mean(d * d, axis=-1, keepdims=True)
    return d * lax.rsqrt(var + LN_EPS) * g + b


def _rms_norm(y, g):
    return y * lax.rsqrt(jnp.mean(y * y, axis=-1, keepdims=True) + RMS_EPS) * g


def _silu(x):
    return x * jax.nn.sigmoid(x)


def _resident(shape):
    return pl.BlockSpec(shape, lambda *_: (0,) * len(shape), pipeline_mode=pl.Buffered(1))


def _ffn_half_step(x, wg_ref, wu_ref, wo_ref, g_ref, b_ref):
    xb = x.astype(BF16)
    gate = _dot(xb, wg_ref[...])
    up = _dot(xb, wu_ref[...])
    act = (_silu(gate) * up).astype(BF16)
    y = ALPHA * x + 0.5 * _dot(act, wo_ref[...])
    return _layer_norm(y, g_ref[...], b_ref[...])


W_IN_ROWS = 64
W_OUT_ROWS = 352
FFN_VMEM_LIMIT = 56 * 1024 * 1024


def _stage_ffn_weights(w_in_hbm, w_out_hbm, wg_scr, wu_scr, wo_scr, stage_in, stage_out, sem):
    for r0 in range(0, w_in_hbm.shape[0], W_IN_ROWS):
        cp = pltpu.make_async_copy(w_in_hbm.at[pl.ds(r0, W_IN_ROWS), :], stage_in, sem)
        cp.start()
        cp.wait()
        wg_scr[r0:r0 + W_IN_ROWS, :] = stage_in[:, :D_FF].astype(BF16)
        wu_scr[r0:r0 + W_IN_ROWS, :] = stage_in[:, D_FF:].astype(BF16)
    for r0 in range(0, w_out_hbm.shape[0], W_OUT_ROWS):
        cp = pltpu.make_async_copy(w_out_hbm.at[pl.ds(r0, W_OUT_ROWS), :], stage_out, sem)
        cp.start()
        cp.wait()
        wo_scr[r0:r0 + W_OUT_ROWS, :] = stage_out[...].astype(BF16)


def _ffn_weight_scratch(D):
    assert D % W_IN_ROWS == 0 and D_FF % W_OUT_ROWS == 0
    return [pltpu.VMEM((D, D_FF), BF16), pltpu.VMEM((D, D_FF), BF16), pltpu.VMEM((D_FF, D), BF16),
            pltpu.VMEM((W_IN_ROWS, 2 * D_FF), F32), pltpu.VMEM((W_OUT_ROWS, D), F32),
            pltpu.SemaphoreType.DMA(())]


def _ffn_ln_kernel(x_ref, w_in_hbm, w_out_hbm, g_ref, b_ref, o_ref,
                   wg_scr, wu_scr, wo_scr, stage_in, stage_out, sem):
    @pl.when(pl.program_id(0) == 0)
    def _():
        _stage_ffn_weights(w_in_hbm, w_out_hbm, wg_scr, wu_scr, wo_scr, stage_in, stage_out, sem)

    o_ref[...] = _ffn_half_step(x_ref[...], wg_scr, wu_scr, wo_scr, g_ref, b_ref)


def _ffn_ln(x, w_in, w_out, g, b, *, tm=1024):
    T, D = x.shape
    return pl.pallas_call(
        _ffn_ln_kernel,
        grid=(T // tm,),
        in_specs=[
            pl.BlockSpec((tm, D), lambda i: (i, 0)),
            pl.BlockSpec(memory_space=pl.ANY), pl.BlockSpec(memory_space=pl.ANY),
            _resident((1, D)), _resident((1, D)),
        ],
        out_specs=pl.BlockSpec((tm, D), lambda i: (i, 0)),
        out_shape=jax.ShapeDtypeStruct((T, D), F32),
        scratch_shapes=_ffn_weight_scratch(D),
        compiler_params=pltpu.CompilerParams(
            dimension_semantics=("arbitrary",), vmem_limit_bytes=FFN_VMEM_LIMIT),
        name="ffn_ln",
    )(x, w_in, w_out, g.reshape(1, D), b.reshape(1, D))


Q_COL0 = 2 * LRU_WIDTH
KCMP_COL0 = Q_COL0 + NSA_WIDTH
VCMP_COL0 = KCMP_COL0 + KV_WIDTH
KSLC_COL0 = VCMP_COL0 + KV_WIDTH
VSLC_COL0 = KSLC_COL0 + KV_WIDTH
KWIN_COL0 = VSLC_COL0 + KV_WIDTH
VWIN_COL0 = KWIN_COL0 + KV_WIDTH
GATE_COL0 = VWIN_COL0 + KV_WIDTH
VT_ROW0 = NSA_WIDTH
GT_ROW0 = VT_ROW0 + 2 * KV_WIDTH
WT_ROWS = GT_ROW0 + NSA_KV_HEADS * GATE_ROWS


def _rope_chunk(xc, cos, sin_signed):
    lane = lax.broadcasted_iota(jnp.int32, xc.shape, 1)
    first = (lane & (HEAD_DIM - 1)) < (HEAD_DIM // 2)
    partner = jnp.where(first, pltpu.roll(xc, LANES - HEAD_DIM // 2, axis=1),
                        pltpu.roll(xc, HEAD_DIM // 2, axis=1))
    return xc * cos + partner * sin_signed


def _proj_kernel(h_ref, wn_ref, wt_ref, cos_ref, sin_ref, cost_ref, sint_ref, cw_ref, cb_ref,
                 lru_ref, qt_ref, cmp_ref, k_ref, vt_ref, gt_ref, seg_scr, conv_scr, *, seq_tiles):
    tm = h_ref.shape[0]
    half = HEAD_DIM // 2

    @pl.when(pl.program_id(0) % seq_tiles == 0)
    def _():
        conv_scr[0:SUBLANES, :] = jnp.zeros((SUBLANES, LRU_WIDTH), F32)

    hb = h_ref[...].astype(BF16)
    p = _dot(hb, wn_ref[...])
    pt = _dot_nt(wt_ref[...], hb)

    x = p[:, :LRU_WIDTH]
    conv_scr[SUBLANES:, :] = x
    xc = cb_ref[...]
    for d in range(CONV_WIDTH):
        xs = conv_scr[SUBLANES - d:SUBLANES - d + tm, :]
        xc = xc + xs * cw_ref[CONV_WIDTH - 1 - d:CONV_WIDTH - d, :]
    conv_scr[0:SUBLANES, :] = x[tm - SUBLANES:, :]
    lru_ref[:, :LRU_WIDTH] = xc
    lru_ref[:, LRU_WIDTH:] = jax.nn.gelu(p[:, LRU_WIDTH:Q_COL0])

    cos_t = cost_ref[...]
    sin_t = sint_ref[...]
    scale = HEAD_DIM ** -0.5 * LOG2E
    for hd in range(NSA_Q_HEADS):
        x1 = pt[hd * HEAD_DIM:hd * HEAD_DIM + half]
        x2 = pt[hd * HEAD_DIM + half:(hd + 1) * HEAD_DIM]
        qt_ref[hd] = (jnp.concatenate([x1 * cos_t - x2 * sin_t, x2 * cos_t + x1 * sin_t], axis=0)
                      * scale).astype(BF16)
    ones = jnp.ones((V_ROWS - HEAD_DIM, tm), F32)
    for c in range(2 * NSA_KV_HEADS):
        v = pt[VT_ROW0 + c * HEAD_DIM:VT_ROW0 + (c + 1) * HEAD_DIM]
        vt_ref[c] = jnp.concatenate([v, ones], axis=0).astype(BF16)
    for g in range(NSA_KV_HEADS):
        gt_ref[g] = jax.nn.sigmoid(pt[GT_ROW0 + g * GATE_ROWS:GT_ROW0 + (g + 1) * GATE_ROWS])

    cos = cos_ref[...]
    sin = sin_ref[...]
    low_seg = lax.broadcasted_iota(jnp.int32, (tm // CMP_STRIDE, LANES), 1) < HEAD_DIM
    for c, xc in enumerate((_rope_chunk(p[:, Q_COL0:Q_COL0 + LANES], cos, sin),
                            p[:, Q_COL0 + LANES:Q_COL0 + 2 * LANES])):
        seg_scr[...] = xc
        tok = [seg_scr[pl.ds(j, tm // CMP_STRIDE, stride=CMP_STRIDE), :]
               for j in range(CMP_STRIDE)]
        g0, g1 = [], []
        for j in range(0, CMP_STRIDE, 2):
            g0.append(jnp.where(low_seg, tok[j], pltpu.roll(tok[j + 1], HEAD_DIM, axis=1)))
            g1.append(jnp.where(low_seg, pltpu.roll(tok[j], HEAD_DIM, axis=1), tok[j + 1]))
        cmp_ref[NSA_KV_HEADS * c] = jnp.concatenate(g0, axis=1).astype(BF16)
        cmp_ref[NSA_KV_HEADS * c + 1] = jnp.concatenate(g1, axis=1).astype(BF16)
    lane = lax.broadcasted_iota(jnp.int32, (tm, LANES), 1)
    low = lane < HEAD_DIM
    pos = (pl.program_id(0) % seq_tiles) * tm + lax.broadcasted_iota(jnp.int32, (tm, LANES), 0)
    ext_blk = jnp.where(lane - HEAD_DIM == (pos >> SLC_SHIFT), NEG, 0.0)
    ext_zero = jnp.zeros((tm, LANES), F32)
    for c, ext in enumerate((ext_blk, ext_zero)):
        xc = _rope_chunk(p[:, Q_COL0 + (2 + c) * LANES:Q_COL0 + (3 + c) * LANES], cos, sin)
        k_ref[2 * c] = jnp.where(low, xc, ext).astype(BF16)
        k_ref[2 * c + 1] = jnp.where(low, pltpu.roll(xc, HEAD_DIM, axis=1), ext).astype(BF16)


def _proj(h, w_in, conv_w, conv_b, S, *, tm=1024):
    T, D = h.shape
    G = NSA_KV_HEADS
    col = lambda c0, n: w_in[:, c0:c0 + n]
    w_nat = jnp.concatenate(
        [col(0, Q_COL0), col(KCMP_COL0, KV_WIDTH), col(VCMP_COL0, KV_WIDTH),
         col(KSLC_COL0, KV_WIDTH), col(KWIN_COL0, KV_WIDTH)], axis=1).astype(BF16)
    gate_rows = [jnp.pad(col(GATE_COL0 + g * N_GATE, N_GATE).T, ((0, GATE_ROWS - N_GATE), (0, 0)))
                 for g in range(G)]
    w_t = jnp.concatenate(
        [col(Q_COL0, NSA_WIDTH).T, col(VSLC_COL0, KV_WIDTH).T, col(VWIN_COL0, KV_WIDTH).T]
        + gate_rows, axis=0).astype(BF16)
    n_nat = w_nat.shape[1]

    half = HEAD_DIM // 2
    inv = ROPE_THETA ** (-jnp.arange(half, dtype=F32) / half)
    ang = jnp.arange(S, dtype=F32)[:, None] * inv[None, :]
    cos = jnp.cos(ang)
    sin = jnp.sin(ang)
    cos_n = jnp.concatenate([cos, cos, cos, cos], axis=1)
    sin_n = jnp.concatenate([-sin, sin, -sin, sin], axis=1)

    nS = S // tm
    seg_w = CMP_STRIDE * HEAD_DIM
    assert G * HEAD_DIM == LANES
    return pl.pallas_call(
        functools.partial(_proj_kernel, seq_tiles=nS),
        grid=(T // tm,),
        in_specs=[
            pl.BlockSpec((tm, D), lambda i: (i, 0)),
            _resident((D, n_nat)), _resident((WT_ROWS, D)),
            pl.BlockSpec((tm, LANES), lambda i: (i % nS, 0)),
            pl.BlockSpec((tm, LANES), lambda i: (i % nS, 0)),
            pl.BlockSpec((half, tm), lambda i: (0, i % nS)),
            pl.BlockSpec((half, tm), lambda i: (0, i % nS)),
            _resident((CONV_WIDTH, LRU_WIDTH)), _resident((1, LRU_WIDTH)),
        ],
        out_specs=[
            pl.BlockSpec((tm, Q_COL0), lambda i: (i, 0)),
            pl.BlockSpec((NSA_Q_HEADS, HEAD_DIM, tm), lambda i: (0, 0, i)),
            pl.BlockSpec((2 * G, tm // CMP_STRIDE, seg_w), lambda i: (0, i, 0)),
            pl.BlockSpec((2 * G, tm, LANES), lambda i: (0, i, 0)),
            pl.BlockSpec((2 * G, V_ROWS, tm), lambda i: (0, 0, i)),
            pl.BlockSpec((G, GATE_ROWS, tm), lambda i: (0, 0, i)),
        ],
        out_shape=[
            jax.ShapeDtypeStruct((T, Q_COL0), F32),
            jax.ShapeDtypeStruct((NSA_Q_HEADS, HEAD_DIM, T), BF16),
            jax.ShapeDtypeStruct((2 * G, T // CMP_STRIDE, seg_w), BF16),
            jax.ShapeDtypeStruct((2 * G, T, LANES), BF16),
            jax.ShapeDtypeStruct((2 * G, V_ROWS, T), BF16),
            jax.ShapeDtypeStruct((G, GATE_ROWS, T), F32),
        ],
        scratch_shapes=[pltpu.VMEM((tm, LANES), F32),
                        pltpu.VMEM((SUBLANES + tm, LRU_WIDTH), F32)],
        compiler_params=pltpu.CompilerParams(
            dimension_semantics=("arbitrary",), vmem_limit_bytes=VMEM_LIMIT),
        name="proj",
    )(h, w_nat, w_t, cos_n, sin_n, cos.T, sin.T, conv_w, conv_b.reshape(1, LRU_WIDTH))


SCAN_UNROLL = 8
LRU_ROWS = 512


def _lru_kernel(x_ref, gate_ref, wa_ref, ba_ref, wx_ref, bx_ref, lam_ref, y_ref, a_scr, b_scr):
    S, C = y_ref.shape
    neg_lam = -lam_ref[...]
    softplus = jnp.maximum(neg_lam, 0.0) + jnp.log1p(jnp.exp(-jnp.abs(neg_lam)))

    for r0 in range(0, S, LRU_ROWS):
        xc = x_ref[r0:r0 + LRU_ROWS, :]
        xb = xc.astype(BF16)
        gate_pre = lambda w_ref: jnp.concatenate(
            [_dot(xb[:, c * LANES:(c + 1) * LANES], w_ref[c]) for c in range(C // LANES)], axis=1)
        r = 0.5 * jnp.tanh(0.5 * (gate_pre(wa_ref) + ba_ref[...])) + 0.5
        ig = 0.5 * jnp.tanh(0.5 * (gate_pre(wx_ref) + bx_ref[...])) + 0.5
        log_a = (-LRU_C) * r * softplus
        a = jnp.exp(log_a)
        z = jnp.tanh(-log_a) * (a * a + 1.0)
        mult = z * lax.rsqrt(jnp.maximum(z, F32_TINY))
        if r0 == 0:
            row = lax.broadcasted_iota(jnp.int32, mult.shape, 0)
            mult = jnp.where(row == 0, 1.0, mult)
        a_scr[r0:r0 + LRU_ROWS, :] = a
        b_scr[r0:r0 + LRU_ROWS, :] = mult * (ig * xc)

    sub = lax.broadcasted_iota(jnp.int32, (SUBLANES, C), 0)

    def scan_tile(g, h_prev):
        rows = pl.ds(pl.multiple_of(g * SUBLANES, SUBLANES), SUBLANES)
        at = a_scr[rows, :]
        bt = b_scr[rows, :]
        d = 1
        while d < SUBLANES:
            keep = sub >= d
            a_sh = jnp.where(keep, pltpu.roll(at, d, axis=0), 1.0)
            b_sh = jnp.where(keep, pltpu.roll(bt, d, axis=0), 0.0)
            bt = at * b_sh + bt
            at = at * a_sh
            d *= 2
        h = bt + at * h_prev
        b_scr[rows, :] = h
        return jnp.broadcast_to(h[SUBLANES - 1:SUBLANES, :], (SUBLANES, C))

    def scan_body(i, h_prev):
        for u in range(SCAN_UNROLL):
            h_prev = scan_tile(i * SCAN_UNROLL + u, h_prev)
        return h_prev

    lax.fori_loop(0, S // (SUBLANES * SCAN_UNROLL), scan_body, jnp.zeros((SUBLANES, C), F32))

    for r0 in range(0, S, LRU_ROWS):
        rows = slice(r0, r0 + LRU_ROWS)
        y_ref[rows, :] = b_scr[rows, :] * gate_ref[rows, :]


def _lru(lru, w_a, b_a, w_x, b_x, lam, B, S):
    T = lru.shape[0]
    C = LRU_WIDTH
    n_ch = C // LANES
    assert S % (SUBLANES * SCAN_UNROLL) == 0 and S % LRU_ROWS == 0

    def blockdiag(w):
        w = w.reshape(n_ch, 2, LRU_BLOCK, LRU_BLOCK)
        z = jnp.zeros((n_ch, LRU_BLOCK, LRU_BLOCK), w.dtype)
        top = jnp.concatenate([w[:, 0], z], axis=2)
        bot = jnp.concatenate([z, w[:, 1]], axis=2)
        return jnp.concatenate([top, bot], axis=1).astype(BF16)

    vec = lambda v: v.reshape(1, C)
    return pl.pallas_call(
        _lru_kernel,
        grid=(B,),
        in_specs=[
            pl.BlockSpec((S, C), lambda b: (b, 0)),
            pl.BlockSpec((S, C), lambda b: (b, 1)),
            _resident((n_ch, LANES, LANES)), _resident((1, C)),
            _resident((n_ch, LANES, LANES)), _resident((1, C)),
            _resident((1, C)),
        ],
        out_specs=pl.BlockSpec((S, C), lambda b: (b, 0)),
        out_shape=jax.ShapeDtypeStruct((T, C), F32),
        scratch_shapes=[pltpu.VMEM((S, C), F32),
                        pltpu.VMEM((S, C), F32)],
        compiler_params=pltpu.CompilerParams(
            dimension_semantics=("parallel",), vmem_limit_bytes=VMEM_LIMIT),
        name="lru",
    )(lru, lru, blockdiag(w_a), vec(b_a.reshape(-1)),
      blockdiag(w_x), vec(b_x.reshape(-1)), vec(lam))


def _cmp_kernel(seg_ref, pek_ref, w1k_ref, w2k_ref, pev_ref, w1v_ref, w2vt_ref, kc_ref, vct_ref):
    n_seg = seg_ref.shape[2]
    half = CMP_STRIDE * HEAD_DIM

    def hidden(idx, pe_ref, w1_ref):
        seg = seg_ref[idx, 0]
        first = _dot(seg, w1_ref[0:half, :])
        second = _dot(seg, w1_ref[half:2 * half, :])
        bias = _dot(pe_ref[...], w1_ref[...])[0:1, :]
        return _silu(first + pltpu.roll(second, n_seg - 1, axis=0) + bias).astype(BF16)

    for g in range(NSA_KV_HEADS):
        tok = _dot(hidden(g, pek_ref, w1k_ref), w2k_ref[...])
        kc_ref[0, g] = jnp.concatenate([tok, jnp.zeros_like(tok)], axis=1).astype(BF16)
        vct_ref[0, g] = _dot_nt(w2vt_ref[...], hidden(NSA_KV_HEADS + g, pev_ref, w1v_ref)
                                ).astype(BF16)


def _compress(cmp_in, pe_k, w1_k, w2_k, pe_v, w1_v, w2_v, B, S):
    G = NSA_KV_HEADS
    n_seg = S // CMP_STRIDE
    seg = cmp_in.reshape(2 * G, B, n_seg, CMP_STRIDE * HEAD_DIM)
    flat = CMP_BLOCK * HEAD_DIM
    pe8 = lambda pe: jnp.broadcast_to(pe.reshape(1, -1), (8, flat)).astype(BF16)
    return pl.pallas_call(
        _cmp_kernel,
        grid=(B,),
        in_specs=[
            pl.BlockSpec((2 * G, 1, n_seg, CMP_STRIDE * HEAD_DIM), lambda b: (0, b, 0, 0)),
            _resident((8, flat)), _resident((flat, CMP_HIDDEN)), _resident((CMP_HIDDEN, HEAD_DIM)),
            _resident((8, flat)), _resident((flat, CMP_HIDDEN)), _resident((HEAD_DIM, CMP_HIDDEN)),
        ],
        out_specs=[
            pl.BlockSpec((1, G, n_seg, LANES), lambda b: (b, 0, 0, 0)),
            pl.BlockSpec((1, G, HEAD_DIM, n_seg), lambda b: (b, 0, 0, 0)),
        ],
        out_shape=[
            jax.ShapeDtypeStruct((B, G, n_seg, LANES), BF16),
            jax.ShapeDtypeStruct((B, G, HEAD_DIM, n_seg), BF16),
        ],
        compiler_params=pltpu.CompilerParams(
            dimension_semantics=("parallel",), vmem_limit_bytes=VMEM_LIMIT),
        name="compress",
    )(seg, pe8(pe_k), w1_k.astype(BF16), w2_k.astype(BF16),
      pe8(pe_v), w1_v.astype(BF16), w2_v.T.astype(BF16))


def _attn_kernel(qt_ref, kc_ref, vct_ref, ks_ref, vst_ref, kw_ref, vwt_ref, gt_ref, ovl_ref,
                 bias_ref, *rest, tq, tiles):
    o_ref = rest[-1]

    def step(n):
        for _ in _interleave(*[
                _attn_step(n, g, qt_ref, kc_ref.at[g], vct_ref.at[g], ks_ref.at[g], vst_ref.at[g],
                           kw_ref.at[g], vwt_ref.at[g], gt_ref.at[g], ovl_ref, bias_ref, o_ref, tq)
                for g in range(NSA_KV_HEADS)]):
            pass

    for n in range(*tiles):
        pl.when(pl.program_id(1) == n - tiles[0])(functools.partial(step, n))


CHUNK_TILES = 1
SCORES_AHEAD = 1
ATTN_TILE_SPLITS = ((0, 4), (4, 6), (6, 8))


def _interleave(*stages):
    live = list(stages)
    while live:
        for st in list(live):
            try:
                next(st)
            except StopIteration:
                live.remove(st)
        yield


def _tile_bias(bias_ref, t0, k0, behind):
    if k0 == t0:
        return bias_ref[0]
    if behind is not None and k0 == t0 - behind:
        return bias_ref[1]
    return None


def _key_chunks(k_ref, vt_ref, k0, n_keys, tile, q, bias_fn):
    starts = list(range(k0, k0 + n_keys, tile))
    return [dict(k_ref=k_ref, vt_ref=vt_ref, tile=tile, q=q, bias_fn=bias_fn,
                 starts=starts[i:i + CHUNK_TILES]) for i in range(0, len(starts), CHUNK_TILES)]


def _scores_stage(ch):
    tiles, m = [], None
    for k0 in ch["starts"]:
        s = _dot(ch["k_ref"][k0:k0 + ch["tile"], :], ch["q"])
        bias = ch["bias_fn"](k0)
        if bias is not None:
            s = s + bias
        cm = jnp.max(s, axis=0, keepdims=True)
        m = cm if m is None else jnp.maximum(m, cm)
        tiles.append(s)
        yield
    ch["s"], ch["m"] = tiles, m


def _pv_stage(ch, heads, tq):
    acc = [None] * heads
    for k0, s in zip(ch["starts"], ch["s"]):
        p = jnp.exp2(s - ch["m"]).astype(BF16)
        vt = ch["vt_ref"][:, k0:k0 + ch["tile"]]
        for r in range(heads):
            d = _dot(vt, p[:, r * tq:(r + 1) * tq])
            acc[r] = d if acc[r] is None else acc[r] + d
        yield
    ch["acc"] = acc


def _merge_chunks(chunks, heads, tq):
    if len(chunks) == 1:
        return chunks[0]["acc"]
    m = functools.reduce(jnp.maximum, [ch["m"] for ch in chunks])
    acc = [None] * heads
    for ch in chunks:
        w = jnp.exp2(ch["m"] - m)
        for r in range(heads):
            term = ch["acc"][r] * w[:, r * tq:(r + 1) * tq]
            acc[r] = term if acc[r] is None else acc[r] + term
    return acc


def _rank_stage(out, score, jblk, n_live):
    rank = jnp.zeros(score.shape, jnp.int32)
    for kb in range(n_live):
        sk = score[kb:kb + 1, :]
        ahead = (sk > score) | ((sk == score) & (jblk > kb))
        rank = rank + ahead.astype(jnp.int32)
        if kb % 4 == 3:
            yield
    out["rank"] = rank


def _attn_step(n, g, qt_ref, kc_ref, vct_ref, ks_ref, vst_ref, kw_ref, vwt_ref, gt_ref, ovl_ref,
               bias_ref, o_ref, tq):
    R = NSA_GROUP
    t0 = n * tq
    n_cmp = kc_ref.shape[0]
    n_blk = ovl_ref.shape[0]
    q_all = jnp.concatenate([qt_ref[g * R + r] for r in range(R)], axis=1)
    q_pad = jnp.concatenate([q_all, jnp.zeros((LANES - HEAD_DIM, R * tq), BF16)], axis=0)
    head = lambda a, r: a[:, r * tq:(r + 1) * tq]

    kw0 = max(t0 - WINDOW, 0)
    win = _key_chunks(kw_ref, vwt_ref, kw0, t0 + tq - kw0, tq, q_pad,
                      lambda k0: _tile_bias(bias_ref, t0, k0, WINDOW))
    win_scores = _scores_stage(win[0])
    next(win_scores)
    yield

    n_vis = (t0 + tq - CMP_BLOCK) // CMP_STRIDE + 1
    n_vis = min(n_cmp, -(-n_vis // BF16_ROWS) * BF16_ROWS)
    tpos_c = t0 + (lax.broadcasted_iota(jnp.int32, (n_vis, R * tq), 1) & (tq - 1))
    cend = lax.broadcasted_iota(jnp.int32, (n_vis, R * tq), 0) * CMP_STRIDE + (CMP_BLOCK - 1)
    s = _dot(kc_ref[0:n_vis, :], q_pad) + jnp.where(cend <= tpos_c, 0.0, NEG)
    e = jnp.exp2(s - jnp.max(s, axis=0, keepdims=True))
    p_cmp = e * (1.0 / jnp.sum(e, axis=0, keepdims=True))
    if t0 < CMP_BLOCK - 1:
        p_cmp = jnp.where(tpos_c >= CMP_BLOCK - 1, p_cmp, 0.0)
    if n_vis < n_cmp:
        p_cmp = jnp.concatenate([p_cmp, jnp.zeros((n_cmp - n_vis, R * tq), F32)], axis=0)
    p_cmp_b = p_cmp.astype(BF16)
    vct = vct_ref[...]
    o_cmp = [_dot(vct, head(p_cmp_b, r)) for r in range(R)]
    psum = functools.reduce(jnp.add, [head(p_cmp, r) for r in range(R)])
    yield

    p_hi = psum.astype(BF16)
    p_lo = (psum - p_hi.astype(F32)).astype(BF16)
    ovl = ovl_ref[...]
    imp = _dot(ovl, p_hi) + _dot(ovl, p_lo)
    n_live = min(n_blk, (t0 + tq - 1) // SLC_BLOCK + 1)
    rows = -(-n_live // SUBLANES) * SUBLANES
    jblk = lax.broadcasted_iota(jnp.int32, (rows, tq), 0)
    tpos_b = t0 + lax.broadcasted_iota(jnp.int32, (rows, tq), 1)
    blk_valid = jblk * SLC_BLOCK <= tpos_b
    back = (tpos_b >> SLC_SHIFT) - jblk
    forced = (jblk == 0) | ((back >= 0) & (back < N_LOCAL_BLOCKS))
    score = jnp.where(blk_valid, jnp.where(forced, jnp.inf, imp[0:rows]), -jnp.inf)
    sel = {}
    yield from _interleave(_rank_stage(sel, score, jblk, n_live), win_scores)
    unsel = 1.0 - (blk_valid & (sel["rank"] < SLC_TOP_N)).astype(F32)
    dead = [jnp.ones((n_blk - rows, tq), F32)] if rows < n_blk else []
    sel_ext = jnp.concatenate(
        [unsel] + dead + [jnp.zeros((LANES - HEAD_DIM - n_blk, tq), F32)],
        axis=0).astype(BF16)
    q_sel = jnp.concatenate([q_all, jnp.concatenate([sel_ext] * R, axis=1)], axis=0)

    slc = _key_chunks(ks_ref, vst_ref, 0, t0 + tq, tq, q_sel,
                      lambda k0: _tile_bias(bias_ref, t0, k0, None))
    chunks = win + slc
    for ch in chunks[1:SCORES_AHEAD]:
        yield from _scores_stage(ch)
    for idx, ch in enumerate(chunks):
        stages = [_pv_stage(ch, R, tq)]
        if idx + SCORES_AHEAD < len(chunks):
            stages.append(_scores_stage(chunks[idx + SCORES_AHEAD]))
        yield from _interleave(*stages)
    acc_w = _merge_chunks(win, R, tq)
    acc_s = _merge_chunks(slc, R, tq)
    yield

    gt = gt_ref[...]
    outs = []
    for r in range(R):
        g_cmp = gt[3 * r:3 * r + 1]
        g_slc = gt[3 * r + 1:3 * r + 2] * (1.0 / acc_s[r][HEAD_DIM:HEAD_DIM + 1])
        g_win = gt[3 * r + 2:3 * r + 3] * (1.0 / acc_w[r][HEAD_DIM:HEAD_DIM + 1])
        outs.append(g_cmp * o_cmp[r] + g_slc * acc_s[r][:HEAD_DIM] + g_win * acc_w[r][:HEAD_DIM])
    width = R * HEAD_DIM
    o_ref[:, g * width:(g + 1) * width] = jnp.concatenate(outs, axis=0).T


def _overlap_t(n_cmp_pad, n_blk):
    cs = np.arange(n_cmp_pad) * CMP_STRIDE
    ce = cs + CMP_BLOCK - 1
    ss = np.arange(n_blk) * SLC_BLOCK
    se = ss + SLC_BLOCK - 1
    return ((cs[None, :] <= se[:, None]) & (ce[None, :] >= ss[:, None])).astype(np.float32)


def _attention(qt, k, vt, kc, vct, gates_t, B, S, *, tq=256):
    T = qt.shape[2]
    nq = S // tq
    n_cmp = S // CMP_STRIDE
    n_blk = S // SLC_BLOCK
    G = NSA_KV_HEADS
    ovl = jnp.asarray(_overlap_t(n_cmp, n_blk), BF16)
    assert WINDOW % tq == 0 and WINDOW + tq <= S
    assert HEAD_DIM + n_blk <= LANES
    assert tq & (tq - 1) == 0
    key_off = np.arange(tq)[:, None]
    q_off = np.tile(np.arange(tq), NSA_GROUP)[None, :]
    bias = jnp.asarray(np.stack([np.where(key_off <= q_off, 0.0, NEG),
                                 np.where(key_off > q_off, 0.0, NEG)]), F32)
    k_spec = lambda c: pl.BlockSpec((G, S, LANES), lambda b, i: (c, b, 0))
    vt_spec = lambda c: pl.BlockSpec((G, V_ROWS, S), lambda b, i: (c, 0, b))

    def call(tiles, prev):
        lo, hi = tiles
        tile = lambda b, i: b * nq + lo + i
        in_specs = [
            pl.BlockSpec((NSA_Q_HEADS, HEAD_DIM, tq), lambda b, i: (0, 0, tile(b, i))),
            pl.BlockSpec((None, G, n_cmp, LANES), lambda b, i: (b, 0, 0, 0)),
            pl.BlockSpec((None, G, HEAD_DIM, n_cmp), lambda b, i: (b, 0, 0, 0)),
            k_spec(0), vt_spec(0), k_spec(1), vt_spec(1),
            pl.BlockSpec((G, GATE_ROWS, tq), lambda b, i: (0, 0, tile(b, i))),
            _resident((n_blk, n_cmp)),
            _resident((2, tq, NSA_GROUP * tq)),
        ]
        args = [qt, kc, vct, k, vt, k, vt, gates_t, ovl, bias]
        aliases = {}
        if prev is not None:
            in_specs.append(pl.BlockSpec(memory_space=pl.ANY))
            aliases = {len(args): 0}
            args.append(prev)
        return pl.pallas_call(
            functools.partial(_attn_kernel, tq=tq, tiles=tiles),
            grid=(B, hi - lo),
            in_specs=in_specs,
            out_specs=pl.BlockSpec((tq, NSA_WIDTH), lambda b, i: (tile(b, i), 0)),
            out_shape=jax.ShapeDtypeStruct((T, NSA_WIDTH), F32),
            input_output_aliases=aliases,
            compiler_params=pltpu.CompilerParams(
                dimension_semantics=("parallel", "arbitrary"),
                vmem_limit_bytes=VMEM_LIMIT),
            name=f"nsa_attn_{lo}_{hi}",
        )(*args)

    assert [t for lo, hi in ATTN_TILE_SPLITS for t in range(lo, hi)] == list(range(nq))
    out = None
    for tiles in ATTN_TILE_SPLITS:
        out = call(tiles, out)
    return out


def _out_ffn_kernel(ylru_ref, ynsa_ref, h_ref, gl_ref, gn_ref, w_ref, g2_ref, b2_ref,
                    w_in_hbm, w_out_hbm, g3_ref, b3_ref, o_ref,
                    wg_scr, wu_scr, wo_scr, stage_in, stage_out, sem):
    @pl.when(pl.program_id(0) == 0)
    def _():
        _stage_ffn_weights(w_in_hbm, w_out_hbm, wg_scr, wu_scr, wo_scr, stage_in, stage_out, sem)

    yl = _rms_norm(ylru_ref[...], gl_ref[...]).astype(BF16)
    yn = _rms_norm(ynsa_ref[...], gn_ref[...]).astype(BF16)
    mix = _dot(yl, w_ref[0:LRU_WIDTH, :]) + _dot(yn, w_ref[LRU_WIDTH:, :])
    h2 = _layer_norm(ALPHA * h_ref[...] + mix, g2_ref[...], b2_ref[...])
    o_ref[...] = _ffn_half_step(h2, wg_scr, wu_scr, wo_scr, g3_ref, b3_ref)


def _out_proj_ffn(y_lru, y_nsa, h, gn_lru, gn_nsa, w_mix, g2, b2, w_in, w_out, g3, b3, *,
                  tm=512):
    T, D = h.shape
    row = lambda v: v.reshape(1, -1)
    return pl.pallas_call(
        _out_ffn_kernel,
        grid=(T // tm,),
        in_specs=[
            pl.BlockSpec((tm, LRU_WIDTH), lambda i: (i, 0)),
            pl.BlockSpec((tm, NSA_WIDTH), lambda i: (i, 0)),
            pl.BlockSpec((tm, D), lambda i: (i, 0)),
            _resident((1, LRU_WIDTH)), _resident((1, NSA_WIDTH)),
            _resident((LRU_WIDTH + NSA_WIDTH, D)),
            _resident((1, D)), _resident((1, D)),
            pl.BlockSpec(memory_space=pl.ANY), pl.BlockSpec(memory_space=pl.ANY),
            _resident((1, D)), _resident((1, D)),
        ],
        out_specs=pl.BlockSpec((tm, D), lambda i: (i, 0)),
        out_shape=jax.ShapeDtypeStruct((T, D), F32),
        scratch_shapes=_ffn_weight_scratch(D),
        compiler_params=pltpu.CompilerParams(
            dimension_semantics=("arbitrary",), vmem_limit_bytes=FFN_VMEM_LIMIT),
        name="out_proj_ffn",
    )(y_lru, y_nsa, h, row(gn_lru), row(gn_nsa), w_mix.astype(BF16), row(g2), row(b2),
      w_in, w_out, row(g3), row(b3))


def kernel(x, ffn1_w_in, ffn1_w_out, ln1_g, ln1_b, mix_w_in, conv_w, conv_b, lru_w_a, lru_b_a,
           lru_w_x, lru_b_x, lru_lam, cmp_pe_k, cmp_w1_k, cmp_w2_k, cmp_pe_v, cmp_w1_v,
           cmp_w2_v, gn_lru, gn_nsa, mix_w_out, ln2_g, ln2_b, ffn2_w_in, ffn2_w_out, ln3_g,
           ln3_b):
    B, S, D = x.shape
    h = x.reshape(B * S, D)
    for l in range(DEPTH):
        h = _ffn_ln(h, ffn1_w_in[l], ffn1_w_out[l], ln1_g[l], ln1_b[l])
        lru, qt, cmp_in, k, vt, gates_t = _proj(h, mix_w_in[l], conv_w[l], conv_b[l], S)
        y_lru = _lru(lru, lru_w_a[l], lru_b_a[l], lru_w_x[l], lru_b_x[l], lru_lam[l], B, S)
        kc, vct = _compress(cmp_in, cmp_pe_k[l], cmp_w1_k[l], cmp_w2_k[l],
                            cmp_pe_v[l], cmp_w1_v[l], cmp_w2_v[l], B, S)
        y_nsa = _attention(qt, k, vt, kc, vct, gates_t, B, S)
        h = _out_proj_ffn(y_lru, y_nsa, h, gn_lru[l], gn_nsa[l], mix_w_out[l], ln2_g[l], ln2_b[l],
                          ffn2_w_in[l], ffn2_w_out[l], ln3_g[l], ln3_b[l])
    return h.reshape(B, S, D)
```

```python
import functools

import numpy as np
import jax
import jax.numpy as jnp
from jax import lax
from jax.experimental import pallas as pl
from jax.experimental.pallas import tpu as pltpu

F32 = jnp.float32
BF16 = jnp.bfloat16

D_MODEL = 1024
LRU_WIDTH = 512
LRU_HEADS = 8
LRU_BLOCK = 64
CONV_WIDTH = 4
LRU_C = 8.0
NSA_Q_HEADS = 8
NSA_KV_HEADS = 2
NSA_GROUP = 4
HEAD_DIM = 64
NSA_WIDTH = 512
KV_WIDTH = 128
CMP_BLOCK = 32
CMP_STRIDE = 16
CMP_HIDDEN = 256
SLC_BLOCK = 64
SLC_SHIFT = 6
SLC_TOP_N = 16
N_LOCAL_BLOCKS = 2
WINDOW = 512
ROPE_THETA = 10000.0
D_FF = 2816
DEPTH = 1
ALPHA = (2.0 * DEPTH) ** 0.25
LN_EPS = 1e-5
RMS_EPS = 1e-6
NEG = -1e30
LOG2E = 1.4426950408889634
F32_TINY = 1.1754943508222875e-38

LANES = 128
SUBLANES = 8
BF16_ROWS = 16
VMEM_LIMIT = 48 * 1024 * 1024
N_GATE = 3 * NSA_GROUP
GATE_ROWS = 16
V_ROWS = HEAD_DIM + BF16_ROWS


def _dot(a, b):
    return jnp.dot(a, b, preferred_element_type=F32)


def _dot_nt(a, b):
    return lax.dot_general(a, b, (((1,), (1,)), ((), ())), preferred_element_type=F32)


def _layer_norm(y, g, b):
    mu = jnp.mean(y, axis=-1, keepdims=True)
    d = y - mu
    var = jnp.mean(d * d, axis=-1, keepdims=True)
    return d * lax.rsqrt(var + LN_EPS) * g + b


def _rms_norm(y, g):
    return y * lax.rsqrt(jnp.mean(y * y, axis=-1, keepdims=True) + RMS_EPS) * g


def _silu(x):
    return x * jax.nn.sigmoid(x)


def _resident(shape):
    return pl.BlockSpec(shape, lambda *_: (0,) * len(shape), pipeline_mode=pl.Buffered(1))


def _ffn_half_step(x, wg_ref, wu_ref, wo_ref, g_ref, b_ref):
    xb = x.astype(BF16)
    gate = _dot(xb, wg_ref[...])
    up = _dot(xb, wu_ref[...])
    act = (_silu(gate) * up).astype(BF16)
    y = ALPHA * x + 0.5 * _dot(act, wo_ref[...])
    return _layer_norm(y, g_ref[...], b_ref[...])


W_IN_ROWS = 64
W_OUT_ROWS = 352
FFN_VMEM_LIMIT = 56 * 1024 * 1024


def _stream_rows(w_hbm, rows, stage, sems, consume):
    def copy(c):
        return pltpu.make_async_copy(w_hbm.at[pl.ds(c * rows, rows), :], stage.at[c % 2],
                                     sems.at[c % 2])

    n = w_hbm.shape[0] // rows
    copy(0).start()
    for c in range(n):
        if c + 1 < n:
            copy(c + 1).start()
        copy(c).wait()
        consume(c * rows, stage.at[c % 2])


def _stage_ffn_weights(w_in_hbm, w_out_hbm, wg_scr, wu_scr, wo_scr, stage_in, stage_out, sems):
    def put_in(r0, buf):
        wg_scr[r0:r0 + W_IN_ROWS, :] = buf[:, :D_FF].astype(BF16)
        wu_scr[r0:r0 + W_IN_ROWS, :] = buf[:, D_FF:].astype(BF16)

    def put_out(r0, buf):
        wo_scr[r0:r0 + W_OUT_ROWS, :] = buf[...].astype(BF16)

    _stream_rows(w_in_hbm, W_IN_ROWS, stage_in, sems.at[0], put_in)
    _stream_rows(w_out_hbm, W_OUT_ROWS, stage_out, sems.at[1], put_out)


def _ffn_weight_scratch(D):
    assert D % W_IN_ROWS == 0 and D_FF % W_OUT_ROWS == 0
    return [pltpu.VMEM((D, D_FF), BF16), pltpu.VMEM((D, D_FF), BF16), pltpu.VMEM((D_FF, D), BF16),
            pltpu.VMEM((2, W_IN_ROWS, 2 * D_FF), F32), pltpu.VMEM((2, W_OUT_ROWS, D), F32),
            pltpu.SemaphoreType.DMA((2, 2))]


def _ffn_ln_kernel(x_ref, w_in_hbm, w_out_hbm, g_ref, b_ref, o_ref,
                   wg_scr, wu_scr, wo_scr, stage_in, stage_out, sem):
    @pl.when(pl.program_id(0) == 0)
    def _():
        _stage_ffn_weights(w_in_hbm, w_out_hbm, wg_scr, wu_scr, wo_scr, stage_in, stage_out, sem)

    o_ref[...] = _ffn_half_step(x_ref[...], wg_scr, wu_scr, wo_scr, g_ref, b_ref)


def _ffn_ln(x, w_in, w_out, g, b, *, tm=1024):
    T, D = x.shape
    return pl.pallas_call(
        _ffn_ln_kernel,
        grid=(T // tm,),
        in_specs=[
            pl.BlockSpec((tm, D), lambda i: (i, 0)),
            pl.BlockSpec(memory_space=pl.ANY), pl.BlockSpec(memory_space=pl.ANY),
            _resident((1, D)), _resident((1, D)),
        ],
        out_specs=pl.BlockSpec((tm, D), lambda i: (i, 0)),
        out_shape=jax.ShapeDtypeStruct((T, D), F32),
        scratch_shapes=_ffn_weight_scratch(D),
        compiler_params=pltpu.CompilerParams(
            dimension_semantics=("arbitrary",), vmem_limit_bytes=FFN_VMEM_LIMIT),
        name="ffn_ln",
    )(x, w_in, w_out, g.reshape(1, D), b.reshape(1, D))


Q_COL0 = 2 * LRU_WIDTH
KCMP_COL0 = Q_COL0 + NSA_WIDTH
VCMP_COL0 = KCMP_COL0 + KV_WIDTH
KSLC_COL0 = VCMP_COL0 + KV_WIDTH
VSLC_COL0 = KSLC_COL0 + KV_WIDTH
KWIN_COL0 = VSLC_COL0 + KV_WIDTH
VWIN_COL0 = KWIN_COL0 + KV_WIDTH
GATE_COL0 = VWIN_COL0 + KV_WIDTH
VT_ROW0 = NSA_WIDTH
GT_ROW0 = VT_ROW0 + 2 * KV_WIDTH
WT_ROWS = GT_ROW0 + NSA_KV_HEADS * GATE_ROWS


def _rope_chunk(xc, cos, sin_signed):
    lane = lax.broadcasted_iota(jnp.int32, xc.shape, 1)
    first = (lane & (HEAD_DIM - 1)) < (HEAD_DIM // 2)
    partner = jnp.where(first, pltpu.roll(xc, LANES - HEAD_DIM // 2, axis=1),
                        pltpu.roll(xc, HEAD_DIM // 2, axis=1))
    return xc * cos + partner * sin_signed


def _proj_kernel(h_ref, wn_ref, wt_ref, cos_ref, sin_ref, cost_ref, sint_ref, cw_ref, cb_ref,
                 lru_ref, qt_ref, cmp_ref, k_ref, vt_ref, gt_ref, seg_scr, conv_scr, *, seq_tiles):
    tm = h_ref.shape[0]
    half = HEAD_DIM // 2

    @pl.when(pl.program_id(0) % seq_tiles == 0)
    def _():
        conv_scr[0:SUBLANES, :] = jnp.zeros((SUBLANES, LRU_WIDTH), F32)

    hb = h_ref[...].astype(BF16)
    p = _dot(hb, wn_ref[...])
    pt = _dot_nt(wt_ref[...], hb)

    x = p[:, :LRU_WIDTH]
    conv_scr[SUBLANES:, :] = x
    xc = cb_ref[...]
    for d in range(CONV_WIDTH):
        xs = conv_scr[SUBLANES - d:SUBLANES - d + tm, :]
        xc = xc + xs * cw_ref[CONV_WIDTH - 1 - d:CONV_WIDTH - d, :]
    conv_scr[0:SUBLANES, :] = x[tm - SUBLANES:, :]
    lru_ref[:, :LRU_WIDTH] = xc
    lru_ref[:, LRU_WIDTH:] = jax.nn.gelu(p[:, LRU_WIDTH:Q_COL0])

    cos_t = cost_ref[...]
    sin_t = sint_ref[...]
    scale = HEAD_DIM ** -0.5 * LOG2E
    for hd in range(NSA_Q_HEADS):
        x1 = pt[hd * HEAD_DIM:hd * HEAD_DIM + half]
        x2 = pt[hd * HEAD_DIM + half:(hd + 1) * HEAD_DIM]
        qt_ref[hd] = (jnp.concatenate([x1 * cos_t - x2 * sin_t, x2 * cos_t + x1 * sin_t], axis=0)
                      * scale).astype(BF16)
    ones = jnp.ones((V_ROWS - HEAD_DIM, tm), F32)
    for c in range(2 * NSA_KV_HEADS):
        v = pt[VT_ROW0 + c * HEAD_DIM:VT_ROW0 + (c + 1) * HEAD_DIM]
        vt_ref[c] = jnp.concatenate([v, ones], axis=0).astype(BF16)
    for g in range(NSA_KV_HEADS):
        gt_ref[g] = jax.nn.sigmoid(pt[GT_ROW0 + g * GATE_ROWS:GT_ROW0 + (g + 1) * GATE_ROWS])

    cos = cos_ref[...]
    sin = sin_ref[...]
    low_seg = lax.broadcasted_iota(jnp.int32, (tm // CMP_STRIDE, LANES), 1) < HEAD_DIM
    for c, xc in enumerate((_rope_chunk(p[:, Q_COL0:Q_COL0 + LANES], cos, sin),
                            p[:, Q_COL0 + LANES:Q_COL0 + 2 * LANES])):
        seg_scr[...] = xc
        tok = [seg_scr[pl.ds(j, tm // CMP_STRIDE, stride=CMP_STRIDE), :]
               for j in range(CMP_STRIDE)]
        g0, g1 = [], []
        for j in range(0, CMP_STRIDE, 2):
            g0.append(jnp.where(low_seg, tok[j], pltpu.roll(tok[j + 1], HEAD_DIM, axis=1)))
            g1.append(jnp.where(low_seg, pltpu.roll(tok[j], HEAD_DIM, axis=1), tok[j + 1]))
        cmp_ref[NSA_KV_HEADS * c] = jnp.concatenate(g0, axis=1).astype(BF16)
        cmp_ref[NSA_KV_HEADS * c + 1] = jnp.concatenate(g1, axis=1).astype(BF16)
    lane = lax.broadcasted_iota(jnp.int32, (tm, LANES), 1)
    low = lane < HEAD_DIM
    pos = (pl.program_id(0) % seq_tiles) * tm + lax.broadcasted_iota(jnp.int32, (tm, LANES), 0)
    ext_blk = jnp.where(lane - HEAD_DIM == (pos >> SLC_SHIFT), NEG, 0.0)
    ext_zero = jnp.zeros((tm, LANES), F32)
    for c, ext in enumerate((ext_blk, ext_zero)):
        xc = _rope_chunk(p[:, Q_COL0 + (2 + c) * LANES:Q_COL0 + (3 + c) * LANES], cos, sin)
        k_ref[2 * c] = jnp.where(low, xc, ext).astype(BF16)
        k_ref[2 * c + 1] = jnp.where(low, pltpu.roll(xc, HEAD_DIM, axis=1), ext).astype(BF16)


def _proj(h, w_in, conv_w, conv_b, S, *, tm=1024):
    T, D = h.shape
    G = NSA_KV_HEADS
    col = lambda c0, n: w_in[:, c0:c0 + n]
    w_nat = jnp.concatenate(
        [col(0, Q_COL0), col(KCMP_COL0, KV_WIDTH), col(VCMP_COL0, KV_WIDTH),
         col(KSLC_COL0, KV_WIDTH), col(KWIN_COL0, KV_WIDTH)], axis=1).astype(BF16)
    gate_rows = [jnp.pad(col(GATE_COL0 + g * N_GATE, N_GATE).T, ((0, GATE_ROWS - N_GATE), (0, 0)))
                 for g in range(G)]
    w_t = jnp.concatenate(
        [col(Q_COL0, NSA_WIDTH).T, col(VSLC_COL0, KV_WIDTH).T, col(VWIN_COL0, KV_WIDTH).T]
        + gate_rows, axis=0).astype(BF16)
    n_nat = w_nat.shape[1]

    half = HEAD_DIM // 2
    inv = ROPE_THETA ** (-jnp.arange(half, dtype=F32) / half)
    ang = jnp.arange(S, dtype=F32)[:, None] * inv[None, :]
    cos = jnp.cos(ang)
    sin = jnp.sin(ang)
    cos_n = jnp.concatenate([cos, cos, cos, cos], axis=1)
    sin_n = jnp.concatenate([-sin, sin, -sin, sin], axis=1)

    nS = S // tm
    seg_w = CMP_STRIDE * HEAD_DIM
    assert G * HEAD_DIM == LANES
    return pl.pallas_call(
        functools.partial(_proj_kernel, seq_tiles=nS),
        grid=(T // tm,),
        in_specs=[
            pl.BlockSpec((tm, D), lambda i: (i, 0)),
            _resident((D, n_nat)), _resident((WT_ROWS, D)),
            pl.BlockSpec((tm, LANES), lambda i: (i % nS, 0)),
            pl.BlockSpec((tm, LANES), lambda i: (i % nS, 0)),
            pl.BlockSpec((half, tm), lambda i: (0, i % nS)),
            pl.BlockSpec((half, tm), lambda i: (0, i % nS)),
            _resident((CONV_WIDTH, LRU_WIDTH)), _resident((1, LRU_WIDTH)),
        ],
        out_specs=[
            pl.BlockSpec((tm, Q_COL0), lambda i: (i, 0)),
            pl.BlockSpec((NSA_Q_HEADS, HEAD_DIM, tm), lambda i: (0, 0, i)),
            pl.BlockSpec((2 * G, tm // CMP_STRIDE, seg_w), lambda i: (0, i, 0)),
            pl.BlockSpec((2 * G, tm, LANES), lambda i: (0, i, 0)),
            pl.BlockSpec((2 * G, V_ROWS, tm), lambda i: (0, 0, i)),
            pl.BlockSpec((G, GATE_ROWS, tm), lambda i: (0, 0, i)),
        ],
        out_shape=[
            jax.ShapeDtypeStruct((T, Q_COL0), F32),
            jax.ShapeDtypeStruct((NSA_Q_HEADS, HEAD_DIM, T), BF16),
            jax.ShapeDtypeStruct((2 * G, T // CMP_STRIDE, seg_w), BF16),
            jax.ShapeDtypeStruct((2 * G, T, LANES), BF16),
            jax.ShapeDtypeStruct((2 * G, V_ROWS, T), BF16),
            jax.ShapeDtypeStruct((G, GATE_ROWS, T), F32),
        ],
        scratch_shapes=[pltpu.VMEM((tm, LANES), F32),
                        pltpu.VMEM((SUBLANES + tm, LRU_WIDTH), F32)],
        compiler_params=pltpu.CompilerParams(
            dimension_semantics=("arbitrary",), vmem_limit_bytes=VMEM_LIMIT),
        name="proj",
    )(h, w_nat, w_t, cos_n, sin_n, cos.T, sin.T, conv_w, conv_b.reshape(1, LRU_WIDTH))


SCAN_UNROLL = 8
LRU_ROWS = 512


def _lru_kernel(x_ref, gate_ref, wa_ref, ba_ref, wx_ref, bx_ref, lam_ref, y_ref, a_scr, b_scr):
    S, C = y_ref.shape
    neg_lam = -lam_ref[...]
    softplus = jnp.maximum(neg_lam, 0.0) + jnp.log1p(jnp.exp(-jnp.abs(neg_lam)))

    for r0 in range(0, S, LRU_ROWS):
        xc = x_ref[r0:r0 + LRU_ROWS, :]
        xb = xc.astype(BF16)
        gate_pre = lambda w_ref: jnp.concatenate(
            [_dot(xb[:, c * LANES:(c + 1) * LANES], w_ref[c]) for c in range(C // LANES)], axis=1)
        r = 0.5 * jnp.tanh(0.5 * (gate_pre(wa_ref) + ba_ref[...])) + 0.5
        ig = 0.5 * jnp.tanh(0.5 * (gate_pre(wx_ref) + bx_ref[...])) + 0.5
        log_a = (-LRU_C) * r * softplus
        a = jnp.exp(log_a)
        z = jnp.tanh(-log_a) * (a * a + 1.0)
        mult = z * lax.rsqrt(jnp.maximum(z, F32_TINY))
        if r0 == 0:
            row = lax.broadcasted_iota(jnp.int32, mult.shape, 0)
            mult = jnp.where(row == 0, 1.0, mult)
        a_scr[r0:r0 + LRU_ROWS, :] = a
        b_scr[r0:r0 + LRU_ROWS, :] = mult * (ig * xc)

    sub = lax.broadcasted_iota(jnp.int32, (SUBLANES, C), 0)

    def scan_tile(g, h_prev):
        rows = pl.ds(pl.multiple_of(g * SUBLANES, SUBLANES), SUBLANES)
        at = a_scr[rows, :]
        bt = b_scr[rows, :]
        d = 1
        while d < SUBLANES:
            keep = sub >= d
            a_sh = jnp.where(keep, pltpu.roll(at, d, axis=0), 1.0)
            b_sh = jnp.where(keep, pltpu.roll(bt, d, axis=0), 0.0)
            bt = at * b_sh + bt
            at = at * a_sh
            d *= 2
        h = bt + at * h_prev
        b_scr[rows, :] = h
        return jnp.broadcast_to(h[SUBLANES - 1:SUBLANES, :], (SUBLANES, C))

    def scan_body(i, h_prev):
        for u in range(SCAN_UNROLL):
            h_prev = scan_tile(i * SCAN_UNROLL + u, h_prev)
        return h_prev

    lax.fori_loop(0, S // (SUBLANES * SCAN_UNROLL), scan_body, jnp.zeros((SUBLANES, C), F32))

    for r0 in range(0, S, LRU_ROWS):
        rows = slice(r0, r0 + LRU_ROWS)
        y_ref[rows, :] = b_scr[rows, :] * gate_ref[rows, :]


def _lru(lru, w_a, b_a, w_x, b_x, lam, B, S):
    T = lru.shape[0]
    C = LRU_WIDTH
    n_ch = C // LANES
    assert S % (SUBLANES * SCAN_UNROLL) == 0 and S % LRU_ROWS == 0

    def blockdiag(w):
        w = w.reshape(n_ch, 2, LRU_BLOCK, LRU_BLOCK)
        z = jnp.zeros((n_ch, LRU_BLOCK, LRU_BLOCK), w.dtype)
        top = jnp.concatenate([w[:, 0], z], axis=2)
        bot = jnp.concatenate([z, w[:, 1]], axis=2)
        return jnp.concatenate([top, bot], axis=1).astype(BF16)

    vec = lambda v: v.reshape(1, C)
    return pl.pallas_call(
        _lru_kernel,
        grid=(B,),
        in_specs=[
            pl.BlockSpec((S, C), lambda b: (b, 0)),
            pl.BlockSpec((S, C), lambda b: (b, 1)),
            _resident((n_ch, LANES, LANES)), _resident((1, C)),
            _resident((n_ch, LANES, LANES)), _resident((1, C)),
            _resident((1, C)),
        ],
        out_specs=pl.BlockSpec((S, C), lambda b: (b, 0)),
        out_shape=jax.ShapeDtypeStruct((T, C), F32),
        scratch_shapes=[pltpu.VMEM((S, C), F32),
                        pltpu.VMEM((S, C), F32)],
        compiler_params=pltpu.CompilerParams(
            dimension_semantics=("parallel",), vmem_limit_bytes=VMEM_LIMIT),
        name="lru",
    )(lru, lru, blockdiag(w_a), vec(b_a.reshape(-1)),
      blockdiag(w_x), vec(b_x.reshape(-1)), vec(lam))


def _cmp_kernel(seg_ref, pek_ref, w1k_ref, w2k_ref, pev_ref, w1v_ref, w2vt_ref, kc_ref, vct_ref):
    n_seg = seg_ref.shape[2]
    half = CMP_STRIDE * HEAD_DIM

    def hidden(idx, pe_ref, w1_ref):
        seg = seg_ref[idx, 0]
        first = _dot(seg, w1_ref[0:half, :])
        second = _dot(seg, w1_ref[half:2 * half, :])
        bias = _dot(pe_ref[...], w1_ref[...])[0:1, :]
        return _silu(first + pltpu.roll(second, n_seg - 1, axis=0) + bias).astype(BF16)

    for g in range(NSA_KV_HEADS):
        tok = _dot(hidden(g, pek_ref, w1k_ref), w2k_ref[...])
        kc_ref[0, g] = jnp.concatenate([tok, jnp.zeros_like(tok)], axis=1).astype(BF16)
        vct_ref[0, g] = _dot_nt(w2vt_ref[...], hidden(NSA_KV_HEADS + g, pev_ref, w1v_ref)
                                ).astype(BF16)


def _compress(cmp_in, pe_k, w1_k, w2_k, pe_v, w1_v, w2_v, B, S):
    G = NSA_KV_HEADS
    n_seg = S // CMP_STRIDE
    seg = cmp_in.reshape(2 * G, B, n_seg, CMP_STRIDE * HEAD_DIM)
    flat = CMP_BLOCK * HEAD_DIM
    pe8 = lambda pe: jnp.broadcast_to(pe.reshape(1, -1), (8, flat)).astype(BF16)
    return pl.pallas_call(
        _cmp_kernel,
        grid=(B,),
        in_specs=[
            pl.BlockSpec((2 * G, 1, n_seg, CMP_STRIDE * HEAD_DIM), lambda b: (0, b, 0, 0)),
            _resident((8, flat)), _resident((flat, CMP_HIDDEN)), _resident((CMP_HIDDEN, HEAD_DIM)),
            _resident((8, flat)), _resident((flat, CMP_HIDDEN)), _resident((HEAD_DIM, CMP_HIDDEN)),
        ],
        out_specs=[
            pl.BlockSpec((1, G, n_seg, LANES), lambda b: (b, 0, 0, 0)),
            pl.BlockSpec((1, G, HEAD_DIM, n_seg), lambda b: (b, 0, 0, 0)),
        ],
        out_shape=[
            jax.ShapeDtypeStruct((B, G, n_seg, LANES), BF16),
            jax.ShapeDtypeStruct((B, G, HEAD_DIM, n_seg), BF16),
        ],
        compiler_params=pltpu.CompilerParams(
            dimension_semantics=("parallel",), vmem_limit_bytes=VMEM_LIMIT),
        name="compress",
    )(seg, pe8(pe_k), w1_k.astype(BF16), w2_k.astype(BF16),
      pe8(pe_v), w1_v.astype(BF16), w2_v.T.astype(BF16))


def _attn_kernel(qt_ref, kc_ref, vct_ref, ks_ref, vst_ref, kw_ref, vwt_ref, gt_ref, ovl_ref,
                 bias_ref, *rest, tq, tiles):
    o_ref = rest[-1]

    def step(n):
        for _ in _interleave(*[
                _attn_step(n, g, qt_ref, kc_ref.at[g], vct_ref.at[g], ks_ref.at[g], vst_ref.at[g],
                           kw_ref.at[g], vwt_ref.at[g], gt_ref.at[g], ovl_ref, bias_ref, o_ref, tq)
                for g in range(NSA_KV_HEADS)]):
            pass

    for n in range(*tiles):
        pl.when(pl.program_id(1) == n - tiles[0])(functools.partial(step, n))


CHUNK_TILES = 1
SCORES_AHEAD = 1
ATTN_TILE_SPLITS = ((0, 4), (4, 6), (6, 8))


def _interleave(*stages):
    live = list(stages)
    while live:
        for st in list(live):
            try:
                next(st)
            except StopIteration:
                live.remove(st)
        yield


def _tile_bias(bias_ref, t0, k0, behind):
    if k0 == t0:
        return bias_ref[0]
    if behind is not None and k0 == t0 - behind:
        return bias_ref[1]
    return None


def _key_chunks(k_ref, vt_ref, k0, n_keys, tile, q, bias_fn):
    starts = list(range(k0, k0 + n_keys, tile))
    return [dict(k_ref=k_ref, vt_ref=vt_ref, tile=tile, q=q, bias_fn=bias_fn,
                 starts=starts[i:i + CHUNK_TILES]) for i in range(0, len(starts), CHUNK_TILES)]


def _scores_stage(ch):
    tiles, m = [], None
    for k0 in ch["starts"]:
        s = _dot(ch["k_ref"][k0:k0 + ch["tile"], :], ch["q"])
        bias = ch["bias_fn"](k0)
        if bias is not None:
            s = s + bias
        cm = jnp.max(s, axis=0, keepdims=True)
        m = cm if m is None else jnp.maximum(m, cm)
        tiles.append(s)
        yield
    ch["s"], ch["m"] = tiles, m


def _pv_stage(ch, heads, tq):
    acc = [None] * heads
    for k0, s in zip(ch["starts"], ch["s"]):
        p = jnp.exp2(s - ch["m"]).astype(BF16)
        vt = ch["vt_ref"][:, k0:k0 + ch["tile"]]
        for r in range(heads):
            d = _dot(vt, p[:, r * tq:(r + 1) * tq])
            acc[r] = d if acc[r] is None else acc[r] + d
        yield
    ch["acc"] = acc


def _merge_chunks(chunks, heads, tq):
    if len(chunks) == 1:
        return chunks[0]["acc"]
    m = functools.reduce(jnp.maximum, [ch["m"] for ch in chunks])
    acc = [None] * heads
    for ch in chunks:
        w = jnp.exp2(ch["m"] - m)
        for r in range(heads):
            term = ch["acc"][r] * w[:, r * tq:(r + 1) * tq]
            acc[r] = term if acc[r] is None else acc[r] + term
    return acc


def _rank_stage(out, score, jblk, n_live):
    rank = jnp.zeros(score.shape, jnp.int32)
    for kb in range(n_live):
        sk = score[kb:kb + 1, :]
        ahead = (sk > score) | ((sk == score) & (jblk > kb))
        rank = rank + ahead.astype(jnp.int32)
        if kb % 4 == 3:
            yield
    out["rank"] = rank


def _attn_step(n, g, qt_ref, kc_ref, vct_ref, ks_ref, vst_ref, kw_ref, vwt_ref, gt_ref, ovl_ref,
               bias_ref, o_ref, tq):
    R = NSA_GROUP
    t0 = n * tq
    n_cmp = kc_ref.shape[0]
    n_blk = ovl_ref.shape[0]
    q_all = jnp.concatenate([qt_ref[g * R + r] for r in range(R)], axis=1)
    q_pad = jnp.concatenate([q_all, jnp.zeros((LANES - HEAD_DIM, R * tq), BF16)], axis=0)
    head = lambda a, r: a[:, r * tq:(r + 1) * tq]

    kw0 = max(t0 - WINDOW, 0)
    win = _key_chunks(kw_ref, vwt_ref, kw0, t0 + tq - kw0, tq, q_pad,
                      lambda k0: _tile_bias(bias_ref, t0, k0, WINDOW))
    win_scores = _scores_stage(win[0])
    next(win_scores)
    yield

    n_vis = (t0 + tq - CMP_BLOCK) // CMP_STRIDE + 1
    n_vis = min(n_cmp, -(-n_vis // BF16_ROWS) * BF16_ROWS)
    tpos_c = t0 + (lax.broadcasted_iota(jnp.int32, (n_vis, R * tq), 1) & (tq - 1))
    cend = lax.broadcasted_iota(jnp.int32, (n_vis, R * tq), 0) * CMP_STRIDE + (CMP_BLOCK - 1)
    s = _dot(kc_ref[0:n_vis, :], q_pad) + jnp.where(cend <= tpos_c, 0.0, NEG)
    e = jnp.exp2(s - jnp.max(s, axis=0, keepdims=True))
    p_cmp = e * (1.0 / jnp.sum(e, axis=0, keepdims=True))
    if t0 < CMP_BLOCK - 1:
        p_cmp = jnp.where(tpos_c >= CMP_BLOCK - 1, p_cmp, 0.0)
    if n_vis < n_cmp:
        p_cmp = jnp.concatenate([p_cmp, jnp.zeros((n_cmp - n_vis, R * tq), F32)], axis=0)
    p_cmp_b = p_cmp.astype(BF16)
    vct = vct_ref[...]
    o_cmp = [_dot(vct, head(p_cmp_b, r)) for r in range(R)]
    psum = functools.reduce(jnp.add, [head(p_cmp, r) for r in range(R)])
    yield

    p_hi = psum.astype(BF16)
    p_lo = (psum - p_hi.astype(F32)).astype(BF16)
    ovl = ovl_ref[...]
    imp = _dot(ovl, p_hi) + _dot(ovl, p_lo)
    n_live = min(n_blk, (t0 + tq - 1) // SLC_BLOCK + 1)
    rows = -(-n_live // SUBLANES) * SUBLANES
    jblk = lax.broadcasted_iota(jnp.int32, (rows, tq), 0)
    tpos_b = t0 + lax.broadcasted_iota(jnp.int32, (rows, tq), 1)
    blk_valid = jblk * SLC_BLOCK <= tpos_b
    back = (tpos_b >> SLC_SHIFT) - jblk
    forced = (jblk == 0) | ((back >= 0) & (back < N_LOCAL_BLOCKS))
    score = jnp.where(blk_valid, jnp.where(forced, jnp.inf, imp[0:rows]), -jnp.inf)
    sel = {}
    yield from _interleave(_rank_stage(sel, score, jblk, n_live), win_scores)
    unsel = 1.0 - (blk_valid & (sel["rank"] < SLC_TOP_N)).astype(F32)
    dead = [jnp.ones((n_blk - rows, tq), F32)] if rows < n_blk else []
    sel_ext = jnp.concatenate(
        [unsel] + dead + [jnp.zeros((LANES - HEAD_DIM - n_blk, tq), F32)],
        axis=0).astype(BF16)
    q_sel = jnp.concatenate([q_all, jnp.concatenate([sel_ext] * R, axis=1)], axis=0)

    slc = _key_chunks(ks_ref, vst_ref, 0, t0 + tq, tq, q_sel,
                      lambda k0: _tile_bias(bias_ref, t0, k0, None))
    chunks = win + slc
    for ch in chunks[1:SCORES_AHEAD]:
        yield from _scores_stage(ch)
    for idx, ch in enumerate(chunks):
        stages = [_pv_stage(ch, R, tq)]
        if idx + SCORES_AHEAD < len(chunks):
            stages.append(_scores_stage(chunks[idx + SCORES_AHEAD]))
        yield from _interleave(*stages)
    acc_w = _merge_chunks(win, R, tq)
    acc_s = _merge_chunks(slc, R, tq)
    yield

    gt = gt_ref[...]
    outs = []
    for r in range(R):
        g_cmp = gt[3 * r:3 * r + 1]
        g_slc = gt[3 * r + 1:3 * r + 2] * (1.0 / acc_s[r][HEAD_DIM:HEAD_DIM + 1])
        g_win = gt[3 * r + 2:3 * r + 3] * (1.0 / acc_w[r][HEAD_DIM:HEAD_DIM + 1])
        outs.append(g_cmp * o_cmp[r] + g_slc * acc_s[r][:HEAD_DIM] + g_win * acc_w[r][:HEAD_DIM])
    width = R * HEAD_DIM
    o_ref[:, g * width:(g + 1) * width] = jnp.concatenate(outs, axis=0).T


def _overlap_t(n_cmp_pad, n_blk):
    cs = np.arange(n_cmp_pad) * CMP_STRIDE
    ce = cs + CMP_BLOCK - 1
    ss = np.arange(n_blk) * SLC_BLOCK
    se = ss + SLC_BLOCK - 1
    return ((cs[None, :] <= se[:, None]) & (ce[None, :] >= ss[:, None])).astype(np.float32)


def _attention(qt, k, vt, kc, vct, gates_t, B, S, *, tq=256):
    T = qt.shape[2]
    nq = S // tq
    n_cmp = S // CMP_STRIDE
    n_blk = S // SLC_BLOCK
    G = NSA_KV_HEADS
    ovl = jnp.asarray(_overlap_t(n_cmp, n_blk), BF16)
    assert WINDOW % tq == 0 and WINDOW + tq <= S
    assert HEAD_DIM + n_blk <= LANES
    assert tq & (tq - 1) == 0
    key_off = np.arange(tq)[:, None]
    q_off = np.tile(np.arange(tq), NSA_GROUP)[None, :]
    bias = jnp.asarray(np.stack([np.where(key_off <= q_off, 0.0, NEG),
                                 np.where(key_off > q_off, 0.0, NEG)]), F32)
    k_spec = lambda c: pl.BlockSpec((G, S, LANES), lambda b, i: (c, b, 0))
    vt_spec = lambda c: pl.BlockSpec((G, V_ROWS, S), lambda b, i: (c, 0, b))

    def call(tiles, prev):
        lo, hi = tiles
        tile = lambda b, i: b * nq + lo + i
        in_specs = [
            pl.BlockSpec((NSA_Q_HEADS, HEAD_DIM, tq), lambda b, i: (0, 0, tile(b, i))),
            pl.BlockSpec((None, G, n_cmp, LANES), lambda b, i: (b, 0, 0, 0)),
            pl.BlockSpec((None, G, HEAD_DIM, n_cmp), lambda b, i: (b, 0, 0, 0)),
            k_spec(0), vt_spec(0), k_spec(1), vt_spec(1),
            pl.BlockSpec((G, GATE_ROWS, tq), lambda b, i: (0, 0, tile(b, i))),
            _resident((n_blk, n_cmp)),
            _resident((2, tq, NSA_GROUP * tq)),
        ]
        args = [qt, kc, vct, k, vt, k, vt, gates_t, ovl, bias]
        aliases = {}
        if prev is not None:
            in_specs.append(pl.BlockSpec(memory_space=pl.ANY))
            aliases = {len(args): 0}
            args.append(prev)
        return pl.pallas_call(
            functools.partial(_attn_kernel, tq=tq, tiles=tiles),
            grid=(B, hi - lo),
            in_specs=in_specs,
            out_specs=pl.BlockSpec((tq, NSA_WIDTH), lambda b, i: (tile(b, i), 0)),
            out_shape=jax.ShapeDtypeStruct((T, NSA_WIDTH), F32),
            input_output_aliases=aliases,
            compiler_params=pltpu.CompilerParams(
                dimension_semantics=("parallel", "arbitrary"),
                vmem_limit_bytes=VMEM_LIMIT),
            name=f"nsa_attn_{lo}_{hi}",
        )(*args)

    assert [t for lo, hi in ATTN_TILE_SPLITS for t in range(lo, hi)] == list(range(nq))
    out = None
    for tiles in ATTN_TILE_SPLITS:
        out = call(tiles, out)
    return out


def _out_ffn_kernel(ylru_ref, ynsa_ref, h_ref, gl_ref, gn_ref, w_ref, g2_ref, b2_ref,
                    w_in_hbm, w_out_hbm, g3_ref, b3_ref, o_ref,
                    wg_scr, wu_scr, wo_scr, stage_in, stage_out, sem):
    @pl.when(pl.program_id(0) == 0)
    def _():
        _stage_ffn_weights(w_in_hbm, w_out_hbm, wg_scr, wu_scr, wo_scr, stage_in, stage_out, sem)

    yl = _rms_norm(ylru_ref[...], gl_ref[...]).astype(BF16)
    yn = _rms_norm(ynsa_ref[...], gn_ref[...]).astype(BF16)
    mix = _dot(yl, w_ref[0:LRU_WIDTH, :]) + _dot(yn, w_ref[LRU_WIDTH:, :])
    h2 = _layer_norm(ALPHA * h_ref[...] + mix, g2_ref[...], b2_ref[...])
    o_ref[...] = _ffn_half_step(h2, wg_scr, wu_scr, wo_scr, g3_ref, b3_ref)


def _out_proj_ffn(y_lru, y_nsa, h, gn_lru, gn_nsa, w_mix, g2, b2, w_in, w_out, g3, b3, *,
                  tm=512):
    T, D = h.shape
    row = lambda v: v.reshape(1, -1)
    return pl.pallas_call(
        _out_ffn_kernel,
        grid=(T // tm,),
        in_specs=[
            pl.BlockSpec((tm, LRU_WIDTH), lambda i: (i, 0)),
            pl.BlockSpec((tm, NSA_WIDTH), lambda i: (i, 0)),
            pl.BlockSpec((tm, D), lambda i: (i, 0)),
            _resident((1, LRU_WIDTH)), _resident((1, NSA_WIDTH)),
            _resident((LRU_WIDTH + NSA_WIDTH, D)),
            _resident((1, D)), _resident((1, D)),
            pl.BlockSpec(memory_space=pl.ANY), pl.BlockSpec(memory_space=pl.ANY),
            _resident((1, D)), _resident((1, D)),
        ],
        out_specs=pl.BlockSpec((tm, D), lambda i: (i, 0)),
        out_shape=jax.ShapeDtypeStruct((T, D), F32),
        scratch_shapes=_ffn_weight_scratch(D),
        compiler_params=pltpu.CompilerParams(
            dimension_semantics=("arbitrary",), vmem_limit_bytes=FFN_VMEM_LIMIT),
        name="out_proj_ffn",
    )(y_lru, y_nsa, h, row(gn_lru), row(gn_nsa), w_mix.astype(BF16), row(g2), row(b2),
      w_in, w_out, row(g3), row(b3))


def kernel(x, ffn1_w_in, ffn1_w_out, ln1_g, ln1_b, mix_w_in, conv_w, conv_b, lru_w_a, lru_b_a,
           lru_w_x, lru_b_x, lru_lam, cmp_pe_k, cmp_w1_k, cmp_w2_k, cmp_pe_v, cmp_w1_v,
           cmp_w2_v, gn_lru, gn_nsa, mix_w_out, ln2_g, ln2_b, ffn2_w_in, ffn2_w_out, ln3_g,
           ln3_b):
    B, S, D = x.shape
    h = x.reshape(B * S, D)
    for l in range(DEPTH):
        h = _ffn_ln(h, ffn1_w_in[l], ffn1_w_out[l], ln1_g[l], ln1_b[l])
        lru, qt, cmp_in, k, vt, gates_t = _proj(h, mix_w_in[l], conv_w[l], conv_b[l], S)
        y_lru = _lru(lru, lru_w_a[l], lru_b_a[l], lru_w_x[l], lru_b_x[l], lru_lam[l], B, S)
        kc, vct = _compress(cmp_in, cmp_pe_k[l], cmp_w1_k[l], cmp_w2_k[l],
                            cmp_pe_v[l], cmp_w1_v[l], cmp_w2_v[l], B, S)
        y_nsa = _attention(qt, k, vt, kc, vct, gates_t, B, S)
        h = _out_proj_ffn(y_lru, y_nsa, h, gn_lru[l], gn_nsa[l], mix_w_out[l], ln2_g[l], ln2_b[l],
                          ffn2_w_in[l], ffn2_w_out[l], ln3_g[l], ln3_b[l])
    return h.reshape(B, S, D)
```

```python
import functools

import numpy as np
import jax
import jax.numpy as jnp
from jax import lax
from jax.experimental import pallas as pl
from jax.experimental.pallas import tpu as pltpu

F32 = jnp.float32
BF16 = jnp.bfloat16

D_MODEL = 1024
LRU_WIDTH = 512
LRU_HEADS = 8
LRU_BLOCK = 64
CONV_WIDTH = 4
LRU_C = 8.0
NSA_Q_HEADS = 8
NSA_KV_HEADS = 2
NSA_GROUP = 4
HEAD_DIM = 64
NSA_WIDTH = 512
KV_WIDTH = 128
CMP_BLOCK = 32
CMP_STRIDE = 16
CMP_HIDDEN = 256
SLC_BLOCK = 64
SLC_SHIFT = 6
SLC_TOP_N = 16
N_LOCAL_BLOCKS = 2
WINDOW = 512
ROPE_THETA = 10000.0
D_FF = 2816
DEPTH = 1
ALPHA = (2.0 * DEPTH) ** 0.25
LN_EPS = 1e-5
RMS_EPS = 1e-6
NEG = -1e30
LOG2E = 1.4426950408889634
F32_TINY = 1.1754943508222875e-38

LANES = 128
SUBLANES = 8
BF16_ROWS = 16
VMEM_LIMIT = 48 * 1024 * 1024
N_GATE = 3 * NSA_GROUP
GATE_ROWS = 16
V_ROWS = HEAD_DIM + BF16_ROWS


def _dot(a, b):
    return jnp.dot(a, b, preferred_element_type=F32)


def _dot_nt(a, b):
    return lax.dot_general(a, b, (((1,), (1,)), ((), ())), preferred_element_type=F32)


def _layer_norm(y, g, b):
    mu = jnp.mean(y, axis=-1, keepdims=True)
    d = y - mu
    var = jnp.mean(d * d, axis=-1, keepdims=True)
    return d * lax.rsqrt(var + LN_EPS) * g + b


def _rms_norm(y, g):
    return y * lax.rsqrt(jnp.mean(y * y, axis=-1, keepdims=True) + RMS_EPS) * g


def _silu(x):
    return x * jax.nn.sigmoid(x)


def _resident(shape):
    return pl.BlockSpec(shape, lambda *_: (0,) * len(shape), pipeline_mode=pl.Buffered(1))


def _ffn_half_step(x, wg_ref, wu_ref, wo_ref, g_ref, b_ref):
    xb = x.astype(BF16)
    gate = _dot(xb, wg_ref[...])
    up = _dot(xb, wu_ref[...])
    act = (_silu(gate) * up).astype(BF16)
    y = ALPHA * x + 0.5 * _dot(act, wo_ref[...])
    return _layer_norm(y, g_ref[...], b_ref[...])


W_IN_ROWS = 64
W_OUT_ROWS = 352
FFN_VMEM_LIMIT = 56 * 1024 * 1024


def _stream_rows(w_hbm, rows, slots, sems, consume):
    def copy(c):
        return pltpu.make_async_copy(w_hbm.at[pl.ds(c * rows, rows), :], slots[c % 2], sems[c % 2])

    n = w_hbm.shape[0] // rows
    copy(0).start()
    for c in range(n):
        if c + 1 < n:
            copy(c + 1).start()
        copy(c).wait()
        consume(c * rows, slots[c % 2])


def _stage_ffn_weights(w_in_hbm, w_out_hbm, wg_scr, wu_scr, wo_scr, in0, in1, out0, out1,
                       sem_in0, sem_in1, sem_out0, sem_out1):
    def put_in(r0, buf):
        wg_scr[r0:r0 + W_IN_ROWS, :] = buf[:, :D_FF].astype(BF16)
        wu_scr[r0:r0 + W_IN_ROWS, :] = buf[:, D_FF:].astype(BF16)

    def put_out(r0, buf):
        wo_scr[r0:r0 + W_OUT_ROWS, :] = buf[...].astype(BF16)

    _stream_rows(w_in_hbm, W_IN_ROWS, (in0, in1), (sem_in0, sem_in1), put_in)
    _stream_rows(w_out_hbm, W_OUT_ROWS, (out0, out1), (sem_out0, sem_out1), put_out)


def _ffn_weight_scratch(D):
    assert D % W_IN_ROWS == 0 and D_FF % W_OUT_ROWS == 0
    return ([pltpu.VMEM((D, D_FF), BF16), pltpu.VMEM((D, D_FF), BF16), pltpu.VMEM((D_FF, D), BF16)]
            + [pltpu.VMEM((W_IN_ROWS, 2 * D_FF), F32)] * 2 + [pltpu.VMEM((W_OUT_ROWS, D), F32)] * 2
            + [pltpu.SemaphoreType.DMA(())] * 4)


def _ffn_ln_kernel(x_ref, w_in_hbm, w_out_hbm, g_ref, b_ref, o_ref, wg_scr, wu_scr, wo_scr,
                   *staging):
    @pl.when(pl.program_id(0) == 0)
    def _():
        _stage_ffn_weights(w_in_hbm, w_out_hbm, wg_scr, wu_scr, wo_scr, *staging)

    o_ref[...] = _ffn_half_step(x_ref[...], wg_scr, wu_scr, wo_scr, g_ref, b_ref)


def _ffn_ln(x, w_in, w_out, g, b, *, tm=1024):
    T, D = x.shape
    return pl.pallas_call(
        _ffn_ln_kernel,
        grid=(T // tm,),
        in_specs=[
            pl.BlockSpec((tm, D), lambda i: (i, 0)),
            pl.BlockSpec(memory_space=pl.ANY), pl.BlockSpec(memory_space=pl.ANY),
            _resident((1, D)), _resident((1, D)),
        ],
        out_specs=pl.BlockSpec((tm, D), lambda i: (i, 0)),
        out_shape=jax.ShapeDtypeStruct((T, D), F32),
        scratch_shapes=_ffn_weight_scratch(D),
        compiler_params=pltpu.CompilerParams(
            dimension_semantics=("arbitrary",), vmem_limit_bytes=FFN_VMEM_LIMIT),
        name="ffn_ln",
    )(x, w_in, w_out, g.reshape(1, D), b.reshape(1, D))


Q_COL0 = 2 * LRU_WIDTH
KCMP_COL0 = Q_COL0 + NSA_WIDTH
VCMP_COL0 = KCMP_COL0 + KV_WIDTH
KSLC_COL0 = VCMP_COL0 + KV_WIDTH
VSLC_COL0 = KSLC_COL0 + KV_WIDTH
KWIN_COL0 = VSLC_COL0 + KV_WIDTH
VWIN_COL0 = KWIN_COL0 + KV_WIDTH
GATE_COL0 = VWIN_COL0 + KV_WIDTH
VT_ROW0 = NSA_WIDTH
GT_ROW0 = VT_ROW0 + 2 * KV_WIDTH
WT_ROWS = GT_ROW0 + NSA_KV_HEADS * GATE_ROWS


def _rope_chunk(xc, cos, sin_signed):
    lane = lax.broadcasted_iota(jnp.int32, xc.shape, 1)
    first = (lane & (HEAD_DIM - 1)) < (HEAD_DIM // 2)
    partner = jnp.where(first, pltpu.roll(xc, LANES - HEAD_DIM // 2, axis=1),
                        pltpu.roll(xc, HEAD_DIM // 2, axis=1))
    return xc * cos + partner * sin_signed


def _proj_kernel(h_ref, wn_ref, wt_ref, cos_ref, sin_ref, cost_ref, sint_ref, cw_ref, cb_ref,
                 lru_ref, qt_ref, cmp_ref, k_ref, vt_ref, gt_ref, seg_scr, conv_scr, *, seq_tiles):
    tm = h_ref.shape[0]
    half = HEAD_DIM // 2

    @pl.when(pl.program_id(0) % seq_tiles == 0)
    def _():
        conv_scr[0:SUBLANES, :] = jnp.zeros((SUBLANES, LRU_WIDTH), F32)

    hb = h_ref[...].astype(BF16)
    p = _dot(hb, wn_ref[...])
    pt = _dot_nt(wt_ref[...], hb)

    x = p[:, :LRU_WIDTH]
    conv_scr[SUBLANES:, :] = x
    xc = cb_ref[...]
    for d in range(CONV_WIDTH):
        xs = conv_scr[SUBLANES - d:SUBLANES - d + tm, :]
        xc = xc + xs * cw_ref[CONV_WIDTH - 1 - d:CONV_WIDTH - d, :]
    conv_scr[0:SUBLANES, :] = x[tm - SUBLANES:, :]
    lru_ref[:, :LRU_WIDTH] = xc
    lru_ref[:, LRU_WIDTH:] = jax.nn.gelu(p[:, LRU_WIDTH:Q_COL0])

    cos_t = cost_ref[...]
    sin_t = sint_ref[...]
    scale = HEAD_DIM ** -0.5 * LOG2E
    for hd in range(NSA_Q_HEADS):
        x1 = pt[hd * HEAD_DIM:hd * HEAD_DIM + half]
        x2 = pt[hd * HEAD_DIM + half:(hd + 1) * HEAD_DIM]
        qt_ref[hd] = (jnp.concatenate([x1 * cos_t - x2 * sin_t, x2 * cos_t + x1 * sin_t], axis=0)
                      * scale).astype(BF16)
    ones = jnp.ones((V_ROWS - HEAD_DIM, tm), F32)
    for c in range(2 * NSA_KV_HEADS):
        v = pt[VT_ROW0 + c * HEAD_DIM:VT_ROW0 + (c + 1) * HEAD_DIM]
        vt_ref[c] = jnp.concatenate([v, ones], axis=0).astype(BF16)
    for g in range(NSA_KV_HEADS):
        gt_ref[g] = jax.nn.sigmoid(pt[GT_ROW0 + g * GATE_ROWS:GT_ROW0 + (g + 1) * GATE_ROWS])

    cos = cos_ref[...]
    sin = sin_ref[...]
    low_seg = lax.broadcasted_iota(jnp.int32, (tm // CMP_STRIDE, LANES), 1) < HEAD_DIM
    for c, xc in enumerate((_rope_chunk(p[:, Q_COL0:Q_COL0 + LANES], cos, sin),
                            p[:, Q_COL0 + LANES:Q_COL0 + 2 * LANES])):
        seg_scr[...] = xc
        tok = [seg_scr[pl.ds(j, tm // CMP_STRIDE, stride=CMP_STRIDE), :]
               for j in range(CMP_STRIDE)]
        g0, g1 = [], []
        for j in range(0, CMP_STRIDE, 2):
            g0.append(jnp.where(low_seg, tok[j], pltpu.roll(tok[j + 1], HEAD_DIM, axis=1)))
            g1.append(jnp.where(low_seg, pltpu.roll(tok[j], HEAD_DIM, axis=1), tok[j + 1]))
        cmp_ref[NSA_KV_HEADS * c] = jnp.concatenate(g0, axis=1).astype(BF16)
        cmp_ref[NSA_KV_HEADS * c + 1] = jnp.concatenate(g1, axis=1).astype(BF16)
    lane = lax.broadcasted_iota(jnp.int32, (tm, LANES), 1)
    low = lane < HEAD_DIM
    pos = (pl.program_id(0) % seq_tiles) * tm + lax.broadcasted_iota(jnp.int32, (tm, LANES), 0)
    ext_blk = jnp.where(lane - HEAD_DIM == (pos >> SLC_SHIFT), NEG, 0.0)
    ext_zero = jnp.zeros((tm, LANES), F32)
    for c, ext in enumerate((ext_blk, ext_zero)):
        xc = _rope_chunk(p[:, Q_COL0 + (2 + c) * LANES:Q_COL0 + (3 + c) * LANES], cos, sin)
        k_ref[2 * c] = jnp.where(low, xc, ext).astype(BF16)
        k_ref[2 * c + 1] = jnp.where(low, pltpu.roll(xc, HEAD_DIM, axis=1), ext).astype(BF16)


def _proj(h, w_in, conv_w, conv_b, S, *, tm=1024):
    T, D = h.shape
    G = NSA_KV_HEADS
    col = lambda c0, n: w_in[:, c0:c0 + n]
    w_nat = jnp.concatenate(
        [col(0, Q_COL0), col(KCMP_COL0, KV_WIDTH), col(VCMP_COL0, KV_WIDTH),
         col(KSLC_COL0, KV_WIDTH), col(KWIN_COL0, KV_WIDTH)], axis=1).astype(BF16)
    gate_rows = [jnp.pad(col(GATE_COL0 + g * N_GATE, N_GATE).T, ((0, GATE_ROWS - N_GATE), (0, 0)))
                 for g in range(G)]
    w_t = jnp.concatenate(
        [col(Q_COL0, NSA_WIDTH).T, col(VSLC_COL0, KV_WIDTH).T, col(VWIN_COL0, KV_WIDTH).T]
        + gate_rows, axis=0).astype(BF16)
    n_nat = w_nat.shape[1]

    half = HEAD_DIM // 2
    inv = ROPE_THETA ** (-jnp.arange(half, dtype=F32) / half)
    ang = jnp.arange(S, dtype=F32)[:, None] * inv[None, :]
    cos = jnp.cos(ang)
    sin = jnp.sin(ang)
    cos_n = jnp.concatenate([cos, cos, cos, cos], axis=1)
    sin_n = jnp.concatenate([-sin, sin, -sin, sin], axis=1)

    nS = S // tm
    seg_w = CMP_STRIDE * HEAD_DIM
    assert G * HEAD_DIM == LANES
    return pl.pallas_call(
        functools.partial(_proj_kernel, seq_tiles=nS),
        grid=(T // tm,),
        in_specs=[
            pl.BlockSpec((tm, D), lambda i: (i, 0)),
            _resident((D, n_nat)), _resident((WT_ROWS, D)),
            pl.BlockSpec((tm, LANES), lambda i: (i % nS, 0)),
            pl.BlockSpec((tm, LANES), lambda i: (i % nS, 0)),
            pl.BlockSpec((half, tm), lambda i: (0, i % nS)),
            pl.BlockSpec((half, tm), lambda i: (0, i % nS)),
            _resident((CONV_WIDTH, LRU_WIDTH)), _resident((1, LRU_WIDTH)),
        ],
        out_specs=[
            pl.BlockSpec((tm, Q_COL0), lambda i: (i, 0)),
            pl.BlockSpec((NSA_Q_HEADS, HEAD_DIM, tm), lambda i: (0, 0, i)),
            pl.BlockSpec((2 * G, tm // CMP_STRIDE, seg_w), lambda i: (0, i, 0)),
            pl.BlockSpec((2 * G, tm, LANES), lambda i: (0, i, 0)),
            pl.BlockSpec((2 * G, V_ROWS, tm), lambda i: (0, 0, i)),
            pl.BlockSpec((G, GATE_ROWS, tm), lambda i: (0, 0, i)),
        ],
        out_shape=[
            jax.ShapeDtypeStruct((T, Q_COL0), F32),
            jax.ShapeDtypeStruct((NSA_Q_HEADS, HEAD_DIM, T), BF16),
            jax.ShapeDtypeStruct((2 * G, T // CMP_STRIDE, seg_w), BF16),
            jax.ShapeDtypeStruct((2 * G, T, LANES), BF16),
            jax.ShapeDtypeStruct((2 * G, V_ROWS, T), BF16),
            jax.ShapeDtypeStruct((G, GATE_ROWS, T), F32),
        ],
        scratch_shapes=[pltpu.VMEM((tm, LANES), F32),
                        pltpu.VMEM((SUBLANES + tm, LRU_WIDTH), F32)],
        compiler_params=pltpu.CompilerParams(
            dimension_semantics=("arbitrary",), vmem_limit_bytes=VMEM_LIMIT),
        name="proj",
    )(h, w_nat, w_t, cos_n, sin_n, cos.T, sin.T, conv_w, conv_b.reshape(1, LRU_WIDTH))


SCAN_UNROLL = 8
LRU_ROWS = 512


def _lru_kernel(x_ref, gate_ref, wa_ref, ba_ref, wx_ref, bx_ref, lam_ref, y_ref, a_scr, b_scr):
    S, C = y_ref.shape
    neg_lam = -lam_ref[...]
    softplus = jnp.maximum(neg_lam, 0.0) + jnp.log1p(jnp.exp(-jnp.abs(neg_lam)))

    for r0 in range(0, S, LRU_ROWS):
        xc = x_ref[r0:r0 + LRU_ROWS, :]
        xb = xc.astype(BF16)
        gate_pre = lambda w_ref: jnp.concatenate(
            [_dot(xb[:, c * LANES:(c + 1) * LANES], w_ref[c]) for c in range(C // LANES)], axis=1)
        r = 0.5 * jnp.tanh(0.5 * (gate_pre(wa_ref) + ba_ref[...])) + 0.5
        ig = 0.5 * jnp.tanh(0.5 * (gate_pre(wx_ref) + bx_ref[...])) + 0.5
        log_a = (-LRU_C) * r * softplus
        a = jnp.exp(log_a)
        z = jnp.tanh(-log_a) * (a * a + 1.0)
        mult = z * lax.rsqrt(jnp.maximum(z, F32_TINY))
        if r0 == 0:
            row = lax.broadcasted_iota(jnp.int32, mult.shape, 0)
            mult = jnp.where(row == 0, 1.0, mult)
        a_scr[r0:r0 + LRU_ROWS, :] = a
        b_scr[r0:r0 + LRU_ROWS, :] = mult * (ig * xc)

    sub = lax.broadcasted_iota(jnp.int32, (SUBLANES, C), 0)

    def scan_tile(g, h_prev):
        rows = pl.ds(pl.multiple_of(g * SUBLANES, SUBLANES), SUBLANES)
        at = a_scr[rows, :]
        bt = b_scr[rows, :]
        d = 1
        while d < SUBLANES:
            keep = sub >= d
            a_sh = jnp.where(keep, pltpu.roll(at, d, axis=0), 1.0)
            b_sh = jnp.where(keep, pltpu.roll(bt, d, axis=0), 0.0)
            bt = at * b_sh + bt
            at = at * a_sh
            d *= 2
        h = bt + at * h_prev
        b_scr[rows, :] = h
        return jnp.broadcast_to(h[SUBLANES - 1:SUBLANES, :], (SUBLANES, C))

    def scan_body(i, h_prev):
        for u in range(SCAN_UNROLL):
            h_prev = scan_tile(i * SCAN_UNROLL + u, h_prev)
        return h_prev

    lax.fori_loop(0, S // (SUBLANES * SCAN_UNROLL), scan_body, jnp.zeros((SUBLANES, C), F32))

    for r0 in range(0, S, LRU_ROWS):
        rows = slice(r0, r0 + LRU_ROWS)
        y_ref[rows, :] = b_scr[rows, :] * gate_ref[rows, :]


def _lru(lru, w_a, b_a, w_x, b_x, lam, B, S):
    T = lru.shape[0]
    C = LRU_WIDTH
    n_ch = C // LANES
    assert S % (SUBLANES * SCAN_UNROLL) == 0 and S % LRU_ROWS == 0

    def blockdiag(w):
        w = w.reshape(n_ch, 2, LRU_BLOCK, LRU_BLOCK)
        z = jnp.zeros((n_ch, LRU_BLOCK, LRU_BLOCK), w.dtype)
        top = jnp.concatenate([w[:, 0], z], axis=2)
        bot = jnp.concatenate([z, w[:, 1]], axis=2)
        return jnp.concatenate([top, bot], axis=1).astype(BF16)

    vec = lambda v: v.reshape(1, C)
    return pl.pallas_call(
        _lru_kernel,
        grid=(B,),
        in_specs=[
            pl.BlockSpec((S, C), lambda b: (b, 0)),
            pl.BlockSpec((S, C), lambda b: (b, 1)),
            _resident((n_ch, LANES, LANES)), _resident((1, C)),
            _resident((n_ch, LANES, LANES)), _resident((1, C)),
            _resident((1, C)),
        ],
        out_specs=pl.BlockSpec((S, C), lambda b: (b, 0)),
        out_shape=jax.ShapeDtypeStruct((T, C), F32),
        scratch_shapes=[pltpu.VMEM((S, C), F32),
                        pltpu.VMEM((S, C), F32)],
        compiler_params=pltpu.CompilerParams(
            dimension_semantics=("parallel",), vmem_limit_bytes=VMEM_LIMIT),
        name="lru",
    )(lru, lru, blockdiag(w_a), vec(b_a.reshape(-1)),
      blockdiag(w_x), vec(b_x.reshape(-1)), vec(lam))


def _cmp_kernel(seg_ref, pek_ref, w1k_ref, w2k_ref, pev_ref, w1v_ref, w2vt_ref, kc_ref, vct_ref):
    n_seg = seg_ref.shape[2]
    half = CMP_STRIDE * HEAD_DIM

    def hidden(idx, pe_ref, w1_ref):
        seg = seg_ref[idx, 0]
        first = _dot(seg, w1_ref[0:half, :])
        second = _dot(seg, w1_ref[half:2 * half, :])
        bias = _dot(pe_ref[...], w1_ref[...])[0:1, :]
        return _silu(first + pltpu.roll(second, n_seg - 1, axis=0) + bias).astype(BF16)

    for g in range(NSA_KV_HEADS):
        tok = _dot(hidden(g, pek_ref, w1k_ref), w2k_ref[...])
        kc_ref[0, g] = jnp.concatenate([tok, jnp.zeros_like(tok)], axis=1).astype(BF16)
        vct_ref[0, g] = _dot_nt(w2vt_ref[...], hidden(NSA_KV_HEADS + g, pev_ref, w1v_ref)
                                ).astype(BF16)


def _compress(cmp_in, pe_k, w1_k, w2_k, pe_v, w1_v, w2_v, B, S):
    G = NSA_KV_HEADS
    n_seg = S // CMP_STRIDE
    seg = cmp_in.reshape(2 * G, B, n_seg, CMP_STRIDE * HEAD_DIM)
    flat = CMP_BLOCK * HEAD_DIM
    pe8 = lambda pe: jnp.broadcast_to(pe.reshape(1, -1), (8, flat)).astype(BF16)
    return pl.pallas_call(
        _cmp_kernel,
        grid=(B,),
        in_specs=[
            pl.BlockSpec((2 * G, 1, n_seg, CMP_STRIDE * HEAD_DIM), lambda b: (0, b, 0, 0)),
            _resident((8, flat)), _resident((flat, CMP_HIDDEN)), _resident((CMP_HIDDEN, HEAD_DIM)),
            _resident((8, flat)), _resident((flat, CMP_HIDDEN)), _resident((HEAD_DIM, CMP_HIDDEN)),
        ],
        out_specs=[
            pl.BlockSpec((1, G, n_seg, LANES), lambda b: (b, 0, 0, 0)),
            pl.BlockSpec((1, G, HEAD_DIM, n_seg), lambda b: (b, 0, 0, 0)),
        ],
        out_shape=[
            jax.ShapeDtypeStruct((B, G, n_seg, LANES), BF16),
            jax.ShapeDtypeStruct((B, G, HEAD_DIM, n_seg), BF16),
        ],
        compiler_params=pltpu.CompilerParams(
            dimension_semantics=("parallel",), vmem_limit_bytes=VMEM_LIMIT),
        name="compress",
    )(seg, pe8(pe_k), w1_k.astype(BF16), w2_k.astype(BF16),
      pe8(pe_v), w1_v.astype(BF16), w2_v.T.astype(BF16))


def _attn_kernel(qt_ref, kc_ref, vct_ref, ks_ref, vst_ref, kw_ref, vwt_ref, gt_ref, ovl_ref,
                 bias_ref, *rest, tq, tiles):
    o_ref = rest[-1]

    def step(n):
        for _ in _interleave(*[
                _attn_step(n, g, qt_ref, kc_ref.at[g], vct_ref.at[g], ks_ref.at[g], vst_ref.at[g],
                           kw_ref.at[g], vwt_ref.at[g], gt_ref.at[g], ovl_ref, bias_ref, o_ref, tq)
                for g in range(NSA_KV_HEADS)]):
            pass

    for n in range(*tiles):
        pl.when(pl.program_id(1) == n - tiles[0])(functools.partial(step, n))


CHUNK_TILES = 1
SCORES_AHEAD = 1
ATTN_TILE_SPLITS = ((0, 4), (4, 6), (6, 8))


def _interleave(*stages):
    live = list(stages)
    while live:
        for st in list(live):
            try:
                next(st)
            except StopIteration:
                live.remove(st)
        yield


def _tile_bias(bias_ref, t0, k0, behind):
    if k0 == t0:
        return bias_ref[0]
    if behind is not None and k0 == t0 - behind:
        return bias_ref[1]
    return None


def _key_chunks(k_ref, vt_ref, k0, n_keys, tile, q, bias_fn):
    starts = list(range(k0, k0 + n_keys, tile))
    return [dict(k_ref=k_ref, vt_ref=vt_ref, tile=tile, q=q, bias_fn=bias_fn,
                 starts=starts[i:i + CHUNK_TILES]) for i in range(0, len(starts), CHUNK_TILES)]


def _scores_stage(ch):
    tiles, m = [], None
    for k0 in ch["starts"]:
        s = _dot(ch["k_ref"][k0:k0 + ch["tile"], :], ch["q"])
        bias = ch["bias_fn"](k0)
        if bias is not None:
            s = s + bias
        cm = jnp.max(s, axis=0, keepdims=True)
        m = cm if m is None else jnp.maximum(m, cm)
        tiles.append(s)
        yield
    ch["s"], ch["m"] = tiles, m


def _pv_stage(ch, heads, tq):
    acc = [None] * heads
    for k0, s in zip(ch["starts"], ch["s"]):
        p = jnp.exp2(s - ch["m"]).astype(BF16)
        vt = ch["vt_ref"][:, k0:k0 + ch["tile"]]
        for r in range(heads):
            d = _dot(vt, p[:, r * tq:(r + 1) * tq])
            acc[r] = d if acc[r] is None else acc[r] + d
        yield
    ch["acc"] = acc


def _merge_chunks(chunks, heads, tq):
    if len(chunks) == 1:
        return chunks[0]["acc"]
    m = functools.reduce(jnp.maximum, [ch["m"] for ch in chunks])
    acc = [None] * heads
    for ch in chunks:
        w = jnp.exp2(ch["m"] - m)
        for r in range(heads):
            term = ch["acc"][r] * w[:, r * tq:(r + 1) * tq]
            acc[r] = term if acc[r] is None else acc[r] + term
    return acc


def _rank_stage(out, score, jblk, n_live):
    rank = jnp.zeros(score.shape, jnp.int32)
    for kb in range(n_live):
        sk = score[kb:kb + 1, :]
        ahead = (sk > score) | ((sk == score) & (jblk > kb))
        rank = rank + ahead.astype(jnp.int32)
        if kb % 4 == 3:
            yield
    out["rank"] = rank


def _attn_step(n, g, qt_ref, kc_ref, vct_ref, ks_ref, vst_ref, kw_ref, vwt_ref, gt_ref, ovl_ref,
               bias_ref, o_ref, tq):
    R = NSA_GROUP
    t0 = n * tq
    n_cmp = kc_ref.shape[0]
    n_blk = ovl_ref.shape[0]
    q_all = jnp.concatenate([qt_ref[g * R + r] for r in range(R)], axis=1)
    q_pad = jnp.concatenate([q_all, jnp.zeros((LANES - HEAD_DIM, R * tq), BF16)], axis=0)
    head = lambda a, r: a[:, r * tq:(r + 1) * tq]

    kw0 = max(t0 - WINDOW, 0)
    win = _key_chunks(kw_ref, vwt_ref, kw0, t0 + tq - kw0, tq, q_pad,
                      lambda k0: _tile_bias(bias_ref, t0, k0, WINDOW))
    win_scores = _scores_stage(win[0])
    next(win_scores)
    yield

    n_vis = (t0 + tq - CMP_BLOCK) // CMP_STRIDE + 1
    n_vis = min(n_cmp, -(-n_vis // BF16_ROWS) * BF16_ROWS)
    tpos_c = t0 + (lax.broadcasted_iota(jnp.int32, (n_vis, R * tq), 1) & (tq - 1))
    cend = lax.broadcasted_iota(jnp.int32, (n_vis, R * tq), 0) * CMP_STRIDE + (CMP_BLOCK - 1)
    s = _dot(kc_ref[0:n_vis, :], q_pad) + jnp.where(cend <= tpos_c, 0.0, NEG)
    e = jnp.exp2(s - jnp.max(s, axis=0, keepdims=True))
    p_cmp = e * (1.0 / jnp.sum(e, axis=0, keepdims=True))
    if t0 < CMP_BLOCK - 1:
        p_cmp = jnp.where(tpos_c >= CMP_BLOCK - 1, p_cmp, 0.0)
    if n_vis < n_cmp:
        p_cmp = jnp.concatenate([p_cmp, jnp.zeros((n_cmp - n_vis, R * tq), F32)], axis=0)
    p_cmp_b = p_cmp.astype(BF16)
    vct = vct_ref[...]
    o_cmp = [_dot(vct, head(p_cmp_b, r)) for r in range(R)]
    psum = functools.reduce(jnp.add, [head(p_cmp, r) for r in range(R)])
    yield

    p_hi = psum.astype(BF16)
    p_lo = (psum - p_hi.astype(F32)).astype(BF16)
    ovl = ovl_ref[...]
    imp = _dot(ovl, p_hi) + _dot(ovl, p_lo)
    n_live = min(n_blk, (t0 + tq - 1) // SLC_BLOCK + 1)
    rows = -(-n_live // SUBLANES) * SUBLANES
    jblk = lax.broadcasted_iota(jnp.int32, (rows, tq), 0)
    tpos_b = t0 + lax.broadcasted_iota(jnp.int32, (rows, tq), 1)
    blk_valid = jblk * SLC_BLOCK <= tpos_b
    back = (tpos_b >> SLC_SHIFT) - jblk
    forced = (jblk == 0) | ((back >= 0) & (back < N_LOCAL_BLOCKS))
    score = jnp.where(blk_valid, jnp.where(forced, jnp.inf, imp[0:rows]), -jnp.inf)
    sel = {}
    yield from _interleave(_rank_stage(sel, score, jblk, n_live), win_scores)
    unsel = 1.0 - (blk_valid & (sel["rank"] < SLC_TOP_N)).astype(F32)
    dead = [jnp.ones((n_blk - rows, tq), F32)] if rows < n_blk else []
    sel_ext = jnp.concatenate(
        [unsel] + dead + [jnp.zeros((LANES - HEAD_DIM - n_blk, tq), F32)],
        axis=0).astype(BF16)
    q_sel = jnp.concatenate([q_all, jnp.concatenate([sel_ext] * R, axis=1)], axis=0)

    slc = _key_chunks(ks_ref, vst_ref, 0, t0 + tq, tq, q_sel,
                      lambda k0: _tile_bias(bias_ref, t0, k0, None))
    chunks = win + slc
    for ch in chunks[1:SCORES_AHEAD]:
        yield from _scores_stage(ch)
    for idx, ch in enumerate(chunks):
        stages = [_pv_stage(ch, R, tq)]
        if idx + SCORES_AHEAD < len(chunks):
            stages.append(_scores_stage(chunks[idx + SCORES_AHEAD]))
        yield from _interleave(*stages)
    acc_w = _merge_chunks(win, R, tq)
    acc_s = _merge_chunks(slc, R, tq)
    yield

    gt = gt_ref[...]
    outs = []
    for r in range(R):
        g_cmp = gt[3 * r:3 * r + 1]
        g_slc = gt[3 * r + 1:3 * r + 2] * (1.0 / acc_s[r][HEAD_DIM:HEAD_DIM + 1])
        g_win = gt[3 * r + 2:3 * r + 3] * (1.0 / acc_w[r][HEAD_DIM:HEAD_DIM + 1])
        outs.append(g_cmp * o_cmp[r] + g_slc * acc_s[r][:HEAD_DIM] + g_win * acc_w[r][:HEAD_DIM])
    width = R * HEAD_DIM
    o_ref[:, g * width:(g + 1) * width] = jnp.concatenate(outs, axis=0).T


def _overlap_t(n_cmp_pad, n_blk):
    cs = np.arange(n_cmp_pad) * CMP_STRIDE
    ce = cs + CMP_BLOCK - 1
    ss = np.arange(n_blk) * SLC_BLOCK
    se = ss + SLC_BLOCK - 1
    return ((cs[None, :] <= se[:, None]) & (ce[None, :] >= ss[:, None])).astype(np.float32)


def _attention(qt, k, vt, kc, vct, gates_t, B, S, *, tq=256):
    T = qt.shape[2]
    nq = S // tq
    n_cmp = S // CMP_STRIDE
    n_blk = S // SLC_BLOCK
    G = NSA_KV_HEADS
    ovl = jnp.asarray(_overlap_t(n_cmp, n_blk), BF16)
    assert WINDOW % tq == 0 and WINDOW + tq <= S
    assert HEAD_DIM + n_blk <= LANES
    assert tq & (tq - 1) == 0
    key_off = np.arange(tq)[:, None]
    q_off = np.tile(np.arange(tq), NSA_GROUP)[None, :]
    bias = jnp.asarray(np.stack([np.where(key_off <= q_off, 0.0, NEG),
                                 np.where(key_off > q_off, 0.0, NEG)]), F32)
    k_spec = lambda c: pl.BlockSpec((G, S, LANES), lambda b, i: (c, b, 0))
    vt_spec = lambda c: pl.BlockSpec((G, V_ROWS, S), lambda b, i: (c, 0, b))

    def call(tiles, prev):
        lo, hi = tiles
        tile = lambda b, i: b * nq + lo + i
        in_specs = [
            pl.BlockSpec((NSA_Q_HEADS, HEAD_DIM, tq), lambda b, i: (0, 0, tile(b, i))),
            pl.BlockSpec((None, G, n_cmp, LANES), lambda b, i: (b, 0, 0, 0)),
            pl.BlockSpec((None, G, HEAD_DIM, n_cmp), lambda b, i: (b, 0, 0, 0)),
            k_spec(0), vt_spec(0), k_spec(1), vt_spec(1),
            pl.BlockSpec((G, GATE_ROWS, tq), lambda b, i: (0, 0, tile(b, i))),
            _resident((n_blk, n_cmp)),
            _resident((2, tq, NSA_GROUP * tq)),
        ]
        args = [qt, kc, vct, k, vt, k, vt, gates_t, ovl, bias]
        aliases = {}
        if prev is not None:
            in_specs.append(pl.BlockSpec(memory_space=pl.ANY))
            aliases = {len(args): 0}
            args.append(prev)
        return pl.pallas_call(
            functools.partial(_attn_kernel, tq=tq, tiles=tiles),
            grid=(B, hi - lo),
            in_specs=in_specs,
            out_specs=pl.BlockSpec((tq, NSA_WIDTH), lambda b, i: (tile(b, i), 0)),
            out_shape=jax.ShapeDtypeStruct((T, NSA_WIDTH), F32),
            input_output_aliases=aliases,
            compiler_params=pltpu.CompilerParams(
                dimension_semantics=("parallel", "arbitrary"),
                vmem_limit_bytes=VMEM_LIMIT),
            name=f"nsa_attn_{lo}_{hi}",
        )(*args)

    assert [t for lo, hi in ATTN_TILE_SPLITS for t in range(lo, hi)] == list(range(nq))
    out = None
    for tiles in ATTN_TILE_SPLITS:
        out = call(tiles, out)
    return out


def _out_ffn_kernel(ylru_ref, ynsa_ref, h_ref, gl_ref, gn_ref, w_ref, g2_ref, b2_ref,
                    w_in_hbm, w_out_hbm, g3_ref, b3_ref, o_ref, wg_scr, wu_scr, wo_scr, *staging):
    @pl.when(pl.program_id(0) == 0)
    def _():
        _stage_ffn_weights(w_in_hbm, w_out_hbm, wg_scr, wu_scr, wo_scr, *staging)

    yl = _rms_norm(ylru_ref[...], gl_ref[...]).astype(BF16)
    yn = _rms_norm(ynsa_ref[...], gn_ref[...]).astype(BF16)
    mix = _dot(yl, w_ref[0:LRU_WIDTH, :]) + _dot(yn, w_ref[LRU_WIDTH:, :])
    h2 = _layer_norm(ALPHA * h_ref[...] + mix, g2_ref[...], b2_ref[...])
    o_ref[...] = _ffn_half_step(h2, wg_scr, wu_scr, wo_scr, g3_ref, b3_ref)


def _out_proj_ffn(y_lru, y_nsa, h, gn_lru, gn_nsa, w_mix, g2, b2, w_in, w_out, g3, b3, *,
                  tm=512):
    T, D = h.shape
    row = lambda v: v.reshape(1, -1)
    return pl.pallas_call(
        _out_ffn_kernel,
        grid=(T // tm,),
        in_specs=[
            pl.BlockSpec((tm, LRU_WIDTH), lambda i: (i, 0)),
            pl.BlockSpec((tm, NSA_WIDTH), lambda i: (i, 0)),
            pl.BlockSpec((tm, D), lambda i: (i, 0)),
            _resident((1, LRU_WIDTH)), _resident((1, NSA_WIDTH)),
            _resident((LRU_WIDTH + NSA_WIDTH, D)),
            _resident((1, D)), _resident((1, D)),
            pl.BlockSpec(memory_space=pl.ANY), pl.BlockSpec(memory_space=pl.ANY),
            _resident((1, D)), _resident((1, D)),
        ],
        out_specs=pl.BlockSpec((tm, D), lambda i: (i, 0)),
        out_shape=jax.ShapeDtypeStruct((T, D), F32),
        scratch_shapes=_ffn_weight_scratch(D),
        compiler_params=pltpu.CompilerParams(
            dimension_semantics=("arbitrary",), vmem_limit_bytes=FFN_VMEM_LIMIT),
        name="out_proj_ffn",
    )(y_lru, y_nsa, h, row(gn_lru), row(gn_nsa), w_mix.astype(BF16), row(g2), row(b2),
      w_in, w_out, row(g3), row(b3))


def kernel(x, ffn1_w_in, ffn1_w_out, ln1_g, ln1_b, mix_w_in, conv_w, conv_b, lru_w_a, lru_b_a,
           lru_w_x, lru_b_x, lru_lam, cmp_pe_k, cmp_w1_k, cmp_w2_k, cmp_pe_v, cmp_w1_v,
           cmp_w2_v, gn_lru, gn_nsa, mix_w_out, ln2_g, ln2_b, ffn2_w_in, ffn2_w_out, ln3_g,
           ln3_b):
    B, S, D = x.shape
    h = x.reshape(B * S, D)
    for l in range(DEPTH):
        h = _ffn_ln(h, ffn1_w_in[l], ffn1_w_out[l], ln1_g[l], ln1_b[l])
        lru, qt, cmp_in, k, vt, gates_t = _proj(h, mix_w_in[l], conv_w[l], conv_b[l], S)
        y_lru = _lru(lru, lru_w_a[l], lru_b_a[l], lru_w_x[l], lru_b_x[l], lru_lam[l], B, S)
        kc, vct = _compress(cmp_in, cmp_pe_k[l], cmp_w1_k[l], cmp_w2_k[l],
                            cmp_pe_v[l], cmp_w1_v[l], cmp_w2_v[l], B, S)
        y_nsa = _attention(qt, k, vt, kc, vct, gates_t, B, S)
        h = _out_proj_ffn(y_lru, y_nsa, h, gn_lru[l], gn_nsa[l], mix_w_out[l], ln2_g[l], ln2_b[l],
                          ffn2_w_in[l], ffn2_w_out[l], ln3_g[l], ln3_b[l])
    return h.reshape(B, S, D)
```

```python
import functools

import numpy as np
import jax
import jax.numpy as jnp
from jax import lax
from jax.experimental import pallas as pl
from jax.experimental.pallas import tpu as pltpu

F32 = jnp.float32
BF16 = jnp.bfloat16

D_MODEL = 1024
LRU_WIDTH = 512
LRU_HEADS = 8
LRU_BLOCK = 64
CONV_WIDTH = 4
LRU_C = 8.0
NSA_Q_HEADS = 8
NSA_KV_HEADS = 2
NSA_GROUP = 4
HEAD_DIM = 64
NSA_WIDTH = 512
KV_WIDTH = 128
CMP_BLOCK = 32
CMP_STRIDE = 16
CMP_HIDDEN = 256
SLC_BLOCK = 64
SLC_SHIFT = 6
SLC_TOP_N = 16
N_LOCAL_BLOCKS = 2
WINDOW = 512
ROPE_THETA = 10000.0
D_FF = 2816
DEPTH = 1
ALPHA = (2.0 * DEPTH) ** 0.25
LN_EPS = 1e-5
RMS_EPS = 1e-6
NEG = -1e30
LOG2E = 1.4426950408889634
F32_TINY = 1.1754943508222875e-38

LANES = 128
SUBLANES = 8
BF16_ROWS = 16
VMEM_LIMIT = 48 * 1024 * 1024
N_GATE = 3 * NSA_GROUP
GATE_ROWS = 16
V_ROWS = HEAD_DIM + BF16_ROWS


def _dot(a, b):
    return jnp.dot(a, b, preferred_element_type=F32)


def _dot_nt(a, b):
    return lax.dot_general(a, b, (((1,), (1,)), ((), ())), preferred_element_type=F32)


def _layer_norm(y, g, b):
    mu = jnp.mean(y, axis=-1, keepdims=True)
    d = y - mu
    var = jnp.mean(d * d, axis=-1, keepdims=True)
    return d * lax.rsqrt(var + LN_EPS) * g + b


def _rms_norm(y, g):
    return y * lax.rsqrt(jnp.mean(y * y, axis=-1, keepdims=True) + RMS_EPS) * g


def _silu(x):
    return x * jax.nn.sigmoid(x)


def _resident(shape):
    return pl.BlockSpec(shape, lambda *_: (0,) * len(shape), pipeline_mode=pl.Buffered(1))


def _ffn_half_step(x, wg_ref, wu_ref, wo_ref, g_ref, b_ref):
    xb = x.astype(BF16)
    gate = _dot(xb, wg_ref[...])
    up = _dot(xb, wu_ref[...])
    act = (_silu(gate) * up).astype(BF16)
    y = ALPHA * x + 0.5 * _dot(act, wo_ref[...])
    return _layer_norm(y, g_ref[...], b_ref[...])


W_IN_ROWS = 64
W_OUT_ROWS = 352
FFN_VMEM_LIMIT = 56 * 1024 * 1024


def _stream_rows(w_hbm, rows, slots, sems, consume):
    def copy(c):
        return pltpu.make_async_copy(w_hbm.at[pl.ds(c * rows, rows), :], slots[c % 2], sems[c % 2])

    n = w_hbm.shape[0] // rows
    copy(0).start()
    for c in range(n):
        if c + 1 < n:
            copy(c + 1).start()
        copy(c).wait()
        consume(c * rows, slots[c % 2])


def _stage_ffn_weights(w_in_hbm, w_out_hbm, wg_scr, wu_scr, wo_scr, in0, in1, out0, out1,
                       sem_in0, sem_in1, sem_out0, sem_out1):
    def put_in(r0, buf):
        wg_scr[r0:r0 + W_IN_ROWS, :] = buf[:, :D_FF].astype(BF16)
        wu_scr[r0:r0 + W_IN_ROWS, :] = buf[:, D_FF:].astype(BF16)

    def put_out(r0, buf):
        wo_scr[r0:r0 + W_OUT_ROWS, :] = buf[...].astype(BF16)

    _stream_rows(w_in_hbm, W_IN_ROWS, (in0, in1), (sem_in0, sem_in1), put_in)
    _stream_rows(w_out_hbm, W_OUT_ROWS, (out0, out1), (sem_out0, sem_out1), put_out)


def _ffn_weight_scratch(D):
    assert D % W_IN_ROWS == 0 and D_FF % W_OUT_ROWS == 0
    return ([pltpu.VMEM((D, D_FF), BF16), pltpu.VMEM((D, D_FF), BF16), pltpu.VMEM((D_FF, D), BF16)]
            + [pltpu.VMEM((W_IN_ROWS, 2 * D_FF), F32)] * 2 + [pltpu.VMEM((W_OUT_ROWS, D), F32)] * 2
            + [pltpu.SemaphoreType.DMA(())] * 4)


def _ffn_ln_kernel(x_ref, w_in_hbm, w_out_hbm, g_ref, b_ref, o_ref, wg_scr, wu_scr, wo_scr,
                   *staging):
    @pl.when(pl.program_id(0) == 0)
    def _():
        _stage_ffn_weights(w_in_hbm, w_out_hbm, wg_scr, wu_scr, wo_scr, *staging)

    o_ref[...] = _ffn_half_step(x_ref[...], wg_scr, wu_scr, wo_scr, g_ref, b_ref)


def _ffn_ln(x, w_in, w_out, g, b, *, tm=1024):
    T, D = x.shape
    return pl.pallas_call(
        _ffn_ln_kernel,
        grid=(T // tm,),
        in_specs=[
            pl.BlockSpec((tm, D), lambda i: (i, 0)),
            pl.BlockSpec(memory_space=pl.ANY), pl.BlockSpec(memory_space=pl.ANY),
            _resident((1, D)), _resident((1, D)),
        ],
        out_specs=pl.BlockSpec((tm, D), lambda i: (i, 0)),
        out_shape=jax.ShapeDtypeStruct((T, D), F32),
        scratch_shapes=_ffn_weight_scratch(D),
        compiler_params=pltpu.CompilerParams(
            dimension_semantics=("arbitrary",), vmem_limit_bytes=FFN_VMEM_LIMIT),
        name="ffn_ln",
    )(x, w_in, w_out, g.reshape(1, D), b.reshape(1, D))


Q_COL0 = 2 * LRU_WIDTH
KCMP_COL0 = Q_COL0 + NSA_WIDTH
VCMP_COL0 = KCMP_COL0 + KV_WIDTH
KSLC_COL0 = VCMP_COL0 + KV_WIDTH
VSLC_COL0 = KSLC_COL0 + KV_WIDTH
KWIN_COL0 = VSLC_COL0 + KV_WIDTH
VWIN_COL0 = KWIN_COL0 + KV_WIDTH
GATE_COL0 = VWIN_COL0 + KV_WIDTH
VT_ROW0 = NSA_WIDTH
GT_ROW0 = VT_ROW0 + 2 * KV_WIDTH
WT_ROWS = GT_ROW0 + NSA_KV_HEADS * GATE_ROWS


def _rope_chunk(xc, cos, sin_signed):
    lane = lax.broadcasted_iota(jnp.int32, xc.shape, 1)
    first = (lane & (HEAD_DIM - 1)) < (HEAD_DIM // 2)
    partner = jnp.where(first, pltpu.roll(xc, LANES - HEAD_DIM // 2, axis=1),
                        pltpu.roll(xc, HEAD_DIM // 2, axis=1))
    return xc * cos + partner * sin_signed


def _proj_kernel(h_ref, wn_ref, wt_ref, cos_ref, sin_ref, cost_ref, sint_ref, cw_ref, cb_ref,
                 lru_ref, qt_ref, cmp_ref, k_ref, vt_ref, gt_ref, seg_scr, conv_scr, *, seq_tiles):
    tm = h_ref.shape[0]
    half = HEAD_DIM // 2

    @pl.when(pl.program_id(0) % seq_tiles == 0)
    def _():
        conv_scr[0:SUBLANES, :] = jnp.zeros((SUBLANES, LRU_WIDTH), F32)

    hb = h_ref[...].astype(BF16)
    p = _dot(hb, wn_ref[...])
    pt = _dot_nt(wt_ref[...], hb)

    x = p[:, :LRU_WIDTH]
    conv_scr[SUBLANES:, :] = x
    xc = cb_ref[...]
    for d in range(CONV_WIDTH):
        xs = conv_scr[SUBLANES - d:SUBLANES - d + tm, :]
        xc = xc + xs * cw_ref[CONV_WIDTH - 1 - d:CONV_WIDTH - d, :]
    conv_scr[0:SUBLANES, :] = x[tm - SUBLANES:, :]
    lru_ref[:, :LRU_WIDTH] = xc
    lru_ref[:, LRU_WIDTH:] = jax.nn.gelu(p[:, LRU_WIDTH:Q_COL0])

    cos_t = cost_ref[...]
    sin_t = sint_ref[...]
    scale = HEAD_DIM ** -0.5 * LOG2E
    for hd in range(NSA_Q_HEADS):
        x1 = pt[hd * HEAD_DIM:hd * HEAD_DIM + half]
        x2 = pt[hd * HEAD_DIM + half:(hd + 1) * HEAD_DIM]
        qt_ref[hd] = (jnp.concatenate([x1 * cos_t - x2 * sin_t, x2 * cos_t + x1 * sin_t], axis=0)
                      * scale).astype(BF16)
    ones = jnp.ones((V_ROWS - HEAD_DIM, tm), F32)
    for c in range(2 * NSA_KV_HEADS):
        v = pt[VT_ROW0 + c * HEAD_DIM:VT_ROW0 + (c + 1) * HEAD_DIM]
        vt_ref[c] = jnp.concatenate([v, ones], axis=0).astype(BF16)
    for g in range(NSA_KV_HEADS):
        gt_ref[g] = jax.nn.sigmoid(pt[GT_ROW0 + g * GATE_ROWS:GT_ROW0 + (g + 1) * GATE_ROWS])

    cos = cos_ref[...]
    sin = sin_ref[...]
    low_seg = lax.broadcasted_iota(jnp.int32, (tm // CMP_STRIDE, LANES), 1) < HEAD_DIM
    for c, xc in enumerate((_rope_chunk(p[:, Q_COL0:Q_COL0 + LANES], cos, sin),
                            p[:, Q_COL0 + LANES:Q_COL0 + 2 * LANES])):
        seg_scr[...] = xc
        tok = [seg_scr[pl.ds(j, tm // CMP_STRIDE, stride=CMP_STRIDE), :]
               for j in range(CMP_STRIDE)]
        g0, g1 = [], []
        for j in range(0, CMP_STRIDE, 2):
            g0.append(jnp.where(low_seg, tok[j], pltpu.roll(tok[j + 1], HEAD_DIM, axis=1)))
            g1.append(jnp.where(low_seg, pltpu.roll(tok[j], HEAD_DIM, axis=1), tok[j + 1]))
        cmp_ref[NSA_KV_HEADS * c] = jnp.concatenate(g0, axis=1).astype(BF16)
        cmp_ref[NSA_KV_HEADS * c + 1] = jnp.concatenate(g1, axis=1).astype(BF16)
    lane = lax.broadcasted_iota(jnp.int32, (tm, LANES), 1)
    low = lane < HEAD_DIM
    pos = (pl.program_id(0) % seq_tiles) * tm + lax.broadcasted_iota(jnp.int32, (tm, LANES), 0)
    ext_blk = jnp.where(lane - HEAD_DIM == (pos >> SLC_SHIFT), NEG, 0.0)
    ext_zero = jnp.zeros((tm, LANES), F32)
    for c, ext in enumerate((ext_blk, ext_zero)):
        xc = _rope_chunk(p[:, Q_COL0 + (2 + c) * LANES:Q_COL0 + (3 + c) * LANES], cos, sin)
        k_ref[2 * c] = jnp.where(low, xc, ext).astype(BF16)
        k_ref[2 * c + 1] = jnp.where(low, pltpu.roll(xc, HEAD_DIM, axis=1), ext).astype(BF16)


def _proj(h, w_in, conv_w, conv_b, S, *, tm=1024):
    T, D = h.shape
    G = NSA_KV_HEADS
    col = lambda c0, n: w_in[:, c0:c0 + n]
    w_nat = jnp.concatenate(
        [col(0, Q_COL0), col(KCMP_COL0, KV_WIDTH), col(VCMP_COL0, KV_WIDTH),
         col(KSLC_COL0, KV_WIDTH), col(KWIN_COL0, KV_WIDTH)], axis=1).astype(BF16)
    gate_rows = [jnp.pad(col(GATE_COL0 + g * N_GATE, N_GATE).T, ((0, GATE_ROWS - N_GATE), (0, 0)))
                 for g in range(G)]
    w_t = jnp.concatenate(
        [col(Q_COL0, NSA_WIDTH).T, col(VSLC_COL0, KV_WIDTH).T, col(VWIN_COL0, KV_WIDTH).T]
        + gate_rows, axis=0).astype(BF16)
    n_nat = w_nat.shape[1]

    half = HEAD_DIM // 2
    inv = ROPE_THETA ** (-jnp.arange(half, dtype=F32) / half)
    ang = jnp.arange(S, dtype=F32)[:, None] * inv[None, :]
    cos = jnp.cos(ang)
    sin = jnp.sin(ang)
    cos_n = jnp.concatenate([cos, cos, cos, cos], axis=1)
    sin_n = jnp.concatenate([-sin, sin, -sin, sin], axis=1)

    nS = S // tm
    seg_w = CMP_STRIDE * HEAD_DIM
    assert G * HEAD_DIM == LANES
    return pl.pallas_call(
        functools.partial(_proj_kernel, seq_tiles=nS),
        grid=(T // tm,),
        in_specs=[
            pl.BlockSpec((tm, D), lambda i: (i, 0)),
            _resident((D, n_nat)), _resident((WT_ROWS, D)),
            pl.BlockSpec((tm, LANES), lambda i: (i % nS, 0)),
            pl.BlockSpec((tm, LANES), lambda i: (i % nS, 0)),
            pl.BlockSpec((half, tm), lambda i: (0, i % nS)),
            pl.BlockSpec((half, tm), lambda i: (0, i % nS)),
            _resident((CONV_WIDTH, LRU_WIDTH)), _resident((1, LRU_WIDTH)),
        ],
        out_specs=[
            pl.BlockSpec((tm, Q_COL0), lambda i: (i, 0)),
            pl.BlockSpec((NSA_Q_HEADS, HEAD_DIM, tm), lambda i: (0, 0, i)),
            pl.BlockSpec((2 * G, tm // CMP_STRIDE, seg_w), lambda i: (0, i, 0)),
            pl.BlockSpec((2 * G, tm, LANES), lambda i: (0, i, 0)),
            pl.BlockSpec((2 * G, V_ROWS, tm), lambda i: (0, 0, i)),
            pl.BlockSpec((G, GATE_ROWS, tm), lambda i: (0, 0, i)),
        ],
        out_shape=[
            jax.ShapeDtypeStruct((T, Q_COL0), F32),
            jax.ShapeDtypeStruct((NSA_Q_HEADS, HEAD_DIM, T), BF16),
            jax.ShapeDtypeStruct((2 * G, T // CMP_STRIDE, seg_w), BF16),
            jax.ShapeDtypeStruct((2 * G, T, LANES), BF16),
            jax.ShapeDtypeStruct((2 * G, V_ROWS, T), BF16),
            jax.ShapeDtypeStruct((G, GATE_ROWS, T), F32),
        ],
        scratch_shapes=[pltpu.VMEM((tm, LANES), F32),
                        pltpu.VMEM((SUBLANES + tm, LRU_WIDTH), F32)],
        compiler_params=pltpu.CompilerParams(
            dimension_semantics=("arbitrary",), vmem_limit_bytes=VMEM_LIMIT),
        name="proj",
    )(h, w_nat, w_t, cos_n, sin_n, cos.T, sin.T, conv_w, conv_b.reshape(1, LRU_WIDTH))


SCAN_UNROLL = 8
LRU_ROWS = 512


def _lru_kernel(x_ref, gate_ref, wa_ref, ba_ref, wx_ref, bx_ref, lam_ref, y_ref, a_scr, b_scr):
    S, C = y_ref.shape
    neg_lam = -lam_ref[...]
    softplus = jnp.maximum(neg_lam, 0.0) + jnp.log1p(jnp.exp(-jnp.abs(neg_lam)))

    for r0 in range(0, S, LRU_ROWS):
        xc = x_ref[r0:r0 + LRU_ROWS, :]
        xb = xc.astype(BF16)
        gate_pre = lambda w_ref: jnp.concatenate(
            [_dot(xb[:, c * LANES:(c + 1) * LANES], w_ref[c]) for c in range(C // LANES)], axis=1)
        r = 0.5 * jnp.tanh(0.5 * (gate_pre(wa_ref) + ba_ref[...])) + 0.5
        ig = 0.5 * jnp.tanh(0.5 * (gate_pre(wx_ref) + bx_ref[...])) + 0.5
        log_a = (-LRU_C) * r * softplus
        a = jnp.exp(log_a)
        z = jnp.tanh(-log_a) * (a * a + 1.0)
        mult = z * lax.rsqrt(jnp.maximum(z, F32_TINY))
        if r0 == 0:
            row = lax.broadcasted_iota(jnp.int32, mult.shape, 0)
            mult = jnp.where(row == 0, 1.0, mult)
        a_scr[r0:r0 + LRU_ROWS, :] = a
        b_scr[r0:r0 + LRU_ROWS, :] = mult * (ig * xc)

    sub = lax.broadcasted_iota(jnp.int32, (SUBLANES, C), 0)

    def scan_tile(g, h_prev):
        rows = pl.ds(pl.multiple_of(g * SUBLANES, SUBLANES), SUBLANES)
        at = a_scr[rows, :]
        bt = b_scr[rows, :]
        d = 1
        while d < SUBLANES:
            keep = sub >= d
            a_sh = jnp.where(keep, pltpu.roll(at, d, axis=0), 1.0)
            b_sh = jnp.where(keep, pltpu.roll(bt, d, axis=0), 0.0)
            bt = at * b_sh + bt
            at = at * a_sh
            d *= 2
        h = bt + at * h_prev
        b_scr[rows, :] = h
        return jnp.broadcast_to(h[SUBLANES - 1:SUBLANES, :], (SUBLANES, C))

    def scan_body(i, h_prev):
        for u in range(SCAN_UNROLL):
            h_prev = scan_tile(i * SCAN_UNROLL + u, h_prev)
        return h_prev

    lax.fori_loop(0, S // (SUBLANES * SCAN_UNROLL), scan_body, jnp.zeros((SUBLANES, C), F32))

    for r0 in range(0, S, LRU_ROWS):
        rows = slice(r0, r0 + LRU_ROWS)
        y_ref[rows, :] = b_scr[rows, :] * gate_ref[rows, :]


def _lru(lru, w_a, b_a, w_x, b_x, lam, B, S):
    T = lru.shape[0]
    C = LRU_WIDTH
    n_ch = C // LANES
    assert S % (SUBLANES * SCAN_UNROLL) == 0 and S % LRU_ROWS == 0

    def blockdiag(w):
        w = w.reshape(n_ch, 2, LRU_BLOCK, LRU_BLOCK)
        z = jnp.zeros((n_ch, LRU_BLOCK, LRU_BLOCK), w.dtype)
        top = jnp.concatenate([w[:, 0], z], axis=2)
        bot = jnp.concatenate([z, w[:, 1]], axis=2)
        return jnp.concatenate([top, bot], axis=1).astype(BF16)

    vec = lambda v: v.reshape(1, C)
    return pl.pallas_call(
        _lru_kernel,
        grid=(B,),
        in_specs=[
            pl.BlockSpec((S, C), lambda b: (b, 0)),
            pl.BlockSpec((S, C), lambda b: (b, 1)),
            _resident((n_ch, LANES, LANES)), _resident((1, C)),
            _resident((n_ch, LANES, LANES)), _resident((1, C)),
            _resident((1, C)),
        ],
        out_specs=pl.BlockSpec((S, C), lambda b: (b, 0)),
        out_shape=jax.ShapeDtypeStruct((T, C), F32),
        scratch_shapes=[pltpu.VMEM((S, C), F32),
                        pltpu.VMEM((S, C), F32)],
        compiler_params=pltpu.CompilerParams(
            dimension_semantics=("parallel",), vmem_limit_bytes=VMEM_LIMIT),
        name="lru",
    )(lru, lru, blockdiag(w_a), vec(b_a.reshape(-1)),
      blockdiag(w_x), vec(b_x.reshape(-1)), vec(lam))


def _cmp_kernel(seg_ref, pek_ref, w1k_ref, w2k_ref, pev_ref, w1v_ref, w2vt_ref, kc_ref, vct_ref):
    n_seg = seg_ref.shape[2]
    half = CMP_STRIDE * HEAD_DIM

    def hidden(idx, pe_ref, w1_ref):
        seg = seg_ref[idx, 0]
        first = _dot(seg, w1_ref[0:half, :])
        second = _dot(seg, w1_ref[half:2 * half, :])
        bias = _dot(pe_ref[...], w1_ref[...])[0:1, :]
        return _silu(first + pltpu.roll(second, n_seg - 1, axis=0) + bias).astype(BF16)

    for g in range(NSA_KV_HEADS):
        tok = _dot(hidden(g, pek_ref, w1k_ref), w2k_ref[...])
        kc_ref[0, g] = jnp.concatenate([tok, jnp.zeros_like(tok)], axis=1).astype(BF16)
        vct_ref[0, g] = _dot_nt(w2vt_ref[...], hidden(NSA_KV_HEADS + g, pev_ref, w1v_ref)
                                ).astype(BF16)


def _compress(cmp_in, pe_k, w1_k, w2_k, pe_v, w1_v, w2_v, B, S):
    G = NSA_KV_HEADS
    n_seg = S // CMP_STRIDE
    seg = cmp_in.reshape(2 * G, B, n_seg, CMP_STRIDE * HEAD_DIM)
    flat = CMP_BLOCK * HEAD_DIM
    pe8 = lambda pe: jnp.broadcast_to(pe.reshape(1, -1), (8, flat)).astype(BF16)
    return pl.pallas_call(
        _cmp_kernel,
        grid=(B,),
        in_specs=[
            pl.BlockSpec((2 * G, 1, n_seg, CMP_STRIDE * HEAD_DIM), lambda b: (0, b, 0, 0)),
            _resident((8, flat)), _resident((flat, CMP_HIDDEN)), _resident((CMP_HIDDEN, HEAD_DIM)),
            _resident((8, flat)), _resident((flat, CMP_HIDDEN)), _resident((HEAD_DIM, CMP_HIDDEN)),
        ],
        out_specs=[
            pl.BlockSpec((1, G, n_seg, LANES), lambda b: (b, 0, 0, 0)),
            pl.BlockSpec((1, G, HEAD_DIM, n_seg), lambda b: (b, 0, 0, 0)),
        ],
        out_shape=[
            jax.ShapeDtypeStruct((B, G, n_seg, LANES), BF16),
            jax.ShapeDtypeStruct((B, G, HEAD_DIM, n_seg), BF16),
        ],
        compiler_params=pltpu.CompilerParams(
            dimension_semantics=("parallel",), vmem_limit_bytes=VMEM_LIMIT),
        name="compress",
    )(seg, pe8(pe_k), w1_k.astype(BF16), w2_k.astype(BF16),
      pe8(pe_v), w1_v.astype(BF16), w2_v.T.astype(BF16))


def _attn_kernel(qt_ref, kc_ref, vct_ref, ks_ref, vst_ref, kw_ref, vwt_ref, gt_ref, ovl_ref,
                 bias_ref, *rest, tq, tiles):
    o_ref = rest[-1]

    def step(n):
        for _ in _interleave(*[
                _attn_step(n, g, qt_ref, kc_ref.at[g], vct_ref.at[g], ks_ref.at[g], vst_ref.at[g],
                           kw_ref.at[g], vwt_ref.at[g], gt_ref.at[g], ovl_ref, bias_ref, o_ref, tq)
                for g in range(NSA_KV_HEADS)]):
            pass

    for n in range(*tiles):
        pl.when(pl.program_id(1) == n - tiles[0])(functools.partial(step, n))


CHUNK_TILES = 1
SCORES_AHEAD = 1
ATTN_TILE_SPLITS = ((0, 5), (5, 8))


def _interleave(*stages):
    live = list(stages)
    while live:
        for st in list(live):
            try:
                next(st)
            except StopIteration:
                live.remove(st)
        yield


def _tile_bias(bias_ref, t0, k0, behind):
    if k0 == t0:
        return bias_ref[0]
    if behind is not None and k0 == t0 - behind:
        return bias_ref[1]
    return None


def _key_chunks(k_ref, vt_ref, k0, n_keys, tile, q, bias_fn):
    starts = list(range(k0, k0 + n_keys, tile))
    return [dict(k_ref=k_ref, vt_ref=vt_ref, tile=tile, q=q, bias_fn=bias_fn,
                 starts=starts[i:i + CHUNK_TILES]) for i in range(0, len(starts), CHUNK_TILES)]


def _scores_stage(ch):
    tiles, m = [], None
    for k0 in ch["starts"]:
        s = _dot(ch["k_ref"][k0:k0 + ch["tile"], :], ch["q"])
        bias = ch["bias_fn"](k0)
        if bias is not None:
            s = s + bias
        cm = jnp.max(s, axis=0, keepdims=True)
        m = cm if m is None else jnp.maximum(m, cm)
        tiles.append(s)
        yield
    ch["s"], ch["m"] = tiles, m


def _pv_stage(ch, heads, tq):
    acc = [None] * heads
    for k0, s in zip(ch["starts"], ch["s"]):
        p = jnp.exp2(s - ch["m"]).astype(BF16)
        vt = ch["vt_ref"][:, k0:k0 + ch["tile"]]
        for r in range(heads):
            d = _dot(vt, p[:, r * tq:(r + 1) * tq])
            acc[r] = d if acc[r] is None else acc[r] + d
        yield
    ch["acc"] = acc


def _merge_chunks(chunks, heads, tq):
    if len(chunks) == 1:
        return chunks[0]["acc"]
    m = functools.reduce(jnp.maximum, [ch["m"] for ch in chunks])
    acc = [None] * heads
    for ch in chunks:
        w = jnp.exp2(ch["m"] - m)
        for r in range(heads):
            term = ch["acc"][r] * w[:, r * tq:(r + 1) * tq]
            acc[r] = term if acc[r] is None else acc[r] + term
    return acc


def _rank_stage(out, score, jblk, n_live):
    rank = jnp.zeros(score.shape, jnp.int32)
    for kb in range(n_live):
        sk = score[kb:kb + 1, :]
        ahead = (sk > score) | ((sk == score) & (jblk > kb))
        rank = rank + ahead.astype(jnp.int32)
        if kb % 4 == 3:
            yield
    out["rank"] = rank


def _attn_step(n, g, qt_ref, kc_ref, vct_ref, ks_ref, vst_ref, kw_ref, vwt_ref, gt_ref, ovl_ref,
               bias_ref, o_ref, tq):
    R = NSA_GROUP
    t0 = n * tq
    n_cmp = kc_ref.shape[0]
    n_blk = ovl_ref.shape[0]
    q_all = jnp.concatenate([qt_ref[g * R + r] for r in range(R)], axis=1)
    q_pad = jnp.concatenate([q_all, jnp.zeros((LANES - HEAD_DIM, R * tq), BF16)], axis=0)
    head = lambda a, r: a[:, r * tq:(r + 1) * tq]

    kw0 = max(t0 - WINDOW, 0)
    win = _key_chunks(kw_ref, vwt_ref, kw0, t0 + tq - kw0, tq, q_pad,
                      lambda k0: _tile_bias(bias_ref, t0, k0, WINDOW))
    win_scores = _scores_stage(win[0])
    next(win_scores)
    yield

    n_vis = (t0 + tq - CMP_BLOCK) // CMP_STRIDE + 1
    n_vis = min(n_cmp, -(-n_vis // BF16_ROWS) * BF16_ROWS)
    tpos_c = t0 + (lax.broadcasted_iota(jnp.int32, (n_vis, R * tq), 1) & (tq - 1))
    cend = lax.broadcasted_iota(jnp.int32, (n_vis, R * tq), 0) * CMP_STRIDE + (CMP_BLOCK - 1)
    s = _dot(kc_ref[0:n_vis, :], q_pad) + jnp.where(cend <= tpos_c, 0.0, NEG)
    e = jnp.exp2(s - jnp.max(s, axis=0, keepdims=True))
    p_cmp = e * (1.0 / jnp.sum(e, axis=0, keepdims=True))
    if t0 < CMP_BLOCK - 1:
        p_cmp = jnp.where(tpos_c >= CMP_BLOCK - 1, p_cmp, 0.0)
    if n_vis < n_cmp:
        p_cmp = jnp.concatenate([p_cmp, jnp.zeros((n_cmp - n_vis, R * tq), F32)], axis=0)
    p_cmp_b = p_cmp.astype(BF16)
    vct = vct_ref[...]
    o_cmp = [_dot(vct, head(p_cmp_b, r)) for r in range(R)]
    psum = functools.reduce(jnp.add, [head(p_cmp, r) for r in range(R)])
    yield

    p_hi = psum.astype(BF16)
    p_lo = (psum - p_hi.astype(F32)).astype(BF16)
    ovl = ovl_ref[...]
    imp = _dot(ovl, p_hi) + _dot(ovl, p_lo)
    n_live = min(n_blk, (t0 + tq - 1) // SLC_BLOCK + 1)
    rows = -(-n_live // SUBLANES) * SUBLANES
    jblk = lax.broadcasted_iota(jnp.int32, (rows, tq), 0)
    tpos_b = t0 + lax.broadcasted_iota(jnp.int32, (rows, tq), 1)
    blk_valid = jblk * SLC_BLOCK <= tpos_b
    back = (tpos_b >> SLC_SHIFT) - jblk
    forced = (jblk == 0) | ((back >= 0) & (back < N_LOCAL_BLOCKS))
    score = jnp.where(blk_valid, jnp.where(forced, jnp.inf, imp[0:rows]), -jnp.inf)
    sel = {}
    yield from _interleave(_rank_stage(sel, score, jblk, n_live), win_scores)
    unsel = 1.0 - (blk_valid & (sel["rank"] < SLC_TOP_N)).astype(F32)
    dead = [jnp.ones((n_blk - rows, tq), F32)] if rows < n_blk else []
    sel_ext = jnp.concatenate(
        [unsel] + dead + [jnp.zeros((LANES - HEAD_DIM - n_blk, tq), F32)],
        axis=0).astype(BF16)
    q_sel = jnp.concatenate([q_all, jnp.concatenate([sel_ext] * R, axis=1)], axis=0)

    slc = _key_chunks(ks_ref, vst_ref, 0, t0 + tq, tq, q_sel,
                      lambda k0: _tile_bias(bias_ref, t0, k0, None))
    chunks = win + slc
    for ch in chunks[1:SCORES_AHEAD]:
        yield from _scores_stage(ch)
    for idx, ch in enumerate(chunks):
        stages = [_pv_stage(ch, R, tq)]
        if idx + SCORES_AHEAD < len(chunks):
            stages.append(_scores_stage(chunks[idx + SCORES_AHEAD]))
        yield from _interleave(*stages)
    acc_w = _merge_chunks(win, R, tq)
    acc_s = _merge_chunks(slc, R, tq)
    yield

    gt = gt_ref[...]
    outs = []
    for r in range(R):
        g_cmp = gt[3 * r:3 * r + 1]
        g_slc = gt[3 * r + 1:3 * r + 2] * (1.0 / acc_s[r][HEAD_DIM:HEAD_DIM + 1])
        g_win = gt[3 * r + 2:3 * r + 3] * (1.0 / acc_w[r][HEAD_DIM:HEAD_DIM + 1])
        outs.append(g_cmp * o_cmp[r] + g_slc * acc_s[r][:HEAD_DIM] + g_win * acc_w[r][:HEAD_DIM])
    width = R * HEAD_DIM
    o_ref[:, g * width:(g + 1) * width] = jnp.concatenate(outs, axis=0).T


def _overlap_t(n_cmp_pad, n_blk):
    cs = np.arange(n_cmp_pad) * CMP_STRIDE
    ce = cs + CMP_BLOCK - 1
    ss = np.arange(n_blk) * SLC_BLOCK
    se = ss + SLC_BLOCK - 1
    return ((cs[None, :] <= se[:, None]) & (ce[None, :] >= ss[:, None])).astype(np.float32)


def _attention(qt, k, vt, kc, vct, gates_t, B, S, *, tq=256):
    T = qt.shape[2]
    nq = S // tq
    n_cmp = S // CMP_STRIDE
    n_blk = S // SLC_BLOCK
    G = NSA_KV_HEADS
    ovl = jnp.asarray(_overlap_t(n_cmp, n_blk), BF16)
    assert WINDOW % tq == 0 and WINDOW + tq <= S
    assert HEAD_DIM + n_blk <= LANES
    assert tq & (tq - 1) == 0
    key_off = np.arange(tq)[:, None]
    q_off = np.tile(np.arange(tq), NSA_GROUP)[None, :]
    bias = jnp.asarray(np.stack([np.where(key_off <= q_off, 0.0, NEG),
                                 np.where(key_off > q_off, 0.0, NEG)]), F32)
    k_spec = lambda c: pl.BlockSpec((G, S, LANES), lambda b, i: (c, b, 0))
    vt_spec = lambda c: pl.BlockSpec((G, V_ROWS, S), lambda b, i: (c, 0, b))

    def call(tiles, prev):
        lo, hi = tiles
        tile = lambda b, i: b * nq + lo + i
        in_specs = [
            pl.BlockSpec((NSA_Q_HEADS, HEAD_DIM, tq), lambda b, i: (0, 0, tile(b, i))),
            pl.BlockSpec((None, G, n_cmp, LANES), lambda b, i: (b, 0, 0, 0)),
            pl.BlockSpec((None, G, HEAD_DIM, n_cmp), lambda b, i: (b, 0, 0, 0)),
            k_spec(0), vt_spec(0), k_spec(1), vt_spec(1),
            pl.BlockSpec((G, GATE_ROWS, tq), lambda b, i: (0, 0, tile(b, i))),
            _resident((n_blk, n_cmp)),
            _resident((2, tq, NSA_GROUP * tq)),
        ]
        args = [qt, kc, vct, k, vt, k, vt, gates_t, ovl, bias]
        aliases = {}
        if prev is not None:
            in_specs.append(pl.BlockSpec(memory_space=pl.ANY))
            aliases = {len(args): 0}
            args.append(prev)
        return pl.pallas_call(
            functools.partial(_attn_kernel, tq=tq, tiles=tiles),
            grid=(B, hi - lo),
            in_specs=in_specs,
            out_specs=pl.BlockSpec((tq, NSA_WIDTH), lambda b, i: (tile(b, i), 0)),
            out_shape=jax.ShapeDtypeStruct((T, NSA_WIDTH), F32),
            input_output_aliases=aliases,
            compiler_params=pltpu.CompilerParams(
                dimension_semantics=("parallel", "arbitrary"),
                vmem_limit_bytes=VMEM_LIMIT),
            name=f"nsa_attn_{lo}_{hi}",
        )(*args)

    assert [t for lo, hi in ATTN_TILE_SPLITS for t in range(lo, hi)] == list(range(nq))
    out = None
    for tiles in ATTN_TILE_SPLITS:
        out = call(tiles, out)
    return out


def _out_ffn_kernel(ylru_ref, ynsa_ref, h_ref, gl_ref, gn_ref, w_ref, g2_ref, b2_ref,
                    w_in_hbm, w_out_hbm, g3_ref, b3_ref, o_ref, wg_scr, wu_scr, wo_scr, *staging):
    @pl.when(pl.program_id(0) == 0)
    def _():
        _stage_ffn_weights(w_in_hbm, w_out_hbm, wg_scr, wu_scr, wo_scr, *staging)

    yl = _rms_norm(ylru_ref[...], gl_ref[...]).astype(BF16)
    yn = _rms_norm(ynsa_ref[...], gn_ref[...]).astype(BF16)
    mix = _dot(yl, w_ref[0:LRU_WIDTH, :]) + _dot(yn, w_ref[LRU_WIDTH:, :])
    h2 = _layer_norm(ALPHA * h_ref[...] + mix, g2_ref[...], b2_ref[...])
    o_ref[...] = _ffn_half_step(h2, wg_scr, wu_scr, wo_scr, g3_ref, b3_ref)


def _out_proj_ffn(y_lru, y_nsa, h, gn_lru, gn_nsa, w_mix, g2, b2, w_in, w_out, g3, b3, *,
                  tm=512):
    T, D = h.shape
    row = lambda v: v.reshape(1, -1)
    return pl.pallas_call(
        _out_ffn_kernel,
        grid=(T // tm,),
        in_specs=[
            pl.BlockSpec((tm, LRU_WIDTH), lambda i: (i, 0)),
            pl.BlockSpec((tm, NSA_WIDTH), lambda i: (i, 0)),
            pl.BlockSpec((tm, D), lambda i: (i, 0)),
            _resident((1, LRU_WIDTH)), _resident((1, NSA_WIDTH)),
            _resident((LRU_WIDTH + NSA_WIDTH, D)),
            _resident((1, D)), _resident((1, D)),
            pl.BlockSpec(memory_space=pl.ANY), pl.BlockSpec(memory_space=pl.ANY),
            _resident((1, D)), _resident((1, D)),
        ],
        out_specs=pl.BlockSpec((tm, D), lambda i: (i, 0)),
        out_shape=jax.ShapeDtypeStruct((T, D), F32),
        scratch_shapes=_ffn_weight_scratch(D),
        compiler_params=pltpu.CompilerParams(
            dimension_semantics=("arbitrary",), vmem_limit_bytes=FFN_VMEM_LIMIT),
        name="out_proj_ffn",
    )(y_lru, y_nsa, h, row(gn_lru), row(gn_nsa), w_mix.astype(BF16), row(g2), row(b2),
      w_in, w_out, row(g3), row(b3))


def kernel(x, ffn1_w_in, ffn1_w_out, ln1_g, ln1_b, mix_w_in, conv_w, conv_b, lru_w_a, lru_b_a,
           lru_w_x, lru_b_x, lru_lam, cmp_pe_k, cmp_w1_k, cmp_w2_k, cmp_pe_v, cmp_w1_v,
           cmp_w2_v, gn_lru, gn_nsa, mix_w_out, ln2_g, ln2_b, ffn2_w_in, ffn2_w_out, ln3_g,
           ln3_b):
    B, S, D = x.shape
    h = x.reshape(B * S, D)
    for l in range(DEPTH):
        h = _ffn_ln(h, ffn1_w_in[l], ffn1_w_out[l], ln1_g[l], ln1_b[l])
        lru, qt, cmp_in, k, vt, gates_t = _proj(h, mix_w_in[l], conv_w[l], conv_b[l], S)
        y_lru = _lru(lru, lru_w_a[l], lru_b_a[l], lru_w_x[l], lru_b_x[l], lru_lam[l], B, S)
        kc, vct = _compress(cmp_in, cmp_pe_k[l], cmp_w1_k[l], cmp_w2_k[l],
                            cmp_pe_v[l], cmp_w1_v[l], cmp_w2_v[l], B, S)
        y_nsa = _attention(qt, k, vt, kc, vct, gates_t, B, S)
        h = _out_proj_ffn(y_lru, y_nsa, h, gn_lru[l], gn_nsa[l], mix_w_out[l], ln2_g[l], ln2_b[l],
                          ffn2_w_in[l], ffn2_w_out[l], ln3_g[l], ln3_b[l])
    return h.reshape(B, S, D)
```

```python
import functools

import numpy as np
import jax
import jax.numpy as jnp
from jax import lax
from jax.experimental import pallas as pl
from jax.experimental.pallas import tpu as pltpu

F32 = jnp.float32
BF16 = jnp.bfloat16

D_MODEL = 1024
LRU_WIDTH = 512
LRU_HEADS = 8
LRU_BLOCK = 64
CONV_WIDTH = 4
LRU_C = 8.0
NSA_Q_HEADS = 8
NSA_KV_HEADS = 2
NSA_GROUP = 4
HEAD_DIM = 64
NSA_WIDTH = 512
KV_WIDTH = 128
CMP_BLOCK = 32
CMP_STRIDE = 16
CMP_HIDDEN = 256
SLC_BLOCK = 64
SLC_SHIFT = 6
SLC_TOP_N = 16
N_LOCAL_BLOCKS = 2
WINDOW = 512
ROPE_THETA = 10000.0
D_FF = 2816
DEPTH = 1
ALPHA = (2.0 * DEPTH) ** 0.25
LN_EPS = 1e-5
RMS_EPS = 1e-6
NEG = -1e30
LOG2E = 1.4426950408889634
F32_TINY = 1.1754943508222875e-38

LANES = 128
SUBLANES = 8
BF16_ROWS = 16
VMEM_LIMIT = 48 * 1024 * 1024
N_GATE = 3 * NSA_GROUP
GATE_ROWS = 16
V_ROWS = HEAD_DIM + BF16_ROWS


def _dot(a, b):
    return jnp.dot(a, b, preferred_element_type=F32)


def _dot_nt(a, b):
    return lax.dot_general(a, b, (((1,), (1,)), ((), ())), preferred_element_type=F32)


def _layer_norm(y, g, b):
    mu = jnp.mean(y, axis=-1, keepdims=True)
    d = y - mu
    var = jnp.mean(d * d, axis=-1, keepdims=True)
    return d * lax.rsqrt(var + LN_EPS) * g + b


def _rms_norm(y, g):
    return y * lax.rsqrt(jnp.mean(y * y, axis=-1, keepdims=True) + RMS_EPS) * g


def _silu(x):
    return x * jax.nn.sigmoid(x)


def _resident(shape):
    return pl.BlockSpec(shape, lambda *_: (0,) * len(shape), pipeline_mode=pl.Buffered(1))


def _ffn_half_step(x, wg_ref, wu_ref, wo_ref, g_ref, b_ref):
    xb = x.astype(BF16)
    gate = _dot(xb, wg_ref[...])
    up = _dot(xb, wu_ref[...])
    act = (_silu(gate) * up).astype(BF16)
    y = ALPHA * x + 0.5 * _dot(act, wo_ref[...])
    return _layer_norm(y, g_ref[...], b_ref[...])


W_IN_ROWS = 64
W_OUT_ROWS = 352
FFN_VMEM_LIMIT = 56 * 1024 * 1024


def _stream_rows(w_hbm, rows, slots, sems, consume):
    def copy(c):
        return pltpu.make_async_copy(w_hbm.at[pl.ds(c * rows, rows), :], slots[c % 2], sems[c % 2])

    n = w_hbm.shape[0] // rows
    copy(0).start()
    for c in range(n):
        if c + 1 < n:
            copy(c + 1).start()
        copy(c).wait()
        consume(c * rows, slots[c % 2])


def _stage_ffn_weights(w_in_hbm, w_out_hbm, wg_scr, wu_scr, wo_scr, in0, in1, out0, out1,
                       sem_in0, sem_in1, sem_out0, sem_out1):
    def put_in(r0, buf):
        wg_scr[r0:r0 + W_IN_ROWS, :] = buf[:, :D_FF].astype(BF16)
        wu_scr[r0:r0 + W_IN_ROWS, :] = buf[:, D_FF:].astype(BF16)

    def put_out(r0, buf):
        wo_scr[r0:r0 + W_OUT_ROWS, :] = buf[...].astype(BF16)

    _stream_rows(w_in_hbm, W_IN_ROWS, (in0, in1), (sem_in0, sem_in1), put_in)
    _stream_rows(w_out_hbm, W_OUT_ROWS, (out0, out1), (sem_out0, sem_out1), put_out)


def _ffn_weight_scratch(D):
    assert D % W_IN_ROWS == 0 and D_FF % W_OUT_ROWS == 0
    return ([pltpu.VMEM((D, D_FF), BF16), pltpu.VMEM((D, D_FF), BF16), pltpu.VMEM((D_FF, D), BF16)]
            + [pltpu.VMEM((W_IN_ROWS, 2 * D_FF), F32)] * 2 + [pltpu.VMEM((W_OUT_ROWS, D), F32)] * 2
            + [pltpu.SemaphoreType.DMA(())] * 4)


def _ffn_ln_kernel(x_ref, w_in_hbm, w_out_hbm, g_ref, b_ref, o_ref, wg_scr, wu_scr, wo_scr,
                   *staging):
    @pl.when(pl.program_id(0) == 0)
    def _():
        _stage_ffn_weights(w_in_hbm, w_out_hbm, wg_scr, wu_scr, wo_scr, *staging)

    o_ref[...] = _ffn_half_step(x_ref[...], wg_scr, wu_scr, wo_scr, g_ref, b_ref)


def _ffn_ln(x, w_in, w_out, g, b, *, tm=1024):
    T, D = x.shape
    return pl.pallas_call(
        _ffn_ln_kernel,
        grid=(T // tm,),
        in_specs=[
            pl.BlockSpec((tm, D), lambda i: (i, 0)),
            pl.BlockSpec(memory_space=pl.ANY), pl.BlockSpec(memory_space=pl.ANY),
            _resident((1, D)), _resident((1, D)),
        ],
        out_specs=pl.BlockSpec((tm, D), lambda i: (i, 0)),
        out_shape=jax.ShapeDtypeStruct((T, D), F32),
        scratch_shapes=_ffn_weight_scratch(D),
        compiler_params=pltpu.CompilerParams(
            dimension_semantics=("arbitrary",), vmem_limit_bytes=FFN_VMEM_LIMIT),
        name="ffn_ln",
    )(x, w_in, w_out, g.reshape(1, D), b.reshape(1, D))


Q_COL0 = 2 * LRU_WIDTH
KCMP_COL0 = Q_COL0 + NSA_WIDTH
VCMP_COL0 = KCMP_COL0 + KV_WIDTH
KSLC_COL0 = VCMP_COL0 + KV_WIDTH
VSLC_COL0 = KSLC_COL0 + KV_WIDTH
KWIN_COL0 = VSLC_COL0 + KV_WIDTH
VWIN_COL0 = KWIN_COL0 + KV_WIDTH
GATE_COL0 = VWIN_COL0 + KV_WIDTH
VT_ROW0 = NSA_WIDTH
GT_ROW0 = VT_ROW0 + 2 * KV_WIDTH
WT_ROWS = GT_ROW0 + NSA_KV_HEADS * GATE_ROWS


def _rope_chunk(xc, cos, sin_signed):
    lane = lax.broadcasted_iota(jnp.int32, xc.shape, 1)
    first = (lane & (HEAD_DIM - 1)) < (HEAD_DIM // 2)
    partner = jnp.where(first, pltpu.roll(xc, LANES - HEAD_DIM // 2, axis=1),
                        pltpu.roll(xc, HEAD_DIM // 2, axis=1))
    return xc * cos + partner * sin_signed


def _proj_kernel(h_ref, wn_ref, wt_ref, cos_ref, sin_ref, cost_ref, sint_ref, cw_ref, cb_ref,
                 lru_ref, qt_ref, cmp_ref, k_ref, vt_ref, gt_ref, seg_scr, conv_scr, *, seq_tiles):
    tm = h_ref.shape[0]
    half = HEAD_DIM // 2

    @pl.when(pl.program_id(0) % seq_tiles == 0)
    def _():
        conv_scr[0:SUBLANES, :] = jnp.zeros((SUBLANES, LRU_WIDTH), F32)

    hb = h_ref[...].astype(BF16)
    p = _dot(hb, wn_ref[...])
    pt = _dot_nt(wt_ref[...], hb)

    x = p[:, :LRU_WIDTH]
    conv_scr[SUBLANES:, :] = x
    xc = cb_ref[...]
    for d in range(CONV_WIDTH):
        xs = conv_scr[SUBLANES - d:SUBLANES - d + tm, :]
        xc = xc + xs * cw_ref[CONV_WIDTH - 1 - d:CONV_WIDTH - d, :]
    conv_scr[0:SUBLANES, :] = x[tm - SUBLANES:, :]
    lru_ref[:, :LRU_WIDTH] = xc
    lru_ref[:, LRU_WIDTH:] = jax.nn.gelu(p[:, LRU_WIDTH:Q_COL0])

    cos_t = cost_ref[...]
    sin_t = sint_ref[...]
    scale = HEAD_DIM ** -0.5 * LOG2E
    for hd in range(NSA_Q_HEADS):
        x1 = pt[hd * HEAD_DIM:hd * HEAD_DIM + half]
        x2 = pt[hd * HEAD_DIM + half:(hd + 1) * HEAD_DIM]
        qt_ref[hd] = (jnp.concatenate([x1 * cos_t - x2 * sin_t, x2 * cos_t + x1 * sin_t], axis=0)
                      * scale).astype(BF16)
    ones = jnp.ones((V_ROWS - HEAD_DIM, tm), F32)
    for c in range(2 * NSA_KV_HEADS):
        v = pt[VT_ROW0 + c * HEAD_DIM:VT_ROW0 + (c + 1) * HEAD_DIM]
        vt_ref[c] = jnp.concatenate([v, ones], axis=0).astype(BF16)
    for g in range(NSA_KV_HEADS):
        gt_ref[g] = jax.nn.sigmoid(pt[GT_ROW0 + g * GATE_ROWS:GT_ROW0 + (g + 1) * GATE_ROWS])

    cos = cos_ref[...]
    sin = sin_ref[...]
    low_seg = lax.broadcasted_iota(jnp.int32, (tm // CMP_STRIDE, LANES), 1) < HEAD_DIM
    for c, xc in enumerate((_rope_chunk(p[:, Q_COL0:Q_COL0 + LANES], cos, sin),
                            p[:, Q_COL0 + LANES:Q_COL0 + 2 * LANES])):
        seg_scr[...] = xc
        tok = [seg_scr[pl.ds(j, tm // CMP_STRIDE, stride=CMP_STRIDE), :]
               for j in range(CMP_STRIDE)]
        g0, g1 = [], []
        for j in range(0, CMP_STRIDE, 2):
            g0.append(jnp.where(low_seg, tok[j], pltpu.roll(tok[j + 1], HEAD_DIM, axis=1)))
            g1.append(jnp.where(low_seg, pltpu.roll(tok[j], HEAD_DIM, axis=1), tok[j + 1]))
        cmp_ref[NSA_KV_HEADS * c] = jnp.concatenate(g0, axis=1).astype(BF16)
        cmp_ref[NSA_KV_HEADS * c + 1] = jnp.concatenate(g1, axis=1).astype(BF16)
    lane = lax.broadcasted_iota(jnp.int32, (tm, LANES), 1)
    low = lane < HEAD_DIM
    pos = (pl.program_id(0) % seq_tiles) * tm + lax.broadcasted_iota(jnp.int32, (tm, LANES), 0)
    ext_blk = jnp.where(lane - HEAD_DIM == (pos >> SLC_SHIFT), NEG, 0.0)
    ext_zero = jnp.zeros((tm, LANES), F32)
    for c, ext in enumerate((ext_blk, ext_zero)):
        xc = _rope_chunk(p[:, Q_COL0 + (2 + c) * LANES:Q_COL0 + (3 + c) * LANES], cos, sin)
        k_ref[2 * c] = jnp.where(low, xc, ext).astype(BF16)
        k_ref[2 * c + 1] = jnp.where(low, pltpu.roll(xc, HEAD_DIM, axis=1), ext).astype(BF16)


def _proj(h, w_in, conv_w, conv_b, S, *, tm=1024):
    T, D = h.shape
    G = NSA_KV_HEADS
    col = lambda c0, n: w_in[:, c0:c0 + n]
    w_nat = jnp.concatenate(
        [col(0, Q_COL0), col(KCMP_COL0, KV_WIDTH), col(VCMP_COL0, KV_WIDTH),
         col(KSLC_COL0, KV_WIDTH), col(KWIN_COL0, KV_WIDTH)], axis=1).astype(BF16)
    gate_rows = [jnp.pad(col(GATE_COL0 + g * N_GATE, N_GATE).T, ((0, GATE_ROWS - N_GATE), (0, 0)))
                 for g in range(G)]
    w_t = jnp.concatenate(
        [col(Q_COL0, NSA_WIDTH).T, col(VSLC_COL0, KV_WIDTH).T, col(VWIN_COL0, KV_WIDTH).T]
        + gate_rows, axis=0).astype(BF16)
    n_nat = w_nat.shape[1]

    half = HEAD_DIM // 2
    inv = ROPE_THETA ** (-jnp.arange(half, dtype=F32) / half)
    ang = jnp.arange(S, dtype=F32)[:, None] * inv[None, :]
    cos = jnp.cos(ang)
    sin = jnp.sin(ang)
    cos_n = jnp.concatenate([cos, cos, cos, cos], axis=1)
    sin_n = jnp.concatenate([-sin, sin, -sin, sin], axis=1)

    nS = S // tm
    seg_w = CMP_STRIDE * HEAD_DIM
    assert G * HEAD_DIM == LANES
    return pl.pallas_call(
        functools.partial(_proj_kernel, seq_tiles=nS),
        grid=(T // tm,),
        in_specs=[
            pl.BlockSpec((tm, D), lambda i: (i, 0)),
            _resident((D, n_nat)), _resident((WT_ROWS, D)),
            pl.BlockSpec((tm, LANES), lambda i: (i % nS, 0)),
            pl.BlockSpec((tm, LANES), lambda i: (i % nS, 0)),
            pl.BlockSpec((half, tm), lambda i: (0, i % nS)),
            pl.BlockSpec((half, tm), lambda i: (0, i % nS)),
            _resident((CONV_WIDTH, LRU_WIDTH)), _resident((1, LRU_WIDTH)),
        ],
        out_specs=[
            pl.BlockSpec((tm, Q_COL0), lambda i: (i, 0)),
            pl.BlockSpec((NSA_Q_HEADS, HEAD_DIM, tm), lambda i: (0, 0, i)),
            pl.BlockSpec((2 * G, tm // CMP_STRIDE, seg_w), lambda i: (0, i, 0)),
            pl.BlockSpec((2 * G, tm, LANES), lambda i: (0, i, 0)),
            pl.BlockSpec((2 * G, V_ROWS, tm), lambda i: (0, 0, i)),
            pl.BlockSpec((G, GATE_ROWS, tm), lambda i: (0, 0, i)),
        ],
        out_shape=[
            jax.ShapeDtypeStruct((T, Q_COL0), F32),
            jax.ShapeDtypeStruct((NSA_Q_HEADS, HEAD_DIM, T), BF16),
            jax.ShapeDtypeStruct((2 * G, T // CMP_STRIDE, seg_w), BF16),
            jax.ShapeDtypeStruct((2 * G, T, LANES), BF16),
            jax.ShapeDtypeStruct((2 * G, V_ROWS, T), BF16),
            jax.ShapeDtypeStruct((G, GATE_ROWS, T), F32),
        ],
        scratch_shapes=[pltpu.VMEM((tm, LANES), F32),
                        pltpu.VMEM((SUBLANES + tm, LRU_WIDTH), F32)],
        compiler_params=pltpu.CompilerParams(
            dimension_semantics=("arbitrary",), vmem_limit_bytes=VMEM_LIMIT),
        name="proj",
    )(h, w_nat, w_t, cos_n, sin_n, cos.T, sin.T, conv_w, conv_b.reshape(1, LRU_WIDTH))


SCAN_UNROLL = 8
LRU_ROWS = 512


def _lru_kernel(x_ref, gate_ref, wa_ref, ba_ref, wx_ref, bx_ref, lam_ref, y_ref, a_scr, b_scr):
    S, C = y_ref.shape
    neg_lam = -lam_ref[...]
    softplus = jnp.maximum(neg_lam, 0.0) + jnp.log1p(jnp.exp(-jnp.abs(neg_lam)))

    for r0 in range(0, S, LRU_ROWS):
        xc = x_ref[r0:r0 + LRU_ROWS, :]
        xb = xc.astype(BF16)
        gate_pre = lambda w_ref: jnp.concatenate(
            [_dot(xb[:, c * LANES:(c + 1) * LANES], w_ref[c]) for c in range(C // LANES)], axis=1)
        r = 0.5 * jnp.tanh(0.5 * (gate_pre(wa_ref) + ba_ref[...])) + 0.5
        ig = 0.5 * jnp.tanh(0.5 * (gate_pre(wx_ref) + bx_ref[...])) + 0.5
        log_a = (-LRU_C) * r * softplus
        a = jnp.exp(log_a)
        z = jnp.tanh(-log_a) * (a * a + 1.0)
        mult = z * lax.rsqrt(jnp.maximum(z, F32_TINY))
        if r0 == 0:
            row = lax.broadcasted_iota(jnp.int32, mult.shape, 0)
            mult = jnp.where(row == 0, 1.0, mult)
        a_scr[r0:r0 + LRU_ROWS, :] = a
        b_scr[r0:r0 + LRU_ROWS, :] = mult * (ig * xc)

    sub = lax.broadcasted_iota(jnp.int32, (SUBLANES, C), 0)

    def scan_tile(g, h_prev):
        rows = pl.ds(pl.multiple_of(g * SUBLANES, SUBLANES), SUBLANES)
        at = a_scr[rows, :]
        bt = b_scr[rows, :]
        d = 1
        while d < SUBLANES:
            keep = sub >= d
            a_sh = jnp.where(keep, pltpu.roll(at, d, axis=0), 1.0)
            b_sh = jnp.where(keep, pltpu.roll(bt, d, axis=0), 0.0)
            bt = at * b_sh + bt
            at = at * a_sh
            d *= 2
        h = bt + at * h_prev
        b_scr[rows, :] = h
        return jnp.broadcast_to(h[SUBLANES - 1:SUBLANES, :], (SUBLANES, C))

    def scan_body(i, h_prev):
        for u in range(SCAN_UNROLL):
            h_prev = scan_tile(i * SCAN_UNROLL + u, h_prev)
        return h_prev

    lax.fori_loop(0, S // (SUBLANES * SCAN_UNROLL), scan_body, jnp.zeros((SUBLANES, C), F32))

    for r0 in range(0, S, LRU_ROWS):
        rows = slice(r0, r0 + LRU_ROWS)
        y_ref[rows, :] = b_scr[rows, :] * gate_ref[rows, :]


def _lru(lru, w_a, b_a, w_x, b_x, lam, B, S):
    T = lru.shape[0]
    C = LRU_WIDTH
    n_ch = C // LANES
    assert S % (SUBLANES * SCAN_UNROLL) == 0 and S % LRU_ROWS == 0

    def blockdiag(w):
        w = w.reshape(n_ch, 2, LRU_BLOCK, LRU_BLOCK)
        z = jnp.zeros((n_ch, LRU_BLOCK, LRU_BLOCK), w.dtype)
        top = jnp.concatenate([w[:, 0], z], axis=2)
        bot = jnp.concatenate([z, w[:, 1]], axis=2)
        return jnp.concatenate([top, bot], axis=1).astype(BF16)

    vec = lambda v: v.reshape(1, C)
    return pl.pallas_call(
        _lru_kernel,
        grid=(B,),
        in_specs=[
            pl.BlockSpec((S, C), lambda b: (b, 0)),
            pl.BlockSpec((S, C), lambda b: (b, 1)),
            _resident((n_ch, LANES, LANES)), _resident((1, C)),
            _resident((n_ch, LANES, LANES)), _resident((1, C)),
            _resident((1, C)),
        ],
        out_specs=pl.BlockSpec((S, C), lambda b: (b, 0)),
        out_shape=jax.ShapeDtypeStruct((T, C), F32),
        scratch_shapes=[pltpu.VMEM((S, C), F32),
                        pltpu.VMEM((S, C), F32)],
        compiler_params=pltpu.CompilerParams(
            dimension_semantics=("parallel",), vmem_limit_bytes=VMEM_LIMIT),
        name="lru",
    )(lru, lru, blockdiag(w_a), vec(b_a.reshape(-1)),
      blockdiag(w_x), vec(b_x.reshape(-1)), vec(lam))


def _cmp_kernel(seg_ref, pek_ref, w1k_ref, w2k_ref, pev_ref, w1v_ref, w2vt_ref, kc_ref, vct_ref):
    n_seg = seg_ref.shape[2]
    half = CMP_STRIDE * HEAD_DIM

    def hidden(idx, pe_ref, w1_ref):
        seg = seg_ref[idx, 0]
        first = _dot(seg, w1_ref[0:half, :])
        second = _dot(seg, w1_ref[half:2 * half, :])
        bias = _dot(pe_ref[...], w1_ref[...])[0:1, :]
        return _silu(first + pltpu.roll(second, n_seg - 1, axis=0) + bias).astype(BF16)

    for g in range(NSA_KV_HEADS):
        tok = _dot(hidden(g, pek_ref, w1k_ref), w2k_ref[...])
        kc_ref[0, g] = jnp.concatenate([tok, jnp.zeros_like(tok)], axis=1).astype(BF16)
        vct_ref[0, g] = _dot_nt(w2vt_ref[...], hidden(NSA_KV_HEADS + g, pev_ref, w1v_ref)
                                ).astype(BF16)


def _compress(cmp_in, pe_k, w1_k, w2_k, pe_v, w1_v, w2_v, B, S):
    G = NSA_KV_HEADS
    n_seg = S // CMP_STRIDE
    seg = cmp_in.reshape(2 * G, B, n_seg, CMP_STRIDE * HEAD_DIM)
    flat = CMP_BLOCK * HEAD_DIM
    pe8 = lambda pe: jnp.broadcast_to(pe.reshape(1, -1), (8, flat)).astype(BF16)
    return pl.pallas_call(
        _cmp_kernel,
        grid=(B,),
        in_specs=[
            pl.BlockSpec((2 * G, 1, n_seg, CMP_STRIDE * HEAD_DIM), lambda b: (0, b, 0, 0)),
            _resident((8, flat)), _resident((flat, CMP_HIDDEN)), _resident((CMP_HIDDEN, HEAD_DIM)),
            _resident((8, flat)), _resident((flat, CMP_HIDDEN)), _resident((HEAD_DIM, CMP_HIDDEN)),
        ],
        out_specs=[
            pl.BlockSpec((1, G, n_seg, LANES), lambda b: (b, 0, 0, 0)),
            pl.BlockSpec((1, G, HEAD_DIM, n_seg), lambda b: (b, 0, 0, 0)),
        ],
        out_shape=[
            jax.ShapeDtypeStruct((B, G, n_seg, LANES), BF16),
            jax.ShapeDtypeStruct((B, G, HEAD_DIM, n_seg), BF16),
        ],
        compiler_params=pltpu.CompilerParams(
            dimension_semantics=("parallel",), vmem_limit_bytes=VMEM_LIMIT),
        name="compress",
    )(seg, pe8(pe_k), w1_k.astype(BF16), w2_k.astype(BF16),
      pe8(pe_v), w1_v.astype(BF16), w2_v.T.astype(BF16))


def _attn_kernel(qt_ref, kc_ref, vct_ref, ks_ref, vst_ref, kw_ref, vwt_ref, gt_ref, ovl_ref,
                 bias_ref, *rest, tq, tiles):
    o_ref = rest[-1]

    def step(n):
        for _ in _interleave(*[
                _attn_step(n, g, qt_ref, kc_ref.at[g], vct_ref.at[g], ks_ref.at[g], vst_ref.at[g],
                           kw_ref.at[g], vwt_ref.at[g], gt_ref.at[g], ovl_ref, bias_ref, o_ref, tq)
                for g in range(NSA_KV_HEADS)]):
            pass

    for n in range(*tiles):
        pl.when(pl.program_id(1) == n - tiles[0])(functools.partial(step, n))


CHUNK_TILES = 1
SCORES_AHEAD = 1
ATTN_TILE_SPLITS = ((0, 3), (3, 5), (5, 7), (7, 8))


def _interleave(*stages):
    live = list(stages)
    while live:
        for st in list(live):
            try:
                next(st)
            except StopIteration:
                live.remove(st)
        yield


def _tile_bias(bias_ref, t0, k0, behind):
    if k0 == t0:
        return bias_ref[0]
    if behind is not None and k0 == t0 - behind:
        return bias_ref[1]
    return None


def _key_chunks(k_ref, vt_ref, k0, n_keys, tile, q, bias_fn):
    starts = list(range(k0, k0 + n_keys, tile))
    return [dict(k_ref=k_ref, vt_ref=vt_ref, tile=tile, q=q, bias_fn=bias_fn,
                 starts=starts[i:i + CHUNK_TILES]) for i in range(0, len(starts), CHUNK_TILES)]


def _scores_stage(ch):
    tiles, m = [], None
    for k0 in ch["starts"]:
        s = _dot(ch["k_ref"][k0:k0 + ch["tile"], :], ch["q"])
        bias = ch["bias_fn"](k0)
        if bias is not None:
            s = s + bias
        cm = jnp.max(s, axis=0, keepdims=True)
        m = cm if m is None else jnp.maximum(m, cm)
        tiles.append(s)
        yield
    ch["s"], ch["m"] = tiles, m


def _pv_stage(ch, heads, tq):
    acc = [None] * heads
    for k0, s in zip(ch["starts"], ch["s"]):
        p = jnp.exp2(s - ch["m"]).astype(BF16)
        vt = ch["vt_ref"][:, k0:k0 + ch["tile"]]
        for r in range(heads):
            d = _dot(vt, p[:, r * tq:(r + 1) * tq])
            acc[r] = d if acc[r] is None else acc[r] + d
        yield
    ch["acc"] = acc


def _merge_chunks(chunks, heads, tq):
    if len(chunks) == 1:
        return chunks[0]["acc"]
    m = functools.reduce(jnp.maximum, [ch["m"] for ch in chunks])
    acc = [None] * heads
    for ch in chunks:
        w = jnp.exp2(ch["m"] - m)
        for r in range(heads):
            term = ch["acc"][r] * w[:, r * tq:(r + 1) * tq]
            acc[r] = term if acc[r] is None else acc[r] + term
    return acc


def _rank_stage(out, score, jblk, n_live):
    rank = jnp.zeros(score.shape, jnp.int32)
    for kb in range(n_live):
        sk = score[kb:kb + 1, :]
        ahead = (sk > score) | ((sk == score) & (jblk > kb))
        rank = rank + ahead.astype(jnp.int32)
        if kb % 4 == 3:
            yield
    out["rank"] = rank


def _attn_step(n, g, qt_ref, kc_ref, vct_ref, ks_ref, vst_ref, kw_ref, vwt_ref, gt_ref, ovl_ref,
               bias_ref, o_ref, tq):
    R = NSA_GROUP
    t0 = n * tq
    n_cmp = kc_ref.shape[0]
    n_blk = ovl_ref.shape[0]
    q_all = jnp.concatenate([qt_ref[g * R + r] for r in range(R)], axis=1)
    q_pad = jnp.concatenate([q_all, jnp.zeros((LANES - HEAD_DIM, R * tq), BF16)], axis=0)
    head = lambda a, r: a[:, r * tq:(r + 1) * tq]

    kw0 = max(t0 - WINDOW, 0)
    win = _key_chunks(kw_ref, vwt_ref, kw0, t0 + tq - kw0, tq, q_pad,
                      lambda k0: _tile_bias(bias_ref, t0, k0, WINDOW))
    win_scores = _scores_stage(win[0])
    next(win_scores)
    yield

    n_vis = (t0 + tq - CMP_BLOCK) // CMP_STRIDE + 1
    n_vis = min(n_cmp, -(-n_vis // BF16_ROWS) * BF16_ROWS)
    tpos_c = t0 + (lax.broadcasted_iota(jnp.int32, (n_vis, R * tq), 1) & (tq - 1))
    cend = lax.broadcasted_iota(jnp.int32, (n_vis, R * tq), 0) * CMP_STRIDE + (CMP_BLOCK - 1)
    s = _dot(kc_ref[0:n_vis, :], q_pad) + jnp.where(cend <= tpos_c, 0.0, NEG)
    e = jnp.exp2(s - jnp.max(s, axis=0, keepdims=True))
    p_cmp = e * (1.0 / jnp.sum(e, axis=0, keepdims=True))
    if t0 < CMP_BLOCK - 1:
        p_cmp = jnp.where(tpos_c >= CMP_BLOCK - 1, p_cmp, 0.0)
    if n_vis < n_cmp:
        p_cmp = jnp.concatenate([p_cmp, jnp.zeros((n_cmp - n_vis, R * tq), F32)], axis=0)
    p_cmp_b = p_cmp.astype(BF16)
    vct = vct_ref[...]
    o_cmp = [_dot(vct, head(p_cmp_b, r)) for r in range(R)]
    psum = functools.reduce(jnp.add, [head(p_cmp, r) for r in range(R)])
    yield

    p_hi = psum.astype(BF16)
    p_lo = (psum - p_hi.astype(F32)).astype(BF16)
    ovl = ovl_ref[...]
    imp = _dot(ovl, p_hi) + _dot(ovl, p_lo)
    n_live = min(n_blk, (t0 + tq - 1) // SLC_BLOCK + 1)
    rows = -(-n_live // SUBLANES) * SUBLANES
    jblk = lax.broadcasted_iota(jnp.int32, (rows, tq), 0)
    tpos_b = t0 + lax.broadcasted_iota(jnp.int32, (rows, tq), 1)
    blk_valid = jblk * SLC_BLOCK <= tpos_b
    back = (tpos_b >> SLC_SHIFT) - jblk
    forced = (jblk == 0) | ((back >= 0) & (back < N_LOCAL_BLOCKS))
    score = jnp.where(blk_valid, jnp.where(forced, jnp.inf, imp[0:rows]), -jnp.inf)
    sel = {}
    yield from _interleave(_rank_stage(sel, score, jblk, n_live), win_scores)
    unsel = 1.0 - (blk_valid & (sel["rank"] < SLC_TOP_N)).astype(F32)
    dead = [jnp.ones((n_blk - rows, tq), F32)] if rows < n_blk else []
    sel_ext = jnp.concatenate(
        [unsel] + dead + [jnp.zeros((LANES - HEAD_DIM - n_blk, tq), F32)],
        axis=0).astype(BF16)
    q_sel = jnp.concatenate([q_all, jnp.concatenate([sel_ext] * R, axis=1)], axis=0)

    slc = _key_chunks(ks_ref, vst_ref, 0, t0 + tq, tq, q_sel,
                      lambda k0: _tile_bias(bias_ref, t0, k0, None))
    chunks = win + slc
    for ch in chunks[1:SCORES_AHEAD]:
        yield from _scores_stage(ch)
    for idx, ch in enumerate(chunks):
        stages = [_pv_stage(ch, R, tq)]
        if idx + SCORES_AHEAD < len(chunks):
            stages.append(_scores_stage(chunks[idx + SCORES_AHEAD]))
        yield from _interleave(*stages)
    acc_w = _merge_chunks(win, R, tq)
    acc_s = _merge_chunks(slc, R, tq)
    yield

    gt = gt_ref[...]
    outs = []
    for r in range(R):
        g_cmp = gt[3 * r:3 * r + 1]
        g_slc = gt[3 * r + 1:3 * r + 2] * (1.0 / acc_s[r][HEAD_DIM:HEAD_DIM + 1])
        g_win = gt[3 * r + 2:3 * r + 3] * (1.0 / acc_w[r][HEAD_DIM:HEAD_DIM + 1])
        outs.append(g_cmp * o_cmp[r] + g_slc * acc_s[r][:HEAD_DIM] + g_win * acc_w[r][:HEAD_DIM])
    width = R * HEAD_DIM
    o_ref[:, g * width:(g + 1) * width] = jnp.concatenate(outs, axis=0).T


def _overlap_t(n_cmp_pad, n_blk):
    cs = np.arange(n_cmp_pad) * CMP_STRIDE
    ce = cs + CMP_BLOCK - 1
    ss = np.arange(n_blk) * SLC_BLOCK
    se = ss + SLC_BLOCK - 1
    return ((cs[None, :] <= se[:, None]) & (ce[None, :] >= ss[:, None])).astype(np.float32)


def _attention(qt, k, vt, kc, vct, gates_t, B, S, *, tq=256):
    T = qt.shape[2]
    nq = S // tq
    n_cmp = S // CMP_STRIDE
    n_blk = S // SLC_BLOCK
    G = NSA_KV_HEADS
    ovl = jnp.asarray(_overlap_t(n_cmp, n_blk), BF16)
    assert WINDOW % tq == 0 and WINDOW + tq <= S
    assert HEAD_DIM + n_blk <= LANES
    assert tq & (tq - 1) == 0
    key_off = np.arange(tq)[:, None]
    q_off = np.tile(np.arange(tq), NSA_GROUP)[None, :]
    bias = jnp.asarray(np.stack([np.where(key_off <= q_off, 0.0, NEG),
                                 np.where(key_off > q_off, 0.0, NEG)]), F32)
    k_spec = lambda c: pl.BlockSpec((G, S, LANES), lambda b, i: (c, b, 0))
    vt_spec = lambda c: pl.BlockSpec((G, V_ROWS, S), lambda b, i: (c, 0, b))

    def call(tiles, prev):
        lo, hi = tiles
        tile = lambda b, i: b * nq + lo + i
        in_specs = [
            pl.BlockSpec((NSA_Q_HEADS, HEAD_DIM, tq), lambda b, i: (0, 0, tile(b, i))),
            pl.BlockSpec((None, G, n_cmp, LANES), lambda b, i: (b, 0, 0, 0)),
            pl.BlockSpec((None, G, HEAD_DIM, n_cmp), lambda b, i: (b, 0, 0, 0)),
            k_spec(0), vt_spec(0), k_spec(1), vt_spec(1),
            pl.BlockSpec((G, GATE_ROWS, tq), lambda b, i: (0, 0, tile(b, i))),
            _resident((n_blk, n_cmp)),
            _resident((2, tq, NSA_GROUP * tq)),
        ]
        args = [qt, kc, vct, k, vt, k, vt, gates_t, ovl, bias]
        aliases = {}
        if prev is not None:
            in_specs.append(pl.BlockSpec(memory_space=pl.ANY))
            aliases = {len(args): 0}
            args.append(prev)
        return pl.pallas_call(
            functools.partial(_attn_kernel, tq=tq, tiles=tiles),
            grid=(B, hi - lo),
            in_specs=in_specs,
            out_specs=pl.BlockSpec((tq, NSA_WIDTH), lambda b, i: (tile(b, i), 0)),
            out_shape=jax.ShapeDtypeStruct((T, NSA_WIDTH), F32),
            input_output_aliases=aliases,
            compiler_params=pltpu.CompilerParams(
                dimension_semantics=("parallel", "arbitrary"),
                vmem_limit_bytes=VMEM_LIMIT),
            name=f"nsa_attn_{lo}_{hi}",
        )(*args)

    assert [t for lo, hi in ATTN_TILE_SPLITS for t in range(lo, hi)] == list(range(nq))
    out = None
    for tiles in ATTN_TILE_SPLITS:
        out = call(tiles, out)
    return out


def _out_ffn_kernel(ylru_ref, ynsa_ref, h_ref, gl_ref, gn_ref, w_ref, g2_ref, b2_ref,
                    w_in_hbm, w_out_hbm, g3_ref, b3_ref, o_ref, wg_scr, wu_scr, wo_scr, *staging):
    @pl.when(pl.program_id(0) == 0)
    def _():
        _stage_ffn_weights(w_in_hbm, w_out_hbm, wg_scr, wu_scr, wo_scr, *staging)

    yl = _rms_norm(ylru_ref[...], gl_ref[...]).astype(BF16)
    yn = _rms_norm(ynsa_ref[...], gn_ref[...]).astype(BF16)
    mix = _dot(yl, w_ref[0:LRU_WIDTH, :]) + _dot(yn, w_ref[LRU_WIDTH:, :])
    h2 = _layer_norm(ALPHA * h_ref[...] + mix, g2_ref[...], b2_ref[...])
    o_ref[...] = _ffn_half_step(h2, wg_scr, wu_scr, wo_scr, g3_ref, b3_ref)


def _out_proj_ffn(y_lru, y_nsa, h, gn_lru, gn_nsa, w_mix, g2, b2, w_in, w_out, g3, b3, *,
                  tm=512):
    T, D = h.shape
    row = lambda v: v.reshape(1, -1)
    return pl.pallas_call(
        _out_ffn_kernel,
        grid=(T // tm,),
        in_specs=[
            pl.BlockSpec((tm, LRU_WIDTH), lambda i: (i, 0)),
            pl.BlockSpec((tm, NSA_WIDTH), lambda i: (i, 0)),
            pl.BlockSpec((tm, D), lambda i: (i, 0)),
            _resident((1, LRU_WIDTH)), _resident((1, NSA_WIDTH)),
            _resident((LRU_WIDTH + NSA_WIDTH, D)),
            _resident((1, D)), _resident((1, D)),
            pl.BlockSpec(memory_space=pl.ANY), pl.BlockSpec(memory_space=pl.ANY),
            _resident((1, D)), _resident((1, D)),
        ],
        out_specs=pl.BlockSpec((tm, D), lambda i: (i, 0)),
        out_shape=jax.ShapeDtypeStruct((T, D), F32),
        scratch_shapes=_ffn_weight_scratch(D),
        compiler_params=pltpu.CompilerParams(
            dimension_semantics=("arbitrary",), vmem_limit_bytes=FFN_VMEM_LIMIT),
        name="out_proj_ffn",
    )(y_lru, y_nsa, h, row(gn_lru), row(gn_nsa), w_mix.astype(BF16), row(g2), row(b2),
      w_in, w_out, row(g3), row(b3))


def kernel(x, ffn1_w_in, ffn1_w_out, ln1_g, ln1_b, mix_w_in, conv_w, conv_b, lru_w_a, lru_b_a,
           lru_w_x, lru_b_x, lru_lam, cmp_pe_k, cmp_w1_k, cmp_w2_k, cmp_pe_v, cmp_w1_v,
           cmp_w2_v, gn_lru, gn_nsa, mix_w_out, ln2_g, ln2_b, ffn2_w_in, ffn2_w_out, ln3_g,
           ln3_b):
    B, S, D = x.shape
    h = x.reshape(B * S, D)
    for l in range(DEPTH):
        h = _ffn_ln(h, ffn1_w_in[l], ffn1_w_out[l], ln1_g[l], ln1_b[l])
        lru, qt, cmp_in, k, vt, gates_t = _proj(h, mix_w_in[l], conv_w[l], conv_b[l], S)
        y_lru = _lru(lru, lru_w_a[l], lru_b_a[l], lru_w_x[l], lru_b_x[l], lru_lam[l], B, S)
        kc, vct = _compress(cmp_in, cmp_pe_k[l], cmp_w1_k[l], cmp_w2_k[l],
                            cmp_pe_v[l], cmp_w1_v[l], cmp_w2_v[l], B, S)
        y_nsa = _attention(qt, k, vt, kc, vct, gates_t, B, S)
        h = _out_proj_ffn(y_lru, y_nsa, h, gn_lru[l], gn_nsa[l], mix_w_out[l], ln2_g[l], ln2_b[l],
                          ffn2_w_in[l], ffn2_w_out[l], ln3_g[l], ln3_b[l])
    return h.reshape(B, S, D)
```

```python
import functools

import numpy as np
import jax
import jax.numpy as jnp
from jax import lax
from jax.experimental import pallas as pl
from jax.experimental.pallas import tpu as pltpu

F32 = jnp.float32
BF16 = jnp.bfloat16

D_MODEL = 1024
LRU_WIDTH = 512
LRU_HEADS = 8
LRU_BLOCK = 64
CONV_WIDTH = 4
LRU_C = 8.0
NSA_Q_HEADS = 8
NSA_KV_HEADS = 2
NSA_GROUP = 4
HEAD_DIM = 64
NSA_WIDTH = 512
KV_WIDTH = 128
CMP_BLOCK = 32
CMP_STRIDE = 16
CMP_HIDDEN = 256
SLC_BLOCK = 64
SLC_SHIFT = 6
SLC_TOP_N = 16
N_LOCAL_BLOCKS = 2
WINDOW = 512
ROPE_THETA = 10000.0
D_FF = 2816
DEPTH = 1
ALPHA = (2.0 * DEPTH) ** 0.25
LN_EPS = 1e-5
RMS_EPS = 1e-6
NEG = -1e30
LOG2E = 1.4426950408889634
F32_TINY = 1.1754943508222875e-38

LANES = 128
SUBLANES = 8
BF16_ROWS = 16
VMEM_LIMIT = 48 * 1024 * 1024
N_GATE = 3 * NSA_GROUP
GATE_ROWS = 16
V_ROWS = HEAD_DIM + BF16_ROWS


def _dot(a, b):
    return jnp.dot(a, b, preferred_element_type=F32)


def _dot_nt(a, b):
    return lax.dot_general(a, b, (((1,), (1,)), ((), ())), preferred_element_type=F32)


def _layer_norm(y, g, b):
    mu = jnp.mean(y, axis=-1, keepdims=True)
    d = y - mu
    var = jnp.mean(d * d, axis=-1, keepdims=True)
    return d * lax.rsqrt(var + LN_EPS) * g + b


def _rms_norm(y, g):
    return y * lax.rsqrt(jnp.mean(y * y, axis=-1, keepdims=True) + RMS_EPS) * g


def _silu(x):
    return x * jax.nn.sigmoid(x)


def _resident(shape):
    return pl.BlockSpec(shape, lambda *_: (0,) * len(shape), pipeline_mode=pl.Buffered(1))


def _ffn_half_step(x, wg_ref, wu_ref, wo_ref, g_ref, b_ref):
    xb = x.astype(BF16)
    gate = _dot(xb, wg_ref[...])
    up = _dot(xb, wu_ref[...])
    act = (_silu(gate) * up).astype(BF16)
    y = ALPHA * x + 0.5 * _dot(act, wo_ref[...])
    return _layer_norm(y, g_ref[...], b_ref[...])


W_IN_ROWS = 64
W_OUT_ROWS = 352
FFN_VMEM_LIMIT = 56 * 1024 * 1024


def _stream_rows(w_hbm, rows, slots, sems, consume):
    def copy(c):
        return pltpu.make_async_copy(w_hbm.at[pl.ds(c * rows, rows), :], slots[c % 2], sems[c % 2])

    n = w_hbm.shape[0] // rows
    copy(0).start()
    for c in range(n):
        if c + 1 < n:
            copy(c + 1).start()
        copy(c).wait()
        consume(c * rows, slots[c % 2])


def _stage_ffn_weights(w_in_hbm, w_out_hbm, wg_scr, wu_scr, wo_scr, in0, in1, out0, out1,
                       sem_in0, sem_in1, sem_out0, sem_out1):
    def put_in(r0, buf):
        wg_scr[r0:r0 + W_IN_ROWS, :] = buf[:, :D_FF].astype(BF16)
        wu_scr[r0:r0 + W_IN_ROWS, :] = buf[:, D_FF:].astype(BF16)

    def put_out(r0, buf):
        wo_scr[r0:r0 + W_OUT_ROWS, :] = buf[...].astype(BF16)

    _stream_rows(w_in_hbm, W_IN_ROWS, (in0, in1), (sem_in0, sem_in1), put_in)
    _stream_rows(w_out_hbm, W_OUT_ROWS, (out0, out1), (sem_out0, sem_out1), put_out)


def _ffn_weight_scratch(D):
    assert D % W_IN_ROWS == 0 and D_FF % W_OUT_ROWS == 0
    return ([pltpu.VMEM((D, D_FF), BF16), pltpu.VMEM((D, D_FF), BF16), pltpu.VMEM((D_FF, D), BF16)]
            + [pltpu.VMEM((W_IN_ROWS, 2 * D_FF), F32)] * 2 + [pltpu.VMEM((W_OUT_ROWS, D), F32)] * 2
            + [pltpu.SemaphoreType.DMA(())] * 4)


def _ffn_ln_kernel(x_ref, w_in_hbm, w_out_hbm, g_ref, b_ref, o_ref, wg_scr, wu_scr, wo_scr,
                   *staging):
    @pl.when(pl.program_id(0) == 0)
    def _():
        _stage_ffn_weights(w_in_hbm, w_out_hbm, wg_scr, wu_scr, wo_scr, *staging)

    o_ref[...] = _ffn_half_step(x_ref[...], wg_scr, wu_scr, wo_scr, g_ref, b_ref)


def _ffn_ln(x, w_in, w_out, g, b, *, tm=1024):
    T, D = x.shape
    return pl.pallas_call(
        _ffn_ln_kernel,
        grid=(T // tm,),
        in_specs=[
            pl.BlockSpec((tm, D), lambda i: (i, 0)),
            pl.BlockSpec(memory_space=pl.ANY), pl.BlockSpec(memory_space=pl.ANY),
            _resident((1, D)), _resident((1, D)),
        ],
        out_specs=pl.BlockSpec((tm, D), lambda i: (i, 0)),
        out_shape=jax.ShapeDtypeStruct((T, D), F32),
        scratch_shapes=_ffn_weight_scratch(D),
        compiler_params=pltpu.CompilerParams(
            dimension_semantics=("arbitrary",), vmem_limit_bytes=FFN_VMEM_LIMIT),
        name="ffn_ln",
    )(x, w_in, w_out, g.reshape(1, D), b.reshape(1, D))


Q_COL0 = 2 * LRU_WIDTH
KCMP_COL0 = Q_COL0 + NSA_WIDTH
VCMP_COL0 = KCMP_COL0 + KV_WIDTH
KSLC_COL0 = VCMP_COL0 + KV_WIDTH
VSLC_COL0 = KSLC_COL0 + KV_WIDTH
KWIN_COL0 = VSLC_COL0 + KV_WIDTH
VWIN_COL0 = KWIN_COL0 + KV_WIDTH
GATE_COL0 = VWIN_COL0 + KV_WIDTH
VT_ROW0 = NSA_WIDTH
GT_ROW0 = VT_ROW0 + 2 * KV_WIDTH
WT_ROWS = GT_ROW0 + NSA_KV_HEADS * GATE_ROWS


def _rope_chunk(xc, cos, sin_signed):
    lane = lax.broadcasted_iota(jnp.int32, xc.shape, 1)
    first = (lane & (HEAD_DIM - 1)) < (HEAD_DIM // 2)
    partner = jnp.where(first, pltpu.roll(xc, LANES - HEAD_DIM // 2, axis=1),
                        pltpu.roll(xc, HEAD_DIM // 2, axis=1))
    return xc * cos + partner * sin_signed


def _proj_kernel(h_ref, wn_ref, wt_ref, cos_ref, sin_ref, cost_ref, sint_ref, cw_ref, cb_ref,
                 lru_ref, qt_ref, cmp_ref, k_ref, vt_ref, gt_ref, seg_scr, conv_scr, *, seq_tiles):
    tm = h_ref.shape[0]
    half = HEAD_DIM // 2

    @pl.when(pl.program_id(0) % seq_tiles == 0)
    def _():
        conv_scr[0:SUBLANES, :] = jnp.zeros((SUBLANES, LRU_WIDTH), F32)

    hb = h_ref[...].astype(BF16)
    p = _dot(hb, wn_ref[...])
    pt = _dot_nt(wt_ref[...], hb)

    x = p[:, :LRU_WIDTH]
    conv_scr[SUBLANES:, :] = x
    xc = cb_ref[...]
    for d in range(CONV_WIDTH):
        xs = conv_scr[SUBLANES - d:SUBLANES - d + tm, :]
        xc = xc + xs * cw_ref[CONV_WIDTH - 1 - d:CONV_WIDTH - d, :]
    conv_scr[0:SUBLANES, :] = x[tm - SUBLANES:, :]
    lru_ref[:, :LRU_WIDTH] = xc
    lru_ref[:, LRU_WIDTH:] = jax.nn.gelu(p[:, LRU_WIDTH:Q_COL0])

    cos_t = cost_ref[...]
    sin_t = sint_ref[...]
    scale = HEAD_DIM ** -0.5 * LOG2E
    for hd in range(NSA_Q_HEADS):
        x1 = pt[hd * HEAD_DIM:hd * HEAD_DIM + half]
        x2 = pt[hd * HEAD_DIM + half:(hd + 1) * HEAD_DIM]
        qt_ref[hd] = (jnp.concatenate([x1 * cos_t - x2 * sin_t, x2 * cos_t + x1 * sin_t], axis=0)
                      * scale).astype(BF16)
    ones = jnp.ones((V_ROWS - HEAD_DIM, tm), F32)
    for c in range(2 * NSA_KV_HEADS):
        v = pt[VT_ROW0 + c * HEAD_DIM:VT_ROW0 + (c + 1) * HEAD_DIM]
        vt_ref[c] = jnp.concatenate([v, ones], axis=0).astype(BF16)
    for g in range(NSA_KV_HEADS):
        gt_ref[g] = jax.nn.sigmoid(pt[GT_ROW0 + g * GATE_ROWS:GT_ROW0 + (g + 1) * GATE_ROWS])

    cos = cos_ref[...]
    sin = sin_ref[...]
    low_seg = lax.broadcasted_iota(jnp.int32, (tm // CMP_STRIDE, LANES), 1) < HEAD_DIM
    for c, xc in enumerate((_rope_chunk(p[:, Q_COL0:Q_COL0 + LANES], cos, sin),
                            p[:, Q_COL0 + LANES:Q_COL0 + 2 * LANES])):
        seg_scr[...] = xc
        tok = [seg_scr[pl.ds(j, tm // CMP_STRIDE, stride=CMP_STRIDE), :]
               for j in range(CMP_STRIDE)]
        g0, g1 = [], []
        for j in range(0, CMP_STRIDE, 2):
            g0.append(jnp.where(low_seg, tok[j], pltpu.roll(tok[j + 1], HEAD_DIM, axis=1)))
            g1.append(jnp.where(low_seg, pltpu.roll(tok[j], HEAD_DIM, axis=1), tok[j + 1]))
        cmp_ref[NSA_KV_HEADS * c] = jnp.concatenate(g0, axis=1).astype(BF16)
        cmp_ref[NSA_KV_HEADS * c + 1] = jnp.concatenate(g1, axis=1).astype(BF16)
    lane = lax.broadcasted_iota(jnp.int32, (tm, LANES), 1)
    low = lane < HEAD_DIM
    pos = (pl.program_id(0) % seq_tiles) * tm + lax.broadcasted_iota(jnp.int32, (tm, LANES), 0)
    ext_blk = jnp.where(lane - HEAD_DIM == (pos >> SLC_SHIFT), NEG, 0.0)
    ext_zero = jnp.zeros((tm, LANES), F32)
    for c, ext in enumerate((ext_blk, ext_zero)):
        xc = _rope_chunk(p[:, Q_COL0 + (2 + c) * LANES:Q_COL0 + (3 + c) * LANES], cos, sin)
        k_ref[2 * c] = jnp.where(low, xc, ext).astype(BF16)
        k_ref[2 * c + 1] = jnp.where(low, pltpu.roll(xc, HEAD_DIM, axis=1), ext).astype(BF16)


def _proj(h, w_in, conv_w, conv_b, S, *, tm=1024):
    T, D = h.shape
    G = NSA_KV_HEADS
    col = lambda c0, n: w_in[:, c0:c0 + n]
    w_nat = jnp.concatenate(
        [col(0, Q_COL0), col(KCMP_COL0, KV_WIDTH), col(VCMP_COL0, KV_WIDTH),
         col(KSLC_COL0, KV_WIDTH), col(KWIN_COL0, KV_WIDTH)], axis=1).astype(BF16)
    gate_rows = [jnp.pad(col(GATE_COL0 + g * N_GATE, N_GATE).T, ((0, GATE_ROWS - N_GATE), (0, 0)))
                 for g in range(G)]
    w_t = jnp.concatenate(
        [col(Q_COL0, NSA_WIDTH).T, col(VSLC_COL0, KV_WIDTH).T, col(VWIN_COL0, KV_WIDTH).T]
        + gate_rows, axis=0).astype(BF16)
    n_nat = w_nat.shape[1]

    half = HEAD_DIM // 2
    inv = ROPE_THETA ** (-jnp.arange(half, dtype=F32) / half)
    ang = jnp.arange(S, dtype=F32)[:, None] * inv[None, :]
    cos = jnp.cos(ang)
    sin = jnp.sin(ang)
    cos_n = jnp.concatenate([cos, cos, cos, cos], axis=1)
    sin_n = jnp.concatenate([-sin, sin, -sin, sin], axis=1)

    nS = S // tm
    seg_w = CMP_STRIDE * HEAD_DIM
    assert G * HEAD_DIM == LANES
    return pl.pallas_call(
        functools.partial(_proj_kernel, seq_tiles=nS),
        grid=(T // tm,),
        in_specs=[
            pl.BlockSpec((tm, D), lambda i: (i, 0)),
            _resident((D, n_nat)), _resident((WT_ROWS, D)),
            pl.BlockSpec((tm, LANES), lambda i: (i % nS, 0)),
            pl.BlockSpec((tm, LANES), lambda i: (i % nS, 0)),
            pl.BlockSpec((half, tm), lambda i: (0, i % nS)),
            pl.BlockSpec((half, tm), lambda i: (0, i % nS)),
            _resident((CONV_WIDTH, LRU_WIDTH)), _resident((1, LRU_WIDTH)),
        ],
        out_specs=[
            pl.BlockSpec((tm, Q_COL0), lambda i: (i, 0)),
            pl.BlockSpec((NSA_Q_HEADS, HEAD_DIM, tm), lambda i: (0, 0, i)),
            pl.BlockSpec((2 * G, tm // CMP_STRIDE, seg_w), lambda i: (0, i, 0)),
            pl.BlockSpec((2 * G, tm, LANES), lambda i: (0, i, 0)),
            pl.BlockSpec((2 * G, V_ROWS, tm), lambda i: (0, 0, i)),
            pl.BlockSpec((G, GATE_ROWS, tm), lambda i: (0, 0, i)),
        ],
        out_shape=[
            jax.ShapeDtypeStruct((T, Q_COL0), F32),
            jax.ShapeDtypeStruct((NSA_Q_HEADS, HEAD_DIM, T), BF16),
            jax.ShapeDtypeStruct((2 * G, T // CMP_STRIDE, seg_w), BF16),
            jax.ShapeDtypeStruct((2 * G, T, LANES), BF16),
            jax.ShapeDtypeStruct((2 * G, V_ROWS, T), BF16),
            jax.ShapeDtypeStruct((G, GATE_ROWS, T), F32),
        ],
        scratch_shapes=[pltpu.VMEM((tm, LANES), F32),
                        pltpu.VMEM((SUBLANES + tm, LRU_WIDTH), F32)],
        compiler_params=pltpu.CompilerParams(
            dimension_semantics=("arbitrary",), vmem_limit_bytes=VMEM_LIMIT),
        name="proj",
    )(h, w_nat, w_t, cos_n, sin_n, cos.T, sin.T, conv_w, conv_b.reshape(1, LRU_WIDTH))


SCAN_UNROLL = 8
LRU_ROWS = 512


def _lru_kernel(x_ref, gate_ref, wa_ref, ba_ref, wx_ref, bx_ref, lam_ref, y_ref, a_scr, b_scr):
    S, C = y_ref.shape
    neg_lam = -lam_ref[...]
    softplus = jnp.maximum(neg_lam, 0.0) + jnp.log1p(jnp.exp(-jnp.abs(neg_lam)))

    for r0 in range(0, S, LRU_ROWS):
        xc = x_ref[r0:r0 + LRU_ROWS, :]
        xb = xc.astype(BF16)
        gate_pre = lambda w_ref: jnp.concatenate(
            [_dot(xb[:, c * LANES:(c + 1) * LANES], w_ref[c]) for c in range(C // LANES)], axis=1)
        r = 0.5 * jnp.tanh(0.5 * (gate_pre(wa_ref) + ba_ref[...])) + 0.5
        ig = 0.5 * jnp.tanh(0.5 * (gate_pre(wx_ref) + bx_ref[...])) + 0.5
        log_a = (-LRU_C) * r * softplus
        a = jnp.exp(log_a)
        z = jnp.tanh(-log_a) * (a * a + 1.0)
        mult = z * lax.rsqrt(jnp.maximum(z, F32_TINY))
        if r0 == 0:
            row = lax.broadcasted_iota(jnp.int32, mult.shape, 0)
            mult = jnp.where(row == 0, 1.0, mult)
        a_scr[r0:r0 + LRU_ROWS, :] = a
        b_scr[r0:r0 + LRU_ROWS, :] = mult * (ig * xc)

    sub = lax.broadcasted_iota(jnp.int32, (SUBLANES, C), 0)

    def scan_tile(g, h_prev):
        rows = pl.ds(pl.multiple_of(g * SUBLANES, SUBLANES), SUBLANES)
        at = a_scr[rows, :]
        bt = b_scr[rows, :]
        d = 1
        while d < SUBLANES:
            keep = sub >= d
            a_sh = jnp.where(keep, pltpu.roll(at, d, axis=0), 1.0)
            b_sh = jnp.where(keep, pltpu.roll(bt, d, axis=0), 0.0)
            bt = at * b_sh + bt
            at = at * a_sh
            d *= 2
        h = bt + at * h_prev
        b_scr[rows, :] = h
        return jnp.broadcast_to(h[SUBLANES - 1:SUBLANES, :], (SUBLANES, C))

    def scan_body(i, h_prev):
        for u in range(SCAN_UNROLL):
            h_prev = scan_tile(i * SCAN_UNROLL + u, h_prev)
        return h_prev

    lax.fori_loop(0, S // (SUBLANES * SCAN_UNROLL), scan_body, jnp.zeros((SUBLANES, C), F32))

    for r0 in range(0, S, LRU_ROWS):
        rows = slice(r0, r0 + LRU_ROWS)
        y_ref[rows, :] = b_scr[rows, :] * gate_ref[rows, :]


def _lru(lru, w_a, b_a, w_x, b_x, lam, B, S):
    T = lru.shape[0]
    C = LRU_WIDTH
    n_ch = C // LANES
    assert S % (SUBLANES * SCAN_UNROLL) == 0 and S % LRU_ROWS == 0

    def blockdiag(w):
        w = w.reshape(n_ch, 2, LRU_BLOCK, LRU_BLOCK)
        z = jnp.zeros((n_ch, LRU_BLOCK, LRU_BLOCK), w.dtype)
        top = jnp.concatenate([w[:, 0], z], axis=2)
        bot = jnp.concatenate([z, w[:, 1]], axis=2)
        return jnp.concatenate([top, bot], axis=1).astype(BF16)

    vec = lambda v: v.reshape(1, C)
    return pl.pallas_call(
        _lru_kernel,
        grid=(B,),
        in_specs=[
            pl.BlockSpec((S, C), lambda b: (b, 0)),
            pl.BlockSpec((S, C), lambda b: (b, 1)),
            _resident((n_ch, LANES, LANES)), _resident((1, C)),
            _resident((n_ch, LANES, LANES)), _resident((1, C)),
            _resident((1, C)),
        ],
        out_specs=pl.BlockSpec((S, C), lambda b: (b, 0)),
        out_shape=jax.ShapeDtypeStruct((T, C), F32),
        scratch_shapes=[pltpu.VMEM((S, C), F32),
                        pltpu.VMEM((S, C), F32)],
        compiler_params=pltpu.CompilerParams(
            dimension_semantics=("parallel",), vmem_limit_bytes=VMEM_LIMIT),
        name="lru",
    )(lru, lru, blockdiag(w_a), vec(b_a.reshape(-1)),
      blockdiag(w_x), vec(b_x.reshape(-1)), vec(lam))


def _cmp_kernel(seg_ref, pek_ref, w1k_ref, w2k_ref, pev_ref, w1v_ref, w2vt_ref, kc_ref, vct_ref):
    n_seg = seg_ref.shape[2]
    half = CMP_STRIDE * HEAD_DIM

    def hidden(idx, pe_ref, w1_ref):
        seg = seg_ref[idx, 0]
        first = _dot(seg, w1_ref[0:half, :])
        second = _dot(seg, w1_ref[half:2 * half, :])
        bias = _dot(pe_ref[...], w1_ref[...])[0:1, :]
        return _silu(first + pltpu.roll(second, n_seg - 1, axis=0) + bias).astype(BF16)

    for g in range(NSA_KV_HEADS):
        tok = _dot(hidden(g, pek_ref, w1k_ref), w2k_ref[...])
        kc_ref[0, g] = jnp.concatenate([tok, jnp.zeros_like(tok)], axis=1).astype(BF16)
        vct_ref[0, g] = _dot_nt(w2vt_ref[...], hidden(NSA_KV_HEADS + g, pev_ref, w1v_ref)
                                ).astype(BF16)


def _compress(cmp_in, pe_k, w1_k, w2_k, pe_v, w1_v, w2_v, B, S):
    G = NSA_KV_HEADS
    n_seg = S // CMP_STRIDE
    seg = cmp_in.reshape(2 * G, B, n_seg, CMP_STRIDE * HEAD_DIM)
    flat = CMP_BLOCK * HEAD_DIM
    pe8 = lambda pe: jnp.broadcast_to(pe.reshape(1, -1), (8, flat)).astype(BF16)
    return pl.pallas_call(
        _cmp_kernel,
        grid=(B,),
        in_specs=[
            pl.BlockSpec((2 * G, 1, n_seg, CMP_STRIDE * HEAD_DIM), lambda b: (0, b, 0, 0)),
            _resident((8, flat)), _resident((flat, CMP_HIDDEN)), _resident((CMP_HIDDEN, HEAD_DIM)),
            _resident((8, flat)), _resident((flat, CMP_HIDDEN)), _resident((HEAD_DIM, CMP_HIDDEN)),
        ],
        out_specs=[
            pl.BlockSpec((1, G, n_seg, LANES), lambda b: (b, 0, 0, 0)),
            pl.BlockSpec((1, G, HEAD_DIM, n_seg), lambda b: (b, 0, 0, 0)),
        ],
        out_shape=[
            jax.ShapeDtypeStruct((B, G, n_seg, LANES), BF16),
            jax.ShapeDtypeStruct((B, G, HEAD_DIM, n_seg), BF16),
        ],
        compiler_params=pltpu.CompilerParams(
            dimension_semantics=("parallel",), vmem_limit_bytes=VMEM_LIMIT),
        name="compress",
    )(seg, pe8(pe_k), w1_k.astype(BF16), w2_k.astype(BF16),
      pe8(pe_v), w1_v.astype(BF16), w2_v.T.astype(BF16))


def _attn_kernel(qt_ref, kc_ref, vct_ref, ks_ref, vst_ref, kw_ref, vwt_ref, gt_ref, ovl_ref,
                 bias_ref, *rest, tq, tiles):
    o_ref = rest[-1]

    def step(n):
        for _ in _interleave(*[
                _attn_step(n, g, qt_ref, kc_ref.at[g], vct_ref.at[g], ks_ref.at[g], vst_ref.at[g],
                           kw_ref.at[g], vwt_ref.at[g], gt_ref.at[g], ovl_ref, bias_ref, o_ref, tq)
                for g in range(NSA_KV_HEADS)]):
            pass

    for n in range(*tiles):
        pl.when(pl.program_id(1) == n - tiles[0])(functools.partial(step, n))


CHUNK_TILES = 1
SCORES_AHEAD = 1
HEAD_PAIRS = 2
ATTN_TILE_SPLITS = ((0, 3), (3, 5), (5, 7), (7, 8))


def _interleave(*stages):
    live = list(stages)
    while live:
        for st in list(live):
            try:
                next(st)
            except StopIteration:
                live.remove(st)
        yield


def _tile_bias(bias_ref, t0, k0, behind, width):
    if k0 == t0:
        return bias_ref[0, :, 0:width]
    if behind is not None and k0 == t0 - behind:
        return bias_ref[1, :, 0:width]
    return None


def _key_chunks(k_ref, vt_ref, k0, n_keys, tile, q, bias_fn):
    starts = list(range(k0, k0 + n_keys, tile))
    return [dict(k_ref=k_ref, vt_ref=vt_ref, tile=tile, q=q, bias_fn=bias_fn,
                 starts=starts[i:i + CHUNK_TILES]) for i in range(0, len(starts), CHUNK_TILES)]


def _scores_stage(ch):
    tiles, m = [], None
    for k0 in ch["starts"]:
        s = _dot(ch["k_ref"][k0:k0 + ch["tile"], :], ch["q"])
        bias = ch["bias_fn"](k0)
        if bias is not None:
            s = s + bias
        cm = jnp.max(s, axis=0, keepdims=True)
        m = cm if m is None else jnp.maximum(m, cm)
        tiles.append(s)
        yield
    ch["s"], ch["m"] = tiles, m


def _pv_stage(ch, heads, tq):
    acc = [None] * heads
    for k0, s in zip(ch["starts"], ch["s"]):
        p = jnp.exp2(s - ch["m"]).astype(BF16)
        vt = ch["vt_ref"][:, k0:k0 + ch["tile"]]
        for r in range(heads):
            d = _dot(vt, p[:, r * tq:(r + 1) * tq])
            acc[r] = d if acc[r] is None else acc[r] + d
        yield
    ch["acc"] = acc


def _merge_chunks(chunks, heads, tq):
    if len(chunks) == 1:
        return chunks[0]["acc"]
    m = functools.reduce(jnp.maximum, [ch["m"] for ch in chunks])
    acc = [None] * heads
    for ch in chunks:
        w = jnp.exp2(ch["m"] - m)
        for r in range(heads):
            term = ch["acc"][r] * w[:, r * tq:(r + 1) * tq]
            acc[r] = term if acc[r] is None else acc[r] + term
    return acc


def _rank_stage(out, score, jblk, n_live):
    rank = jnp.zeros(score.shape, jnp.int32)
    for kb in range(n_live):
        sk = score[kb:kb + 1, :]
        ahead = (sk > score) | ((sk == score) & (jblk > kb))
        rank = rank + ahead.astype(jnp.int32)
        if kb % 4 == 3:
            yield
    out["rank"] = rank


def _attn_step(n, g, qt_ref, kc_ref, vct_ref, ks_ref, vst_ref, kw_ref, vwt_ref, gt_ref, ovl_ref,
               bias_ref, o_ref, tq):
    R = NSA_GROUP
    t0 = n * tq
    n_cmp = kc_ref.shape[0]
    n_blk = ovl_ref.shape[0]
    q_all = jnp.concatenate([qt_ref[g * R + r] for r in range(R)], axis=1)
    q_pad = jnp.concatenate([q_all, jnp.zeros((LANES - HEAD_DIM, R * tq), BF16)], axis=0)
    head = lambda a, r: a[:, r * tq:(r + 1) * tq]

    kw0 = max(t0 - WINDOW, 0)
    HP = R // HEAD_PAIRS
    pair_lanes = [slice(p * HP * tq, (p + 1) * HP * tq) for p in range(HEAD_PAIRS)]
    win = [_key_chunks(kw_ref, vwt_ref, kw0, t0 + tq - kw0, tq, q_pad[:, ln],
                       lambda k0: _tile_bias(bias_ref, t0, k0, WINDOW, HP * tq))
           for ln in pair_lanes]
    win_scores = _interleave(*[_scores_stage(w[0]) for w in win])
    next(win_scores)
    yield

    n_vis = (t0 + tq - CMP_BLOCK) // CMP_STRIDE + 1
    n_vis = min(n_cmp, -(-n_vis // BF16_ROWS) * BF16_ROWS)
    tpos_c = t0 + (lax.broadcasted_iota(jnp.int32, (n_vis, R * tq), 1) & (tq - 1))
    cend = lax.broadcasted_iota(jnp.int32, (n_vis, R * tq), 0) * CMP_STRIDE + (CMP_BLOCK - 1)
    s = _dot(kc_ref[0:n_vis, :], q_pad) + jnp.where(cend <= tpos_c, 0.0, NEG)
    e = jnp.exp2(s - jnp.max(s, axis=0, keepdims=True))
    p_cmp = e * (1.0 / jnp.sum(e, axis=0, keepdims=True))
    if t0 < CMP_BLOCK - 1:
        p_cmp = jnp.where(tpos_c >= CMP_BLOCK - 1, p_cmp, 0.0)
    if n_vis < n_cmp:
        p_cmp = jnp.concatenate([p_cmp, jnp.zeros((n_cmp - n_vis, R * tq), F32)], axis=0)
    p_cmp_b = p_cmp.astype(BF16)
    vct = vct_ref[...]
    o_cmp = [_dot(vct, head(p_cmp_b, r)) for r in range(R)]
    psum = functools.reduce(jnp.add, [head(p_cmp, r) for r in range(R)])
    yield

    p_hi = psum.astype(BF16)
    p_lo = (psum - p_hi.astype(F32)).astype(BF16)
    ovl = ovl_ref[...]
    imp = _dot(ovl, p_hi) + _dot(ovl, p_lo)
    n_live = min(n_blk, (t0 + tq - 1) // SLC_BLOCK + 1)
    rows = -(-n_live // SUBLANES) * SUBLANES
    jblk = lax.broadcasted_iota(jnp.int32, (rows, tq), 0)
    tpos_b = t0 + lax.broadcasted_iota(jnp.int32, (rows, tq), 1)
    blk_valid = jblk * SLC_BLOCK <= tpos_b
    back = (tpos_b >> SLC_SHIFT) - jblk
    forced = (jblk == 0) | ((back >= 0) & (back < N_LOCAL_BLOCKS))
    score = jnp.where(blk_valid, jnp.where(forced, jnp.inf, imp[0:rows]), -jnp.inf)
    sel = {}
    yield from _interleave(_rank_stage(sel, score, jblk, n_live), win_scores)
    unsel = 1.0 - (blk_valid & (sel["rank"] < SLC_TOP_N)).astype(F32)
    dead = [jnp.ones((n_blk - rows, tq), F32)] if rows < n_blk else []
    sel_ext = jnp.concatenate(
        [unsel] + dead + [jnp.zeros((LANES - HEAD_DIM - n_blk, tq), F32)],
        axis=0).astype(BF16)
    q_sel = jnp.concatenate([q_all, jnp.concatenate([sel_ext] * R, axis=1)], axis=0)

    slc = [_key_chunks(ks_ref, vst_ref, 0, t0 + tq, tq, q_sel[:, ln],
                       lambda k0: _tile_bias(bias_ref, t0, k0, None, HP * tq))
           for ln in pair_lanes]

    def pipeline(chunks):
        for ch in chunks[1:SCORES_AHEAD]:
            yield from _scores_stage(ch)
        for idx, ch in enumerate(chunks):
            stages = [_pv_stage(ch, HP, tq)]
            if idx + SCORES_AHEAD < len(chunks):
                stages.append(_scores_stage(chunks[idx + SCORES_AHEAD]))
            yield from _interleave(*stages)

    yield from _interleave(*[pipeline(w + s) for w, s in zip(win, slc)])
    acc_w = [a for w in win for a in _merge_chunks(w, HP, tq)]
    acc_s = [a for s in slc for a in _merge_chunks(s, HP, tq)]
    yield

    gt = gt_ref[...]
    outs = []
    for r in range(R):
        g_cmp = gt[3 * r:3 * r + 1]
        g_slc = gt[3 * r + 1:3 * r + 2] * (1.0 / acc_s[r][HEAD_DIM:HEAD_DIM + 1])
        g_win = gt[3 * r + 2:3 * r + 3] * (1.0 / acc_w[r][HEAD_DIM:HEAD_DIM + 1])
        outs.append(g_cmp * o_cmp[r] + g_slc * acc_s[r][:HEAD_DIM] + g_win * acc_w[r][:HEAD_DIM])
    width = R * HEAD_DIM
    o_ref[:, g * width:(g + 1) * width] = jnp.concatenate(outs, axis=0).T


def _overlap_t(n_cmp_pad, n_blk):
    cs = np.arange(n_cmp_pad) * CMP_STRIDE
    ce = cs + CMP_BLOCK - 1
    ss = np.arange(n_blk) * SLC_BLOCK
    se = ss + SLC_BLOCK - 1
    return ((cs[None, :] <= se[:, None]) & (ce[None, :] >= ss[:, None])).astype(np.float32)


def _attention(qt, k, vt, kc, vct, gates_t, B, S, *, tq=256):
    T = qt.shape[2]
    nq = S // tq
    n_cmp = S // CMP_STRIDE
    n_blk = S // SLC_BLOCK
    G = NSA_KV_HEADS
    ovl = jnp.asarray(_overlap_t(n_cmp, n_blk), BF16)
    assert WINDOW % tq == 0 and WINDOW + tq <= S
    assert HEAD_DIM + n_blk <= LANES
    assert tq & (tq - 1) == 0
    key_off = np.arange(tq)[:, None]
    q_off = np.tile(np.arange(tq), NSA_GROUP)[None, :]
    bias = jnp.asarray(np.stack([np.where(key_off <= q_off, 0.0, NEG),
                                 np.where(key_off > q_off, 0.0, NEG)]), F32)
    k_spec = lambda c: pl.BlockSpec((G, S, LANES), lambda b, i: (c, b, 0))
    vt_spec = lambda c: pl.BlockSpec((G, V_ROWS, S), lambda b, i: (c, 0, b))

    def call(tiles, prev):
        lo, hi = tiles
        tile = lambda b, i: b * nq + lo + i
        in_specs = [
            pl.BlockSpec((NSA_Q_HEADS, HEAD_DIM, tq), lambda b, i: (0, 0, tile(b, i))),
            pl.BlockSpec((None, G, n_cmp, LANES), lambda b, i: (b, 0, 0, 0)),
            pl.BlockSpec((None, G, HEAD_DIM, n_cmp), lambda b, i: (b, 0, 0, 0)),
            k_spec(0), vt_spec(0), k_spec(1), vt_spec(1),
            pl.BlockSpec((G, GATE_ROWS, tq), lambda b, i: (0, 0, tile(b, i))),
            _resident((n_blk, n_cmp)),
            _resident((2, tq, NSA_GROUP * tq)),
        ]
        args = [qt, kc, vct, k, vt, k, vt, gates_t, ovl, bias]
        aliases = {}
        if prev is not None:
            in_specs.append(pl.BlockSpec(memory_space=pl.ANY))
            aliases = {len(args): 0}
            args.append(prev)
        return pl.pallas_call(
            functools.partial(_attn_kernel, tq=tq, tiles=tiles),
            grid=(B, hi - lo),
            in_specs=in_specs,
            out_specs=pl.BlockSpec((tq, NSA_WIDTH), lambda b, i: (tile(b, i), 0)),
            out_shape=jax.ShapeDtypeStruct((T, NSA_WIDTH), F32),
            input_output_aliases=aliases,
            compiler_params=pltpu.CompilerParams(
                dimension_semantics=("parallel", "arbitrary"),
                vmem_limit_bytes=VMEM_LIMIT),
            name=f"nsa_attn_{lo}_{hi}",
        )(*args)

    assert [t for lo, hi in ATTN_TILE_SPLITS for t in range(lo, hi)] == list(range(nq))
    out = None
    for tiles in ATTN_TILE_SPLITS:
        out = call(tiles, out)
    return out


def _out_ffn_kernel(ylru_ref, ynsa_ref, h_ref, gl_ref, gn_ref, w_ref, g2_ref, b2_ref,
                    w_in_hbm, w_out_hbm, g3_ref, b3_ref, o_ref, wg_scr, wu_scr, wo_scr, *staging):
    @pl.when(pl.program_id(0) == 0)
    def _():
        _stage_ffn_weights(w_in_hbm, w_out_hbm, wg_scr, wu_scr, wo_scr, *staging)

    yl = _rms_norm(ylru_ref[...], gl_ref[...]).astype(BF16)
    yn = _rms_norm(ynsa_ref[...], gn_ref[...]).astype(BF16)
    mix = _dot(yl, w_ref[0:LRU_WIDTH, :]) + _dot(yn, w_ref[LRU_WIDTH:, :])
    h2 = _layer_norm(ALPHA * h_ref[...] + mix, g2_ref[...], b2_ref[...])
    o_ref[...] = _ffn_half_step(h2, wg_scr, wu_scr, wo_scr, g3_ref, b3_ref)


def _out_proj_ffn(y_lru, y_nsa, h, gn_lru, gn_nsa, w_mix, g2, b2, w_in, w_out, g3, b3, *,
                  tm=512):
    T, D = h.shape
    row = lambda v: v.reshape(1, -1)
    return pl.pallas_call(
        _out_ffn_kernel,
        grid=(T // tm,),
        in_specs=[
            pl.BlockSpec((tm, LRU_WIDTH), lambda i: (i, 0)),
            pl.BlockSpec((tm, NSA_WIDTH), lambda i: (i, 0)),
            pl.BlockSpec((tm, D), lambda i: (i, 0)),
            _resident((1, LRU_WIDTH)), _resident((1, NSA_WIDTH)),
            _resident((LRU_WIDTH + NSA_WIDTH, D)),
            _resident((1, D)), _resident((1, D)),
            pl.BlockSpec(memory_space=pl.ANY), pl.BlockSpec(memory_space=pl.ANY),
            _resident((1, D)), _resident((1, D)),
        ],
        out_specs=pl.BlockSpec((tm, D), lambda i: (i, 0)),
        out_shape=jax.ShapeDtypeStruct((T, D), F32),
        scratch_shapes=_ffn_weight_scratch(D),
        compiler_params=pltpu.CompilerParams(
            dimension_semantics=("arbitrary",), vmem_limit_bytes=FFN_VMEM_LIMIT),
        name="out_proj_ffn",
    )(y_lru, y_nsa, h, row(gn_lru), row(gn_nsa), w_mix.astype(BF16), row(g2), row(b2),
      w_in, w_out, row(g3), row(b3))


def kernel(x, ffn1_w_in, ffn1_w_out, ln1_g, ln1_b, mix_w_in, conv_w, conv_b, lru_w_a, lru_b_a,
           lru_w_x, lru_b_x, lru_lam, cmp_pe_k, cmp_w1_k, cmp_w2_k, cmp_pe_v, cmp_w1_v,
           cmp_w2_v, gn_lru, gn_nsa, mix_w_out, ln2_g, ln2_b, ffn2_w_in, ffn2_w_out, ln3_g,
           ln3_b):
    B, S, D = x.shape
    h = x.reshape(B * S, D)
    for l in range(DEPTH):
        h = _ffn_ln(h, ffn1_w_in[l], ffn1_w_out[l], ln1_g[l], ln1_b[l])
        lru, qt, cmp_in, k, vt, gates_t = _proj(h, mix_w_in[l], conv_w[l], conv_b[l], S)
        y_lru = _lru(lru, lru_w_a[l], lru_b_a[l], lru_w_x[l], lru_b_x[l], lru_lam[l], B, S)
        kc, vct = _compress(cmp_in, cmp_pe_k[l], cmp_w1_k[l], cmp_w2_k[l],
                            cmp_pe_v[l], cmp_w1_v[l], cmp_w2_v[l], B, S)
        y_nsa = _attention(qt, k, vt, kc, vct, gates_t, B, S)
        h = _out_proj_ffn(y_lru, y_nsa, h, gn_lru[l], gn_nsa[l], mix_w_out[l], ln2_g[l], ln2_b[l],
                          ffn2_w_in[l], ffn2_w_out[l], ln3_g[l], ln3_b[l])
    return h.reshape(B, S, D)
```

```python
import functools

import numpy as np
import jax
import jax.numpy as jnp
from jax import lax
from jax.experimental import pallas as pl
from jax.experimental.pallas import tpu as pltpu

F32 = jnp.float32
BF16 = jnp.bfloat16

D_MODEL = 1024
LRU_WIDTH = 512
LRU_HEADS = 8
LRU_BLOCK = 64
CONV_WIDTH = 4
LRU_C = 8.0
NSA_Q_HEADS = 8
NSA_KV_HEADS = 2
NSA_GROUP = 4
HEAD_DIM = 64
NSA_WIDTH = 512
KV_WIDTH = 128
CMP_BLOCK = 32
CMP_STRIDE = 16
CMP_HIDDEN = 256
SLC_BLOCK = 64
SLC_SHIFT = 6
SLC_TOP_N = 16
N_LOCAL_BLOCKS = 2
WINDOW = 512
ROPE_THETA = 10000.0
D_FF = 2816
DEPTH = 1
ALPHA = (2.0 * DEPTH) ** 0.25
LN_EPS = 1e-5
RMS_EPS = 1e-6
NEG = -1e30
LOG2E = 1.4426950408889634
F32_TINY = 1.1754943508222875e-38

LANES = 128
SUBLANES = 8
BF16_ROWS = 16
VMEM_LIMIT = 48 * 1024 * 1024
N_GATE = 3 * NSA_GROUP
GATE_ROWS = 16
V_ROWS = HEAD_DIM + BF16_ROWS


def _dot(a, b):
    return jnp.dot(a, b, preferred_element_type=F32)


def _dot_nt(a, b):
    return lax.dot_general(a, b, (((1,), (1,)), ((), ())), preferred_element_type=F32)


def _layer_norm(y, g, b):
    mu = jnp.mean(y, axis=-1, keepdims=True)
    d = y - mu
    var = jnp.mean(d * d, axis=-1, keepdims=True)
    return d * lax.rsqrt(var + LN_EPS) * g + b


def _rms_norm(y, g):
    return y * lax.rsqrt(jnp.mean(y * y, axis=-1, keepdims=True) + RMS_EPS) * g


def _silu(x):
    return x * jax.nn.sigmoid(x)


def _resident(shape):
    return pl.BlockSpec(shape, lambda *_: (0,) * len(shape), pipeline_mode=pl.Buffered(1))


def _ffn_half_step(x, wg_ref, wu_ref, wo_ref, g_ref, b_ref):
    xb = x.astype(BF16)
    gate = _dot(xb, wg_ref[...])
    up = _dot(xb, wu_ref[...])
    act = (_silu(gate) * up).astype(BF16)
    y = ALPHA * x + 0.5 * _dot(act, wo_ref[...])
    return _layer_norm(y, g_ref[...], b_ref[...])


W_IN_ROWS = 64
W_OUT_ROWS = 352
FFN_VMEM_LIMIT = 56 * 1024 * 1024


def _stream_rows(w_hbm, rows, slots, sems, consume):
    def copy(c):
        return pltpu.make_async_copy(w_hbm.at[pl.ds(c * rows, rows), :], slots[c % 2], sems[c % 2])

    n = w_hbm.shape[0] // rows
    copy(0).start()
    for c in range(n):
        if c + 1 < n:
            copy(c + 1).start()
        copy(c).wait()
        consume(c * rows, slots[c % 2])


def _stage_ffn_weights(w_in_hbm, w_out_hbm, wg_scr, wu_scr, wo_scr, in0, in1, out0, out1,
                       sem_in0, sem_in1, sem_out0, sem_out1):
    def put_in(r0, buf):
        wg_scr[r0:r0 + W_IN_ROWS, :] = buf[:, :D_FF].astype(BF16)
        wu_scr[r0:r0 + W_IN_ROWS, :] = buf[:, D_FF:].astype(BF16)

    def put_out(r0, buf):
        wo_scr[r0:r0 + W_OUT_ROWS, :] = buf[...].astype(BF16)

    _stream_rows(w_in_hbm, W_IN_ROWS, (in0, in1), (sem_in0, sem_in1), put_in)
    _stream_rows(w_out_hbm, W_OUT_ROWS, (out0, out1), (sem_out0, sem_out1), put_out)


def _ffn_weight_scratch(D):
    assert D % W_IN_ROWS == 0 and D_FF % W_OUT_ROWS == 0
    return ([pltpu.VMEM((D, D_FF), BF16), pltpu.VMEM((D, D_FF), BF16), pltpu.VMEM((D_FF, D), BF16)]
            + [pltpu.VMEM((W_IN_ROWS, 2 * D_FF), F32)] * 2 + [pltpu.VMEM((W_OUT_ROWS, D), F32)] * 2
            + [pltpu.SemaphoreType.DMA(())] * 4)


def _ffn_ln_kernel(x_ref, w_in_hbm, w_out_hbm, g_ref, b_ref, o_ref, wg_scr, wu_scr, wo_scr,
                   *staging):
    @pl.when(pl.program_id(0) == 0)
    def _():
        _stage_ffn_weights(w_in_hbm, w_out_hbm, wg_scr, wu_scr, wo_scr, *staging)

    o_ref[...] = _ffn_half_step(x_ref[...], wg_scr, wu_scr, wo_scr, g_ref, b_ref)


def _ffn_ln(x, w_in, w_out, g, b, *, tm=1024):
    T, D = x.shape
    return pl.pallas_call(
        _ffn_ln_kernel,
        grid=(T // tm,),
        in_specs=[
            pl.BlockSpec((tm, D), lambda i: (i, 0)),
            pl.BlockSpec(memory_space=pl.ANY), pl.BlockSpec(memory_space=pl.ANY),
            _resident((1, D)), _resident((1, D)),
        ],
        out_specs=pl.BlockSpec((tm, D), lambda i: (i, 0)),
        out_shape=jax.ShapeDtypeStruct((T, D), F32),
        scratch_shapes=_ffn_weight_scratch(D),
        compiler_params=pltpu.CompilerParams(
            dimension_semantics=("arbitrary",), vmem_limit_bytes=FFN_VMEM_LIMIT),
        name="ffn_ln",
    )(x, w_in, w_out, g.reshape(1, D), b.reshape(1, D))


Q_COL0 = 2 * LRU_WIDTH
KCMP_COL0 = Q_COL0 + NSA_WIDTH
VCMP_COL0 = KCMP_COL0 + KV_WIDTH
KSLC_COL0 = VCMP_COL0 + KV_WIDTH
VSLC_COL0 = KSLC_COL0 + KV_WIDTH
KWIN_COL0 = VSLC_COL0 + KV_WIDTH
VWIN_COL0 = KWIN_COL0 + KV_WIDTH
GATE_COL0 = VWIN_COL0 + KV_WIDTH
VT_ROW0 = NSA_WIDTH
GT_ROW0 = VT_ROW0 + 2 * KV_WIDTH
WT_ROWS = GT_ROW0 + NSA_KV_HEADS * GATE_ROWS


def _rope_chunk(xc, cos, sin_signed):
    lane = lax.broadcasted_iota(jnp.int32, xc.shape, 1)
    first = (lane & (HEAD_DIM - 1)) < (HEAD_DIM // 2)
    partner = jnp.where(first, pltpu.roll(xc, LANES - HEAD_DIM // 2, axis=1),
                        pltpu.roll(xc, HEAD_DIM // 2, axis=1))
    return xc * cos + partner * sin_signed


def _proj_kernel(h_ref, wn_ref, wt_ref, cos_ref, sin_ref, cost_ref, sint_ref, cw_ref, cb_ref,
                 lru_ref, qt_ref, cmp_ref, k_ref, vt_ref, gt_ref, seg_scr, conv_scr, *, seq_tiles):
    tm = h_ref.shape[0]
    half = HEAD_DIM // 2

    @pl.when(pl.program_id(0) % seq_tiles == 0)
    def _():
        conv_scr[0:SUBLANES, :] = jnp.zeros((SUBLANES, LRU_WIDTH), F32)

    hb = h_ref[...].astype(BF16)
    p = _dot(hb, wn_ref[...])
    pt = _dot_nt(wt_ref[...], hb)

    x = p[:, :LRU_WIDTH]
    conv_scr[SUBLANES:, :] = x
    xc = cb_ref[...]
    for d in range(CONV_WIDTH):
        xs = conv_scr[SUBLANES - d:SUBLANES - d + tm, :]
        xc = xc + xs * cw_ref[CONV_WIDTH - 1 - d:CONV_WIDTH - d, :]
    conv_scr[0:SUBLANES, :] = x[tm - SUBLANES:, :]
    lru_ref[:, :LRU_WIDTH] = xc
    lru_ref[:, LRU_WIDTH:] = jax.nn.gelu(p[:, LRU_WIDTH:Q_COL0])

    cos_t = cost_ref[...]
    sin_t = sint_ref[...]
    scale = HEAD_DIM ** -0.5 * LOG2E
    for hd in range(NSA_Q_HEADS):
        x1 = pt[hd * HEAD_DIM:hd * HEAD_DIM + half]
        x2 = pt[hd * HEAD_DIM + half:(hd + 1) * HEAD_DIM]
        qt_ref[hd] = (jnp.concatenate([x1 * cos_t - x2 * sin_t, x2 * cos_t + x1 * sin_t], axis=0)
                      * scale).astype(BF16)
    ones = jnp.ones((V_ROWS - HEAD_DIM, tm), F32)
    for c in range(2 * NSA_KV_HEADS):
        v = pt[VT_ROW0 + c * HEAD_DIM:VT_ROW0 + (c + 1) * HEAD_DIM]
        vt_ref[c] = jnp.concatenate([v, ones], axis=0).astype(BF16)
    for g in range(NSA_KV_HEADS):
        gt_ref[g] = jax.nn.sigmoid(pt[GT_ROW0 + g * GATE_ROWS:GT_ROW0 + (g + 1) * GATE_ROWS])

    cos = cos_ref[...]
    sin = sin_ref[...]
    low_seg = lax.broadcasted_iota(jnp.int32, (tm // CMP_STRIDE, LANES), 1) < HEAD_DIM
    for c, xc in enumerate((_rope_chunk(p[:, Q_COL0:Q_COL0 + LANES], cos, sin),
                            p[:, Q_COL0 + LANES:Q_COL0 + 2 * LANES])):
        seg_scr[...] = xc
        tok = [seg_scr[pl.ds(j, tm // CMP_STRIDE, stride=CMP_STRIDE), :]
               for j in range(CMP_STRIDE)]
        g0, g1 = [], []
        for j in range(0, CMP_STRIDE, 2):
            g0.append(jnp.where(low_seg, tok[j], pltpu.roll(tok[j + 1], HEAD_DIM, axis=1)))
            g1.append(jnp.where(low_seg, pltpu.roll(tok[j], HEAD_DIM, axis=1), tok[j + 1]))
        cmp_ref[NSA_KV_HEADS * c] = jnp.concatenate(g0, axis=1).astype(BF16)
        cmp_ref[NSA_KV_HEADS * c + 1] = jnp.concatenate(g1, axis=1).astype(BF16)
    lane = lax.broadcasted_iota(jnp.int32, (tm, LANES), 1)
    low = lane < HEAD_DIM
    pos = (pl.program_id(0) % seq_tiles) * tm + lax.broadcasted_iota(jnp.int32, (tm, LANES), 0)
    ext_blk = jnp.where(lane - HEAD_DIM == (pos >> SLC_SHIFT), NEG, 0.0)
    ext_zero = jnp.zeros((tm, LANES), F32)
    for c, ext in enumerate((ext_blk, ext_zero)):
        xc = _rope_chunk(p[:, Q_COL0 + (2 + c) * LANES:Q_COL0 + (3 + c) * LANES], cos, sin)
        k_ref[2 * c] = jnp.where(low, xc, ext).astype(BF16)
        k_ref[2 * c + 1] = jnp.where(low, pltpu.roll(xc, HEAD_DIM, axis=1), ext).astype(BF16)


def _proj(h, w_in, conv_w, conv_b, S, *, tm=1024):
    T, D = h.shape
    G = NSA_KV_HEADS
    col = lambda c0, n: w_in[:, c0:c0 + n]
    w_nat = jnp.concatenate(
        [col(0, Q_COL0), col(KCMP_COL0, KV_WIDTH), col(VCMP_COL0, KV_WIDTH),
         col(KSLC_COL0, KV_WIDTH), col(KWIN_COL0, KV_WIDTH)], axis=1).astype(BF16)
    gate_rows = [jnp.pad(col(GATE_COL0 + g * N_GATE, N_GATE).T, ((0, GATE_ROWS - N_GATE), (0, 0)))
                 for g in range(G)]
    w_t = jnp.concatenate(
        [col(Q_COL0, NSA_WIDTH).T, col(VSLC_COL0, KV_WIDTH).T, col(VWIN_COL0, KV_WIDTH).T]
        + gate_rows, axis=0).astype(BF16)
    n_nat = w_nat.shape[1]

    half = HEAD_DIM // 2
    inv = ROPE_THETA ** (-jnp.arange(half, dtype=F32) / half)
    ang = jnp.arange(S, dtype=F32)[:, None] * inv[None, :]
    cos = jnp.cos(ang)
    sin = jnp.sin(ang)
    cos_n = jnp.concatenate([cos, cos, cos, cos], axis=1)
    sin_n = jnp.concatenate([-sin, sin, -sin, sin], axis=1)

    nS = S // tm
    seg_w = CMP_STRIDE * HEAD_DIM
    assert G * HEAD_DIM == LANES
    return pl.pallas_call(
        functools.partial(_proj_kernel, seq_tiles=nS),
        grid=(T // tm,),
        in_specs=[
            pl.BlockSpec((tm, D), lambda i: (i, 0)),
            _resident((D, n_nat)), _resident((WT_ROWS, D)),
            pl.BlockSpec((tm, LANES), lambda i: (i % nS, 0)),
            pl.BlockSpec((tm, LANES), lambda i: (i % nS, 0)),
            pl.BlockSpec((half, tm), lambda i: (0, i % nS)),
            pl.BlockSpec((half, tm), lambda i: (0, i % nS)),
            _resident((CONV_WIDTH, LRU_WIDTH)), _resident((1, LRU_WIDTH)),
        ],
        out_specs=[
            pl.BlockSpec((tm, Q_COL0), lambda i: (i, 0)),
            pl.BlockSpec((NSA_Q_HEADS, HEAD_DIM, tm), lambda i: (0, 0, i)),
            pl.BlockSpec((2 * G, tm // CMP_STRIDE, seg_w), lambda i: (0, i, 0)),
            pl.BlockSpec((2 * G, tm, LANES), lambda i: (0, i, 0)),
            pl.BlockSpec((2 * G, V_ROWS, tm), lambda i: (0, 0, i)),
            pl.BlockSpec((G, GATE_ROWS, tm), lambda i: (0, 0, i)),
        ],
        out_shape=[
            jax.ShapeDtypeStruct((T, Q_COL0), F32),
            jax.ShapeDtypeStruct((NSA_Q_HEADS, HEAD_DIM, T), BF16),
            jax.ShapeDtypeStruct((2 * G, T // CMP_STRIDE, seg_w), BF16),
            jax.ShapeDtypeStruct((2 * G, T, LANES), BF16),
            jax.ShapeDtypeStruct((2 * G, V_ROWS, T), BF16),
            jax.ShapeDtypeStruct((G, GATE_ROWS, T), F32),
        ],
        scratch_shapes=[pltpu.VMEM((tm, LANES), F32),
                        pltpu.VMEM((SUBLANES + tm, LRU_WIDTH), F32)],
        compiler_params=pltpu.CompilerParams(
            dimension_semantics=("arbitrary",), vmem_limit_bytes=VMEM_LIMIT),
        name="proj",
    )(h, w_nat, w_t, cos_n, sin_n, cos.T, sin.T, conv_w, conv_b.reshape(1, LRU_WIDTH))


SCAN_UNROLL = 8
LRU_ROWS = 512


def _lru_kernel(x_ref, gate_ref, wa_ref, ba_ref, wx_ref, bx_ref, lam_ref, y_ref, a_scr, b_scr):
    S, C = y_ref.shape
    neg_lam = -lam_ref[...]
    softplus = jnp.maximum(neg_lam, 0.0) + jnp.log1p(jnp.exp(-jnp.abs(neg_lam)))

    for r0 in range(0, S, LRU_ROWS):
        xc = x_ref[r0:r0 + LRU_ROWS, :]
        xb = xc.astype(BF16)
        gate_pre = lambda w_ref: jnp.concatenate(
            [_dot(xb[:, c * LANES:(c + 1) * LANES], w_ref[c]) for c in range(C // LANES)], axis=1)
        r = 0.5 * jnp.tanh(0.5 * (gate_pre(wa_ref) + ba_ref[...])) + 0.5
        ig = 0.5 * jnp.tanh(0.5 * (gate_pre(wx_ref) + bx_ref[...])) + 0.5
        log_a = (-LRU_C) * r * softplus
        a = jnp.exp(log_a)
        z = jnp.tanh(-log_a) * (a * a + 1.0)
        mult = z * lax.rsqrt(jnp.maximum(z, F32_TINY))
        if r0 == 0:
            row = lax.broadcasted_iota(jnp.int32, mult.shape, 0)
            mult = jnp.where(row == 0, 1.0, mult)
        a_scr[r0:r0 + LRU_ROWS, :] = a
        b_scr[r0:r0 + LRU_ROWS, :] = mult * (ig * xc)

    sub = lax.broadcasted_iota(jnp.int32, (SUBLANES, C), 0)

    def scan_tile(g, h_prev):
        rows = pl.ds(pl.multiple_of(g * SUBLANES, SUBLANES), SUBLANES)
        at = a_scr[rows, :]
        bt = b_scr[rows, :]
        d = 1
        while d < SUBLANES:
            keep = sub >= d
            a_sh = jnp.where(keep, pltpu.roll(at, d, axis=0), 1.0)
            b_sh = jnp.where(keep, pltpu.roll(bt, d, axis=0), 0.0)
            bt = at * b_sh + bt
            at = at * a_sh
            d *= 2
        h = bt + at * h_prev
        b_scr[rows, :] = h
        return jnp.broadcast_to(h[SUBLANES - 1:SUBLANES, :], (SUBLANES, C))

    def scan_body(i, h_prev):
        for u in range(SCAN_UNROLL):
            h_prev = scan_tile(i * SCAN_UNROLL + u, h_prev)
        return h_prev

    lax.fori_loop(0, S // (SUBLANES * SCAN_UNROLL), scan_body, jnp.zeros((SUBLANES, C), F32))

    for r0 in range(0, S, LRU_ROWS):
        rows = slice(r0, r0 + LRU_ROWS)
        y_ref[rows, :] = b_scr[rows, :] * gate_ref[rows, :]


def _lru(lru, w_a, b_a, w_x, b_x, lam, B, S):
    T = lru.shape[0]
    C = LRU_WIDTH
    n_ch = C // LANES
    assert S % (SUBLANES * SCAN_UNROLL) == 0 and S % LRU_ROWS == 0

    def blockdiag(w):
        w = w.reshape(n_ch, 2, LRU_BLOCK, LRU_BLOCK)
        z = jnp.zeros((n_ch, LRU_BLOCK, LRU_BLOCK), w.dtype)
        top = jnp.concatenate([w[:, 0], z], axis=2)
        bot = jnp.concatenate([z, w[:, 1]], axis=2)
        return jnp.concatenate([top, bot], axis=1).astype(BF16)

    vec = lambda v: v.reshape(1, C)
    return pl.pallas_call(
        _lru_kernel,
        grid=(B,),
        in_specs=[
            pl.BlockSpec((S, C), lambda b: (b, 0)),
            pl.BlockSpec((S, C), lambda b: (b, 1)),
            _resident((n_ch, LANES, LANES)), _resident((1, C)),
            _resident((n_ch, LANES, LANES)), _resident((1, C)),
            _resident((1, C)),
        ],
        out_specs=pl.BlockSpec((S, C), lambda b: (b, 0)),
        out_shape=jax.ShapeDtypeStruct((T, C), F32),
        scratch_shapes=[pltpu.VMEM((S, C), F32),
                        pltpu.VMEM((S, C), F32)],
        compiler_params=pltpu.CompilerParams(
            dimension_semantics=("parallel",), vmem_limit_bytes=VMEM_LIMIT),
        name="lru",
    )(lru, lru, blockdiag(w_a), vec(b_a.reshape(-1)),
      blockdiag(w_x), vec(b_x.reshape(-1)), vec(lam))


def _cmp_kernel(seg_ref, pek_ref, w1k_ref, w2k_ref, pev_ref, w1v_ref, w2vt_ref, kc_ref, vct_ref):
    n_seg = seg_ref.shape[2]
    half = CMP_STRIDE * HEAD_DIM

    def hidden(idx, pe_ref, w1_ref):
        seg = seg_ref[idx, 0]
        first = _dot(seg, w1_ref[0:half, :])
        second = _dot(seg, w1_ref[half:2 * half, :])
        bias = _dot(pe_ref[...], w1_ref[...])[0:1, :]
        return _silu(first + pltpu.roll(second, n_seg - 1, axis=0) + bias).astype(BF16)

    for g in range(NSA_KV_HEADS):
        tok = _dot(hidden(g, pek_ref, w1k_ref), w2k_ref[...])
        kc_ref[0, g] = jnp.concatenate([tok, jnp.zeros_like(tok)], axis=1).astype(BF16)
        vct_ref[0, g] = _dot_nt(w2vt_ref[...], hidden(NSA_KV_HEADS + g, pev_ref, w1v_ref)
                                ).astype(BF16)


def _compress(cmp_in, pe_k, w1_k, w2_k, pe_v, w1_v, w2_v, B, S):
    G = NSA_KV_HEADS
    n_seg = S // CMP_STRIDE
    seg = cmp_in.reshape(2 * G, B, n_seg, CMP_STRIDE * HEAD_DIM)
    flat = CMP_BLOCK * HEAD_DIM
    pe8 = lambda pe: jnp.broadcast_to(pe.reshape(1, -1), (8, flat)).astype(BF16)
    return pl.pallas_call(
        _cmp_kernel,
        grid=(B,),
        in_specs=[
            pl.BlockSpec((2 * G, 1, n_seg, CMP_STRIDE * HEAD_DIM), lambda b: (0, b, 0, 0)),
            _resident((8, flat)), _resident((flat, CMP_HIDDEN)), _resident((CMP_HIDDEN, HEAD_DIM)),
            _resident((8, flat)), _resident((flat, CMP_HIDDEN)), _resident((HEAD_DIM, CMP_HIDDEN)),
        ],
        out_specs=[
            pl.BlockSpec((1, G, n_seg, LANES), lambda b: (b, 0, 0, 0)),
            pl.BlockSpec((1, G, HEAD_DIM, n_seg), lambda b: (b, 0, 0, 0)),
        ],
        out_shape=[
            jax.ShapeDtypeStruct((B, G, n_seg, LANES), BF16),
            jax.ShapeDtypeStruct((B, G, HEAD_DIM, n_seg), BF16),
        ],
        compiler_params=pltpu.CompilerParams(
            dimension_semantics=("parallel",), vmem_limit_bytes=VMEM_LIMIT),
        name="compress",
    )(seg, pe8(pe_k), w1_k.astype(BF16), w2_k.astype(BF16),
      pe8(pe_v), w1_v.astype(BF16), w2_v.T.astype(BF16))


def _attn_kernel(qt_ref, kc_ref, vct_ref, ks_ref, vst_ref, kw_ref, vwt_ref, gt_ref, ovl_ref,
                 bias_ref, *rest, tq, tiles):
    o_ref = rest[-1]

    def step(n):
        for _ in _interleave(*[
                _attn_step(n, g, qt_ref, kc_ref.at[g], vct_ref.at[g], ks_ref.at[g], vst_ref.at[g],
                           kw_ref.at[g], vwt_ref.at[g], gt_ref.at[g], ovl_ref, bias_ref, o_ref, tq)
                for g in range(NSA_KV_HEADS)]):
            pass

    for n in range(*tiles):
        pl.when(pl.program_id(1) == n - tiles[0])(functools.partial(step, n))


CHUNK_TILES = 1
SCORES_AHEAD = 1
HEAD_PAIRS = 4
ATTN_TILE_SPLITS = ((0, 3), (3, 5), (5, 7), (7, 8))


def _interleave(*stages):
    live = list(stages)
    while live:
        for st in list(live):
            try:
                next(st)
            except StopIteration:
                live.remove(st)
        yield


def _tile_bias(bias_ref, t0, k0, behind, width):
    if k0 == t0:
        return bias_ref[0, :, 0:width]
    if behind is not None and k0 == t0 - behind:
        return bias_ref[1, :, 0:width]
    return None


def _key_chunks(k_ref, vt_ref, k0, n_keys, tile, q, bias_fn):
    starts = list(range(k0, k0 + n_keys, tile))
    return [dict(k_ref=k_ref, vt_ref=vt_ref, tile=tile, q=q, bias_fn=bias_fn,
                 starts=starts[i:i + CHUNK_TILES]) for i in range(0, len(starts), CHUNK_TILES)]


def _scores_stage(ch):
    tiles, m = [], None
    for k0 in ch["starts"]:
        s = _dot(ch["k_ref"][k0:k0 + ch["tile"], :], ch["q"])
        bias = ch["bias_fn"](k0)
        if bias is not None:
            s = s + bias
        cm = jnp.max(s, axis=0, keepdims=True)
        m = cm if m is None else jnp.maximum(m, cm)
        tiles.append(s)
        yield
    ch["s"], ch["m"] = tiles, m


def _pv_stage(ch, heads, tq):
    acc = [None] * heads
    for k0, s in zip(ch["starts"], ch["s"]):
        p = jnp.exp2(s - ch["m"]).astype(BF16)
        vt = ch["vt_ref"][:, k0:k0 + ch["tile"]]
        for r in range(heads):
            d = _dot(vt, p[:, r * tq:(r + 1) * tq])
            acc[r] = d if acc[r] is None else acc[r] + d
        yield
    ch["acc"] = acc


def _merge_chunks(chunks, heads, tq):
    if len(chunks) == 1:
        return chunks[0]["acc"]
    m = functools.reduce(jnp.maximum, [ch["m"] for ch in chunks])
    acc = [None] * heads
    for ch in chunks:
        w = jnp.exp2(ch["m"] - m)
        for r in range(heads):
            term = ch["acc"][r] * w[:, r * tq:(r + 1) * tq]
            acc[r] = term if acc[r] is None else acc[r] + term
    return acc


def _rank_stage(out, score, jblk, n_live):
    rank = jnp.zeros(score.shape, jnp.int32)
    for kb in range(n_live):
        sk = score[kb:kb + 1, :]
        ahead = (sk > score) | ((sk == score) & (jblk > kb))
        rank = rank + ahead.astype(jnp.int32)
        if kb % 4 == 3:
            yield
    out["rank"] = rank


def _attn_step(n, g, qt_ref, kc_ref, vct_ref, ks_ref, vst_ref, kw_ref, vwt_ref, gt_ref, ovl_ref,
               bias_ref, o_ref, tq):
    R = NSA_GROUP
    t0 = n * tq
    n_cmp = kc_ref.shape[0]
    n_blk = ovl_ref.shape[0]
    q_all = jnp.concatenate([qt_ref[g * R + r] for r in range(R)], axis=1)
    q_pad = jnp.concatenate([q_all, jnp.zeros((LANES - HEAD_DIM, R * tq), BF16)], axis=0)
    head = lambda a, r: a[:, r * tq:(r + 1) * tq]

    kw0 = max(t0 - WINDOW, 0)
    HP = R // HEAD_PAIRS
    pair_lanes = [slice(p * HP * tq, (p + 1) * HP * tq) for p in range(HEAD_PAIRS)]
    win = [_key_chunks(kw_ref, vwt_ref, kw0, t0 + tq - kw0, tq, q_pad[:, ln],
                       lambda k0: _tile_bias(bias_ref, t0, k0, WINDOW, HP * tq))
           for ln in pair_lanes]
    win_scores = _interleave(*[_scores_stage(w[0]) for w in win])
    next(win_scores)
    yield

    n_vis = (t0 + tq - CMP_BLOCK) // CMP_STRIDE + 1
    n_vis = min(n_cmp, -(-n_vis // BF16_ROWS) * BF16_ROWS)
    tpos_c = t0 + (lax.broadcasted_iota(jnp.int32, (n_vis, R * tq), 1) & (tq - 1))
    cend = lax.broadcasted_iota(jnp.int32, (n_vis, R * tq), 0) * CMP_STRIDE + (CMP_BLOCK - 1)
    s = _dot(kc_ref[0:n_vis, :], q_pad) + jnp.where(cend <= tpos_c, 0.0, NEG)
    e = jnp.exp2(s - jnp.max(s, axis=0, keepdims=True))
    p_cmp = e * (1.0 / jnp.sum(e, axis=0, keepdims=True))
    if t0 < CMP_BLOCK - 1:
        p_cmp = jnp.where(tpos_c >= CMP_BLOCK - 1, p_cmp, 0.0)
    if n_vis < n_cmp:
        p_cmp = jnp.concatenate([p_cmp, jnp.zeros((n_cmp - n_vis, R * tq), F32)], axis=0)
    p_cmp_b = p_cmp.astype(BF16)
    vct = vct_ref[...]
    o_cmp = [_dot(vct, head(p_cmp_b, r)) for r in range(R)]
    psum = functools.reduce(jnp.add, [head(p_cmp, r) for r in range(R)])
    yield

    p_hi = psum.astype(BF16)
    p_lo = (psum - p_hi.astype(F32)).astype(BF16)
    ovl = ovl_ref[...]
    imp = _dot(ovl, p_hi) + _dot(ovl, p_lo)
    n_live = min(n_blk, (t0 + tq - 1) // SLC_BLOCK + 1)
    rows = -(-n_live // SUBLANES) * SUBLANES
    jblk = lax.broadcasted_iota(jnp.int32, (rows, tq), 0)
    tpos_b = t0 + lax.broadcasted_iota(jnp.int32, (rows, tq), 1)
    blk_valid = jblk * SLC_BLOCK <= tpos_b
    back = (tpos_b >> SLC_SHIFT) - jblk
    forced = (jblk == 0) | ((back >= 0) & (back < N_LOCAL_BLOCKS))
    score = jnp.where(blk_valid, jnp.where(forced, jnp.inf, imp[0:rows]), -jnp.inf)
    sel = {}
    yield from _interleave(_rank_stage(sel, score, jblk, n_live), win_scores)
    unsel = 1.0 - (blk_valid & (sel["rank"] < SLC_TOP_N)).astype(F32)
    dead = [jnp.ones((n_blk - rows, tq), F32)] if rows < n_blk else []
    sel_ext = jnp.concatenate(
        [unsel] + dead + [jnp.zeros((LANES - HEAD_DIM - n_blk, tq), F32)],
        axis=0).astype(BF16)
    q_sel = jnp.concatenate([q_all, jnp.concatenate([sel_ext] * R, axis=1)], axis=0)

    slc = [_key_chunks(ks_ref, vst_ref, 0, t0 + tq, tq, q_sel[:, ln],
                       lambda k0: _tile_bias(bias_ref, t0, k0, None, HP * tq))
           for ln in pair_lanes]

    def pipeline(chunks):
        for ch in chunks[1:SCORES_AHEAD]:
            yield from _scores_stage(ch)
        for idx, ch in enumerate(chunks):
            stages = [_pv_stage(ch, HP, tq)]
            if idx + SCORES_AHEAD < len(chunks):
                stages.append(_scores_stage(chunks[idx + SCORES_AHEAD]))
            yield from _interleave(*stages)

    yield from _interleave(*[pipeline(w + s) for w, s in zip(win, slc)])
    acc_w = [a for w in win for a in _merge_chunks(w, HP, tq)]
    acc_s = [a for s in slc for a in _merge_chunks(s, HP, tq)]
    yield

    gt = gt_ref[...]
    outs = []
    for r in range(R):
        g_cmp = gt[3 * r:3 * r + 1]
        g_slc = gt[3 * r + 1:3 * r + 2] * (1.0 / acc_s[r][HEAD_DIM:HEAD_DIM + 1])
        g_win = gt[3 * r + 2:3 * r + 3] * (1.0 / acc_w[r][HEAD_DIM:HEAD_DIM + 1])
        outs.append(g_cmp * o_cmp[r] + g_slc * acc_s[r][:HEAD_DIM] + g_win * acc_w[r][:HEAD_DIM])
    width = R * HEAD_DIM
    o_ref[:, g * width:(g + 1) * width] = jnp.concatenate(outs, axis=0).T


def _overlap_t(n_cmp_pad, n_blk):
    cs = np.arange(n_cmp_pad) * CMP_STRIDE
    ce = cs + CMP_BLOCK - 1
    ss = np.arange(n_blk) * SLC_BLOCK
    se = ss + SLC_BLOCK - 1
    return ((cs[None, :] <= se[:, None]) & (ce[None, :] >= ss[:, None])).astype(np.float32)


def _attention(qt, k, vt, kc, vct, gates_t, B, S, *, tq=256):
    T = qt.shape[2]
    nq = S // tq
    n_cmp = S // CMP_STRIDE
    n_blk = S // SLC_BLOCK
    G = NSA_KV_HEADS
    ovl = jnp.asarray(_overlap_t(n_cmp, n_blk), BF16)
    assert WINDOW % tq == 0 and WINDOW + tq <= S
    assert HEAD_DIM + n_blk <= LANES
    assert tq & (tq - 1) == 0
    key_off = np.arange(tq)[:, None]
    q_off = np.tile(np.arange(tq), NSA_GROUP)[None, :]
    bias = jnp.asarray(np.stack([np.where(key_off <= q_off, 0.0, NEG),
                                 np.where(key_off > q_off, 0.0, NEG)]), F32)
    k_spec = lambda c: pl.BlockSpec((G, S, LANES), lambda b, i: (c, b, 0))
    vt_spec = lambda c: pl.BlockSpec((G, V_ROWS, S), lambda b, i: (c, 0, b))

    def call(tiles, prev):
        lo, hi = tiles
        tile = lambda b, i: b * nq + lo + i
        in_specs = [
            pl.BlockSpec((NSA_Q_HEADS, HEAD_DIM, tq), lambda b, i: (0, 0, tile(b, i))),
            pl.BlockSpec((None, G, n_cmp, LANES), lambda b, i: (b, 0, 0, 0)),
            pl.BlockSpec((None, G, HEAD_DIM, n_cmp), lambda b, i: (b, 0, 0, 0)),
            k_spec(0), vt_spec(0), k_spec(1), vt_spec(1),
            pl.BlockSpec((G, GATE_ROWS, tq), lambda b, i: (0, 0, tile(b, i))),
            _resident((n_blk, n_cmp)),
            _resident((2, tq, NSA_GROUP * tq)),
        ]
        args = [qt, kc, vct, k, vt, k, vt, gates_t, ovl, bias]
        aliases = {}
        if prev is not None:
            in_specs.append(pl.BlockSpec(memory_space=pl.ANY))
            aliases = {len(args): 0}
            args.append(prev)
        return pl.pallas_call(
            functools.partial(_attn_kernel, tq=tq, tiles=tiles),
            grid=(B, hi - lo),
            in_specs=in_specs,
            out_specs=pl.BlockSpec((tq, NSA_WIDTH), lambda b, i: (tile(b, i), 0)),
            out_shape=jax.ShapeDtypeStruct((T, NSA_WIDTH), F32),
            input_output_aliases=aliases,
            compiler_params=pltpu.CompilerParams(
                dimension_semantics=("parallel", "arbitrary"),
                vmem_limit_bytes=VMEM_LIMIT),
            name=f"nsa_attn_{lo}_{hi}",
        )(*args)

    assert [t for lo, hi in ATTN_TILE_SPLITS for t in range(lo, hi)] == list(range(nq))
    out = None
    for tiles in ATTN_TILE_SPLITS:
        out = call(tiles, out)
    return out


def _out_ffn_kernel(ylru_ref, ynsa_ref, h_ref, gl_ref, gn_ref, w_ref, g2_ref, b2_ref,
                    w_in_hbm, w_out_hbm, g3_ref, b3_ref, o_ref, wg_scr, wu_scr, wo_scr, *staging):
    @pl.when(pl.program_id(0) == 0)
    def _():
        _stage_ffn_weights(w_in_hbm, w_out_hbm, wg_scr, wu_scr, wo_scr, *staging)

    yl = _rms_norm(ylru_ref[...], gl_ref[...]).astype(BF16)
    yn = _rms_norm(ynsa_ref[...], gn_ref[...]).astype(BF16)
    mix = _dot(yl, w_ref[0:LRU_WIDTH, :]) + _dot(yn, w_ref[LRU_WIDTH:, :])
    h2 = _layer_norm(ALPHA * h_ref[...] + mix, g2_ref[...], b2_ref[...])
    o_ref[...] = _ffn_half_step(h2, wg_scr, wu_scr, wo_scr, g3_ref, b3_ref)


def _out_proj_ffn(y_lru, y_nsa, h, gn_lru, gn_nsa, w_mix, g2, b2, w_in, w_out, g3, b3, *,
                  tm=512):
    T, D = h.shape
    row = lambda v: v.reshape(1, -1)
    return pl.pallas_call(
        _out_ffn_kernel,
        grid=(T // tm,),
        in_specs=[
            pl.BlockSpec((tm, LRU_WIDTH), lambda i: (i, 0)),
            pl.BlockSpec((tm, NSA_WIDTH), lambda i: (i, 0)),
            pl.BlockSpec((tm, D), lambda i: (i, 0)),
            _resident((1, LRU_WIDTH)), _resident((1, NSA_WIDTH)),
            _resident((LRU_WIDTH + NSA_WIDTH, D)),
            _resident((1, D)), _resident((1, D)),
            pl.BlockSpec(memory_space=pl.ANY), pl.BlockSpec(memory_space=pl.ANY),
            _resident((1, D)), _resident((1, D)),
        ],
        out_specs=pl.BlockSpec((tm, D), lambda i: (i, 0)),
        out_shape=jax.ShapeDtypeStruct((T, D), F32),
        scratch_shapes=_ffn_weight_scratch(D),
        compiler_params=pltpu.CompilerParams(
            dimension_semantics=("arbitrary",), vmem_limit_bytes=FFN_VMEM_LIMIT),
        name="out_proj_ffn",
    )(y_lru, y_nsa, h, row(gn_lru), row(gn_nsa), w_mix.astype(BF16), row(g2), row(b2),
      w_in, w_out, row(g3), row(b3))


def kernel(x, ffn1_w_in, ffn1_w_out, ln1_g, ln1_b, mix_w_in, conv_w, conv_b, lru_w_a, lru_b_a,
           lru_w_x, lru_b_x, lru_lam, cmp_pe_k, cmp_w1_k, cmp_w2_k, cmp_pe_v, cmp_w1_v,
           cmp_w2_v, gn_lru, gn_nsa, mix_w_out, ln2_g, ln2_b, ffn2_w_in, ffn2_w_out, ln3_g,
           ln3_b):
    B, S, D = x.shape
    h = x.reshape(B * S, D)
    for l in range(DEPTH):
        h = _ffn_ln(h, ffn1_w_in[l], ffn1_w_out[l], ln1_g[l], ln1_b[l])
        lru, qt, cmp_in, k, vt, gates_t = _proj(h, mix_w_in[l], conv_w[l], conv_b[l], S)
        y_lru = _lru(lru, lru_w_a[l], lru_b_a[l], lru_w_x[l], lru_b_x[l], lru_lam[l], B, S)
        kc, vct = _compress(cmp_in, cmp_pe_k[l], cmp_w1_k[l], cmp_w2_k[l],
                            cmp_pe_v[l], cmp_w1_v[l], cmp_w2_v[l], B, S)
        y_nsa = _attention(qt, k, vt, kc, vct, gates_t, B, S)
        h = _out_proj_ffn(y_lru, y_nsa, h, gn_lru[l], gn_nsa[l], mix_w_out[l], ln2_g[l], ln2_b[l],
                          ffn2_w_in[l], ffn2_w_out[l], ln3_g[l], ln3_b[l])
    return h.reshape(B, S, D)
```
